```python
import math
import jax, jax.numpy as jnp
from jax import lax
import numpy as np

D_MODEL = 1024
BATCH = 8
SEQ = 16384
DEPTH = 2

N_MIXERS = 2
N_ATTN_LAYERS = (DEPTH + N_MIXERS - 1) // N_MIXERS
N_SSM_LAYERS = DEPTH // N_MIXERS

ATTN_HEAD_DIM = 64
ATTN_HEADS = D_MODEL // ATTN_HEAD_DIM
Q_BLOCK = 128
SSM_EXPAND = 2
SSM_D_INNER = SSM_EXPAND * D_MODEL
SSM_HEAD_DIM = 64
SSM_HEADS = SSM_D_INNER // SSM_HEAD_DIM
SSM_GROUPS = 8
SSM_STATE = 128
SSM_CONV = 4
SSM_CHUNK = 128
SSM_XBC = SSM_D_INNER + 2 * SSM_GROUPS * SSM_STATE
FFN_DIM = 2816
FFN_CONV = 3
PLE_DIM = 256

LN_EPS = 1e-5
RMS_EPS = 1e-5
DEEPNORM_ALPHA = (2 * DEPTH) ** 0.25
DEEPNORM_BETA = (8 * DEPTH) ** -0.25

kernel_name = "fox_ssd_interleaved_deepnorm_trunk"


def layer_norm(x, g, b):
    xf = x.astype(jnp.float32)
    mu = jnp.mean(xf, axis=-1, keepdims=True)
    var = jnp.mean(jnp.square(xf - mu), axis=-1, keepdims=True)
    return ((xf - mu) * lax.rsqrt(var + LN_EPS) * g + b).astype(x.dtype)


def causal_dwconv(u, w, b):
    K = w.shape[0]
    S = u.shape[1]
    up = jnp.pad(u, ((0, 0), (K - 1, 0), (0, 0)))
    out = b
    for k in range(K):
        out = out + up[:, k:k + S] * w[k]
    return out


def fox_mixer(x, w_in, b_f, w_out):
    Bsz, S, _ = x.shape
    H, Dh = ATTN_HEADS, ATTN_HEAD_DIM
    proj = x @ w_in
    q, k, v, f_logit = jnp.split(proj, [D_MODEL, 2 * D_MODEL, 3 * D_MODEL], axis=-1)
    q = q.reshape(Bsz, S, H, Dh).transpose(0, 2, 1, 3)
    k = k.reshape(Bsz, S, H, Dh).transpose(0, 2, 1, 3)
    v = v.reshape(Bsz, S, H, Dh).transpose(0, 2, 1, 3)
    log_f = jax.nn.log_sigmoid(f_logit.astype(jnp.float32) + b_f)
    c = jnp.cumsum(log_f, axis=1).transpose(0, 2, 1)
    nb = S // Q_BLOCK
    qb = q.reshape(Bsz, H, nb, Q_BLOCK, Dh).transpose(2, 0, 1, 3, 4)
    cqb = c.reshape(Bsz, H, nb, Q_BLOCK).transpose(2, 0, 1, 3)
    kpos = jnp.arange(S)
    scale = 1.0 / math.sqrt(Dh)

    def block(args):
        qi, cqi, bi = args
        s = jnp.einsum('bhqd,bhkd->bhqk', qi, k).astype(jnp.float32) * scale
        s = s + cqi[..., None] - c[:, :, None, :]
        qpos = bi * Q_BLOCK + jnp.arange(Q_BLOCK)
        s = jnp.where(kpos[None, :] <= qpos[:, None], s, -jnp.inf)
        pr = jax.nn.softmax(s, axis=-1)
        return jnp.einsum('bhqk,bhkd->bhqd', pr.astype(v.dtype), v)

    o = lax.map(block, (qb, cqb, jnp.arange(nb)))
    o = o.transpose(1, 0, 3, 2, 4).reshape(Bsz, S, D_MODEL)
    return o @ w_out


def ssd_scan(xh, dt, A, Bm, Cm):
    Bsz, S = xh.shape[0], xh.shape[1]
    G, R, P, N, Q = SSM_GROUPS, SSM_HEADS // SSM_GROUPS, SSM_HEAD_DIM, SSM_STATE, SSM_CHUNK
    nc = S // Q

    def to_chunks(t):
        return jnp.moveaxis(t.reshape((Bsz, nc, Q) + t.shape[2:]), 1, 0)

    xc = to_chunks(xh.reshape(Bsz, S, G, R, P))
    dtc = to_chunks(dt.reshape(Bsz, S, G, R))
    Bc = to_chunks(Bm)
    Cc = to_chunks(Cm)
    A_g = A.reshape(G, R)
    tri = jnp.tril(jnp.ones((Q, Q), dtype=bool))[None, :, :, None, None]

    def step(state, inp):
        x_, dt_, B_, C_ = inp
        acum = jnp.cumsum(dt_ * A_g, axis=1)
        seg = acum[:, :, None] - acum[:, None, :]
        L = jnp.exp(jnp.where(tri, seg, -jnp.inf))
        CB = jnp.einsum('btgn,bsgn->btsg', C_, B_)
        y_intra = jnp.einsum('btsg,btsgr,bsgr,bsgrp->btgrp', CB, L, dt_, x_)
        y_inter = jnp.einsum('btgn,bgrpn,btgr->btgrp', C_, state, jnp.exp(acum))
        w_end = jnp.exp(acum[:, -1:] - acum) * dt_
        new_state = state * jnp.exp(acum[:, -1])[..., None, None] + \
            jnp.einsum('bsgn,bsgr,bsgrp->bgrpn', B_, w_end, x_)
        return new_state, (y_intra + y_inter).astype(jnp.float32)

    state0 = jnp.zeros((Bsz, G, R, P, N), jnp.float32)
    _, y = lax.scan(step, state0, (xc, dtc, Bc, Cc))
    return jnp.moveaxis(y, 0, 1).reshape(Bsz, S, SSM_HEADS, P)


def ssd_mixer(x, w_in, conv_w, conv_b, dt_bias, A_log, D_skip, norm_w, w_out):
    Bsz, S, _ = x.shape
    GN = SSM_GROUPS * SSM_STATE
    proj = x @ w_in
    z, xBC, dt_raw = jnp.split(proj, [SSM_D_INNER, SSM_D_INNER + SSM_XBC], axis=-1)
    xBC = jax.nn.silu(causal_dwconv(xBC, conv_w, conv_b))
    xs, Bm, Cm = jnp.split(xBC, [SSM_D_INNER, SSM_D_INNER + GN], axis=-1)
    dt = jax.nn.softplus(dt_raw.astype(jnp.float32) + dt_bias)
    A = -jnp.exp(A_log.astype(jnp.float32))
    xh = xs.reshape(Bsz, S, SSM_HEADS, SSM_HEAD_DIM)
    y = ssd_scan(xh, dt, A,
                 Bm.reshape(Bsz, S, SSM_GROUPS, SSM_STATE),
                 Cm.reshape(Bsz, S, SSM_GROUPS, SSM_STATE))
    y = y + D_skip[:, None] * xh
    y = y.reshape(Bsz, S, SSM_D_INNER) * jax.nn.silu(z.astype(jnp.float32))
    yg = y.reshape(Bsz, S, SSM_GROUPS, SSM_D_INNER // SSM_GROUPS)
    yg = yg * lax.rsqrt(jnp.mean(jnp.square(yg), axis=-1, keepdims=True) + RMS_EPS)
    y = yg.reshape(Bsz, S, SSM_D_INNER) * norm_w
    return y.astype(x.dtype) @ w_out


def conv_ffn(x, w_up, conv_w, conv_b, w_down):
    u, g = jnp.split(x @ w_up, [FFN_DIM], axis=-1)
    g = causal_dwconv(g, conv_w, conv_b)
    return (jax.nn.gelu(g, approximate=False) * u) @ w_down


def _fwd_setup_inputs(seed: int = 0) -> dict:
    key = jax.random.key(seed)
    ks = iter(jax.random.split(key, 40))

    def nrm(shape, scale):
        return jax.random.normal(next(ks), shape, jnp.float32) * scale

    D, H = D_MODEL, ATTN_HEADS
    NA, NB = N_ATTN_LAYERS, N_SSM_LAYERS
    beta = DEEPNORM_BETA
    x = nrm((BATCH, SEQ, D), 1.0)
    p = nrm((DEPTH, BATCH, SEQ, PLE_DIM), 1.0)
    attn_w_in = nrm((NA, D, 3 * D + H), D ** -0.5)
    attn_w_in = attn_w_in.at[:, :, 2 * D:3 * D].multiply(beta)
    attn_b_f = jax.random.uniform(next(ks), (NA, H), jnp.float32, 1.0, 6.0)
    attn_w_out = nrm((NA, D, D), beta * D ** -0.5)
    ssm_in_dim = 2 * SSM_D_INNER + 2 * SSM_GROUPS * SSM_STATE + SSM_HEADS
    ssm_w_in = nrm((NB, D, ssm_in_dim), D ** -0.5)
    ssm_conv_w = nrm((NB, SSM_CONV, SSM_XBC), SSM_CONV ** -0.5)
    ssm_conv_b = nrm((NB, SSM_XBC), 0.02)
    dt0 = jnp.exp(jax.random.uniform(next(ks), (NB, SSM_HEADS), jnp.float32,
                                     math.log(1e-3), math.log(1e-1)))
    ssm_dt_bias = dt0 + jnp.log(-jnp.expm1(-dt0))
    ssm_A_log = jnp.log(jax.random.uniform(next(ks), (NB, SSM_HEADS), jnp.float32, 1.0, 16.0))
    ssm_D = 1.0 + nrm((NB, SSM_HEADS), 0.1)
    ssm_norm_w = 1.0 + nrm((NB, SSM_D_INNER), 0.05)
    ssm_w_out = nrm((NB, SSM_D_INNER, D), beta * SSM_D_INNER ** -0.5)
    ln_mix_g = 1.0 + nrm((DEPTH, D), 0.05)
    ln_mix_b = nrm((DEPTH, D), 0.02)
    ffn_w_up = nrm((DEPTH, D, 2 * FFN_DIM), D ** -0.5)
    ffn_conv_w = nrm((DEPTH, FFN_CONV, FFN_DIM), FFN_CONV ** -0.5)
    ffn_conv_b = nrm((DEPTH, FFN_DIM), 0.02)
    ffn_w_down = nrm((DEPTH, FFN_DIM, D), beta * FFN_DIM ** -0.5)
    ln_ffn_g = 1.0 + nrm((DEPTH, D), 0.05)
    ln_ffn_b = nrm((DEPTH, D), 0.02)
    ple_w_proj = nrm((DEPTH, PLE_DIM, D), beta * PLE_DIM ** -0.5)
    ple_w_gate = nrm((DEPTH, D, D), D ** -0.5)
    ple_b_gate = nrm((DEPTH, D), 0.02)
    return {"x": x, "p": p,
            "attn_w_in": attn_w_in, "attn_b_f": attn_b_f, "attn_w_out": attn_w_out,
            "ssm_w_in": ssm_w_in, "ssm_conv_w": ssm_conv_w, "ssm_conv_b": ssm_conv_b,
            "ssm_dt_bias": ssm_dt_bias, "ssm_A_log": ssm_A_log, "ssm_D": ssm_D,
            "ssm_norm_w": ssm_norm_w, "ssm_w_out": ssm_w_out,
            "ln_mix_g": ln_mix_g, "ln_mix_b": ln_mix_b,
            "ffn_w_up": ffn_w_up, "ffn_conv_w": ffn_conv_w, "ffn_conv_b": ffn_conv_b,
            "ffn_w_down": ffn_w_down, "ln_ffn_g": ln_ffn_g, "ln_ffn_b": ln_ffn_b,
            "ple_w_proj": ple_w_proj, "ple_w_gate": ple_w_gate, "ple_b_gate": ple_b_gate}


def _fwd_reference(x, p, attn_w_in, attn_b_f, attn_w_out, ssm_w_in, ssm_conv_w, ssm_conv_b,
              ssm_dt_bias, ssm_A_log, ssm_D, ssm_norm_w, ssm_w_out, ln_mix_g, ln_mix_b,
              ffn_w_up, ffn_conv_w, ffn_conv_b, ffn_w_down, ln_ffn_g, ln_ffn_b,
              ple_w_proj, ple_w_gate, ple_b_gate):
    for i in range(DEPTH):
        j = i // N_MIXERS
        if i % N_MIXERS == 0:
            mix = fox_mixer(x, attn_w_in[j], attn_b_f[j], attn_w_out[j])
        else:
            mix = ssd_mixer(x, ssm_w_in[j], ssm_conv_w[j], ssm_conv_b[j], ssm_dt_bias[j],
                            ssm_A_log[j], ssm_D[j], ssm_norm_w[j], ssm_w_out[j])
        x = layer_norm(DEEPNORM_ALPHA * x + mix, ln_mix_g[i], ln_mix_b[i])
        ffn = conv_ffn(x, ffn_w_up[i], ffn_conv_w[i], ffn_conv_b[i], ffn_w_down[i])
        x = layer_norm(DEEPNORM_ALPHA * x + ffn, ln_ffn_g[i], ln_ffn_b[i])
        gate = jax.nn.sigmoid(x @ ple_w_gate[i] + ple_b_gate[i])
        x = x + gate * (p[i] @ ple_w_proj[i])
    return x


import jax as _jax
import jax.numpy as _jnp

TWIN_FORMAT = 'train_step'
FWD_PARAMS = ['x', 'p', 'attn_w_in', 'attn_b_f', 'attn_w_out', 'ssm_w_in', 'ssm_conv_w', 'ssm_conv_b', 'ssm_dt_bias', 'ssm_A_log', 'ssm_D', 'ssm_norm_w', 'ssm_w_out', 'ln_mix_g', 'ln_mix_b', 'ffn_w_up', 'ffn_conv_w', 'ffn_conv_b', 'ffn_w_down', 'ln_ffn_g', 'ln_ffn_b', 'ple_w_proj', 'ple_w_gate', 'ple_b_gate']
TWIN_WEIGHTS = ['attn_w_in', 'attn_b_f', 'attn_w_out', 'ssm_w_in', 'ssm_conv_w', 'ssm_conv_b', 'ssm_dt_bias', 'ssm_A_log', 'ssm_D', 'ssm_norm_w', 'ssm_w_out', 'ln_mix_g', 'ln_mix_b', 'ffn_w_up', 'ffn_conv_w', 'ffn_conv_b', 'ffn_w_down', 'ln_ffn_g', 'ln_ffn_b', 'ple_w_proj', 'ple_w_gate', 'ple_b_gate']
TWIN_DIFF_INPUT = 'x'
TWIN_INPUTS = ['x', 'p', 'attn_w_in', 'attn_b_f', 'attn_w_out', 'ssm_w_in', 'ssm_conv_w', 'ssm_conv_b', 'ssm_dt_bias', 'ssm_A_log', 'ssm_D', 'ssm_norm_w', 'ssm_w_out', 'ln_mix_g', 'ln_mix_b', 'ffn_w_up', 'ffn_conv_w', 'ffn_conv_b', 'ffn_w_down', 'ln_ffn_g', 'ln_ffn_b', 'ple_w_proj', 'ple_w_gate', 'ple_b_gate', 'loss_target', 'm_attn_w_in', 'm_attn_b_f', 'm_attn_w_out', 'm_ssm_w_in', 'm_ssm_conv_w', 'm_ssm_conv_b', 'm_ssm_dt_bias', 'm_ssm_A_log', 'm_ssm_D', 'm_ssm_norm_w', 'm_ssm_w_out', 'm_ln_mix_g', 'm_ln_mix_b', 'm_ffn_w_up', 'm_ffn_conv_w', 'm_ffn_conv_b', 'm_ffn_w_down', 'm_ln_ffn_g', 'm_ln_ffn_b', 'm_ple_w_proj', 'm_ple_w_gate', 'm_ple_b_gate', 'v_attn_w_in', 'v_attn_b_f', 'v_attn_w_out', 'v_ssm_w_in', 'v_ssm_conv_w', 'v_ssm_conv_b', 'v_ssm_dt_bias', 'v_ssm_A_log', 'v_ssm_D', 'v_ssm_norm_w', 'v_ssm_w_out', 'v_ln_mix_g', 'v_ln_mix_b', 'v_ffn_w_up', 'v_ffn_conv_w', 'v_ffn_conv_b', 'v_ffn_w_down', 'v_ln_ffn_g', 'v_ln_ffn_b', 'v_ple_w_proj', 'v_ple_w_gate', 'v_ple_b_gate']
TWIN_OUTPUTS = ['loss', 'grad_x', 'grad_attn_w_in', 'grad_attn_b_f', 'grad_attn_w_out', 'grad_ssm_w_in', 'grad_ssm_conv_w', 'grad_ssm_conv_b', 'grad_ssm_dt_bias', 'grad_ssm_A_log', 'grad_ssm_D', 'grad_ssm_norm_w', 'grad_ssm_w_out', 'grad_ln_mix_g', 'grad_ln_mix_b', 'grad_ffn_w_up', 'grad_ffn_conv_w', 'grad_ffn_conv_b', 'grad_ffn_w_down', 'grad_ln_ffn_g', 'grad_ln_ffn_b', 'grad_ple_w_proj', 'grad_ple_w_gate', 'grad_ple_b_gate', 'delta_attn_w_in', 'delta_attn_b_f', 'delta_attn_w_out', 'delta_ssm_w_in', 'delta_ssm_conv_w', 'delta_ssm_conv_b', 'delta_ssm_dt_bias', 'delta_ssm_A_log', 'delta_ssm_D', 'delta_ssm_norm_w', 'delta_ssm_w_out', 'delta_ln_mix_g', 'delta_ln_mix_b', 'delta_ffn_w_up', 'delta_ffn_conv_w', 'delta_ffn_conv_b', 'delta_ffn_w_down', 'delta_ln_ffn_g', 'delta_ln_ffn_b', 'delta_ple_w_proj', 'delta_ple_w_gate', 'delta_ple_b_gate', 'new_m_attn_w_in', 'new_m_attn_b_f', 'new_m_attn_w_out', 'new_m_ssm_w_in', 'new_m_ssm_conv_w', 'new_m_ssm_conv_b', 'new_m_ssm_dt_bias', 'new_m_ssm_A_log', 'new_m_ssm_D', 'new_m_ssm_norm_w', 'new_m_ssm_w_out', 'new_m_ln_mix_g', 'new_m_ln_mix_b', 'new_m_ffn_w_up', 'new_m_ffn_conv_w', 'new_m_ffn_conv_b', 'new_m_ffn_w_down', 'new_m_ln_ffn_g', 'new_m_ln_ffn_b', 'new_m_ple_w_proj', 'new_m_ple_w_gate', 'new_m_ple_b_gate', 'new_v_attn_w_in', 'new_v_attn_b_f', 'new_v_attn_w_out', 'new_v_ssm_w_in', 'new_v_ssm_conv_w', 'new_v_ssm_conv_b', 'new_v_ssm_dt_bias', 'new_v_ssm_A_log', 'new_v_ssm_D', 'new_v_ssm_norm_w', 'new_v_ssm_w_out', 'new_v_ln_mix_g', 'new_v_ln_mix_b', 'new_v_ffn_w_up', 'new_v_ffn_conv_w', 'new_v_ffn_conv_b', 'new_v_ffn_w_down', 'new_v_ln_ffn_g', 'new_v_ln_ffn_b', 'new_v_ple_w_proj', 'new_v_ple_w_gate', 'new_v_ple_b_gate']
TWIN_LEAF_KINDS = {'loss': 'loss', 'grad_x': 'grad_x', 'grad_attn_w_in': 'grad_w', 'grad_attn_b_f': 'grad_w', 'grad_attn_w_out': 'grad_w', 'grad_ssm_w_in': 'grad_w', 'grad_ssm_conv_w': 'grad_w', 'grad_ssm_conv_b': 'grad_w', 'grad_ssm_dt_bias': 'grad_w', 'grad_ssm_A_log': 'grad_w', 'grad_ssm_D': 'grad_w', 'grad_ssm_norm_w': 'grad_w', 'grad_ssm_w_out': 'grad_w', 'grad_ln_mix_g': 'grad_w', 'grad_ln_mix_b': 'grad_w', 'grad_ffn_w_up': 'grad_w', 'grad_ffn_conv_w': 'grad_w', 'grad_ffn_conv_b': 'grad_w', 'grad_ffn_w_down': 'grad_w', 'grad_ln_ffn_g': 'grad_w', 'grad_ln_ffn_b': 'grad_w', 'grad_ple_w_proj': 'grad_w', 'grad_ple_w_gate': 'grad_w', 'grad_ple_b_gate': 'grad_w', 'delta_attn_w_in': 'delta_w', 'delta_attn_b_f': 'delta_w', 'delta_attn_w_out': 'delta_w', 'delta_ssm_w_in': 'delta_w', 'delta_ssm_conv_w': 'delta_w', 'delta_ssm_conv_b': 'delta_w', 'delta_ssm_dt_bias': 'delta_w', 'delta_ssm_A_log': 'delta_w', 'delta_ssm_D': 'delta_w', 'delta_ssm_norm_w': 'delta_w', 'delta_ssm_w_out': 'delta_w', 'delta_ln_mix_g': 'delta_w', 'delta_ln_mix_b': 'delta_w', 'delta_ffn_w_up': 'delta_w', 'delta_ffn_conv_w': 'delta_w', 'delta_ffn_conv_b': 'delta_w', 'delta_ffn_w_down': 'delta_w', 'delta_ln_ffn_g': 'delta_w', 'delta_ln_ffn_b': 'delta_w', 'delta_ple_w_proj': 'delta_w', 'delta_ple_w_gate': 'delta_w', 'delta_ple_b_gate': 'delta_w', 'new_m_attn_w_in': 'new_m', 'new_m_attn_b_f': 'new_m', 'new_m_attn_w_out': 'new_m', 'new_m_ssm_w_in': 'new_m', 'new_m_ssm_conv_w': 'new_m', 'new_m_ssm_conv_b': 'new_m', 'new_m_ssm_dt_bias': 'new_m', 'new_m_ssm_A_log': 'new_m', 'new_m_ssm_D': 'new_m', 'new_m_ssm_norm_w': 'new_m', 'new_m_ssm_w_out': 'new_m', 'new_m_ln_mix_g': 'new_m', 'new_m_ln_mix_b': 'new_m', 'new_m_ffn_w_up': 'new_m', 'new_m_ffn_conv_w': 'new_m', 'new_m_ffn_conv_b': 'new_m', 'new_m_ffn_w_down': 'new_m', 'new_m_ln_ffn_g': 'new_m', 'new_m_ln_ffn_b': 'new_m', 'new_m_ple_w_proj': 'new_m', 'new_m_ple_w_gate': 'new_m', 'new_m_ple_b_gate': 'new_m', 'new_v_attn_w_in': 'new_v', 'new_v_attn_b_f': 'new_v', 'new_v_attn_w_out': 'new_v', 'new_v_ssm_w_in': 'new_v', 'new_v_ssm_conv_w': 'new_v', 'new_v_ssm_conv_b': 'new_v', 'new_v_ssm_dt_bias': 'new_v', 'new_v_ssm_A_log': 'new_v', 'new_v_ssm_D': 'new_v', 'new_v_ssm_norm_w': 'new_v', 'new_v_ssm_w_out': 'new_v', 'new_v_ln_mix_g': 'new_v', 'new_v_ln_mix_b': 'new_v', 'new_v_ffn_w_up': 'new_v', 'new_v_ffn_conv_w': 'new_v', 'new_v_ffn_conv_b': 'new_v', 'new_v_ffn_w_down': 'new_v', 'new_v_ln_ffn_g': 'new_v', 'new_v_ln_ffn_b': 'new_v', 'new_v_ple_w_proj': 'new_v', 'new_v_ple_w_gate': 'new_v', 'new_v_ple_b_gate': 'new_v'}


def _forward(args):
    return _fwd_reference(*[args[k] for k in FWD_PARAMS])


def _output_shape():
    def fwd():
        inp = _fwd_setup_inputs(0)
        return _fwd_reference(*[inp[k] for k in FWD_PARAMS])
    out = _jax.eval_shape(fwd)
    return out.shape, out.dtype

N_MICROBATCH = 1
ADAM_LR = 0.001
ADAM_B1 = 0.9
ADAM_B2 = 0.999
ADAM_EPS = 1e-08
ADAM_WD = 0.01
ADAM_STEP = 10
PER_EXAMPLE_BATCH_AXIS = {'x': 0, 'p': 1, 'loss_target': 0}
SHARED_INPUTS = []
_WEIGHT_DTYPES = {'attn_w_in': _jnp.float32, 'attn_b_f': _jnp.float32, 'attn_w_out': _jnp.float32, 'ssm_w_in': _jnp.float32, 'ssm_conv_w': _jnp.float32, 'ssm_conv_b': _jnp.float32, 'ssm_dt_bias': _jnp.float32, 'ssm_A_log': _jnp.float32, 'ssm_D': _jnp.float32, 'ssm_norm_w': _jnp.float32, 'ssm_w_out': _jnp.float32, 'ln_mix_g': _jnp.float32, 'ln_mix_b': _jnp.float32, 'ffn_w_up': _jnp.float32, 'ffn_conv_w': _jnp.float32, 'ffn_conv_b': _jnp.float32, 'ffn_w_down': _jnp.float32, 'ln_ffn_g': _jnp.float32, 'ln_ffn_b': _jnp.float32, 'ple_w_proj': _jnp.float32, 'ple_w_gate': _jnp.float32, 'ple_b_gate': _jnp.float32}
MOMENT_SCALE = {'attn_w_in': 3.271524e-02, 'attn_b_f': 2.136605e-01, 'attn_w_out': 5.017612e-02, 'ssm_w_in': 7.536267e-02, 'ssm_conv_w': 8.146255e-02, 'ssm_conv_b': 2.131616e-01, 'ssm_dt_bias': 2.032794e-01, 'ssm_A_log': 2.815628e-01, 'ssm_D': 4.635732e-01, 'ssm_norm_w': 1.476441e-01, 'ssm_w_out': 3.809674e-01, 'ln_mix_g': 1.038341e+01, 'ln_mix_b': 3.505613e+00, 'ffn_w_up': 5.431367e-02, 'ffn_conv_w': 5.488417e-02, 'ffn_conv_b': 5.534582e-02, 'ffn_w_down': 1.798326e-01, 'ln_ffn_g': 9.271571e+01, 'ln_ffn_b': 7.812870e+00, 'ple_w_proj': 4.694275e-01, 'ple_w_gate': 1.047371e-01, 'ple_b_gate': 2.315506e+00}


def _to_microbatches(a, axis):
    t = _jnp.moveaxis(a, axis, 0)
    t = t.reshape((N_MICROBATCH, t.shape[0] // N_MICROBATCH) + t.shape[1:])
    return _jnp.moveaxis(t, 1, axis + 1)


def setup_inputs(seed: int = 0) -> dict:
    inp = _fwd_setup_inputs(seed)
    key = _jax.random.fold_in(_jax.random.key(seed), 7919)
    shape, _ = _output_shape()
    out = dict(inp)
    out["loss_target"] = _jax.random.normal(_jax.random.fold_in(key, 0), shape, _jnp.float32)
    for i, name in enumerate(TWIN_WEIGHTS):
        w = inp[name].astype(_jnp.float32)
        if MOMENT_SCALE is None:
            s = _jnp.sqrt(_jnp.mean(_jnp.square(w)) + 1e-30)
        else:
            s = MOMENT_SCALE[name]
        km, kv = _jax.random.split(_jax.random.fold_in(key, i + 1))
        out[name] = w
        out["m_" + name] = s * _jax.random.normal(km, w.shape, _jnp.float32)
        out["v_" + name] = (s * s) * _jax.random.uniform(kv, w.shape, _jnp.float32, 0.5, 1.5)
    if N_MICROBATCH > 1:
        for name, axis in PER_EXAMPLE_BATCH_AXIS.items():
            out[name] = _to_microbatches(out[name], axis)
    return {'x': out['x'], 'p': out['p'], 'attn_w_in': out['attn_w_in'], 'attn_b_f': out['attn_b_f'], 'attn_w_out': out['attn_w_out'], 'ssm_w_in': out['ssm_w_in'], 'ssm_conv_w': out['ssm_conv_w'], 'ssm_conv_b': out['ssm_conv_b'], 'ssm_dt_bias': out['ssm_dt_bias'], 'ssm_A_log': out['ssm_A_log'], 'ssm_D': out['ssm_D'], 'ssm_norm_w': out['ssm_norm_w'], 'ssm_w_out': out['ssm_w_out'], 'ln_mix_g': out['ln_mix_g'], 'ln_mix_b': out['ln_mix_b'], 'ffn_w_up': out['ffn_w_up'], 'ffn_conv_w': out['ffn_conv_w'], 'ffn_conv_b': out['ffn_conv_b'], 'ffn_w_down': out['ffn_w_down'], 'ln_ffn_g': out['ln_ffn_g'], 'ln_ffn_b': out['ln_ffn_b'], 'ple_w_proj': out['ple_w_proj'], 'ple_w_gate': out['ple_w_gate'], 'ple_b_gate': out['ple_b_gate'], 'loss_target': out['loss_target'], 'm_attn_w_in': out['m_attn_w_in'], 'm_attn_b_f': out['m_attn_b_f'], 'm_attn_w_out': out['m_attn_w_out'], 'm_ssm_w_in': out['m_ssm_w_in'], 'm_ssm_conv_w': out['m_ssm_conv_w'], 'm_ssm_conv_b': out['m_ssm_conv_b'], 'm_ssm_dt_bias': out['m_ssm_dt_bias'], 'm_ssm_A_log': out['m_ssm_A_log'], 'm_ssm_D': out['m_ssm_D'], 'm_ssm_norm_w': out['m_ssm_norm_w'], 'm_ssm_w_out': out['m_ssm_w_out'], 'm_ln_mix_g': out['m_ln_mix_g'], 'm_ln_mix_b': out['m_ln_mix_b'], 'm_ffn_w_up': out['m_ffn_w_up'], 'm_ffn_conv_w': out['m_ffn_conv_w'], 'm_ffn_conv_b': out['m_ffn_conv_b'], 'm_ffn_w_down': out['m_ffn_w_down'], 'm_ln_ffn_g': out['m_ln_ffn_g'], 'm_ln_ffn_b': out['m_ln_ffn_b'], 'm_ple_w_proj': out['m_ple_w_proj'], 'm_ple_w_gate': out['m_ple_w_gate'], 'm_ple_b_gate': out['m_ple_b_gate'], 'v_attn_w_in': out['v_attn_w_in'], 'v_attn_b_f': out['v_attn_b_f'], 'v_attn_w_out': out['v_attn_w_out'], 'v_ssm_w_in': out['v_ssm_w_in'], 'v_ssm_conv_w': out['v_ssm_conv_w'], 'v_ssm_conv_b': out['v_ssm_conv_b'], 'v_ssm_dt_bias': out['v_ssm_dt_bias'], 'v_ssm_A_log': out['v_ssm_A_log'], 'v_ssm_D': out['v_ssm_D'], 'v_ssm_norm_w': out['v_ssm_norm_w'], 'v_ssm_w_out': out['v_ssm_w_out'], 'v_ln_mix_g': out['v_ln_mix_g'], 'v_ln_mix_b': out['v_ln_mix_b'], 'v_ffn_w_up': out['v_ffn_w_up'], 'v_ffn_conv_w': out['v_ffn_conv_w'], 'v_ffn_conv_b': out['v_ffn_conv_b'], 'v_ffn_w_down': out['v_ffn_w_down'], 'v_ln_ffn_g': out['v_ln_ffn_g'], 'v_ln_ffn_b': out['v_ln_ffn_b'], 'v_ple_w_proj': out['v_ple_w_proj'], 'v_ple_w_gate': out['v_ple_w_gate'], 'v_ple_b_gate': out['v_ple_b_gate']}


def _loss(weights, diff, rest, loss_target):
    with _jax.named_scope("forward"):
        args = {**rest, TWIN_DIFF_INPUT: diff, **{k: w.astype(_WEIGHT_DTYPES[k]) for k, w in weights.items()}}
        y = _forward(args)
    with _jax.named_scope("loss_head"):
        err = _jnp.square(y.astype(_jnp.float32) - loss_target)
        return 0.5 * _jnp.sum(_jnp.mean(err, axis=-1)) if err.ndim else 0.5 * err


def _adamw(w, g, m, v):
    m = ADAM_B1 * m + (1.0 - ADAM_B1) * g
    v = ADAM_B2 * v + (1.0 - ADAM_B2) * _jnp.square(g)
    m_hat = m / (1.0 - ADAM_B1 ** ADAM_STEP)
    v_hat = v / (1.0 - ADAM_B2 ** ADAM_STEP)
    delta = -ADAM_LR * (m_hat / (_jnp.sqrt(v_hat) + ADAM_EPS) + ADAM_WD * w)
    return delta, m, v


def reference(x, p, attn_w_in, attn_b_f, attn_w_out, ssm_w_in, ssm_conv_w, ssm_conv_b, ssm_dt_bias, ssm_A_log, ssm_D, ssm_norm_w, ssm_w_out, ln_mix_g, ln_mix_b, ffn_w_up, ffn_conv_w, ffn_conv_b, ffn_w_down, ln_ffn_g, ln_ffn_b, ple_w_proj, ple_w_gate, ple_b_gate, loss_target, m_attn_w_in, m_attn_b_f, m_attn_w_out, m_ssm_w_in, m_ssm_conv_w, m_ssm_conv_b, m_ssm_dt_bias, m_ssm_A_log, m_ssm_D, m_ssm_norm_w, m_ssm_w_out, m_ln_mix_g, m_ln_mix_b, m_ffn_w_up, m_ffn_conv_w, m_ffn_conv_b, m_ffn_w_down, m_ln_ffn_g, m_ln_ffn_b, m_ple_w_proj, m_ple_w_gate, m_ple_b_gate, v_attn_w_in, v_attn_b_f, v_attn_w_out, v_ssm_w_in, v_ssm_conv_w, v_ssm_conv_b, v_ssm_dt_bias, v_ssm_A_log, v_ssm_D, v_ssm_norm_w, v_ssm_w_out, v_ln_mix_g, v_ln_mix_b, v_ffn_w_up, v_ffn_conv_w, v_ffn_conv_b, v_ffn_w_down, v_ln_ffn_g, v_ln_ffn_b, v_ple_w_proj, v_ple_w_gate, v_ple_b_gate):
    given = dict(x=x, p=p, attn_w_in=attn_w_in, attn_b_f=attn_b_f, attn_w_out=attn_w_out, ssm_w_in=ssm_w_in, ssm_conv_w=ssm_conv_w, ssm_conv_b=ssm_conv_b, ssm_dt_bias=ssm_dt_bias, ssm_A_log=ssm_A_log, ssm_D=ssm_D, ssm_norm_w=ssm_norm_w, ssm_w_out=ssm_w_out, ln_mix_g=ln_mix_g, ln_mix_b=ln_mix_b, ffn_w_up=ffn_w_up, ffn_conv_w=ffn_conv_w, ffn_conv_b=ffn_conv_b, ffn_w_down=ffn_w_down, ln_ffn_g=ln_ffn_g, ln_ffn_b=ln_ffn_b, ple_w_proj=ple_w_proj, ple_w_gate=ple_w_gate, ple_b_gate=ple_b_gate, loss_target=loss_target, m_attn_w_in=m_attn_w_in, m_attn_b_f=m_attn_b_f, m_attn_w_out=m_attn_w_out, m_ssm_w_in=m_ssm_w_in, m_ssm_conv_w=m_ssm_conv_w, m_ssm_conv_b=m_ssm_conv_b, m_ssm_dt_bias=m_ssm_dt_bias, m_ssm_A_log=m_ssm_A_log, m_ssm_D=m_ssm_D, m_ssm_norm_w=m_ssm_norm_w, m_ssm_w_out=m_ssm_w_out, m_ln_mix_g=m_ln_mix_g, m_ln_mix_b=m_ln_mix_b, m_ffn_w_up=m_ffn_w_up, m_ffn_conv_w=m_ffn_conv_w, m_ffn_conv_b=m_ffn_conv_b, m_ffn_w_down=m_ffn_w_down, m_ln_ffn_g=m_ln_ffn_g, m_ln_ffn_b=m_ln_ffn_b, m_ple_w_proj=m_ple_w_proj, m_ple_w_gate=m_ple_w_gate, m_ple_b_gate=m_ple_b_gate, v_attn_w_in=v_attn_w_in, v_attn_b_f=v_attn_b_f, v_attn_w_out=v_attn_w_out, v_ssm_w_in=v_ssm_w_in, v_ssm_conv_w=v_ssm_conv_w, v_ssm_conv_b=v_ssm_conv_b, v_ssm_dt_bias=v_ssm_dt_bias, v_ssm_A_log=v_ssm_A_log, v_ssm_D=v_ssm_D, v_ssm_norm_w=v_ssm_norm_w, v_ssm_w_out=v_ssm_w_out, v_ln_mix_g=v_ln_mix_g, v_ln_mix_b=v_ln_mix_b, v_ffn_w_up=v_ffn_w_up, v_ffn_conv_w=v_ffn_conv_w, v_ffn_conv_b=v_ffn_conv_b, v_ffn_w_down=v_ffn_w_down, v_ln_ffn_g=v_ln_ffn_g, v_ln_ffn_b=v_ln_ffn_b, v_ple_w_proj=v_ple_w_proj, v_ple_w_gate=v_ple_w_gate, v_ple_b_gate=v_ple_b_gate)
    weights = {n: given[n] for n in TWIN_WEIGHTS}
    shared = {n: given[n] for n in SHARED_INPUTS}
    per_example = {n: given[n] for n in ['x', 'p']}
    grad_fn = _jax.value_and_grad(_loss, argnums=(0, 1))

    def one_microbatch(ex, loss_target):
        ex = dict(ex)
        diff = ex.pop(TWIN_DIFF_INPUT)
        return grad_fn(weights, diff, {**shared, **ex}, loss_target)

    if N_MICROBATCH == 1:
        loss, (grad_w, grad_x) = one_microbatch(per_example, given["loss_target"])
    else:
        def body(carry, xs):
            loss_sum, grad_sum = carry
            l_k, (gw_k, gx_k) = one_microbatch(xs[0], xs[1])
            with _jax.named_scope("update"):
                return (loss_sum + l_k, _jax.tree.map(_jnp.add, grad_sum, gw_k)), gx_k

        init = (_jnp.zeros((), _jnp.float32), _jax.tree.map(_jnp.zeros_like, weights))
        (loss, grad_w), grad_x = _jax.lax.scan(body, init, (per_example, given["loss_target"]))
    with _jax.named_scope("update"):
        delta_w, new_m, new_v = {}, {}, {}
        for n in TWIN_WEIGHTS:
            delta_w[n], new_m[n], new_v[n] = _adamw(weights[n], grad_w[n], given["m_" + n], given["v_" + n])
    return (loss, grad_x, *[grad_w[n] for n in TWIN_WEIGHTS], *[delta_w[n] for n in TWIN_WEIGHTS],
            *[new_m[n] for n in TWIN_WEIGHTS], *[new_v[n] for n in TWIN_WEIGHTS])
```

```python
import functools
import math

import jax
import jax.numpy as jnp
from jax import lax
from jax.experimental import pallas as pl
from jax.experimental.pallas import tpu as pltpu

F32 = jnp.float32
BF16 = jnp.bfloat16

LANES = 128
SUBLANES = 8
VMEM_LIMIT_BYTES = 56 * 1024 * 1024

N_DEV = 8
HEAD_DIM = 64
SSM_GROUPS = 8
SSM_STATE = 128
SSM_CHUNK = 128
SSM_CONV = 4
FFN_CONV = 3
LN_EPS = 1e-5
RMS_EPS = 1e-5
ADAM_LR, ADAM_B1, ADAM_B2, ADAM_EPS, ADAM_WD, ADAM_STEP = 0.001, 0.9, 0.999, 1e-08, 0.01, 10
NEG_INF = float("-inf")
HIGHEST = lax.Precision.HIGHEST
NT_DIMS = (((1,), (1,)), ((), ()))
TN_DIMS = (((0,), (0,)), ((), ()))


def _cparams(*sem):
    return pltpu.CompilerParams(dimension_semantics=sem, vmem_limit_bytes=VMEM_LIMIT_BYTES)


def _pick(n, candidates):
    for c in candidates:
        if n % c == 0:
            return c
    return n


def _dot(a, b):
    return jnp.dot(a.astype(BF16), b.astype(BF16), preferred_element_type=F32)


def _dot_nt(a, b):
    return lax.dot_general(a.astype(BF16), b.astype(BF16), NT_DIMS, preferred_element_type=F32)


def _dot_tn(a, b):
    return lax.dot_general(a.astype(BF16), b.astype(BF16), TN_DIMS, preferred_element_type=F32)


def _sigmoid(x):
    return 1.0 / (1.0 + jnp.exp(-x))


def _log1p_small(u):
    return jnp.where(u < 1e-3, u * (1.0 - u * (0.5 - u * (1.0 / 3.0))), jnp.log(1.0 + u))


def _softplus(x):
    return jnp.maximum(x, 0.0) + _log1p_small(jnp.exp(-jnp.abs(x)))


def mm_nn(a, b, out_dtype, name, add=None, add_scale=1.0):
    M, K = a.shape
    _, N = b.shape
    tm = _pick(M, (1024, 512, 256, 128))
    tn = N if N <= 1024 else _pick(N, (1408, 1024, 896, 768, 640, 512, 384, 256, 128))
    tk = K if K <= 2048 else _pick(K, (1408, 1024, 896, 768, 640, 512, 384, 256, 128))
    nk = K // tk

    def body(*refs):
        if add is None:
            a_ref, b_ref, o_ref, acc_ref = refs
        else:
            a_ref, b_ref, c_ref, o_ref, acc_ref = refs
        k = pl.program_id(2)
        part = _dot(a_ref[...], b_ref[...])

        @pl.when(k == 0)
        def _():
            acc_ref[...] = part

        @pl.when(k > 0)
        def _():
            acc_ref[...] += part

        @pl.when(k == nk - 1)
        def _():
            r = acc_ref[...]
            if add is not None:
                r = r + add_scale * c_ref[...].astype(F32)
            o_ref[...] = r.astype(o_ref.dtype)

    in_specs = [pl.BlockSpec((tm, tk), lambda i, j, k: (i, k)), pl.BlockSpec((tk, tn), lambda i, j, k: (k, j))]
    args = [a, b]
    if add is not None:
        in_specs.append(pl.BlockSpec((tm, tn), lambda i, j, k: (i, j)))
        args.append(add)
    return pl.pallas_call(
        body,
        name=name,
        grid=(M // tm, N // tn, nk),
        in_specs=in_specs,
        out_specs=pl.BlockSpec((tm, tn), lambda i, j, k: (i, j)),
        out_shape=jax.ShapeDtypeStruct((M, N), out_dtype),
        scratch_shapes=[pltpu.VMEM((tm, tn), F32)],
        compiler_params=_cparams("parallel", "parallel", "arbitrary"),
    )(*args)


def mm_tn(a, b, name):
    M, K = a.shape
    _, N = b.shape
    tm = _pick(M, (512, 256, 128))
    tk = K if K <= 1024 else _pick(K, (1408, 1024, 896, 768, 640, 512, 384, 256, 128))
    tn = N if N <= 1408 else _pick(N, (1408, 1024, 896, 768, 640, 512, 384, 256, 128))
    nm = M // tm

    def body(a_ref, b_ref, o_ref):
        m = pl.program_id(2)
        part = _dot_tn(a_ref[...], b_ref[...])

        @pl.when(m == 0)
        def _():
            o_ref[...] = part

        @pl.when(m > 0)
        def _():
            o_ref[...] += part

    return pl.pallas_call(
        body,
        name=name,
        grid=(K // tk, N // tn, nm),
        in_specs=[pl.BlockSpec((tm, tk), lambda i, j, m: (m, i)), pl.BlockSpec((tm, tn), lambda i, j, m: (m, j))],
        out_specs=pl.BlockSpec((tk, tn), lambda i, j, m: (i, j)),
        out_shape=jax.ShapeDtypeStruct((K, N), F32),
        compiler_params=_cparams("parallel", "parallel", "arbitrary"),
    )(a, b)


def ln_fwd(x, mix, g, b, alpha, name):
    S, D = x.shape
    tm = _pick(S, (512, 256, 128))

    def body(x_ref, mix_ref, g_ref, b_ref, y_ref, xhat_ref, rstd_ref):
        r = alpha * x_ref[...] + mix_ref[...]
        mu = jnp.mean(r, axis=-1, keepdims=True)
        xc = r - mu
        var = jnp.mean(xc * xc, axis=-1, keepdims=True)
        rstd = lax.rsqrt(var + LN_EPS)
        xhat = xc * rstd
        y_ref[...] = xhat * g_ref[...] + b_ref[...]
        xhat_ref[...] = xhat
        rstd_ref[...] = rstd

    row = pl.BlockSpec((tm, D), lambda i: (i, 0))
    vec = pl.BlockSpec((1, D), lambda i: (0, 0))
    return pl.pallas_call(
        body,
        name=name,
        grid=(S // tm,),
        in_specs=[row, row, vec, vec],
        out_specs=[row, row, pl.BlockSpec((tm, 1), lambda i: (i, 0))],
        out_shape=[jax.ShapeDtypeStruct((S, D), F32), jax.ShapeDtypeStruct((S, D), F32), jax.ShapeDtypeStruct((S, 1), F32)],
        compiler_params=_cparams("parallel"),
    )(x, mix, g, b)


def ln_bwd(dy, xhat, rstd, g, name):
    S, D = dy.shape
    tm = _pick(S, (512, 256, 128))

    def body(dy_ref, xhat_ref, rstd_ref, g_ref, dr_ref, dg_ref, db_ref):
        i = pl.program_id(0)
        dyv = dy_ref[...]
        xh = xhat_ref[...]
        dxh = dyv * g_ref[...]
        m1 = jnp.mean(dxh, axis=-1, keepdims=True)
        m2 = jnp.mean(dxh * xh, axis=-1, keepdims=True)
        dr_ref[...] = rstd_ref[...] * (dxh - m1 - xh * m2)
        dg_part = jnp.sum(dyv * xh, axis=0, keepdims=True)
        db_part = jnp.sum(dyv, axis=0, keepdims=True)

        @pl.when(i == 0)
        def _():
            dg_ref[...] = dg_part
            db_ref[...] = db_part

        @pl.when(i > 0)
        def _():
            dg_ref[...] += dg_part
            db_ref[...] += db_part

    row = pl.BlockSpec((tm, D), lambda i: (i, 0))
    vec = pl.BlockSpec((1, D), lambda i: (0, 0))
    return pl.pallas_call(
        body,
        name=name,
        grid=(S // tm,),
        in_specs=[row, row, pl.BlockSpec((tm, 1), lambda i: (i, 0)), vec],
        out_specs=[row, vec, vec],
        out_shape=[jax.ShapeDtypeStruct((S, D), F32), jax.ShapeDtypeStruct((1, D), F32), jax.ShapeDtypeStruct((1, D), F32)],
        compiler_params=_cparams("arbitrary"),
    )(dy, xhat, rstd, g)


HALO = SUBLANES


def _prev_halo_spec(tm, tc, col0):
    return pl.BlockSpec((HALO, tc), lambda i, j: (jnp.maximum(i * (tm // HALO) - 1, 0), j + col0))


def _next_halo_spec(tm, tc, col0, n_row_tiles):
    last = n_row_tiles * (tm // HALO) - 1
    return pl.BlockSpec((HALO, tc), lambda i, j: (jnp.minimum((i + 1) * (tm // HALO), last), j + col0))


def _causal_conv(ext_ref, w, n_taps, tm, row0=HALO):
    acc = None
    for k in range(n_taps):
        term = ext_ref[pl.ds(row0 - (n_taps - 1) + k, tm), :] * w[k:k + 1, :]
        acc = term if acc is None else acc + term
    return acc


def _anticausal_conv(ext_ref, w, n_taps, tm):
    acc = None
    for k in range(n_taps):
        term = ext_ref[pl.ds(n_taps - 1 - k, tm), :] * w[k:k + 1, :]
        acc = term if acc is None else acc + term
    return acc


INV_SQRT2 = 1.0 / math.sqrt(2.0)
INV_SQRT_2PI = 1.0 / math.sqrt(2.0 * math.pi)


def _gelu(g):
    return 0.5 * g * (1.0 + lax.erf(g * INV_SQRT2))


def _gelu_grad(g):
    return 0.5 * (1.0 + lax.erf(g * INV_SQRT2)) + g * jnp.exp(-0.5 * g * g) * INV_SQRT_2PI


def _silu(x):
    return x * _sigmoid(x)


def _silu_grad(x):
    s = _sigmoid(x)
    return s * (1.0 + x * (1.0 - s))


def _conv_tiles(S, C, cols):
    tm = _pick(S, (256, 128))
    for tc in (1408, 1024, 512, 256, 128):
        if C % tc == 0 and all(c % tc == 0 for c in cols):
            return tm, tc
    raise ValueError("no column tile for the conv kernels")


def conv_act_fwd(src, in_col, C, conv_w, conv_b, out_dtype, name, gate_col=None):
    S = src.shape[0]
    K = conv_w.shape[0]
    gated = gate_col is not None
    tm, tc = _conv_tiles(S, C, [in_col] + ([gate_col] if gated else []))
    c_in = in_col // tc
    c_gate = gate_col // tc if gated else 0

    def body(*refs):
        if gated:
            x_ref, xp_ref, w_ref, b_ref, u_ref, o_ref, ext_ref = refs
        else:
            x_ref, xp_ref, w_ref, b_ref, o_ref, ext_ref = refs
        i = pl.program_id(0)
        ext_ref[0:HALO] = jnp.where(i > 0, xp_ref[...], 0.0)
        ext_ref[HALO:HALO + tm] = x_ref[...]
        pre = _causal_conv(ext_ref, w_ref[...], K, tm) + b_ref[...]
        out = _gelu(pre) * u_ref[...] if gated else _silu(pre)
        o_ref[...] = out.astype(o_ref.dtype)

    in_specs = [
        pl.BlockSpec((tm, tc), lambda i, j: (i, j + c_in)),
        _prev_halo_spec(tm, tc, c_in),
        pl.BlockSpec((K, tc), lambda i, j: (0, j)),
        pl.BlockSpec((1, tc), lambda i, j: (0, j)),
    ]
    args = [src, src, conv_w, conv_b]
    if gated:
        in_specs.append(pl.BlockSpec((tm, tc), lambda i, j: (i, j + c_gate)))
        args.append(src)
    return pl.pallas_call(
        body,
        name=name,
        grid=(S // tm, C // tc),
        in_specs=in_specs,
        out_specs=pl.BlockSpec((tm, tc), lambda i, j: (i, j)),
        out_shape=jax.ShapeDtypeStruct((S, C), out_dtype),
        scratch_shapes=[pltpu.VMEM((tm + HALO, tc), F32)],
        compiler_params=_cparams("parallel", "parallel"),
    )(*args)


def conv_act_bwd(d_out, src, in_col, C, conv_w, conv_b, name, gate_col=None):
    S = src.shape[0]
    K = conv_w.shape[0]
    gated = gate_col is not None
    tm, tc = _conv_tiles(S, C, [in_col] + ([gate_col] if gated else []))
    c_in = in_col // tc
    c_gate = gate_col // tc if gated else 0
    ni = S // tm
    te = tm + HALO

    def body(*refs):
        if gated:
            (d_ref, dn_ref, x_ref, xp_ref, xn_ref, w_ref, b_ref, u_ref, un_ref,
             dx_ref, dw_ref, db_ref, du_ref, xext_ref, dext_ref) = refs
        else:
            (d_ref, dn_ref, x_ref, xp_ref, xn_ref, w_ref, b_ref,
             dx_ref, dw_ref, db_ref, xext_ref, dext_ref) = refs
        i = pl.program_id(1)
        w = w_ref[...]
        xext_ref[0:HALO] = jnp.where(i > 0, xp_ref[...], 0.0)
        xext_ref[HALO:HALO + tm] = x_ref[...]
        xext_ref[HALO + tm:HALO + te] = xn_ref[...]
        pre = _causal_conv(xext_ref, w, K, te) + b_ref[...]
        d_cur = d_ref[...].astype(F32)
        d_next = dn_ref[...].astype(F32)
        if gated:
            du_ref[...] = (d_cur * _gelu(pre[0:tm])).astype(du_ref.dtype)
            dpre_cur = d_cur * u_ref[...] * _gelu_grad(pre[0:tm])
            dpre_next = d_next * un_ref[...] * _gelu_grad(pre[tm:te])
        else:
            dpre_cur = d_cur * _silu_grad(pre[0:tm])
            dpre_next = d_next * _silu_grad(pre[tm:te])
        dext_ref[0:tm] = dpre_cur
        dext_ref[tm:te] = jnp.where(i < ni - 1, dpre_next, 0.0)
        dx_ref[...] = _anticausal_conv(dext_ref, w, K, tm).astype(dx_ref.dtype)
        dw_rows = [jnp.sum(dpre_cur * xext_ref[pl.ds(HALO - (K - 1) + k, tm), :], axis=0, keepdims=True) for k in range(K)]
        dw_part = jnp.concatenate(dw_rows + [jnp.zeros((SUBLANES - K, tc), F32)], axis=0)
        db_part = jnp.sum(dpre_cur, axis=0, keepdims=True)

        @pl.when(i == 0)
        def _():
            dw_ref[...] = dw_part
            db_ref[...] = db_part

        @pl.when(i > 0)
        def _():
            dw_ref[...] += dw_part
            db_ref[...] += db_part

    last = ni * (tm // HALO) - 1
    cur = lambda c0: pl.BlockSpec((tm, tc), lambda j, i: (i, j + c0))
    prev = lambda c0: pl.BlockSpec((HALO, tc), lambda j, i: (jnp.maximum(i * (tm // HALO) - 1, 0), j + c0))
    nxt = lambda c0: pl.BlockSpec((HALO, tc), lambda j, i: (jnp.minimum((i + 1) * (tm // HALO), last), j + c0))
    vec = lambda rows: pl.BlockSpec((rows, tc), lambda j, i: (0, j))
    in_specs = [cur(0), nxt(0), cur(c_in), prev(c_in), nxt(c_in), vec(K), vec(1)]
    args = [d_out, d_out, src, src, src, conv_w, conv_b]
    out_specs = [cur(0), vec(SUBLANES), vec(1)]
    out_shape = [jax.ShapeDtypeStruct((S, C), BF16), jax.ShapeDtypeStruct((SUBLANES, C), F32), jax.ShapeDtypeStruct((1, C), F32)]
    if gated:
        in_specs += [cur(c_gate), nxt(c_gate)]
        args += [src, src]
        out_specs.append(cur(0))
        out_shape.append(jax.ShapeDtypeStruct((S, C), BF16))
    outs = pl.pallas_call(
        body,
        name=name,
        grid=(C // tc, ni),
        in_specs=in_specs,
        out_specs=out_specs,
        out_shape=out_shape,
        scratch_shapes=[pltpu.VMEM((tm + 2 * HALO, tc), F32), pltpu.VMEM((te, tc), F32)],
        compiler_params=_cparams("parallel", "arbitrary"),
    )(*args)
    return outs[0], (outs[3] if gated else None), outs[1][:K], outs[2]


def ple_fwd(x2, zg, pp, bg, name):
    S, D = x2.shape
    tm = _pick(S, (512, 256, 128))

    def body(x_ref, z_ref, p_ref, b_ref, o_ref):
        o_ref[...] = x_ref[...] + _sigmoid(z_ref[...] + b_ref[...]) * p_ref[...]

    row = pl.BlockSpec((tm, D), lambda i: (i, 0))
    return pl.pallas_call(
        body, name=name, grid=(S // tm,), in_specs=[row, row, row, pl.BlockSpec((1, D), lambda i: (0, 0))], out_specs=row,
        out_shape=jax.ShapeDtypeStruct((S, D), F32), compiler_params=_cparams("parallel"),
    )(x2, zg, pp, bg)


def ple_bwd(dx3, zg, pp, bg, name):
    S, D = dx3.shape
    tm = _pick(S, (512, 256, 128))

    def body(d_ref, z_ref, p_ref, b_ref, dz_ref, dp_ref, db_ref):
        i = pl.program_id(0)
        d = d_ref[...]
        gate = _sigmoid(z_ref[...] + b_ref[...])
        dz = d * p_ref[...] * gate * (1.0 - gate)
        dz_ref[...] = dz.astype(dz_ref.dtype)
        dp_ref[...] = (d * gate).astype(dp_ref.dtype)
        part = jnp.sum(dz, axis=0, keepdims=True)

        @pl.when(i == 0)
        def _():
            db_ref[...] = part

        @pl.when(i > 0)
        def _():
            db_ref[...] += part

    row = pl.BlockSpec((tm, D), lambda i: (i, 0))
    vec = pl.BlockSpec((1, D), lambda i: (0, 0))
    return pl.pallas_call(
        body, name=name, grid=(S // tm,), in_specs=[row, row, row, vec], out_specs=[row, row, vec],
        out_shape=[jax.ShapeDtypeStruct((S, D), BF16), jax.ShapeDtypeStruct((S, D), BF16), jax.ShapeDtypeStruct((1, D), F32)],
        compiler_params=_cparams("arbitrary"),
    )(dx3, zg, pp, bg)


def loss_head(y, target, name):
    S, D = y.shape
    tm = _pick(S, (512, 256, 128))

    def body(y_ref, t_ref, loss_ref, dy_ref, acc_ref):
        i = pl.program_id(0)
        err = y_ref[...] - t_ref[...]
        dy_ref[...] = err * (1.0 / D)
        part = jnp.sum(err * err, axis=0, keepdims=True)

        @pl.when(i == 0)
        def _():
            acc_ref[...] = part

        @pl.when(i > 0)
        def _():
            acc_ref[...] += part

        @pl.when(i == pl.num_programs(0) - 1)
        def _():
            loss_ref[...] = jnp.zeros((1, LANES), F32) + (0.5 / D) * jnp.sum(acc_ref[...])

    row = pl.BlockSpec((tm, D), lambda i: (i, 0))
    return pl.pallas_call(
        body, name=name, grid=(S // tm,), in_specs=[row, row],
        out_specs=[pl.BlockSpec((1, LANES), lambda i: (0, 0)), row],
        out_shape=[jax.ShapeDtypeStruct((1, LANES), F32), jax.ShapeDtypeStruct((S, D), F32)],
        scratch_shapes=[pltpu.VMEM((1, D), F32)],
        compiler_params=_cparams("arbitrary"),
    )(y, target)


ATTN_TILE = 512
ATTN_SCALE = 1.0 / math.sqrt(HEAD_DIM)


def _attn_tile(S):
    return _pick(S, (ATTN_TILE, 256, 128))


def fox_gate_fwd(zt, bf, name):
    H, S = zt.shape
    tl = _pick(S, (512, 256, 128))

    def body(z_ref, b_ref, c_ref, carry_ref):
        i = pl.program_id(0)

        @pl.when(i == 0)
        def _():
            carry_ref[...] = jnp.zeros_like(carry_ref)

        z = z_ref[...] + b_ref[...]
        logf = jnp.minimum(z, 0.0) - _log1p_small(jnp.exp(-jnp.abs(z)))
        r = lax.broadcasted_iota(jnp.int32, (tl, tl), 0)
        c = lax.broadcasted_iota(jnp.int32, (tl, tl), 1)
        upper = (r <= c).astype(F32)
        cum = jnp.dot(logf, upper, precision=HIGHEST, preferred_element_type=F32) + carry_ref[...]
        c_ref[...] = cum
        carry_ref[...] = cum[:, tl - 1:tl]

    return pl.pallas_call(
        body, name=name, grid=(S // tl,),
        in_specs=[pl.BlockSpec((H, tl), lambda i: (0, i)), pl.BlockSpec((H, 1), lambda i: (0, 0))],
        out_specs=pl.BlockSpec((H, tl), lambda i: (0, i)),
        out_shape=jax.ShapeDtypeStruct((H, S), F32),
        scratch_shapes=[pltpu.VMEM((H, 1), F32)],
        compiler_params=_cparams("arbitrary"),
    )(zt, bf)


def fox_gate_bwd(dc_q, dc_k, zt, bf, name):
    H, S = zt.shape
    tl = _pick(S, (512, 256, 128))
    nt = S // tl

    def body(dcq_ref, dck_ref, z_ref, b_ref, dz_ref, db_ref, carry_ref):
        i = pl.program_id(0)

        @pl.when(i == 0)
        def _():
            carry_ref[...] = jnp.zeros_like(carry_ref)
            db_ref[...] = jnp.zeros_like(db_ref)

        r = lax.broadcasted_iota(jnp.int32, (tl, tl), 0)
        c = lax.broadcasted_iota(jnp.int32, (tl, tl), 1)
        lower = (r >= c).astype(F32)
        dc = dcq_ref[...] + dck_ref[...]
        suffix = jnp.dot(dc, lower, precision=HIGHEST, preferred_element_type=F32) + carry_ref[...]
        carry_ref[...] = suffix[:, 0:1]
        dz = suffix * _sigmoid(-(z_ref[...] + b_ref[...]))
        dz_ref[...] = dz
        db_ref[...] += jnp.sum(dz, axis=1, keepdims=True)

    rev = pl.BlockSpec((H, tl), lambda i: (0, nt - 1 - i))
    return pl.pallas_call(
        body, name=name, grid=(nt,),
        in_specs=[rev, rev, rev, pl.BlockSpec((H, 1), lambda i: (0, 0))],
        out_specs=[rev, pl.BlockSpec((H, 1), lambda i: (0, 0))],
        out_shape=[jax.ShapeDtypeStruct((H, S), F32), jax.ShapeDtypeStruct((H, 1), F32)],
        scratch_shapes=[pltpu.VMEM((H, 1), F32)],
        compiler_params=_cparams("arbitrary"),
    )(dc_q, dc_k, zt, bf)


def _causal_mask(t, transposed=False):
    r = lax.broadcasted_iota(jnp.int32, (t, t), 0)
    c = lax.broadcasted_iota(jnp.int32, (t, t), 1)
    return (c >= r) if transposed else (r >= c)


def flash_fwd(q, k, v, c_col, c_row, name):
    H, S, Dh = q.shape
    T = _attn_tile(S)
    NT = S // T

    def body(q_ref, k_ref, v_ref, cq_ref, ck_ref, o_ref, lse_ref, m_ref, l_ref, acc_ref):
        i = pl.program_id(1)
        qs = q_ref[0] * ATTN_SCALE
        cq = cq_ref[0]
        m_ref[...] = jnp.full_like(m_ref, NEG_INF)
        l_ref[...] = jnp.zeros_like(l_ref)
        acc_ref[...] = jnp.zeros_like(acc_ref)

        def step(j, masked):
            off = pl.multiple_of(j * T, T)
            s = _dot_nt(qs, k_ref[0, pl.ds(off, T), :]) + (cq - ck_ref[0, j])
            if masked:
                s = jnp.where(_causal_mask(T), s, NEG_INF)
            m_prev = m_ref[...]
            m_new = jnp.maximum(m_prev, jnp.max(s, axis=1, keepdims=True))
            p = jnp.exp(s - m_new)
            corr = jnp.exp(m_prev - m_new)
            l_ref[...] = corr * l_ref[...] + jnp.sum(p, axis=1, keepdims=True)
            acc_ref[...] = corr * acc_ref[...] + _dot(p, v_ref[0, pl.ds(off, T), :])
            m_ref[...] = m_new

        def loop_body(j, carry):
            step(j, False)
            return carry

        lax.fori_loop(0, i, loop_body, 0)
        step(i, True)
        o_ref[0] = (acc_ref[...] / l_ref[...]).astype(o_ref.dtype)
        lse_ref[0] = m_ref[...] + jnp.log(l_ref[...])

    tile = pl.BlockSpec((1, T, Dh), lambda h, i: (h, i, 0))
    whole = pl.BlockSpec((1, S, Dh), lambda h, i: (h, 0, 0))
    col = pl.BlockSpec((1, T, 1), lambda h, i: (h, i, 0))
    rows = pl.BlockSpec((1, NT, 1, T), lambda h, i: (h, 0, 0, 0))
    return pl.pallas_call(
        body, name=name, grid=(H, NT),
        in_specs=[tile, whole, whole, col, rows],
        out_specs=[tile, col],
        out_shape=[jax.ShapeDtypeStruct((H, S, Dh), BF16), jax.ShapeDtypeStruct((H, S, 1), F32)],
        scratch_shapes=[pltpu.VMEM((T, 1), F32), pltpu.VMEM((T, 1), F32), pltpu.VMEM((T, Dh), F32)],
        compiler_params=_cparams("parallel", "parallel"),
    )(q, k, v, c_col, c_row)


def flash_bwd_dq(q, k, v, o, do, lse, c_col, c_row, name):
    H, S, Dh = q.shape
    T = _attn_tile(S)
    NT = S // T

    def body(q_ref, k_ref, v_ref, o_ref, do_ref, lse_ref, cq_ref, ck_ref, dq_ref, delta_ref, dcq_ref, acc_ref, rs_ref):
        i = pl.program_id(1)
        qs = q_ref[0] * ATTN_SCALE
        do = do_ref[0]
        delta = jnp.sum(do.astype(F32) * o_ref[0].astype(F32), axis=1, keepdims=True)
        delta_ref[0] = delta
        bias = cq_ref[0] - lse_ref[0]
        acc_ref[...] = jnp.zeros_like(acc_ref)
        rs_ref[...] = jnp.zeros_like(rs_ref)

        def step(j, masked):
            off = pl.multiple_of(j * T, T)
            kj = k_ref[0, pl.ds(off, T), :]
            s = _dot_nt(qs, kj) + (bias - ck_ref[0, j])
            p = jnp.exp(s)
            if masked:
                p = jnp.where(_causal_mask(T), p, 0.0)
            dp = _dot_nt(do, v_ref[0, pl.ds(off, T), :])
            ds = p * (dp - delta)
            acc_ref[...] += _dot(ds, kj)
            rs_ref[...] += jnp.sum(ds, axis=1, keepdims=True)

        def loop_body(j, carry):
            step(j, False)
            return carry

        lax.fori_loop(0, i, loop_body, 0)
        step(i, True)
        dq_ref[0] = (acc_ref[...] * ATTN_SCALE).astype(dq_ref.dtype)
        dcq_ref[0] = rs_ref[...]

    tile = pl.BlockSpec((1, T, Dh), lambda h, i: (h, i, 0))
    whole = pl.BlockSpec((1, S, Dh), lambda h, i: (h, 0, 0))
    col = pl.BlockSpec((1, T, 1), lambda h, i: (h, i, 0))
    rows = pl.BlockSpec((1, NT, 1, T), lambda h, i: (h, 0, 0, 0))
    return pl.pallas_call(
        body, name=name, grid=(H, NT),
        in_specs=[tile, whole, whole, tile, tile, col, col, rows],
        out_specs=[tile, col, col],
        out_shape=[jax.ShapeDtypeStruct((H, S, Dh), BF16), jax.ShapeDtypeStruct((H, S, 1), F32),
                   jax.ShapeDtypeStruct((H, S, 1), F32)],
        scratch_shapes=[pltpu.VMEM((T, Dh), F32), pltpu.VMEM((T, 1), F32)],
        compiler_params=_cparams("parallel", "parallel"),
    )(q, k, v, o, do, lse, c_col, c_row)


def flash_bwd_dkv(q, k, v, do, lse_row, delta_row, c_col, c_row, name):
    H, S, Dh = q.shape
    T = _attn_tile(S)
    NT = S // T

    def body(q_ref, k_ref, v_ref, do_ref, lse_ref, delta_ref, ck_ref, cq_ref, dk_ref, dv_ref, dc_ref,
             dk_acc, dv_acc, dc_acc):
        j = pl.program_id(1)
        kj = k_ref[0]
        vj = v_ref[0]
        ck = ck_ref[0]
        dk_acc[...] = jnp.zeros_like(dk_acc)
        dv_acc[...] = jnp.zeros_like(dv_acc)
        dc_acc[...] = jnp.zeros_like(dc_acc)

        def step(i, masked):
            off = pl.multiple_of(i * T, T)
            qi = q_ref[0, pl.ds(off, T), :]
            doi = do_ref[0, pl.ds(off, T), :]
            st = _dot_nt(kj, qi * ATTN_SCALE) + ((cq_ref[0, i] - lse_ref[0, i]) - ck)
            pt = jnp.exp(st)
            if masked:
                pt = jnp.where(_causal_mask(T, transposed=True), pt, 0.0)
            dv_acc[...] += _dot(pt, doi)
            dpt = _dot_nt(vj, doi)
            dst = pt * (dpt - delta_ref[0, i])
            dk_acc[...] += _dot(dst, qi)
            dc_acc[...] -= jnp.sum(dst, axis=1, keepdims=True)

        step(j, True)

        def loop_body(i, carry):
            step(i, False)
            return carry

        lax.fori_loop(j + 1, NT, loop_body, 0)
        dk_ref[0] = (dk_acc[...] * ATTN_SCALE).astype(dk_ref.dtype)
        dv_ref[0] = dv_acc[...].astype(dv_ref.dtype)
        dc_ref[0] = dc_acc[...]

    tile = pl.BlockSpec((1, T, Dh), lambda h, j: (h, j, 0))
    whole = pl.BlockSpec((1, S, Dh), lambda h, j: (h, 0, 0))
    col = pl.BlockSpec((1, T, 1), lambda h, j: (h, j, 0))
    rows = pl.BlockSpec((1, NT, 1, T), lambda h, j: (h, 0, 0, 0))
    return pl.pallas_call(
        body, name=name, grid=(H, NT),
        in_specs=[whole, tile, tile, whole, rows, rows, col, rows],
        out_specs=[tile, tile, col],
        out_shape=[jax.ShapeDtypeStruct((H, S, Dh), BF16), jax.ShapeDtypeStruct((H, S, Dh), BF16),
                   jax.ShapeDtypeStruct((H, S, 1), F32)],
        scratch_shapes=[pltpu.VMEM((T, Dh), F32), pltpu.VMEM((T, Dh), F32), pltpu.VMEM((T, 1), F32)],
        compiler_params=_cparams("parallel", "parallel"),
    )(q, k, v, do, lse_row, delta_row, c_col, c_row)


PAIR = 2 * HEAD_DIM


def _tri(n, lower):
    r = lax.broadcasted_iota(jnp.int32, (n, n), 0)
    c = lax.broadcasted_iota(jnp.int32, (n, n), 1)
    return (r >= c) if lower else (r <= c)


def _ssd_chunk_scalars(dt_raw, bias, a_log):
    Q = dt_raw.shape[0]
    dt = _softplus(dt_raw + bias)
    A = -jnp.exp(a_log)
    cum = jnp.dot(_tri(Q, True).astype(F32), dt * A, precision=HIGHEST, preferred_element_type=F32)
    tot = cum[Q - 1:Q, :]
    return dt, A, cum, tot


def _lane_pair(lo_mask, v, h0):
    return jnp.where(lo_mask, v[:, h0:h0 + 1], v[:, h0 + 1:h0 + 2])


def ssd_scan_fwd(xbc, proj, dt_col, dt_bias, a_log, d_inner, name):
    S, W = xbc.shape
    Q, N, G = SSM_CHUNK, SSM_STATE, SSM_GROUPS
    nc = S // Q
    n_pairs = d_inner // PAIR
    pairs_per_group = n_pairs // G
    GN = G * N

    def body(xbc_ref, dt_ref, bias_ref, alog_ref, y_ref, sin_ref, st_ref):
        c = pl.program_id(0)

        @pl.when(c == 0)
        def _():
            st_ref[...] = jnp.zeros_like(st_ref)

        sin_ref[0] = st_ref[...]
        dt, A, cum, tot = _ssd_chunk_scalars(dt_ref[...], bias_ref[...], alog_ref[...])
        cum_t = cum.T
        dt_t = dt.T
        ecum = jnp.exp(cum)
        wend = jnp.exp(tot - cum) * dt
        etot = jnp.exp(tot)
        lower = _tri(Q, True)
        lo = lax.broadcasted_iota(jnp.int32, (Q, PAIR), 1) < HEAD_DIM
        lo_row = lax.broadcasted_iota(jnp.int32, (1, PAIR), 1) < HEAD_DIM
        for g in range(G):
            Bg = xbc_ref[:, d_inner + g * N:d_inner + (g + 1) * N]
            Cg = xbc_ref[:, d_inner + GN + g * N:d_inner + GN + (g + 1) * N]
            CB = _dot_nt(Cg, Bg)
            for pp in range(pairs_per_group):
                pr = g * pairs_per_group + pp
                h0 = 2 * pr
                xw = xbc_ref[:, pr * PAIR:(pr + 1) * PAIR]
                ys = []
                for h in (h0, h0 + 1):
                    L = jnp.where(lower, jnp.exp(cum[:, h:h + 1] - cum_t[h:h + 1, :]), 0.0)
                    ys.append(_dot(CB * L * dt_t[h:h + 1, :], xw))
                st = st_ref[pr]
                y_inter = _dot(Cg, st) * _lane_pair(lo, ecum, h0)
                y_ref[:, pr * PAIR:(pr + 1) * PAIR] = jnp.where(lo, ys[0], ys[1]) + y_inter
                st_ref[pr] = _lane_pair(lo_row, etot, h0) * st + _dot_tn(Bg, xw * _lane_pair(lo, wend, h0))

    return pl.pallas_call(
        body, name=name, grid=(nc,),
        in_specs=[pl.BlockSpec((Q, W), lambda c: (c, 0)), pl.BlockSpec((Q, LANES), lambda c: (c, dt_col // LANES)),
                  pl.BlockSpec((1, LANES), lambda c: (0, 0)), pl.BlockSpec((1, LANES), lambda c: (0, 0))],
        out_specs=[pl.BlockSpec((Q, d_inner), lambda c: (c, 0)), pl.BlockSpec((1, n_pairs, N, PAIR), lambda c: (c, 0, 0, 0))],
        out_shape=[jax.ShapeDtypeStruct((S, d_inner), F32), jax.ShapeDtypeStruct((nc, n_pairs, N, PAIR), F32)],
        scratch_shapes=[pltpu.VMEM((n_pairs, N, PAIR), F32)],
        compiler_params=_cparams("arbitrary"),
    )(xbc, proj, dt_bias, a_log)


def ssd_scan_bwd(dy, dskip, xbc, proj, dt_col, dt_bias, a_log, states, d_inner, name):
    S, W = xbc.shape
    Q, N, G = SSM_CHUNK, SSM_STATE, SSM_GROUPS
    nc = S // Q
    n_pairs = d_inner // PAIR
    pairs_per_group = n_pairs // G
    GN = G * N

    def body(dy_ref, dskip_ref, xbc_ref, dt_ref, bias_ref, alog_ref, sin_ref,
             dxbc_ref, ddt_ref, dalog_ref, dbias_ref, dst_ref, rows_cum_ref, rows_dt_ref):
        step = pl.program_id(0)

        @pl.when(step == 0)
        def _():
            dst_ref[...] = jnp.zeros_like(dst_ref)
            dalog_ref[...] = jnp.zeros_like(dalog_ref)
            dbias_ref[...] = jnp.zeros_like(dbias_ref)

        rows_cum_ref[...] = jnp.zeros_like(rows_cum_ref)
        rows_dt_ref[...] = jnp.zeros_like(rows_dt_ref)
        dt_raw = dt_ref[...]
        bias = bias_ref[...]
        dt, A, cum, tot = _ssd_chunk_scalars(dt_raw, bias, alog_ref[...])
        cum_t = cum.T
        dt_t = dt.T
        ecum = jnp.exp(cum)
        eend = jnp.exp(tot - cum)
        wend = eend * dt
        etot = jnp.exp(tot)
        lower = _tri(Q, True)
        lane = lax.broadcasted_iota(jnp.int32, (Q, LANES), 1)
        lo = lane < HEAD_DIM
        lo_row = lax.broadcasted_iota(jnp.int32, (1, PAIR), 1) < HEAD_DIM
        lo_st = lax.broadcasted_iota(jnp.int32, (N, PAIR), 1) < HEAD_DIM
        last_row = lax.broadcasted_iota(jnp.int32, (Q, LANES), 0) == Q - 1
        dcum = jnp.zeros((Q, LANES), F32)
        ddt = jnp.zeros((Q, LANES), F32)
        for g in range(G):
            Bg = xbc_ref[:, d_inner + g * N:d_inner + (g + 1) * N]
            Cg = xbc_ref[:, d_inner + GN + g * N:d_inner + GN + (g + 1) * N]
            CB = _dot_nt(Cg, Bg)
            dCB = jnp.zeros((Q, Q), F32)
            dBg = jnp.zeros((Q, N), F32)
            dCg = jnp.zeros((Q, N), F32)
            for pp in range(pairs_per_group):
                pr = g * pairs_per_group + pp
                h0 = 2 * pr
                xw = xbc_ref[:, pr * PAIR:(pr + 1) * PAIR]
                dyp = dy_ref[:, pr * PAIR:(pr + 1) * PAIR]
                st = sin_ref[0, pr]
                dst = dst_ref[pr]
                ecum_p = _lane_pair(lo, ecum, h0)
                wend_p = _lane_pair(lo, wend, h0)
                etot_p = _lane_pair(lo_row, etot, h0)
                y2 = _dot(Cg, st)
                dye = dyp * ecum_p
                dCg = dCg + _dot_nt(dye, st)
                t1 = dye * y2
                bds = _dot(Bg, dst)
                qv = xw * bds
                dBg = dBg + _dot_nt(xw * wend_p, dst)
                sdot = dst * st
                dx = wend_p * bds
                for h, half, half_st in ((h0, lo, lo_st), (h0 + 1, ~lo, ~lo_st)):
                    sel = lane == h
                    t1_h = jnp.sum(jnp.where(half, t1, 0.0), axis=1, keepdims=True)
                    q_h = jnp.sum(jnp.where(half, qv, 0.0), axis=1, keepdims=True)
                    wq = wend[:, h:h + 1] * q_h
                    dtot_h = etot[:, h:h + 1] * jnp.sum(jnp.where(half_st, sdot, 0.0)) + jnp.sum(wq)
                    dcum = dcum + jnp.where(sel, t1_h - wq, 0.0) + jnp.where(sel & last_row, dtot_h, 0.0)
                    ddt = ddt + jnp.where(sel, eend[:, h:h + 1] * q_h, 0.0)
                    dt_row = dt_t[h:h + 1, :]
                    L = jnp.where(lower, jnp.exp(cum[:, h:h + 1] - cum_t[h:h + 1, :]), 0.0)
                    dyh = jnp.where(half, dyp, 0.0)
                    dM = _dot_nt(dyh, xw)
                    CBL = CB * L
                    Gp = dM * CBL
                    Gm = Gp * dt_row
                    rows_dt_ref[h:h + 1, :] = jnp.sum(Gp, axis=0, keepdims=True)
                    rows_cum_ref[h:h + 1, :] = -jnp.sum(Gm, axis=0, keepdims=True)
                    dcum = dcum + jnp.where(sel, jnp.sum(Gm, axis=1, keepdims=True), 0.0)
                    dCB = dCB + dM * L * dt_row
                    dx = dx + _dot_tn(CBL * dt_row, dyh)
                dxbc_ref[:, pr * PAIR:(pr + 1) * PAIR] = dx + dskip_ref[:, pr * PAIR:(pr + 1) * PAIR]
                dst_ref[pr] = _dot_tn(Cg, dye) + etot_p * dst
            dxbc_ref[:, d_inner + g * N:d_inner + (g + 1) * N] = dBg + _dot_tn(dCB, Cg)
            dxbc_ref[:, d_inner + GN + g * N:d_inner + GN + (g + 1) * N] = dCg + _dot(dCB, Bg)
        dcum = dcum + rows_cum_ref[...].T
        ddt = ddt + rows_dt_ref[...].T
        da = jnp.dot(_tri(Q, False).astype(F32), dcum, precision=HIGHEST, preferred_element_type=F32)
        ddt = ddt + da * A
        ddt_raw = ddt * _sigmoid(dt_raw + bias)
        ddt_ref[...] = ddt_raw
        dalog_ref[...] += jnp.sum(da * dt, axis=0, keepdims=True) * A
        dbias_ref[...] += jnp.sum(ddt_raw, axis=0, keepdims=True)

    rev = lambda width, col: pl.BlockSpec((Q, width), lambda s: (nc - 1 - s, col))
    vec = pl.BlockSpec((1, LANES), lambda s: (0, 0))
    return pl.pallas_call(
        body, name=name, grid=(nc,),
        in_specs=[rev(d_inner, 0), rev(d_inner, 0), rev(W, 0), rev(LANES, dt_col // LANES), vec, vec,
                  pl.BlockSpec((1, n_pairs, N, PAIR), lambda s: (nc - 1 - s, 0, 0, 0))],
        out_specs=[rev(W, 0), rev(LANES, 0), vec, vec],
        out_shape=[jax.ShapeDtypeStruct((S, W), F32), jax.ShapeDtypeStruct((S, LANES), F32),
                   jax.ShapeDtypeStruct((1, LANES), F32), jax.ShapeDtypeStruct((1, LANES), F32)],
        scratch_shapes=[pltpu.VMEM((n_pairs, N, PAIR), F32), pltpu.VMEM((LANES, Q), F32), pltpu.VMEM((LANES, Q), F32)],
        compiler_params=_cparams("arbitrary"),
    )(dy, dskip, xbc, proj, dt_bias, a_log, states)


def ssd_gate_fwd(y, xbc, proj, d_skip, norm_w, d_inner, name):
    S = y.shape[0]
    tm = _pick(S, (256, 128))
    gs = d_inner // SSM_GROUPS

    def body(y_ref, x_ref, z_ref, d_ref, w_ref, o_ref):
        for g in range(SSM_GROUPS):
            sl = slice(g * gs, (g + 1) * gs)
            y2 = (y_ref[:, sl] + d_ref[:, sl] * x_ref[:, sl]) * _silu(z_ref[:, sl])
            r = lax.rsqrt(jnp.mean(y2 * y2, axis=-1, keepdims=True) + RMS_EPS)
            o_ref[:, sl] = (y2 * r * w_ref[:, sl]).astype(o_ref.dtype)

    row = pl.BlockSpec((tm, d_inner), lambda i: (i, 0))
    vec = pl.BlockSpec((1, d_inner), lambda i: (0, 0))
    return pl.pallas_call(
        body, name=name, grid=(S // tm,), in_specs=[row, row, row, vec, vec], out_specs=row,
        out_shape=jax.ShapeDtypeStruct((S, d_inner), BF16), compiler_params=_cparams("parallel"),
    )(y, xbc, proj, d_skip, norm_w)


def ssd_gate_bwd(dyn, y, xbc, proj, d_skip, norm_w, d_inner, name):
    S = y.shape[0]
    tm = _pick(S, (256, 128))
    gs = d_inner // SSM_GROUPS

    def body(dyn_ref, y_ref, x_ref, z_ref, d_ref, w_ref, dy_ref, dskip_ref, dz_ref, dw_ref, dd_ref):
        i = pl.program_id(0)

        @pl.when(i == 0)
        def _():
            dw_ref[...] = jnp.zeros_like(dw_ref)
            dd_ref[...] = jnp.zeros_like(dd_ref)

        for g in range(SSM_GROUPS):
            sl = slice(g * gs, (g + 1) * gs)
            z = z_ref[:, sl]
            x = x_ref[:, sl]
            sz = _silu(z)
            ysum = y_ref[:, sl] + d_ref[:, sl] * x
            y2 = ysum * sz
            r = lax.rsqrt(jnp.mean(y2 * y2, axis=-1, keepdims=True) + RMS_EPS)
            dyn = dyn_ref[:, sl]
            a = dyn * w_ref[:, sl]
            dy2 = r * a - y2 * (r * r * r) * jnp.mean(a * y2, axis=-1, keepdims=True)
            dysum = dy2 * sz
            dy_ref[:, sl] = dysum
            dskip_ref[:, sl] = dysum * d_ref[:, sl]
            dz_ref[:, sl] = (dy2 * ysum * _silu_grad(z)).astype(dz_ref.dtype)
            dw_ref[:, sl] += jnp.sum(dyn * y2 * r, axis=0, keepdims=True)
            dd_ref[:, sl] += jnp.sum(dysum * x, axis=0, keepdims=True)

    row = pl.BlockSpec((tm, d_inner), lambda i: (i, 0))
    vec = pl.BlockSpec((1, d_inner), lambda i: (0, 0))
    return pl.pallas_call(
        body, name=name, grid=(S // tm,), in_specs=[row, row, row, row, vec, vec], out_specs=[row, row, row, vec, vec],
        out_shape=[jax.ShapeDtypeStruct((S, d_inner), F32), jax.ShapeDtypeStruct((S, d_inner), F32),
                   jax.ShapeDtypeStruct((S, d_inner), BF16), jax.ShapeDtypeStruct((1, d_inner), F32),
                   jax.ShapeDtypeStruct((1, d_inner), F32)],
        compiler_params=_cparams("arbitrary"),
    )(dyn, y, xbc, proj, d_skip, norm_w)


def _pad_to(a, axis, mult=LANES):
    n = a.shape[axis]
    extra = (-n) % mult
    if extra == 0:
        return a
    widths = [(0, 0)] * a.ndim
    widths[axis] = (0, extra)
    return jnp.pad(a, widths)


def _attn_fwd(x, w_in, b_f, w_out, tag):
    S, D = x.shape
    H = D // HEAD_DIM
    T = _attn_tile(S)
    proj = mm_nn(x, _pad_to(w_in, 1), F32, f"{tag}_proj")
    qkv = proj[:, :3 * D].astype(BF16).reshape(S, 3, H, HEAD_DIM).transpose(1, 2, 0, 3)
    zt = proj[:, 3 * D:3 * D + H].T
    bf = b_f.reshape(H, 1)
    c = fox_gate_fwd(zt, bf, f"{tag}_gate")
    c_col, c_row = c.reshape(H, S, 1), c.reshape(H, S // T, 1, T)
    o, lse = flash_fwd(qkv[0], qkv[1], qkv[2], c_col, c_row, f"{tag}_flash")
    o_flat = o.transpose(1, 0, 2).reshape(S, D)
    mix = mm_nn(o_flat, w_out, F32, f"{tag}_out")
    return mix, (qkv, zt, bf, c_col, c_row, o, lse, o_flat)


def _attn_bwd(x, dmix, dx_add, alpha, w_in, w_out, saved, tag):
    S, D = x.shape
    H = D // HEAD_DIM
    T = _attn_tile(S)
    qkv, zt, bf, c_col, c_row, o, lse, o_flat = saved
    g_w_out = mm_tn(o_flat, dmix, f"{tag}_gwout")
    do = mm_nn(dmix, w_out.T, BF16, f"{tag}_do").reshape(S, H, HEAD_DIM).transpose(1, 0, 2)
    dq, delta, dc_q = flash_bwd_dq(qkv[0], qkv[1], qkv[2], o, do, lse, c_col, c_row, f"{tag}_flash_dq")
    rows = lambda t: t.reshape(H, S // T, 1, T)
    dk, dv, dc_k = flash_bwd_dkv(qkv[0], qkv[1], qkv[2], do, rows(lse), rows(delta), c_col, c_row, f"{tag}_flash_dkv")
    dzt, dbf = fox_gate_bwd(dc_q.reshape(H, S), dc_k.reshape(H, S), zt, bf, f"{tag}_gate_bwd")
    dqkv = jnp.stack([dq, dk, dv]).transpose(2, 0, 1, 3).reshape(S, 3 * D)
    dzf = _pad_to(dzt.T, 1)
    g_w_in = jnp.concatenate([mm_tn(x, dqkv, f"{tag}_gwqkv"), mm_tn(x, dzf, f"{tag}_gwf")[:, :H]], axis=1)
    w_in_t = w_in.T
    dx = mm_nn(dqkv, w_in_t[:3 * D], F32, f"{tag}_dx_qkv", add=dx_add, add_scale=alpha)
    dx = mm_nn(dzf, _pad_to(w_in_t[3 * D:], 0), F32, f"{tag}_dx_f", add=dx)
    return dx, (g_w_in, dbf.reshape(H), g_w_out)


def _ssm_dims(D):
    d_inner = 2 * D
    gn = SSM_GROUPS * SSM_STATE
    return d_inner, d_inner + 2 * gn, d_inner // HEAD_DIM


def _ssm_fwd(x, w_in, conv_w, conv_b, dt_bias, a_log, d_skip, norm_w, w_out, tag):
    S, D = x.shape
    DI, XBC, HS = _ssm_dims(D)
    dt_col = DI + XBC
    proj = mm_nn(x, _pad_to(w_in, 1), F32, f"{tag}_proj")
    conv_b = conv_b.reshape(1, XBC)
    xbc = conv_act_fwd(proj, DI, XBC, conv_w, conv_b, F32, f"{tag}_conv")
    dt_bias_p = _pad_to(dt_bias.reshape(1, HS), 1)
    a_log_p = _pad_to(a_log.reshape(1, HS), 1)
    y, states = ssd_scan_fwd(xbc, proj, dt_col, dt_bias_p, a_log_p, DI, f"{tag}_scan")
    d_vec = jnp.repeat(d_skip, HEAD_DIM).reshape(1, DI)
    norm_w = norm_w.reshape(1, DI)
    yn = ssd_gate_fwd(y, xbc, proj, d_vec, norm_w, DI, f"{tag}_gate")
    mix = mm_nn(yn, w_out, F32, f"{tag}_out")
    return mix, (proj, xbc, conv_b, dt_bias_p, a_log_p, y, states, d_vec, norm_w, yn)


def _ssm_bwd(x, dmix, dx_add, alpha, w_in, conv_w, w_out, saved, tag):
    S, D = x.shape
    DI, XBC, HS = _ssm_dims(D)
    dt_col = DI + XBC
    proj, xbc, conv_b, dt_bias_p, a_log_p, y, states, d_vec, norm_w, yn = saved
    g_w_out = mm_tn(yn, dmix, f"{tag}_gwout")
    dyn = mm_nn(dmix, w_out.T, F32, f"{tag}_dyn")
    dy, dskip, dz, g_norm_w, g_dvec = ssd_gate_bwd(dyn, y, xbc, proj, d_vec, norm_w, DI, f"{tag}_gate_bwd")
    dxbc, ddt_raw, g_a_log, g_dt_bias = ssd_scan_bwd(dy, dskip, xbc, proj, dt_col, dt_bias_p, a_log_p, states, DI,
                                                     f"{tag}_scan_bwd")
    dxbc_raw, _, g_conv_w, g_conv_b = conv_act_bwd(dxbc, proj, DI, XBC, conv_w, conv_b, f"{tag}_conv_bwd")
    g_w_in = jnp.concatenate([mm_tn(x, dz, f"{tag}_gwz"), mm_tn(x, dxbc_raw, f"{tag}_gwxbc"),
                              mm_tn(x, ddt_raw, f"{tag}_gwdt")[:, :HS]], axis=1)
    w_in_t = w_in.T
    dx = mm_nn(dz, w_in_t[:DI], F32, f"{tag}_dx_z", add=dx_add, add_scale=alpha)
    dx = mm_nn(dxbc_raw, w_in_t[DI:dt_col], F32, f"{tag}_dx_xbc", add=dx)
    dx = mm_nn(ddt_raw, _pad_to(w_in_t[dt_col:], 0), F32, f"{tag}_dx_dt", add=dx)
    g_d = g_dvec.reshape(HS, HEAD_DIM).sum(axis=-1)
    grads = (g_w_in, g_conv_w, g_conv_b.reshape(XBC), g_dt_bias[0, :HS], g_a_log[0, :HS], g_d, g_norm_w.reshape(DI), g_w_out)
    return dx, grads


ATTN_KEYS = ("attn_w_in", "attn_b_f", "attn_w_out")
SSM_KEYS = ("ssm_w_in", "ssm_conv_w", "ssm_conv_b", "ssm_dt_bias", "ssm_A_log", "ssm_D", "ssm_norm_w", "ssm_w_out")
LAYER_KEYS = ("ln_mix_g", "ln_mix_b", "ffn_w_up", "ffn_conv_w", "ffn_conv_b", "ffn_w_down", "ln_ffn_g", "ln_ffn_b",
              "ple_w_proj", "ple_w_gate", "ple_b_gate")


def local_step(x, p, target, w):
    S, D = x.shape
    depth = p.shape[0]
    alpha = (2 * depth) ** 0.25
    F = w["ffn_w_down"].shape[1]
    saved = []
    for i in range(depth):
        j, tag = i // 2, f"l{i}"
        if i % 2 == 0:
            mix, msaved = _attn_fwd(x, w["attn_w_in"][j], w["attn_b_f"][j], w["attn_w_out"][j], tag + "_attn")
        else:
            mix, msaved = _ssm_fwd(x, w["ssm_w_in"][j], w["ssm_conv_w"][j], w["ssm_conv_b"][j], w["ssm_dt_bias"][j],
                                   w["ssm_A_log"][j], w["ssm_D"][j], w["ssm_norm_w"][j], w["ssm_w_out"][j], tag + "_ssm")
        row = lambda k: w[k][i].reshape(1, -1)
        x1, xhat1, rstd1 = ln_fwd(x, mix, row("ln_mix_g"), row("ln_mix_b"), alpha, tag + "_ln_mix")
        h = mm_nn(x1, w["ffn_w_up"][i], F32, tag + "_ffn_up")
        a = conv_act_fwd(h, F, F, w["ffn_conv_w"][i], row("ffn_conv_b"), BF16, tag + "_ffn_act", gate_col=0)
        ffn = mm_nn(a, w["ffn_w_down"][i], F32, tag + "_ffn_down")
        x2, xhat2, rstd2 = ln_fwd(x1, ffn, row("ln_ffn_g"), row("ln_ffn_b"), alpha, tag + "_ln_ffn")
        zg = mm_nn(x2, w["ple_w_gate"][i], F32, tag + "_ple_gate")
        pp = mm_nn(p[i], w["ple_w_proj"][i], F32, tag + "_ple_proj")
        x3 = ple_fwd(x2, zg, pp, row("ple_b_gate"), tag + "_ple")
        saved.append((x, msaved, x1, xhat1, rstd1, h, a, x2, xhat2, rstd2, zg, pp))
        x = x3

    loss_vec, d = loss_head(x, target, "loss_head")

    grads = {k: [None] * w[k].shape[0] for k in ATTN_KEYS + SSM_KEYS + LAYER_KEYS}
    for i in reversed(range(depth)):
        j, tag = i // 2, f"l{i}"
        x0, msaved, x1, xhat1, rstd1, h, a, x2, xhat2, rstd2, zg, pp = saved[i]
        row = lambda k: w[k][i].reshape(1, -1)
        dzg, dpp, g_bg = ple_bwd(d, zg, pp, row("ple_b_gate"), tag + "_ple_bwd")
        grads["ple_w_gate"][i] = mm_tn(x2, dzg, tag + "_gw_ple_gate")
        grads["ple_w_proj"][i] = mm_tn(p[i], dpp, tag + "_gw_ple_proj")
        grads["ple_b_gate"][i] = g_bg.reshape(D)
        dx2 = mm_nn(dzg, w["ple_w_gate"][i].T, F32, tag + "_dx2", add=d)
        dr2, g_g2, g_b2 = ln_bwd(dx2, xhat2, rstd2, row("ln_ffn_g"), tag + "_ln_ffn_bwd")
        grads["ln_ffn_g"][i], grads["ln_ffn_b"][i] = g_g2.reshape(D), g_b2.reshape(D)
        grads["ffn_w_down"][i] = mm_tn(a, dr2, tag + "_gw_down")
        da = mm_nn(dr2, w["ffn_w_down"][i].T, F32, tag + "_da")
        dgin, du, g_cw, g_cb = conv_act_bwd(da, h, F, F, w["ffn_conv_w"][i], row("ffn_conv_b"), tag + "_ffn_act_bwd", gate_col=0)
        grads["ffn_conv_w"][i], grads["ffn_conv_b"][i] = g_cw, g_cb.reshape(F)
        grads["ffn_w_up"][i] = jnp.concatenate([mm_tn(x1, du, tag + "_gw_up_u"), mm_tn(x1, dgin, tag + "_gw_up_g")], axis=1)
        w_up_t = w["ffn_w_up"][i].T
        dx1 = mm_nn(du, w_up_t[:F], F32, tag + "_dx1_u", add=dr2, add_scale=alpha)
        dx1 = mm_nn(dgin, w_up_t[F:], F32, tag + "_dx1_g", add=dx1)
        dr1, g_g1, g_b1 = ln_bwd(dx1, xhat1, rstd1, row("ln_mix_g"), tag + "_ln_mix_bwd")
        grads["ln_mix_g"][i], grads["ln_mix_b"][i] = g_g1.reshape(D), g_b1.reshape(D)
        if i % 2 == 0:
            d, mg = _attn_bwd(x0, dr1, dr1, alpha, w["attn_w_in"][j], w["attn_w_out"][j], msaved, tag + "_attn")
            for k, g in zip(ATTN_KEYS, mg):
                grads[k][j] = g
        else:
            d, mg = _ssm_bwd(x0, dr1, dr1, alpha, w["ssm_w_in"][j], w["ssm_conv_w"][j], w["ssm_w_out"][j], msaved, tag + "_ssm")
            for k, g in zip(SSM_KEYS, mg):
                grads[k][j] = g
    return loss_vec, d, {k: jnp.stack(v) for k, v in grads.items()}


MESH = pl.DeviceIdType.MESH
PACK_ELEMS = 2 * SUBLANES * LANES


def _exchange(src, gather, name):
    out_shape = (N_DEV,) + tuple(src.shape[-2:])

    def body(src_ref, out_ref, send_sems, recv_sems, local_sem):
        x, y, c = lax.axis_index("x"), lax.axis_index("y"), lax.axis_index("c")
        me = 4 * x + 2 * y + c

        def block_for(dev):
            return src_ref if gather else src_ref.at[dev]

        local = pltpu.make_async_copy(block_for(me), out_ref.at[me], local_sem)
        local.start()
        copies = []
        for k in range(1, N_DEV):
            px = 1 - x if k & 4 else x
            py = 1 - y if k & 2 else y
            pc = 1 - c if k & 1 else c
            cp = pltpu.make_async_remote_copy(
                src_ref=block_for(4 * px + 2 * py + pc), dst_ref=out_ref.at[me],
                send_sem=send_sems.at[k - 1], recv_sem=recv_sems.at[k - 1],
                device_id=(px, py, pc), device_id_type=MESH)
            cp.start()
            copies.append(cp)
        for cp in copies:
            cp.wait()
        local.wait()

    return pl.pallas_call(
        body, name=name,
        in_specs=[pl.BlockSpec(memory_space=pl.ANY)],
        out_specs=pl.BlockSpec(memory_space=pl.ANY),
        out_shape=jax.ShapeDtypeStruct(out_shape, src.dtype),
        scratch_shapes=[pltpu.SemaphoreType.DMA((N_DEV - 1,)), pltpu.SemaphoreType.DMA((N_DEV - 1,)), pltpu.SemaphoreType.DMA],
    )(src)


def reduce_adamw(parts, w, m, v, name):
    _, R, _ = parts.shape
    tr = _pick(R, (512, 256, 128, 64, 32, 16))

    def body(p_ref, w_ref, m_ref, v_ref, g_ref, d_ref, nm_ref, nv_ref):
        g = p_ref[0]
        for s in range(1, N_DEV):
            g = g + p_ref[s]
        nm = ADAM_B1 * m_ref[...] + (1.0 - ADAM_B1) * g
        nv = ADAM_B2 * v_ref[...] + (1.0 - ADAM_B2) * (g * g)
        m_hat = nm / (1.0 - ADAM_B1 ** ADAM_STEP)
        v_hat = nv / (1.0 - ADAM_B2 ** ADAM_STEP)
        g_ref[...] = g
        d_ref[...] = -ADAM_LR * (m_hat / (jnp.sqrt(v_hat) + ADAM_EPS) + ADAM_WD * w_ref[...])
        nm_ref[...] = nm
        nv_ref[...] = nv

    row = pl.BlockSpec((tr, LANES), lambda i: (i, 0))
    return pl.pallas_call(
        body, name=name, grid=(R // tr,),
        in_specs=[pl.BlockSpec((N_DEV, tr, LANES), lambda i: (0, i, 0)), row, row, row],
        out_specs=[row, row, row, row],
        out_shape=[jax.ShapeDtypeStruct((R, LANES), F32)] * 4,
        compiler_params=_cparams("parallel"),
    )(parts, w, m, v)


def _pack(arrays, dtype, lead=0):
    parts = []
    for a in arrays:
        head = a.shape[:lead]
        flat = a.astype(dtype).reshape(head + (-1,))
        flat = jnp.pad(flat, [(0, 0)] * lead + [(0, (-flat.shape[-1]) % PACK_ELEMS)])
        parts.append(flat.reshape(head + (-1, LANES)))
    return jnp.concatenate(parts, axis=lead)


def _unpack(packed, shapes):
    lead = packed.shape[:-2]
    out, r0 = [], 0
    for shape in shapes:
        n = math.prod(shape)
        rows = -(-n // PACK_ELEMS) * (PACK_ELEMS // LANES)
        seg = packed[..., r0:r0 + rows, :].reshape(lead + (rows * LANES,))[..., :n]
        out.append(seg.reshape(lead + tuple(shape)))
        r0 += rows
    return out


MATMUL_SHARDED = {"attn_w_in": 2, "attn_w_out": 1, "ssm_w_in": 2, "ssm_w_out": 1, "ffn_w_up": 2, "ffn_w_down": 1,
                  "ple_w_proj": 2, "ple_w_gate": 1}
SMALL_SHARDED = {"ssm_conv_w": 2, "ssm_conv_b": 1, "ssm_norm_w": 1, "ffn_conv_w": 2}
REPLICATED = ("attn_b_f", "ssm_dt_bias", "ssm_A_log", "ssm_D", "ln_mix_g", "ln_mix_b", "ffn_conv_b", "ln_ffn_g",
              "ln_ffn_b", "ple_b_gate")
WEIGHT_ORDER = ("attn_w_in", "attn_b_f", "attn_w_out", "ssm_w_in", "ssm_conv_w", "ssm_conv_b", "ssm_dt_bias", "ssm_A_log",
                "ssm_D", "ssm_norm_w", "ssm_w_out", "ln_mix_g", "ln_mix_b", "ffn_w_up", "ffn_conv_w", "ffn_conv_b",
                "ffn_w_down", "ln_ffn_g", "ln_ffn_b", "ple_w_proj", "ple_w_gate", "ple_b_gate")


def _join_shards(gathered, axis):
    moved = jnp.moveaxis(gathered, 0, axis)
    shape = list(moved.shape)
    shape[axis:axis + 2] = [shape[axis] * shape[axis + 1]]
    return moved.reshape(shape)


def _split_shards(full, axis):
    shape = list(full.shape)
    shape[axis:axis + 1] = [N_DEV, shape[axis] // N_DEV]
    return jnp.moveaxis(full.reshape(shape), axis, 0)


def kernel(x, p, attn_w_in, attn_b_f, attn_w_out, ssm_w_in, ssm_conv_w, ssm_conv_b, ssm_dt_bias, ssm_A_log, ssm_D, ssm_norm_w, ssm_w_out, ln_mix_g, ln_mix_b, ffn_w_up, ffn_conv_w, ffn_conv_b, ffn_w_down, ln_ffn_g, ln_ffn_b, ple_w_proj, ple_w_gate, ple_b_gate, loss_target, m_attn_w_in, m_attn_b_f, m_attn_w_out, m_ssm_w_in, m_ssm_conv_w, m_ssm_conv_b, m_ssm_dt_bias, m_ssm_A_log, m_ssm_D, m_ssm_norm_w, m_ssm_w_out, m_ln_mix_g, m_ln_mix_b, m_ffn_w_up, m_ffn_conv_w, m_ffn_conv_b, m_ffn_w_down, m_ln_ffn_g, m_ln_ffn_b, m_ple_w_proj, m_ple_w_gate, m_ple_b_gate, v_attn_w_in, v_attn_b_f, v_attn_w_out, v_ssm_w_in, v_ssm_conv_w, v_ssm_conv_b, v_ssm_dt_bias, v_ssm_A_log, v_ssm_D, v_ssm_norm_w, v_ssm_w_out, v_ln_mix_g, v_ln_mix_b, v_ffn_w_up, v_ffn_conv_w, v_ffn_conv_b, v_ffn_w_down, v_ln_ffn_g, v_ln_ffn_b, v_ple_w_proj, v_ple_w_gate, v_ple_b_gate):
    args = dict(locals())
    w_loc = {k: args[k] for k in WEIGHT_ORDER}
    m_loc = {k: args["m_" + k] for k in WEIGHT_ORDER}
    v_loc = {k: args["v_" + k] for k in WEIGHT_ORDER}
    mm_names, small_names = tuple(MATMUL_SHARDED), tuple(SMALL_SHARDED)
    sharded = mm_names + small_names
    axis_of = {**MATMUL_SHARDED, **SMALL_SHARDED}

    g_mm = _exchange(_pack([w_loc[k] for k in mm_names], BF16), True, "gather_matmul_weights")
    g_small = _exchange(_pack([w_loc[k] for k in small_names], F32), True, "gather_small_weights")
    w_full = {k: w_loc[k] for k in REPLICATED}
    for names, gathered in ((mm_names, g_mm), (small_names, g_small)):
        for k, blocks in zip(names, _unpack(gathered, [w_loc[k].shape for k in names])):
            w_full[k] = _join_shards(blocks, axis_of[k])

    loss_vec, grad_x, g_full = local_step(x[0], p[:, 0], loss_target[0], w_full)

    send = _pack([_split_shards(g_full[k], axis_of[k]) for k in sharded], F32, lead=1)
    parts = _exchange(send, False, "exchange_weight_grads")
    shapes = [w_loc[k].shape for k in sharded]
    pk = lambda d: _pack([d[k] for k in sharded], F32)
    outs = reduce_adamw(parts, pk(w_loc), pk(m_loc), pk(v_loc), "reduce_adamw_sharded")
    res = {k: vals for k, vals in zip(sharded, zip(*[_unpack(o, shapes) for o in outs]))}

    rep_shapes = [w_loc[k].shape for k in REPLICATED] + [(1, LANES)]
    rparts = _exchange(_pack([g_full[k] for k in REPLICATED] + [loss_vec], F32), True, "gather_replicated_grads")
    zero = jnp.zeros((1, LANES), F32)
    rk = lambda d: _pack([d[k] for k in REPLICATED] + [zero], F32)
    routs = reduce_adamw(rparts, rk(w_loc), rk(m_loc), rk(v_loc), "reduce_adamw_replicated")
    runp = [_unpack(o, rep_shapes) for o in routs]
    for i, k in enumerate(REPLICATED):
        res[k] = tuple(u[i] for u in runp)
    loss = runp[0][-1][0, 0]

    return (loss, grad_x[None], *[res[k][0] for k in WEIGHT_ORDER], *[res[k][1] for k in WEIGHT_ORDER],
            *[res[k][2] for k in WEIGHT_ORDER], *[res[k][3] for k in WEIGHT_ORDER])
```

```python
import functools
import math

import jax
import jax.numpy as jnp
from jax import lax
from jax.experimental import pallas as pl
from jax.experimental.pallas import tpu as pltpu

F32 = jnp.float32
BF16 = jnp.bfloat16

LANES = 128
SUBLANES = 8
VMEM_LIMIT_BYTES = 56 * 1024 * 1024

N_DEV = 8
HEAD_DIM = 64
SSM_GROUPS = 8
SSM_STATE = 128
SSM_CHUNK = 128
SSM_CONV = 4
FFN_CONV = 3
LN_EPS = 1e-5
RMS_EPS = 1e-5
ADAM_LR, ADAM_B1, ADAM_B2, ADAM_EPS, ADAM_WD, ADAM_STEP = 0.001, 0.9, 0.999, 1e-08, 0.01, 10
NEG_INF = float("-inf")
HIGHEST = lax.Precision.HIGHEST
NT_DIMS = (((1,), (1,)), ((), ()))
TN_DIMS = (((0,), (0,)), ((), ()))


def _cparams(*sem):
    return pltpu.CompilerParams(dimension_semantics=sem, vmem_limit_bytes=VMEM_LIMIT_BYTES)


def _pick(n, candidates):
    for c in candidates:
        if n % c == 0:
            return c
    return n


def _dot(a, b):
    return jnp.dot(a.astype(BF16), b.astype(BF16), preferred_element_type=F32)


def _dot_nt(a, b):
    return lax.dot_general(a.astype(BF16), b.astype(BF16), NT_DIMS, preferred_element_type=F32)


def _dot_tn(a, b):
    return lax.dot_general(a.astype(BF16), b.astype(BF16), TN_DIMS, preferred_element_type=F32)


def _sigmoid(x):
    return 1.0 / (1.0 + jnp.exp(-x))


def _log1p_small(u):
    return jnp.where(u < 1e-3, u * (1.0 - u * (0.5 - u * (1.0 / 3.0))), jnp.log(1.0 + u))


def _softplus(x):
    return jnp.maximum(x, 0.0) + _log1p_small(jnp.exp(-jnp.abs(x)))


def mm_nn(a, b, out_dtype, name, add=None, add_scale=1.0):
    M, K = a.shape
    _, N = b.shape
    tm = _pick(M, (1024, 512, 256, 128))
    tn = N if N <= 1024 else _pick(N, (1408, 1024, 896, 768, 640, 512, 384, 256, 128))
    tk = K if K <= 2048 else _pick(K, (1408, 1024, 896, 768, 640, 512, 384, 256, 128))
    nk = K // tk

    def body(*refs):
        if add is None:
            a_ref, b_ref, o_ref, acc_ref = refs
        else:
            a_ref, b_ref, c_ref, o_ref, acc_ref = refs
        k = pl.program_id(2)
        part = _dot(a_ref[...], b_ref[...])

        @pl.when(k == 0)
        def _():
            acc_ref[...] = part

        @pl.when(k > 0)
        def _():
            acc_ref[...] += part

        @pl.when(k == nk - 1)
        def _():
            r = acc_ref[...]
            if add is not None:
                r = r + add_scale * c_ref[...].astype(F32)
            o_ref[...] = r.astype(o_ref.dtype)

    in_specs = [pl.BlockSpec((tm, tk), lambda i, j, k: (i, k)), pl.BlockSpec((tk, tn), lambda i, j, k: (k, j))]
    args = [a, b]
    if add is not None:
        in_specs.append(pl.BlockSpec((tm, tn), lambda i, j, k: (i, j)))
        args.append(add)
    return pl.pallas_call(
        body,
        name=name,
        grid=(M // tm, N // tn, nk),
        in_specs=in_specs,
        out_specs=pl.BlockSpec((tm, tn), lambda i, j, k: (i, j)),
        out_shape=jax.ShapeDtypeStruct((M, N), out_dtype),
        scratch_shapes=[pltpu.VMEM((tm, tn), F32)],
        compiler_params=_cparams("parallel", "parallel", "arbitrary"),
    )(*args)


def mm_tn(a, b, name):
    M, K = a.shape
    _, N = b.shape
    tm = _pick(M, (512, 256, 128))
    tk = K if K <= 1024 else _pick(K, (1408, 1024, 896, 768, 640, 512, 384, 256, 128))
    tn = N if N <= 1408 else _pick(N, (1408, 1024, 896, 768, 640, 512, 384, 256, 128))
    nm = M // tm

    def body(a_ref, b_ref, o_ref):
        m = pl.program_id(2)
        part = _dot_tn(a_ref[...], b_ref[...])

        @pl.when(m == 0)
        def _():
            o_ref[...] = part

        @pl.when(m > 0)
        def _():
            o_ref[...] += part

    return pl.pallas_call(
        body,
        name=name,
        grid=(K // tk, N // tn, nm),
        in_specs=[pl.BlockSpec((tm, tk), lambda i, j, m: (m, i)), pl.BlockSpec((tm, tn), lambda i, j, m: (m, j))],
        out_specs=pl.BlockSpec((tk, tn), lambda i, j, m: (i, j)),
        out_shape=jax.ShapeDtypeStruct((K, N), F32),
        compiler_params=_cparams("parallel", "parallel", "arbitrary"),
    )(a, b)


def ln_fwd(x, mix, g, b, alpha, name):
    S, D = x.shape
    tm = _pick(S, (512, 256, 128))

    def body(x_ref, mix_ref, g_ref, b_ref, y_ref, xhat_ref, rstd_ref):
        r = alpha * x_ref[...] + mix_ref[...]
        mu = jnp.mean(r, axis=-1, keepdims=True)
        xc = r - mu
        var = jnp.mean(xc * xc, axis=-1, keepdims=True)
        rstd = lax.rsqrt(var + LN_EPS)
        xhat = xc * rstd
        y_ref[...] = xhat * g_ref[...] + b_ref[...]
        xhat_ref[...] = xhat
        rstd_ref[...] = rstd

    row = pl.BlockSpec((tm, D), lambda i: (i, 0))
    vec = pl.BlockSpec((1, D), lambda i: (0, 0))
    return pl.pallas_call(
        body,
        name=name,
        grid=(S // tm,),
        in_specs=[row, row, vec, vec],
        out_specs=[row, row, pl.BlockSpec((tm, 1), lambda i: (i, 0))],
        out_shape=[jax.ShapeDtypeStruct((S, D), F32), jax.ShapeDtypeStruct((S, D), F32), jax.ShapeDtypeStruct((S, 1), F32)],
        compiler_params=_cparams("parallel"),
    )(x, mix, g, b)


def ln_bwd(dy, xhat, rstd, g, name):
    S, D = dy.shape
    tm = _pick(S, (512, 256, 128))

    def body(dy_ref, xhat_ref, rstd_ref, g_ref, dr_ref, dg_ref, db_ref):
        i = pl.program_id(0)
        dyv = dy_ref[...]
        xh = xhat_ref[...]
        dxh = dyv * g_ref[...]
        m1 = jnp.mean(dxh, axis=-1, keepdims=True)
        m2 = jnp.mean(dxh * xh, axis=-1, keepdims=True)
        dr_ref[...] = rstd_ref[...] * (dxh - m1 - xh * m2)
        dg_part = jnp.sum(dyv * xh, axis=0, keepdims=True)
        db_part = jnp.sum(dyv, axis=0, keepdims=True)

        @pl.when(i == 0)
        def _():
            dg_ref[...] = dg_part
            db_ref[...] = db_part

        @pl.when(i > 0)
        def _():
            dg_ref[...] += dg_part
            db_ref[...] += db_part

    row = pl.BlockSpec((tm, D), lambda i: (i, 0))
    vec = pl.BlockSpec((1, D), lambda i: (0, 0))
    return pl.pallas_call(
        body,
        name=name,
        grid=(S // tm,),
        in_specs=[row, row, pl.BlockSpec((tm, 1), lambda i: (i, 0)), vec],
        out_specs=[row, vec, vec],
        out_shape=[jax.ShapeDtypeStruct((S, D), F32), jax.ShapeDtypeStruct((1, D), F32), jax.ShapeDtypeStruct((1, D), F32)],
        compiler_params=_cparams("arbitrary"),
    )(dy, xhat, rstd, g)


HALO = SUBLANES


def _prev_halo_spec(tm, tc, col0):
    return pl.BlockSpec((HALO, tc), lambda i, j: (jnp.maximum(i * (tm // HALO) - 1, 0), j + col0))


def _next_halo_spec(tm, tc, col0, n_row_tiles):
    last = n_row_tiles * (tm // HALO) - 1
    return pl.BlockSpec((HALO, tc), lambda i, j: (jnp.minimum((i + 1) * (tm // HALO), last), j + col0))


def _causal_conv(ext_ref, w, n_taps, tm, row0=HALO):
    acc = None
    for k in range(n_taps):
        term = ext_ref[pl.ds(row0 - (n_taps - 1) + k, tm), :] * w[k:k + 1, :]
        acc = term if acc is None else acc + term
    return acc


def _anticausal_conv(ext_ref, w, n_taps, tm):
    acc = None
    for k in range(n_taps):
        term = ext_ref[pl.ds(n_taps - 1 - k, tm), :] * w[k:k + 1, :]
        acc = term if acc is None else acc + term
    return acc


INV_SQRT2 = 1.0 / math.sqrt(2.0)
INV_SQRT_2PI = 1.0 / math.sqrt(2.0 * math.pi)


def _gelu(g):
    return 0.5 * g * (1.0 + lax.erf(g * INV_SQRT2))


def _gelu_grad(g):
    return 0.5 * (1.0 + lax.erf(g * INV_SQRT2)) + g * jnp.exp(-0.5 * g * g) * INV_SQRT_2PI


def _silu(x):
    return x * _sigmoid(x)


def _silu_grad(x):
    s = _sigmoid(x)
    return s * (1.0 + x * (1.0 - s))


def _conv_tiles(S, C, cols):
    tm = _pick(S, (256, 128))
    for tc in (1408, 1024, 512, 256, 128):
        if C % tc == 0 and all(c % tc == 0 for c in cols):
            return tm, tc
    raise ValueError("no column tile for the conv kernels")


def conv_act_fwd(src, in_col, C, conv_w, conv_b, out_dtype, name, gate_col=None):
    S = src.shape[0]
    K = conv_w.shape[0]
    gated = gate_col is not None
    tm, tc = _conv_tiles(S, C, [in_col] + ([gate_col] if gated else []))
    c_in = in_col // tc
    c_gate = gate_col // tc if gated else 0

    def body(*refs):
        if gated:
            x_ref, xp_ref, w_ref, b_ref, u_ref, o_ref, ext_ref = refs
        else:
            x_ref, xp_ref, w_ref, b_ref, o_ref, ext_ref = refs
        i = pl.program_id(0)
        ext_ref[0:HALO] = jnp.where(i > 0, xp_ref[...], 0.0)
        ext_ref[HALO:HALO + tm] = x_ref[...]
        pre = _causal_conv(ext_ref, w_ref[...], K, tm) + b_ref[...]
        out = _gelu(pre) * u_ref[...] if gated else _silu(pre)
        o_ref[...] = out.astype(o_ref.dtype)

    in_specs = [
        pl.BlockSpec((tm, tc), lambda i, j: (i, j + c_in)),
        _prev_halo_spec(tm, tc, c_in),
        pl.BlockSpec((K, tc), lambda i, j: (0, j)),
        pl.BlockSpec((1, tc), lambda i, j: (0, j)),
    ]
    args = [src, src, conv_w, conv_b]
    if gated:
        in_specs.append(pl.BlockSpec((tm, tc), lambda i, j: (i, j + c_gate)))
        args.append(src)
    return pl.pallas_call(
        body,
        name=name,
        grid=(S // tm, C // tc),
        in_specs=in_specs,
        out_specs=pl.BlockSpec((tm, tc), lambda i, j: (i, j)),
        out_shape=jax.ShapeDtypeStruct((S, C), out_dtype),
        scratch_shapes=[pltpu.VMEM((tm + HALO, tc), F32)],
        compiler_params=_cparams("parallel", "parallel"),
    )(*args)


def conv_act_bwd(d_out, src, in_col, C, conv_w, conv_b, name, gate_col=None):
    S = src.shape[0]
    K = conv_w.shape[0]
    gated = gate_col is not None
    tm, tc = _conv_tiles(S, C, [in_col] + ([gate_col] if gated else []))
    c_in = in_col // tc
    c_gate = gate_col // tc if gated else 0
    ni = S // tm
    te = tm + HALO

    def body(*refs):
        if gated:
            (d_ref, dn_ref, x_ref, xp_ref, xn_ref, w_ref, b_ref, u_ref, un_ref,
             dx_ref, dw_ref, db_ref, du_ref, xext_ref, dext_ref) = refs
        else:
            (d_ref, dn_ref, x_ref, xp_ref, xn_ref, w_ref, b_ref,
             dx_ref, dw_ref, db_ref, xext_ref, dext_ref) = refs
        i = pl.program_id(1)
        w = w_ref[...]
        xext_ref[0:HALO] = jnp.where(i > 0, xp_ref[...], 0.0)
        xext_ref[HALO:HALO + tm] = x_ref[...]
        xext_ref[HALO + tm:HALO + te] = xn_ref[...]
        pre = _causal_conv(xext_ref, w, K, te) + b_ref[...]
        d_cur = d_ref[...].astype(F32)
        d_next = dn_ref[...].astype(F32)
        if gated:
            du_ref[...] = (d_cur * _gelu(pre[0:tm])).astype(du_ref.dtype)
            dpre_cur = d_cur * u_ref[...] * _gelu_grad(pre[0:tm])
            dpre_next = d_next * un_ref[...] * _gelu_grad(pre[tm:te])
        else:
            dpre_cur = d_cur * _silu_grad(pre[0:tm])
            dpre_next = d_next * _silu_grad(pre[tm:te])
        dext_ref[0:tm] = dpre_cur
        dext_ref[tm:te] = jnp.where(i < ni - 1, dpre_next, 0.0)
        dx_ref[...] = _anticausal_conv(dext_ref, w, K, tm).astype(dx_ref.dtype)
        dw_rows = [jnp.sum(dpre_cur * xext_ref[pl.ds(HALO - (K - 1) + k, tm), :], axis=0, keepdims=True) for k in range(K)]
        dw_part = jnp.concatenate(dw_rows + [jnp.zeros((SUBLANES - K, tc), F32)], axis=0)
        db_part = jnp.sum(dpre_cur, axis=0, keepdims=True)

        @pl.when(i == 0)
        def _():
            dw_ref[...] = dw_part
            db_ref[...] = db_part

        @pl.when(i > 0)
        def _():
            dw_ref[...] += dw_part
            db_ref[...] += db_part

    last = ni * (tm // HALO) - 1
    cur = lambda c0: pl.BlockSpec((tm, tc), lambda j, i: (i, j + c0))
    prev = lambda c0: pl.BlockSpec((HALO, tc), lambda j, i: (jnp.maximum(i * (tm // HALO) - 1, 0), j + c0))
    nxt = lambda c0: pl.BlockSpec((HALO, tc), lambda j, i: (jnp.minimum((i + 1) * (tm // HALO), last), j + c0))
    vec = lambda rows: pl.BlockSpec((rows, tc), lambda j, i: (0, j))
    in_specs = [cur(0), nxt(0), cur(c_in), prev(c_in), nxt(c_in), vec(K), vec(1)]
    args = [d_out, d_out, src, src, src, conv_w, conv_b]
    out_specs = [cur(0), vec(SUBLANES), vec(1)]
    out_shape = [jax.ShapeDtypeStruct((S, C), BF16), jax.ShapeDtypeStruct((SUBLANES, C), F32), jax.ShapeDtypeStruct((1, C), F32)]
    if gated:
        in_specs += [cur(c_gate), nxt(c_gate)]
        args += [src, src]
        out_specs.append(cur(0))
        out_shape.append(jax.ShapeDtypeStruct((S, C), BF16))
    outs = pl.pallas_call(
        body,
        name=name,
        grid=(C // tc, ni),
        in_specs=in_specs,
        out_specs=out_specs,
        out_shape=out_shape,
        scratch_shapes=[pltpu.VMEM((tm + 2 * HALO, tc), F32), pltpu.VMEM((te, tc), F32)],
        compiler_params=_cparams("parallel", "arbitrary"),
    )(*args)
    return outs[0], (outs[3] if gated else None), outs[1][:K], outs[2]


def ple_fwd(x2, zg, pp, bg, name):
    S, D = x2.shape
    tm = _pick(S, (512, 256, 128))

    def body(x_ref, z_ref, p_ref, b_ref, o_ref):
        o_ref[...] = x_ref[...] + _sigmoid(z_ref[...] + b_ref[...]) * p_ref[...]

    row = pl.BlockSpec((tm, D), lambda i: (i, 0))
    return pl.pallas_call(
        body, name=name, grid=(S // tm,), in_specs=[row, row, row, pl.BlockSpec((1, D), lambda i: (0, 0))], out_specs=row,
        out_shape=jax.ShapeDtypeStruct((S, D), F32), compiler_params=_cparams("parallel"),
    )(x2, zg, pp, bg)


def ple_bwd(dx3, zg, pp, bg, name):
    S, D = dx3.shape
    tm = _pick(S, (512, 256, 128))

    def body(d_ref, z_ref, p_ref, b_ref, dz_ref, dp_ref, db_ref):
        i = pl.program_id(0)
        d = d_ref[...]
        gate = _sigmoid(z_ref[...] + b_ref[...])
        dz = d * p_ref[...] * gate * (1.0 - gate)
        dz_ref[...] = dz.astype(dz_ref.dtype)
        dp_ref[...] = (d * gate).astype(dp_ref.dtype)
        part = jnp.sum(dz, axis=0, keepdims=True)

        @pl.when(i == 0)
        def _():
            db_ref[...] = part

        @pl.when(i > 0)
        def _():
            db_ref[...] += part

    row = pl.BlockSpec((tm, D), lambda i: (i, 0))
    vec = pl.BlockSpec((1, D), lambda i: (0, 0))
    return pl.pallas_call(
        body, name=name, grid=(S // tm,), in_specs=[row, row, row, vec], out_specs=[row, row, vec],
        out_shape=[jax.ShapeDtypeStruct((S, D), BF16), jax.ShapeDtypeStruct((S, D), BF16), jax.ShapeDtypeStruct((1, D), F32)],
        compiler_params=_cparams("arbitrary"),
    )(dx3, zg, pp, bg)


def loss_head(y, target, name):
    S, D = y.shape
    tm = _pick(S, (512, 256, 128))

    def body(y_ref, t_ref, loss_ref, dy_ref, acc_ref):
        i = pl.program_id(0)
        err = y_ref[...] - t_ref[...]
        dy_ref[...] = err * (1.0 / D)
        part = jnp.sum(err * err, axis=0, keepdims=True)

        @pl.when(i == 0)
        def _():
            acc_ref[...] = part

        @pl.when(i > 0)
        def _():
            acc_ref[...] += part

        @pl.when(i == pl.num_programs(0) - 1)
        def _():
            loss_ref[...] = jnp.zeros((1, LANES), F32) + (0.5 / D) * jnp.sum(acc_ref[...])

    row = pl.BlockSpec((tm, D), lambda i: (i, 0))
    return pl.pallas_call(
        body, name=name, grid=(S // tm,), in_specs=[row, row],
        out_specs=[pl.BlockSpec((1, LANES), lambda i: (0, 0)), row],
        out_shape=[jax.ShapeDtypeStruct((1, LANES), F32), jax.ShapeDtypeStruct((S, D), F32)],
        scratch_shapes=[pltpu.VMEM((1, D), F32)],
        compiler_params=_cparams("arbitrary"),
    )(y, target)


ATTN_TILE = 512
ATTN_SCALE = 1.0 / math.sqrt(HEAD_DIM)


def _attn_tile(S):
    return _pick(S, (ATTN_TILE, 256, 128))


def fox_gate_fwd(zt, bf, name):
    H, S = zt.shape
    tl = _pick(S, (512, 256, 128))

    def body(z_ref, b_ref, c_ref, carry_ref):
        i = pl.program_id(0)

        @pl.when(i == 0)
        def _():
            carry_ref[...] = jnp.zeros_like(carry_ref)

        z = z_ref[...] + b_ref[...]
        logf = jnp.minimum(z, 0.0) - _log1p_small(jnp.exp(-jnp.abs(z)))
        r = lax.broadcasted_iota(jnp.int32, (tl, tl), 0)
        c = lax.broadcasted_iota(jnp.int32, (tl, tl), 1)
        upper = (r <= c).astype(F32)
        cum = jnp.dot(logf, upper, precision=HIGHEST, preferred_element_type=F32) + carry_ref[...]
        c_ref[...] = cum
        carry_ref[...] = cum[:, tl - 1:tl]

    return pl.pallas_call(
        body, name=name, grid=(S // tl,),
        in_specs=[pl.BlockSpec((H, tl), lambda i: (0, i)), pl.BlockSpec((H, 1), lambda i: (0, 0))],
        out_specs=pl.BlockSpec((H, tl), lambda i: (0, i)),
        out_shape=jax.ShapeDtypeStruct((H, S), F32),
        scratch_shapes=[pltpu.VMEM((H, 1), F32)],
        compiler_params=_cparams("arbitrary"),
    )(zt, bf)


def fox_gate_bwd(dc_q, dc_k, zt, bf, name):
    H, S = zt.shape
    tl = _pick(S, (512, 256, 128))
    nt = S // tl

    def body(dcq_ref, dck_ref, z_ref, b_ref, dz_ref, db_ref, carry_ref):
        i = pl.program_id(0)

        @pl.when(i == 0)
        def _():
            carry_ref[...] = jnp.zeros_like(carry_ref)
            db_ref[...] = jnp.zeros_like(db_ref)

        r = lax.broadcasted_iota(jnp.int32, (tl, tl), 0)
        c = lax.broadcasted_iota(jnp.int32, (tl, tl), 1)
        lower = (r >= c).astype(F32)
        dc = dcq_ref[...] + dck_ref[...]
        suffix = jnp.dot(dc, lower, precision=HIGHEST, preferred_element_type=F32) + carry_ref[...]
        carry_ref[...] = suffix[:, 0:1]
        dz = suffix * _sigmoid(-(z_ref[...] + b_ref[...]))
        dz_ref[...] = dz
        db_ref[...] += jnp.sum(dz, axis=1, keepdims=True)

    rev = pl.BlockSpec((H, tl), lambda i: (0, nt - 1 - i))
    return pl.pallas_call(
        body, name=name, grid=(nt,),
        in_specs=[rev, rev, rev, pl.BlockSpec((H, 1), lambda i: (0, 0))],
        out_specs=[rev, pl.BlockSpec((H, 1), lambda i: (0, 0))],
        out_shape=[jax.ShapeDtypeStruct((H, S), F32), jax.ShapeDtypeStruct((H, 1), F32)],
        scratch_shapes=[pltpu.VMEM((H, 1), F32)],
        compiler_params=_cparams("arbitrary"),
    )(dc_q, dc_k, zt, bf)


ATTN_SUB = 2


def flash_fwd(q, k, v_ones, c_col, c_row, name):
    H, S, Dh = q.shape
    T = _attn_tile(S)
    NT = S // T
    TS = T // ATTN_SUB

    def body(q_ref, k_ref, v_ref, cq_ref, ck_ref, o_ref, lse_ref, m_ref, acc_ref):
        i = pl.program_id(1)
        qs = q_ref[0] * ATTN_SCALE
        c_ref = cq_ref[0, 0:1, :]
        m_ref[...] = jnp.full_like(m_ref, NEG_INF)
        acc_ref[...] = jnp.zeros_like(acc_ref)

        def process(off, width, masked, ckj):
            kj = k_ref[0, pl.ds(off, width), :]
            vj = v_ref[0, pl.ds(off, width), :]
            ss = [_dot_nt(qs[u * TS:(u + 1) * TS], kj) - ckj for u in range(ATTN_SUB)]
            for u, s in enumerate(ss):
                rows = slice(u * TS, (u + 1) * TS)
                if masked:
                    r = lax.broadcasted_iota(jnp.int32, (TS, width), 0) + u * TS
                    c = lax.broadcasted_iota(jnp.int32, (TS, width), 1)
                    s = jnp.where(r >= c, s, NEG_INF)
                m_prev = m_ref[rows]
                m_new = jnp.maximum(m_prev, jnp.max(s, axis=1, keepdims=True))
                p = jnp.exp(s - jnp.tile(m_new, (1, width // LANES)))
                acc_ref[rows] = jnp.exp(m_prev - m_new) * acc_ref[rows] + _dot(p, vj)
                m_ref[rows] = m_new

        def pair_body(j2, carry):
            ckj = jnp.concatenate([ck_ref[0, 2 * j2], ck_ref[0, 2 * j2 + 1]], axis=1) - c_ref
            process(pl.multiple_of(j2 * 2 * T, 2 * T), 2 * T, False, ckj)
            return carry

        lax.fori_loop(0, i // 2, pair_body, 0)

        @pl.when(i % 2 == 1)
        def _():
            process(pl.multiple_of((i - 1) * T, T), T, False, ck_ref[0, i - 1] - c_ref)

        process(pl.multiple_of(i * T, T), T, True, ck_ref[0, i] - c_ref)
        acc = acc_ref[...]
        l = acc[:, Dh:Dh + 1]
        o_ref[0] = (acc[:, 0:Dh] / l).astype(o_ref.dtype)
        lse_ref[0] = m_ref[:, 0:1] + jnp.log(l) + (cq_ref[0] - c_ref)

    tile = pl.BlockSpec((1, T, Dh), lambda h, i: (h, i, 0))
    whole = pl.BlockSpec((1, S, Dh), lambda h, i: (h, 0, 0))
    whole_v = pl.BlockSpec((1, S, 2 * Dh), lambda h, i: (h, 0, 0))
    col = pl.BlockSpec((1, T, 1), lambda h, i: (h, i, 0))
    rows = pl.BlockSpec((1, NT, 1, T), lambda h, i: (h, 0, 0, 0))
    return pl.pallas_call(
        body, name=name, grid=(H, NT),
        in_specs=[tile, whole, whole_v, col, rows],
        out_specs=[tile, col],
        out_shape=[jax.ShapeDtypeStruct((H, S, Dh), BF16), jax.ShapeDtypeStruct((H, S, 1), F32)],
        scratch_shapes=[pltpu.VMEM((T, LANES), F32), pltpu.VMEM((T, 2 * Dh), F32)],
        compiler_params=_cparams("parallel", "parallel"),
    )(q, k, v_ones, c_col, c_row)


def attn_delta(do, o, name):
    H, S, Dh = o.shape
    T = _attn_tile(S)

    def body(do_ref, o_ref, d_ref):
        prod = do_ref[0].astype(F32) * o_ref[0].astype(F32)
        ones = jnp.ones((SUBLANES, Dh), F32)
        d_ref[0, 0] = lax.dot_general(ones, prod, NT_DIMS, precision=HIGHEST, preferred_element_type=F32)[0:1]

    tile = pl.BlockSpec((1, T, Dh), lambda h, i: (h, i, 0))
    return pl.pallas_call(
        body, name=name, grid=(H, S // T), in_specs=[tile, tile],
        out_specs=pl.BlockSpec((1, 1, 1, T), lambda h, i: (h, i, 0, 0)),
        out_shape=jax.ShapeDtypeStruct((H, S // T, 1, T), F32),
        compiler_params=_cparams("parallel", "parallel"),
    )(do, o)


def flash_bwd(q, k, v, do, lse_row, delta_row, c_col, c_row, name):
    H, S, Dh = q.shape
    T = _attn_tile(S)
    NT = S // T
    TS = T // ATTN_SUB

    def body(q_ref, k_ref, v_ref, do_ref, lse_ref, delta_ref, ck_ref, cq_ref,
             dk_ref, dv_ref, dck_ref, dq_ref, dcq_ref, dk_acc, dv_acc, dck_acc):
        j = pl.program_id(1)

        @pl.when(j == 0)
        def _():
            dq_ref[...] = jnp.zeros_like(dq_ref)
            dcq_ref[...] = jnp.zeros_like(dcq_ref)

        kj = k_ref[0]
        vj = v_ref[0]
        ck = ck_ref[0]
        dk_acc[...] = jnp.zeros_like(dk_acc)
        dv_acc[...] = jnp.zeros_like(dv_acc)
        dck_acc[...] = jnp.zeros_like(dck_acc)

        def process(i, n_tiles, masked):
            width = n_tiles * T
            off = pl.multiple_of(i * T, T)
            qi = q_ref[0, pl.ds(off, width), :]
            doi = do_ref[0, pl.ds(off, width), :]
            row = lambda ref: jnp.concatenate([ref[0, i + t] for t in range(n_tiles)], axis=1) if n_tiles > 1 else ref[0, i]
            brow = row(cq_ref) - row(lse_ref)
            delta = row(delta_ref)
            qs = qi * ATTN_SCALE
            halves = [slice(u * TS, (u + 1) * TS) for u in range(ATTN_SUB)]
            sts = [_dot_nt(kj[h], qs) for h in halves]
            dpts = [_dot_nt(vj[h], doi) for h in halves]
            dq_parts, dcq_parts = [], []
            for u, h in enumerate(halves):
                pt = jnp.exp(sts[u] + (brow - ck[h]))
                if masked:
                    r = lax.broadcasted_iota(jnp.int32, (TS, width), 0) + u * TS
                    c = lax.broadcasted_iota(jnp.int32, (TS, width), 1)
                    pt = jnp.where(c >= r, pt, 0.0)
                dst = pt * (dpts[u] - delta)
                dv_acc[h] += _dot(pt, doi)
                dk_acc[h] += _dot(dst, qi)
                dck_acc[h] -= jnp.sum(dst, axis=1, keepdims=True)
                dq_parts.append(_dot_tn(dst, kj[h]))
                dcq_parts.append(jnp.sum(dst, axis=0, keepdims=True))
            dq_ref[0, pl.ds(off, width), :] += sum(dq_parts) * ATTN_SCALE
            dcq = sum(dcq_parts)
            for t in range(n_tiles):
                dcq_ref[0, i + t] += dcq[:, t * T:(t + 1) * T]

        process(j, 1, True)
        n_off = NT - 1 - j

        @pl.when(n_off % 2 == 1)
        def _():
            process(j + 1, 1, False)

        start = j + 1 + n_off % 2

        def pair_body(t, carry):
            process(start + 2 * t, 2, False)
            return carry

        lax.fori_loop(0, n_off // 2, pair_body, 0)
        dk_ref[0] = (dk_acc[...] * ATTN_SCALE).astype(dk_ref.dtype)
        dv_ref[0] = dv_acc[...].astype(dv_ref.dtype)
        dck_ref[0] = dck_acc[...]

    tile = pl.BlockSpec((1, T, Dh), lambda h, j: (h, j, 0))
    whole = pl.BlockSpec((1, S, Dh), lambda h, j: (h, 0, 0))
    col = pl.BlockSpec((1, T, 1), lambda h, j: (h, j, 0))
    rows = pl.BlockSpec((1, NT, 1, T), lambda h, j: (h, 0, 0, 0))
    return pl.pallas_call(
        body, name=name, grid=(H, NT),
        in_specs=[whole, tile, tile, whole, rows, rows, col, rows],
        out_specs=[tile, tile, col, whole, rows],
        out_shape=[jax.ShapeDtypeStruct((H, S, Dh), BF16), jax.ShapeDtypeStruct((H, S, Dh), BF16),
                   jax.ShapeDtypeStruct((H, S, 1), F32), jax.ShapeDtypeStruct((H, S, Dh), F32),
                   jax.ShapeDtypeStruct((H, NT, 1, T), F32)],
        scratch_shapes=[pltpu.VMEM((T, Dh), F32), pltpu.VMEM((T, Dh), F32), pltpu.VMEM((T, 1), F32)],
        compiler_params=_cparams("parallel", "arbitrary"),
    )(q, k, v, do, lse_row, delta_row, c_col, c_row)


PAIR = 2 * HEAD_DIM


def _tri(n, lower):
    r = lax.broadcasted_iota(jnp.int32, (n, n), 0)
    c = lax.broadcasted_iota(jnp.int32, (n, n), 1)
    return (r >= c) if lower else (r <= c)


def _ssd_chunk_scalars(dt_raw, bias, a_log):
    Q = dt_raw.shape[0]
    dt = _softplus(dt_raw + bias)
    A = -jnp.exp(a_log)
    cum = jnp.dot(_tri(Q, True).astype(F32), dt * A, precision=HIGHEST, preferred_element_type=F32)
    tot = cum[Q - 1:Q, :]
    return dt, A, cum, tot


def _lane_pair(lo_mask, v, h0):
    return jnp.where(lo_mask, v[:, h0:h0 + 1], v[:, h0 + 1:h0 + 2])


def ssd_scan_fwd(xbc, proj, dt_col, dt_bias, a_log, d_inner, name):
    S, W = xbc.shape
    Q, N, G = SSM_CHUNK, SSM_STATE, SSM_GROUPS
    nc = S // Q
    n_pairs = d_inner // PAIR
    pairs_per_group = n_pairs // G
    GN = G * N

    def body(xbc_ref, dt_ref, bias_ref, alog_ref, y_ref, sin_ref, st_ref):
        c = pl.program_id(0)

        @pl.when(c == 0)
        def _():
            st_ref[...] = jnp.zeros_like(st_ref)

        sin_ref[0] = st_ref[...]
        dt, A, cum, tot = _ssd_chunk_scalars(dt_ref[...], bias_ref[...], alog_ref[...])
        cum_t = cum.T
        dt_t = dt.T
        ecum = jnp.exp(cum)
        wend = jnp.exp(tot - cum) * dt
        etot = jnp.exp(tot)
        lower = _tri(Q, True)
        lo = lax.broadcasted_iota(jnp.int32, (Q, PAIR), 1) < HEAD_DIM
        lo_row = lax.broadcasted_iota(jnp.int32, (1, PAIR), 1) < HEAD_DIM
        for g in range(G):
            Bg = xbc_ref[:, d_inner + g * N:d_inner + (g + 1) * N]
            Cg = xbc_ref[:, d_inner + GN + g * N:d_inner + GN + (g + 1) * N]
            CB = _dot_nt(Cg, Bg)
            for pp in range(pairs_per_group):
                pr = g * pairs_per_group + pp
                h0 = 2 * pr
                xw = xbc_ref[:, pr * PAIR:(pr + 1) * PAIR]
                ys = []
                for h in (h0, h0 + 1):
                    L = jnp.where(lower, jnp.exp(cum[:, h:h + 1] - cum_t[h:h + 1, :]), 0.0)
                    ys.append(_dot(CB * L * dt_t[h:h + 1, :], xw))
                st = st_ref[pr]
                y_inter = _dot(Cg, st) * _lane_pair(lo, ecum, h0)
                y_ref[:, pr * PAIR:(pr + 1) * PAIR] = jnp.where(lo, ys[0], ys[1]) + y_inter
                st_ref[pr] = _lane_pair(lo_row, etot, h0) * st + _dot_tn(Bg, xw * _lane_pair(lo, wend, h0))

    return pl.pallas_call(
        body, name=name, grid=(nc,),
        in_specs=[pl.BlockSpec((Q, W), lambda c: (c, 0)), pl.BlockSpec((Q, LANES), lambda c: (c, dt_col // LANES)),
                  pl.BlockSpec((1, LANES), lambda c: (0, 0)), pl.BlockSpec((1, LANES), lambda c: (0, 0))],
        out_specs=[pl.BlockSpec((Q, d_inner), lambda c: (c, 0)), pl.BlockSpec((1, n_pairs, N, PAIR), lambda c: (c, 0, 0, 0))],
        out_shape=[jax.ShapeDtypeStruct((S, d_inner), F32), jax.ShapeDtypeStruct((nc, n_pairs, N, PAIR), F32)],
        scratch_shapes=[pltpu.VMEM((n_pairs, N, PAIR), F32)],
        compiler_params=_cparams("arbitrary"),
    )(xbc, proj, dt_bias, a_log)


def ssd_scan_bwd(dy, dskip, xbc, proj, dt_col, dt_bias, a_log, states, d_inner, name):
    S, W = xbc.shape
    Q, N, G = SSM_CHUNK, SSM_STATE, SSM_GROUPS
    nc = S // Q
    n_pairs = d_inner // PAIR
    pairs_per_group = n_pairs // G
    GN = G * N

    def body(dy_ref, dskip_ref, xbc_ref, dt_ref, bias_ref, alog_ref, sin_ref,
             dxbc_ref, ddt_ref, dalog_ref, dbias_ref, dst_ref, rows_cum_ref, rows_dt_ref):
        step = pl.program_id(0)

        @pl.when(step == 0)
        def _():
            dst_ref[...] = jnp.zeros_like(dst_ref)
            dalog_ref[...] = jnp.zeros_like(dalog_ref)
            dbias_ref[...] = jnp.zeros_like(dbias_ref)

        rows_cum_ref[...] = jnp.zeros_like(rows_cum_ref)
        rows_dt_ref[...] = jnp.zeros_like(rows_dt_ref)
        dt_raw = dt_ref[...]
        bias = bias_ref[...]
        dt, A, cum, tot = _ssd_chunk_scalars(dt_raw, bias, alog_ref[...])
        cum_t = cum.T
        dt_t = dt.T
        ecum = jnp.exp(cum)
        eend = jnp.exp(tot - cum)
        wend = eend * dt
        etot = jnp.exp(tot)
        lower = _tri(Q, True)
        lane = lax.broadcasted_iota(jnp.int32, (Q, LANES), 1)
        lo = lane < HEAD_DIM
        lo_row = lax.broadcasted_iota(jnp.int32, (1, PAIR), 1) < HEAD_DIM
        lo_st = lax.broadcasted_iota(jnp.int32, (N, PAIR), 1) < HEAD_DIM
        last_row = lax.broadcasted_iota(jnp.int32, (Q, LANES), 0) == Q - 1
        dcum = jnp.zeros((Q, LANES), F32)
        ddt = jnp.zeros((Q, LANES), F32)
        for g in range(G):
            Bg = xbc_ref[:, d_inner + g * N:d_inner + (g + 1) * N]
            Cg = xbc_ref[:, d_inner + GN + g * N:d_inner + GN + (g + 1) * N]
            CB = _dot_nt(Cg, Bg)
            dCB = jnp.zeros((Q, Q), F32)
            dBg = jnp.zeros((Q, N), F32)
            dCg = jnp.zeros((Q, N), F32)
            for pp in range(pairs_per_group):
                pr = g * pairs_per_group + pp
                h0 = 2 * pr
                xw = xbc_ref[:, pr * PAIR:(pr + 1) * PAIR]
                dyp = dy_ref[:, pr * PAIR:(pr + 1) * PAIR]
                st = sin_ref[0, pr]
                dst = dst_ref[pr]
                ecum_p = _lane_pair(lo, ecum, h0)
                wend_p = _lane_pair(lo, wend, h0)
                etot_p = _lane_pair(lo_row, etot, h0)
                y2 = _dot(Cg, st)
                dye = dyp * ecum_p
                dCg = dCg + _dot_nt(dye, st)
                t1 = dye * y2
                bds = _dot(Bg, dst)
                qv = xw * bds
                dBg = dBg + _dot_nt(xw * wend_p, dst)
                sdot = dst * st
                dx = wend_p * bds
                for h, half, half_st in ((h0, lo, lo_st), (h0 + 1, ~lo, ~lo_st)):
                    sel = lane == h
                    t1_h = jnp.sum(jnp.where(half, t1, 0.0), axis=1, keepdims=True)
                    q_h = jnp.sum(jnp.where(half, qv, 0.0), axis=1, keepdims=True)
                    wq = wend[:, h:h + 1] * q_h
                    dtot_h = etot[:, h:h + 1] * jnp.sum(jnp.where(half_st, sdot, 0.0)) + jnp.sum(wq)
                    dcum = dcum + jnp.where(sel, t1_h - wq, 0.0) + jnp.where(sel & last_row, dtot_h, 0.0)
                    ddt = ddt + jnp.where(sel, eend[:, h:h + 1] * q_h, 0.0)
                    dt_row = dt_t[h:h + 1, :]
                    L = jnp.where(lower, jnp.exp(cum[:, h:h + 1] - cum_t[h:h + 1, :]), 0.0)
                    dyh = jnp.where(half, dyp, 0.0)
                    dM = _dot_nt(dyh, xw)
                    CBL = CB * L
                    Gp = dM * CBL
                    Gm = Gp * dt_row
                    rows_dt_ref[h:h + 1, :] = jnp.sum(Gp, axis=0, keepdims=True)
                    rows_cum_ref[h:h + 1, :] = -jnp.sum(Gm, axis=0, keepdims=True)
                    dcum = dcum + jnp.where(sel, jnp.sum(Gm, axis=1, keepdims=True), 0.0)
                    dCB = dCB + dM * L * dt_row
                    dx = dx + _dot_tn(CBL * dt_row, dyh)
                dxbc_ref[:, pr * PAIR:(pr + 1) * PAIR] = dx + dskip_ref[:, pr * PAIR:(pr + 1) * PAIR]
                dst_ref[pr] = _dot_tn(Cg, dye) + etot_p * dst
            dxbc_ref[:, d_inner + g * N:d_inner + (g + 1) * N] = dBg + _dot_tn(dCB, Cg)
            dxbc_ref[:, d_inner + GN + g * N:d_inner + GN + (g + 1) * N] = dCg + _dot(dCB, Bg)
        dcum = dcum + rows_cum_ref[...].T
        ddt = ddt + rows_dt_ref[...].T
        da = jnp.dot(_tri(Q, False).astype(F32), dcum, precision=HIGHEST, preferred_element_type=F32)
        ddt = ddt + da * A
        ddt_raw = ddt * _sigmoid(dt_raw + bias)
        ddt_ref[...] = ddt_raw
        dalog_ref[...] += jnp.sum(da * dt, axis=0, keepdims=True) * A
        dbias_ref[...] += jnp.sum(ddt_raw, axis=0, keepdims=True)

    rev = lambda width, col: pl.BlockSpec((Q, width), lambda s: (nc - 1 - s, col))
    vec = pl.BlockSpec((1, LANES), lambda s: (0, 0))
    return pl.pallas_call(
        body, name=name, grid=(nc,),
        in_specs=[rev(d_inner, 0), rev(d_inner, 0), rev(W, 0), rev(LANES, dt_col // LANES), vec, vec,
                  pl.BlockSpec((1, n_pairs, N, PAIR), lambda s: (nc - 1 - s, 0, 0, 0))],
        out_specs=[rev(W, 0), rev(LANES, 0), vec, vec],
        out_shape=[jax.ShapeDtypeStruct((S, W), F32), jax.ShapeDtypeStruct((S, LANES), F32),
                   jax.ShapeDtypeStruct((1, LANES), F32), jax.ShapeDtypeStruct((1, LANES), F32)],
        scratch_shapes=[pltpu.VMEM((n_pairs, N, PAIR), F32), pltpu.VMEM((LANES, Q), F32), pltpu.VMEM((LANES, Q), F32)],
        compiler_params=_cparams("arbitrary"),
    )(dy, dskip, xbc, proj, dt_bias, a_log, states)


def ssd_gate_fwd(y, xbc, proj, d_skip, norm_w, d_inner, name):
    S = y.shape[0]
    tm = _pick(S, (256, 128))
    gs = d_inner // SSM_GROUPS

    def body(y_ref, x_ref, z_ref, d_ref, w_ref, o_ref):
        for g in range(SSM_GROUPS):
            sl = slice(g * gs, (g + 1) * gs)
            y2 = (y_ref[:, sl] + d_ref[:, sl] * x_ref[:, sl]) * _silu(z_ref[:, sl])
            r = lax.rsqrt(jnp.mean(y2 * y2, axis=-1, keepdims=True) + RMS_EPS)
            o_ref[:, sl] = (y2 * r * w_ref[:, sl]).astype(o_ref.dtype)

    row = pl.BlockSpec((tm, d_inner), lambda i: (i, 0))
    vec = pl.BlockSpec((1, d_inner), lambda i: (0, 0))
    return pl.pallas_call(
        body, name=name, grid=(S // tm,), in_specs=[row, row, row, vec, vec], out_specs=row,
        out_shape=jax.ShapeDtypeStruct((S, d_inner), BF16), compiler_params=_cparams("parallel"),
    )(y, xbc, proj, d_skip, norm_w)


def ssd_gate_bwd(dyn, y, xbc, proj, d_skip, norm_w, d_inner, name):
    S = y.shape[0]
    tm = _pick(S, (256, 128))
    gs = d_inner // SSM_GROUPS

    def body(dyn_ref, y_ref, x_ref, z_ref, d_ref, w_ref, dy_ref, dskip_ref, dz_ref, dw_ref, dd_ref):
        i = pl.program_id(0)

        @pl.when(i == 0)
        def _():
            dw_ref[...] = jnp.zeros_like(dw_ref)
            dd_ref[...] = jnp.zeros_like(dd_ref)

        for g in range(SSM_GROUPS):
            sl = slice(g * gs, (g + 1) * gs)
            z = z_ref[:, sl]
            x = x_ref[:, sl]
            sz = _silu(z)
            ysum = y_ref[:, sl] + d_ref[:, sl] * x
            y2 = ysum * sz
            r = lax.rsqrt(jnp.mean(y2 * y2, axis=-1, keepdims=True) + RMS_EPS)
            dyn = dyn_ref[:, sl]
            a = dyn * w_ref[:, sl]
            dy2 = r * a - y2 * (r * r * r) * jnp.mean(a * y2, axis=-1, keepdims=True)
            dysum = dy2 * sz
            dy_ref[:, sl] = dysum
            dskip_ref[:, sl] = dysum * d_ref[:, sl]
            dz_ref[:, sl] = (dy2 * ysum * _silu_grad(z)).astype(dz_ref.dtype)
            dw_ref[:, sl] += jnp.sum(dyn * y2 * r, axis=0, keepdims=True)
            dd_ref[:, sl] += jnp.sum(dysum * x, axis=0, keepdims=True)

    row = pl.BlockSpec((tm, d_inner), lambda i: (i, 0))
    vec = pl.BlockSpec((1, d_inner), lambda i: (0, 0))
    return pl.pallas_call(
        body, name=name, grid=(S // tm,), in_specs=[row, row, row, row, vec, vec], out_specs=[row, row, row, vec, vec],
        out_shape=[jax.ShapeDtypeStruct((S, d_inner), F32), jax.ShapeDtypeStruct((S, d_inner), F32),
                   jax.ShapeDtypeStruct((S, d_inner), BF16), jax.ShapeDtypeStruct((1, d_inner), F32),
                   jax.ShapeDtypeStruct((1, d_inner), F32)],
        compiler_params=_cparams("arbitrary"),
    )(dyn, y, xbc, proj, d_skip, norm_w)


def _pad_to(a, axis, mult=LANES):
    n = a.shape[axis]
    extra = (-n) % mult
    if extra == 0:
        return a
    widths = [(0, 0)] * a.ndim
    widths[axis] = (0, extra)
    return jnp.pad(a, widths)


def _attn_fwd(x, w_in, b_f, w_out, tag):
    S, D = x.shape
    H = D // HEAD_DIM
    T = _attn_tile(S)
    proj = mm_nn(x, _pad_to(w_in, 1), F32, f"{tag}_proj")
    qkv = proj[:, :3 * D].astype(BF16).reshape(S, 3, H, HEAD_DIM).transpose(1, 2, 0, 3)
    zt = proj[:, 3 * D:3 * D + H].T
    bf = b_f.reshape(H, 1)
    c = fox_gate_fwd(zt, bf, f"{tag}_gate")
    c_col, c_row = c.reshape(H, S, 1), c.reshape(H, S // T, 1, T)
    v_ones = jnp.concatenate([qkv[2], jnp.ones_like(qkv[2])], axis=-1)
    o, lse = flash_fwd(qkv[0], qkv[1], v_ones, c_col, c_row, f"{tag}_flash")
    o_flat = o.transpose(1, 0, 2).reshape(S, D)
    mix = mm_nn(o_flat, w_out, F32, f"{tag}_out")
    return mix, (qkv, zt, bf, c_col, c_row, o, lse, o_flat)


def _attn_bwd(x, dmix, dx_add, alpha, w_in, w_out, saved, tag):
    S, D = x.shape
    H = D // HEAD_DIM
    T = _attn_tile(S)
    qkv, zt, bf, c_col, c_row, o, lse, o_flat = saved
    g_w_out = mm_tn(o_flat, dmix, f"{tag}_gwout")
    do = mm_nn(dmix, w_out.T, BF16, f"{tag}_do").reshape(S, H, HEAD_DIM).transpose(1, 0, 2)
    delta = attn_delta(do, o, f"{tag}_delta")
    dk, dv, dc_k, dq, dc_q = flash_bwd(qkv[0], qkv[1], qkv[2], do, lse.reshape(H, S // T, 1, T), delta, c_col, c_row,
                                       f"{tag}_flash_bwd")
    dzt, dbf = fox_gate_bwd(dc_q.reshape(H, S), dc_k.reshape(H, S), zt, bf, f"{tag}_gate_bwd")
    dqkv = jnp.stack([dq.astype(BF16), dk, dv]).transpose(2, 0, 1, 3).reshape(S, 3 * D)
    dzf = _pad_to(dzt.T, 1)
    g_w_in = jnp.concatenate([mm_tn(x, dqkv, f"{tag}_gwqkv"), mm_tn(x, dzf, f"{tag}_gwf")[:, :H]], axis=1)
    w_in_t = w_in.T
    dx = mm_nn(dqkv, w_in_t[:3 * D], F32, f"{tag}_dx_qkv", add=dx_add, add_scale=alpha)
    dx = mm_nn(dzf, _pad_to(w_in_t[3 * D:], 0), F32, f"{tag}_dx_f", add=dx)
    return dx, (g_w_in, dbf.reshape(H), g_w_out)


def _ssm_dims(D):
    d_inner = 2 * D
    gn = SSM_GROUPS * SSM_STATE
    return d_inner, d_inner + 2 * gn, d_inner // HEAD_DIM


def _ssm_fwd(x, w_in, conv_w, conv_b, dt_bias, a_log, d_skip, norm_w, w_out, tag):
    S, D = x.shape
    DI, XBC, HS = _ssm_dims(D)
    dt_col = DI + XBC
    proj = mm_nn(x, _pad_to(w_in, 1), F32, f"{tag}_proj")
    conv_b = conv_b.reshape(1, XBC)
    xbc = conv_act_fwd(proj, DI, XBC, conv_w, conv_b, F32, f"{tag}_conv")
    dt_bias_p = _pad_to(dt_bias.reshape(1, HS), 1)
    a_log_p = _pad_to(a_log.reshape(1, HS), 1)
    y, states = ssd_scan_fwd(xbc, proj, dt_col, dt_bias_p, a_log_p, DI, f"{tag}_scan")
    d_vec = jnp.repeat(d_skip, HEAD_DIM).reshape(1, DI)
    norm_w = norm_w.reshape(1, DI)
    yn = ssd_gate_fwd(y, xbc, proj, d_vec, norm_w, DI, f"{tag}_gate")
    mix = mm_nn(yn, w_out, F32, f"{tag}_out")
    return mix, (proj, xbc, conv_b, dt_bias_p, a_log_p, y, states, d_vec, norm_w, yn)


def _ssm_bwd(x, dmix, dx_add, alpha, w_in, conv_w, w_out, saved, tag):
    S, D = x.shape
    DI, XBC, HS = _ssm_dims(D)
    dt_col = DI + XBC
    proj, xbc, conv_b, dt_bias_p, a_log_p, y, states, d_vec, norm_w, yn = saved
    g_w_out = mm_tn(yn, dmix, f"{tag}_gwout")
    dyn = mm_nn(dmix, w_out.T, F32, f"{tag}_dyn")
    dy, dskip, dz, g_norm_w, g_dvec = ssd_gate_bwd(dyn, y, xbc, proj, d_vec, norm_w, DI, f"{tag}_gate_bwd")
    dxbc, ddt_raw, g_a_log, g_dt_bias = ssd_scan_bwd(dy, dskip, xbc, proj, dt_col, dt_bias_p, a_log_p, states, DI,
                                                     f"{tag}_scan_bwd")
    dxbc_raw, _, g_conv_w, g_conv_b = conv_act_bwd(dxbc, proj, DI, XBC, conv_w, conv_b, f"{tag}_conv_bwd")
    g_w_in = jnp.concatenate([mm_tn(x, dz, f"{tag}_gwz"), mm_tn(x, dxbc_raw, f"{tag}_gwxbc"),
                              mm_tn(x, ddt_raw, f"{tag}_gwdt")[:, :HS]], axis=1)
    w_in_t = w_in.T
    dx = mm_nn(dz, w_in_t[:DI], F32, f"{tag}_dx_z", add=dx_add, add_scale=alpha)
    dx = mm_nn(dxbc_raw, w_in_t[DI:dt_col], F32, f"{tag}_dx_xbc", add=dx)
    dx = mm_nn(ddt_raw, _pad_to(w_in_t[dt_col:], 0), F32, f"{tag}_dx_dt", add=dx)
    g_d = g_dvec.reshape(HS, HEAD_DIM).sum(axis=-1)
    grads = (g_w_in, g_conv_w, g_conv_b.reshape(XBC), g_dt_bias[0, :HS], g_a_log[0, :HS], g_d, g_norm_w.reshape(DI), g_w_out)
    return dx, grads


ATTN_KEYS = ("attn_w_in", "attn_b_f", "attn_w_out")
SSM_KEYS = ("ssm_w_in", "ssm_conv_w", "ssm_conv_b", "ssm_dt_bias", "ssm_A_log", "ssm_D", "ssm_norm_w", "ssm_w_out")
LAYER_KEYS = ("ln_mix_g", "ln_mix_b", "ffn_w_up", "ffn_conv_w", "ffn_conv_b", "ffn_w_down", "ln_ffn_g", "ln_ffn_b",
              "ple_w_proj", "ple_w_gate", "ple_b_gate")


def local_step(x, p, target, w):
    S, D = x.shape
    depth = p.shape[0]
    alpha = (2 * depth) ** 0.25
    F = w["ffn_w_down"].shape[1]
    saved = []
    for i in range(depth):
        j, tag = i // 2, f"l{i}"
        if i % 2 == 0:
            mix, msaved = _attn_fwd(x, w["attn_w_in"][j], w["attn_b_f"][j], w["attn_w_out"][j], tag + "_attn")
        else:
            mix, msaved = _ssm_fwd(x, w["ssm_w_in"][j], w["ssm_conv_w"][j], w["ssm_conv_b"][j], w["ssm_dt_bias"][j],
                                   w["ssm_A_log"][j], w["ssm_D"][j], w["ssm_norm_w"][j], w["ssm_w_out"][j], tag + "_ssm")
        row = lambda k: w[k][i].reshape(1, -1)
        x1, xhat1, rstd1 = ln_fwd(x, mix, row("ln_mix_g"), row("ln_mix_b"), alpha, tag + "_ln_mix")
        h = mm_nn(x1, w["ffn_w_up"][i], F32, tag + "_ffn_up")
        a = conv_act_fwd(h, F, F, w["ffn_conv_w"][i], row("ffn_conv_b"), BF16, tag + "_ffn_act", gate_col=0)
        ffn = mm_nn(a, w["ffn_w_down"][i], F32, tag + "_ffn_down")
        x2, xhat2, rstd2 = ln_fwd(x1, ffn, row("ln_ffn_g"), row("ln_ffn_b"), alpha, tag + "_ln_ffn")
        zg = mm_nn(x2, w["ple_w_gate"][i], F32, tag + "_ple_gate")
        pp = mm_nn(p[i], w["ple_w_proj"][i], F32, tag + "_ple_proj")
        x3 = ple_fwd(x2, zg, pp, row("ple_b_gate"), tag + "_ple")
        saved.append((x, msaved, x1, xhat1, rstd1, h, a, x2, xhat2, rstd2, zg, pp))
        x = x3

    loss_vec, d = loss_head(x, target, "loss_head")

    grads = {k: [None] * w[k].shape[0] for k in ATTN_KEYS + SSM_KEYS + LAYER_KEYS}
    for i in reversed(range(depth)):
        j, tag = i // 2, f"l{i}"
        x0, msaved, x1, xhat1, rstd1, h, a, x2, xhat2, rstd2, zg, pp = saved[i]
        row = lambda k: w[k][i].reshape(1, -1)
        dzg, dpp, g_bg = ple_bwd(d, zg, pp, row("ple_b_gate"), tag + "_ple_bwd")
        grads["ple_w_gate"][i] = mm_tn(x2, dzg, tag + "_gw_ple_gate")
        grads["ple_w_proj"][i] = mm_tn(p[i], dpp, tag + "_gw_ple_proj")
        grads["ple_b_gate"][i] = g_bg.reshape(D)
        dx2 = mm_nn(dzg, w["ple_w_gate"][i].T, F32, tag + "_dx2", add=d)
        dr2, g_g2, g_b2 = ln_bwd(dx2, xhat2, rstd2, row("ln_ffn_g"), tag + "_ln_ffn_bwd")
        grads["ln_ffn_g"][i], grads["ln_ffn_b"][i] = g_g2.reshape(D), g_b2.reshape(D)
        grads["ffn_w_down"][i] = mm_tn(a, dr2, tag + "_gw_down")
        da = mm_nn(dr2, w["ffn_w_down"][i].T, F32, tag + "_da")
        dgin, du, g_cw, g_cb = conv_act_bwd(da, h, F, F, w["ffn_conv_w"][i], row("ffn_conv_b"), tag + "_ffn_act_bwd", gate_col=0)
        grads["ffn_conv_w"][i], grads["ffn_conv_b"][i] = g_cw, g_cb.reshape(F)
        grads["ffn_w_up"][i] = jnp.concatenate([mm_tn(x1, du, tag + "_gw_up_u"), mm_tn(x1, dgin, tag + "_gw_up_g")], axis=1)
        w_up_t = w["ffn_w_up"][i].T
        dx1 = mm_nn(du, w_up_t[:F], F32, tag + "_dx1_u", add=dr2, add_scale=alpha)
        dx1 = mm_nn(dgin, w_up_t[F:], F32, tag + "_dx1_g", add=dx1)
        dr1, g_g1, g_b1 = ln_bwd(dx1, xhat1, rstd1, row("ln_mix_g"), tag + "_ln_mix_bwd")
        grads["ln_mix_g"][i], grads["ln_mix_b"][i] = g_g1.reshape(D), g_b1.reshape(D)
        if i % 2 == 0:
            d, mg = _attn_bwd(x0, dr1, dr1, alpha, w["attn_w_in"][j], w["attn_w_out"][j], msaved, tag + "_attn")
            for k, g in zip(ATTN_KEYS, mg):
                grads[k][j] = g
        else:
            d, mg = _ssm_bwd(x0, dr1, dr1, alpha, w["ssm_w_in"][j], w["ssm_conv_w"][j], w["ssm_w_out"][j], msaved, tag + "_ssm")
            for k, g in zip(SSM_KEYS, mg):
                grads[k][j] = g
    return loss_vec, d, {k: jnp.stack(v) for k, v in grads.items()}


MESH = pl.DeviceIdType.MESH
PACK_ELEMS = 2 * SUBLANES * LANES


def _exchange(src, gather, name):
    out_shape = (N_DEV,) + tuple(src.shape[-2:])

    def body(src_ref, out_ref, send_sems, recv_sems, local_sem):
        x, y, c = lax.axis_index("x"), lax.axis_index("y"), lax.axis_index("c")
        me = 4 * x + 2 * y + c

        def block_for(dev):
            return src_ref if gather else src_ref.at[dev]

        local = pltpu.make_async_copy(block_for(me), out_ref.at[me], local_sem)
        local.start()
        copies = []
        for k in range(1, N_DEV):
            px = 1 - x if k & 4 else x
            py = 1 - y if k & 2 else y
            pc = 1 - c if k & 1 else c
            cp = pltpu.make_async_remote_copy(
                src_ref=block_for(4 * px + 2 * py + pc), dst_ref=out_ref.at[me],
                send_sem=send_sems.at[k - 1], recv_sem=recv_sems.at[k - 1],
                device_id=(px, py, pc), device_id_type=MESH)
            cp.start()
            copies.append(cp)
        for cp in copies:
            cp.wait()
        local.wait()

    return pl.pallas_call(
        body, name=name,
        in_specs=[pl.BlockSpec(memory_space=pl.ANY)],
        out_specs=pl.BlockSpec(memory_space=pl.ANY),
        out_shape=jax.ShapeDtypeStruct(out_shape, src.dtype),
        scratch_shapes=[pltpu.SemaphoreType.DMA((N_DEV - 1,)), pltpu.SemaphoreType.DMA((N_DEV - 1,)), pltpu.SemaphoreType.DMA],
    )(src)


def reduce_adamw(parts, w, m, v, name):
    _, R, _ = parts.shape
    tr = _pick(R, (512, 256, 128, 64, 32, 16))

    def body(p_ref, w_ref, m_ref, v_ref, g_ref, d_ref, nm_ref, nv_ref):
        g = p_ref[0].astype(F32)
        for s in range(1, N_DEV):
            g = g + p_ref[s].astype(F32)
        nm = ADAM_B1 * m_ref[...] + (1.0 - ADAM_B1) * g
        nv = ADAM_B2 * v_ref[...] + (1.0 - ADAM_B2) * (g * g)
        m_hat = nm / (1.0 - ADAM_B1 ** ADAM_STEP)
        v_hat = nv / (1.0 - ADAM_B2 ** ADAM_STEP)
        g_ref[...] = g
        d_ref[...] = -ADAM_LR * (m_hat / (jnp.sqrt(v_hat) + ADAM_EPS) + ADAM_WD * w_ref[...])
        nm_ref[...] = nm
        nv_ref[...] = nv

    row = pl.BlockSpec((tr, LANES), lambda i: (i, 0))
    return pl.pallas_call(
        body, name=name, grid=(R // tr,),
        in_specs=[pl.BlockSpec((N_DEV, tr, LANES), lambda i: (0, i, 0)), row, row, row],
        out_specs=[row, row, row, row],
        out_shape=[jax.ShapeDtypeStruct((R, LANES), F32)] * 4,
        compiler_params=_cparams("parallel"),
    )(parts, w, m, v)


def _pack(arrays, dtype, lead=0):
    parts = []
    for a in arrays:
        head = a.shape[:lead]
        flat = a.astype(dtype).reshape(head + (-1,))
        flat = jnp.pad(flat, [(0, 0)] * lead + [(0, (-flat.shape[-1]) % PACK_ELEMS)])
        parts.append(flat.reshape(head + (-1, LANES)))
    return jnp.concatenate(parts, axis=lead)


def _unpack(packed, shapes):
    lead = packed.shape[:-2]
    out, r0 = [], 0
    for shape in shapes:
        n = math.prod(shape)
        rows = -(-n // PACK_ELEMS) * (PACK_ELEMS // LANES)
        seg = packed[..., r0:r0 + rows, :].reshape(lead + (rows * LANES,))[..., :n]
        out.append(seg.reshape(lead + tuple(shape)))
        r0 += rows
    return out


MATMUL_SHARDED = {"attn_w_in": 2, "attn_w_out": 1, "ssm_w_in": 2, "ssm_w_out": 1, "ffn_w_up": 2, "ffn_w_down": 1,
                  "ple_w_proj": 2, "ple_w_gate": 1}
SMALL_SHARDED = {"ssm_conv_w": 2, "ssm_conv_b": 1, "ssm_norm_w": 1, "ffn_conv_w": 2}
REPLICATED = ("attn_b_f", "ssm_dt_bias", "ssm_A_log", "ssm_D", "ln_mix_g", "ln_mix_b", "ffn_conv_b", "ln_ffn_g",
              "ln_ffn_b", "ple_b_gate")
WEIGHT_ORDER = ("attn_w_in", "attn_b_f", "attn_w_out", "ssm_w_in", "ssm_conv_w", "ssm_conv_b", "ssm_dt_bias", "ssm_A_log",
                "ssm_D", "ssm_norm_w", "ssm_w_out", "ln_mix_g", "ln_mix_b", "ffn_w_up", "ffn_conv_w", "ffn_conv_b",
                "ffn_w_down", "ln_ffn_g", "ln_ffn_b", "ple_w_proj", "ple_w_gate", "ple_b_gate")


def _join_shards(gathered, axis):
    moved = jnp.moveaxis(gathered, 0, axis)
    shape = list(moved.shape)
    shape[axis:axis + 2] = [shape[axis] * shape[axis + 1]]
    return moved.reshape(shape)


def _split_shards(full, axis):
    shape = list(full.shape)
    shape[axis:axis + 1] = [N_DEV, shape[axis] // N_DEV]
    return jnp.moveaxis(full.reshape(shape), axis, 0)


def kernel(x, p, attn_w_in, attn_b_f, attn_w_out, ssm_w_in, ssm_conv_w, ssm_conv_b, ssm_dt_bias, ssm_A_log, ssm_D, ssm_norm_w, ssm_w_out, ln_mix_g, ln_mix_b, ffn_w_up, ffn_conv_w, ffn_conv_b, ffn_w_down, ln_ffn_g, ln_ffn_b, ple_w_proj, ple_w_gate, ple_b_gate, loss_target, m_attn_w_in, m_attn_b_f, m_attn_w_out, m_ssm_w_in, m_ssm_conv_w, m_ssm_conv_b, m_ssm_dt_bias, m_ssm_A_log, m_ssm_D, m_ssm_norm_w, m_ssm_w_out, m_ln_mix_g, m_ln_mix_b, m_ffn_w_up, m_ffn_conv_w, m_ffn_conv_b, m_ffn_w_down, m_ln_ffn_g, m_ln_ffn_b, m_ple_w_proj, m_ple_w_gate, m_ple_b_gate, v_attn_w_in, v_attn_b_f, v_attn_w_out, v_ssm_w_in, v_ssm_conv_w, v_ssm_conv_b, v_ssm_dt_bias, v_ssm_A_log, v_ssm_D, v_ssm_norm_w, v_ssm_w_out, v_ln_mix_g, v_ln_mix_b, v_ffn_w_up, v_ffn_conv_w, v_ffn_conv_b, v_ffn_w_down, v_ln_ffn_g, v_ln_ffn_b, v_ple_w_proj, v_ple_w_gate, v_ple_b_gate):
    args = dict(locals())
    w_loc = {k: args[k] for k in WEIGHT_ORDER}
    m_loc = {k: args["m_" + k] for k in WEIGHT_ORDER}
    v_loc = {k: args["v_" + k] for k in WEIGHT_ORDER}
    mm_names, small_names = tuple(MATMUL_SHARDED), tuple(SMALL_SHARDED)
    sharded = mm_names + small_names
    axis_of = {**MATMUL_SHARDED, **SMALL_SHARDED}

    g_mm = _exchange(_pack([w_loc[k] for k in mm_names], BF16), True, "gather_matmul_weights")
    g_small = _exchange(_pack([w_loc[k] for k in small_names], F32), True, "gather_small_weights")
    w_full = {k: w_loc[k] for k in REPLICATED}
    for names, gathered in ((mm_names, g_mm), (small_names, g_small)):
        for k, blocks in zip(names, _unpack(gathered, [w_loc[k].shape for k in names])):
            w_full[k] = _join_shards(blocks, axis_of[k])

    loss_vec, grad_x, g_full = local_step(x[0], p[:, 0], loss_target[0], w_full)

    send = _pack([_split_shards(g_full[k], axis_of[k]) for k in sharded], BF16, lead=1)
    parts = _exchange(send, False, "exchange_weight_grads")
    shapes = [w_loc[k].shape for k in sharded]
    pk = lambda d: _pack([d[k] for k in sharded], F32)
    outs = reduce_adamw(parts, pk(w_loc), pk(m_loc), pk(v_loc), "reduce_adamw_sharded")
    res = {k: vals for k, vals in zip(sharded, zip(*[_unpack(o, shapes) for o in outs]))}

    rep_shapes = [w_loc[k].shape for k in REPLICATED] + [(1, LANES)]
    rparts = _exchange(_pack([g_full[k] for k in REPLICATED] + [loss_vec], F32), True, "gather_replicated_grads")
    zero = jnp.zeros((1, LANES), F32)
    rk = lambda d: _pack([d[k] for k in REPLICATED] + [zero], F32)
    routs = reduce_adamw(rparts, rk(w_loc), rk(m_loc), rk(v_loc), "reduce_adamw_replicated")
    runp = [_unpack(o, rep_shapes) for o in routs]
    for i, k in enumerate(REPLICATED):
        res[k] = tuple(u[i] for u in runp)
    loss = runp[0][-1][0, 0]

    return (loss, grad_x[None], *[res[k][0] for k in WEIGHT_ORDER], *[res[k][1] for k in WEIGHT_ORDER],
            *[res[k][2] for k in WEIGHT_ORDER], *[res[k][3] for k in WEIGHT_ORDER])
```

```python
import functools
import math

import jax
import jax.numpy as jnp
from jax import lax
from jax.experimental import pallas as pl
from jax.experimental.pallas import tpu as pltpu

F32 = jnp.float32
BF16 = jnp.bfloat16

LANES = 128
SUBLANES = 8
VMEM_LIMIT_BYTES = 56 * 1024 * 1024

N_DEV = 8
HEAD_DIM = 64
SSM_GROUPS = 8
SSM_STATE = 128
SSM_CHUNK = 128
SSM_CONV = 4
FFN_CONV = 3
LN_EPS = 1e-5
RMS_EPS = 1e-5
ADAM_LR, ADAM_B1, ADAM_B2, ADAM_EPS, ADAM_WD, ADAM_STEP = 0.001, 0.9, 0.999, 1e-08, 0.01, 10
NEG_INF = float("-inf")
HIGHEST = lax.Precision.HIGHEST
NT_DIMS = (((1,), (1,)), ((), ()))
TN_DIMS = (((0,), (0,)), ((), ()))


def _cparams(*sem):
    return pltpu.CompilerParams(dimension_semantics=sem, vmem_limit_bytes=VMEM_LIMIT_BYTES)


def _pick(n, candidates):
    for c in candidates:
        if n % c == 0:
            return c
    return n


def _dot(a, b):
    return jnp.dot(a.astype(BF16), b.astype(BF16), preferred_element_type=F32)


def _dot_nt(a, b):
    return lax.dot_general(a.astype(BF16), b.astype(BF16), NT_DIMS, preferred_element_type=F32)


def _dot_tn(a, b):
    return lax.dot_general(a.astype(BF16), b.astype(BF16), TN_DIMS, preferred_element_type=F32)


def _sigmoid(x):
    return 1.0 / (1.0 + jnp.exp(-x))


def _log1p_small(u):
    return jnp.where(u < 1e-3, u * (1.0 - u * (0.5 - u * (1.0 / 3.0))), jnp.log(1.0 + u))


def _softplus(x):
    return jnp.maximum(x, 0.0) + _log1p_small(jnp.exp(-jnp.abs(x)))


def mm_nn(a, b, out_dtype, name, add=None, add_scale=1.0):
    M, K = a.shape
    _, N = b.shape
    tm = _pick(M, (1024, 512, 256, 128))
    tn = N if N <= 1024 else _pick(N, (1408, 1024, 896, 768, 640, 512, 384, 256, 128))
    tk = K if K <= 2048 else _pick(K, (1408, 1024, 896, 768, 640, 512, 384, 256, 128))
    nk = K // tk

    def body(*refs):
        if add is None:
            a_ref, b_ref, o_ref, acc_ref = refs
        else:
            a_ref, b_ref, c_ref, o_ref, acc_ref = refs
        k = pl.program_id(2)
        part = _dot(a_ref[...], b_ref[...])

        @pl.when(k == 0)
        def _():
            acc_ref[...] = part

        @pl.when(k > 0)
        def _():
            acc_ref[...] += part

        @pl.when(k == nk - 1)
        def _():
            r = acc_ref[...]
            if add is not None:
                r = r + add_scale * c_ref[...].astype(F32)
            o_ref[...] = r.astype(o_ref.dtype)

    in_specs = [pl.BlockSpec((tm, tk), lambda i, j, k: (i, k)), pl.BlockSpec((tk, tn), lambda i, j, k: (k, j))]
    args = [a, b]
    if add is not None:
        in_specs.append(pl.BlockSpec((tm, tn), lambda i, j, k: (i, j)))
        args.append(add)
    return pl.pallas_call(
        body,
        name=name,
        grid=(M // tm, N // tn, nk),
        in_specs=in_specs,
        out_specs=pl.BlockSpec((tm, tn), lambda i, j, k: (i, j)),
        out_shape=jax.ShapeDtypeStruct((M, N), out_dtype),
        scratch_shapes=[pltpu.VMEM((tm, tn), F32)],
        compiler_params=_cparams("parallel", "parallel", "arbitrary"),
    )(*args)


def mm_tn(a, b, name):
    M, K = a.shape
    _, N = b.shape
    tm = _pick(M, (512, 256, 128))
    tk = K if K <= 1024 else _pick(K, (1408, 1024, 896, 768, 640, 512, 384, 256, 128))
    tn = N if N <= 1408 else _pick(N, (1408, 1024, 896, 768, 640, 512, 384, 256, 128))
    nm = M // tm

    def body(a_ref, b_ref, o_ref):
        m = pl.program_id(2)
        part = _dot_tn(a_ref[...], b_ref[...])

        @pl.when(m == 0)
        def _():
            o_ref[...] = part

        @pl.when(m > 0)
        def _():
            o_ref[...] += part

    return pl.pallas_call(
        body,
        name=name,
        grid=(K // tk, N // tn, nm),
        in_specs=[pl.BlockSpec((tm, tk), lambda i, j, m: (m, i)), pl.BlockSpec((tm, tn), lambda i, j, m: (m, j))],
        out_specs=pl.BlockSpec((tk, tn), lambda i, j, m: (i, j)),
        out_shape=jax.ShapeDtypeStruct((K, N), F32),
        compiler_params=_cparams("parallel", "parallel", "arbitrary"),
    )(a, b)


def ln_fwd(x, mix, g, b, alpha, name):
    S, D = x.shape
    tm = _pick(S, (512, 256, 128))

    def body(x_ref, mix_ref, g_ref, b_ref, y_ref, xhat_ref, rstd_ref):
        r = alpha * x_ref[...] + mix_ref[...]
        mu = jnp.mean(r, axis=-1, keepdims=True)
        xc = r - mu
        var = jnp.mean(xc * xc, axis=-1, keepdims=True)
        rstd = lax.rsqrt(var + LN_EPS)
        xhat = xc * rstd
        y_ref[...] = xhat * g_ref[...] + b_ref[...]
        xhat_ref[...] = xhat
        rstd_ref[...] = rstd

    row = pl.BlockSpec((tm, D), lambda i: (i, 0))
    vec = pl.BlockSpec((1, D), lambda i: (0, 0))
    return pl.pallas_call(
        body,
        name=name,
        grid=(S // tm,),
        in_specs=[row, row, vec, vec],
        out_specs=[row, row, pl.BlockSpec((tm, 1), lambda i: (i, 0))],
        out_shape=[jax.ShapeDtypeStruct((S, D), F32), jax.ShapeDtypeStruct((S, D), F32), jax.ShapeDtypeStruct((S, 1), F32)],
        compiler_params=_cparams("parallel"),
    )(x, mix, g, b)


def ln_bwd(dy, xhat, rstd, g, name):
    S, D = dy.shape
    tm = _pick(S, (512, 256, 128))

    def body(dy_ref, xhat_ref, rstd_ref, g_ref, dr_ref, dg_ref, db_ref):
        i = pl.program_id(0)
        dyv = dy_ref[...]
        xh = xhat_ref[...]
        dxh = dyv * g_ref[...]
        m1 = jnp.mean(dxh, axis=-1, keepdims=True)
        m2 = jnp.mean(dxh * xh, axis=-1, keepdims=True)
        dr_ref[...] = rstd_ref[...] * (dxh - m1 - xh * m2)
        dg_part = jnp.sum(dyv * xh, axis=0, keepdims=True)
        db_part = jnp.sum(dyv, axis=0, keepdims=True)

        @pl.when(i == 0)
        def _():
            dg_ref[...] = dg_part
            db_ref[...] = db_part

        @pl.when(i > 0)
        def _():
            dg_ref[...] += dg_part
            db_ref[...] += db_part

    row = pl.BlockSpec((tm, D), lambda i: (i, 0))
    vec = pl.BlockSpec((1, D), lambda i: (0, 0))
    return pl.pallas_call(
        body,
        name=name,
        grid=(S // tm,),
        in_specs=[row, row, pl.BlockSpec((tm, 1), lambda i: (i, 0)), vec],
        out_specs=[row, vec, vec],
        out_shape=[jax.ShapeDtypeStruct((S, D), F32), jax.ShapeDtypeStruct((1, D), F32), jax.ShapeDtypeStruct((1, D), F32)],
        compiler_params=_cparams("arbitrary"),
    )(dy, xhat, rstd, g)


HALO = SUBLANES


def _prev_halo_spec(tm, tc, col0):
    return pl.BlockSpec((HALO, tc), lambda i, j: (jnp.maximum(i * (tm // HALO) - 1, 0), j + col0))


def _next_halo_spec(tm, tc, col0, n_row_tiles):
    last = n_row_tiles * (tm // HALO) - 1
    return pl.BlockSpec((HALO, tc), lambda i, j: (jnp.minimum((i + 1) * (tm // HALO), last), j + col0))


CONV_CHUNK = 32


def _causal_conv(ext_ref, w, n_taps, n_rows, row0, lanes):
    acc = None
    for k in range(n_taps):
        term = ext_ref[pl.ds(row0 - (n_taps - 1) + k, n_rows), lanes] * w[k:k + 1, :]
        acc = term if acc is None else acc + term
    return acc


def _anticausal_conv(ext_ref, w, n_taps, n_rows, row0, lanes):
    acc = None
    for k in range(n_taps):
        term = ext_ref[pl.ds(row0 + n_taps - 1 - k, n_rows), lanes] * w[k:k + 1, :]
        acc = term if acc is None else acc + term
    return acc


def _fold8(a):
    acc = a[0:SUBLANES]
    for g in range(SUBLANES, a.shape[0], SUBLANES):
        acc = acc + a[g:g + SUBLANES]
    return acc


INV_SQRT2 = 1.0 / math.sqrt(2.0)
INV_SQRT_2PI = 1.0 / math.sqrt(2.0 * math.pi)


def _gelu(g):
    return 0.5 * g * (1.0 + lax.erf(g * INV_SQRT2))


def _silu(x):
    return x * _sigmoid(x)


def _silu_grad(x):
    s = _sigmoid(x)
    return s * (1.0 + x * (1.0 - s))


def _conv_tiles(S, C, cols):
    tm = _pick(S, (256, 128))
    for tc in (1408, 1024, 512, 256, 128):
        if C % tc == 0 and all(c % tc == 0 for c in cols):
            return tm, tc
    raise ValueError("no column tile for the conv kernels")


def conv_act_fwd(src, in_col, C, conv_w, conv_b, out_dtype, name, gate_col=None):
    S = src.shape[0]
    K = conv_w.shape[0]
    gated = gate_col is not None
    tm, tc = _conv_tiles(S, C, [in_col] + ([gate_col] if gated else []))
    c_in = in_col // tc
    c_gate = gate_col // tc if gated else 0

    def body(*refs):
        if gated:
            x_ref, xp_ref, w_ref, b_ref, u_ref, o_ref, ext_ref = refs
        else:
            x_ref, xp_ref, w_ref, b_ref, o_ref, ext_ref = refs
        i = pl.program_id(0)
        ext_ref[0:HALO] = jnp.where(i > 0, xp_ref[...], 0.0)
        ext_ref[HALO:HALO + tm] = x_ref[...]
        for l0 in range(0, tc, LANES):
            ls = slice(l0, l0 + LANES)
            w = w_ref[:, ls]
            b = b_ref[:, ls]
            for r0 in range(0, tm, CONV_CHUNK):
                pre = _causal_conv(ext_ref, w, K, CONV_CHUNK, HALO + r0, ls) + b
                out = _gelu(pre) * u_ref[pl.ds(r0, CONV_CHUNK), ls] if gated else _silu(pre)
                o_ref[pl.ds(r0, CONV_CHUNK), ls] = out.astype(o_ref.dtype)

    in_specs = [
        pl.BlockSpec((tm, tc), lambda i, j: (i, j + c_in)),
        _prev_halo_spec(tm, tc, c_in),
        pl.BlockSpec((K, tc), lambda i, j: (0, j)),
        pl.BlockSpec((1, tc), lambda i, j: (0, j)),
    ]
    args = [src, src, conv_w, conv_b]
    if gated:
        in_specs.append(pl.BlockSpec((tm, tc), lambda i, j: (i, j + c_gate)))
        args.append(src)
    return pl.pallas_call(
        body,
        name=name,
        grid=(S // tm, C // tc),
        in_specs=in_specs,
        out_specs=pl.BlockSpec((tm, tc), lambda i, j: (i, j)),
        out_shape=jax.ShapeDtypeStruct((S, C), out_dtype),
        scratch_shapes=[pltpu.VMEM((tm + HALO, tc), F32)],
        compiler_params=_cparams("parallel", "parallel"),
    )(*args)


def conv_act_bwd(d_out, src, in_col, C, conv_w, conv_b, name, gate_col=None):
    S = src.shape[0]
    K = conv_w.shape[0]
    gated = gate_col is not None
    tm, tc = _conv_tiles(S, C, [in_col] + ([gate_col] if gated else []))
    c_in = in_col // tc
    c_gate = gate_col // tc if gated else 0
    ni = S // tm
    te = tm + HALO

    def body(*refs):
        if gated:
            (d_ref, dn_ref, x_ref, xp_ref, xn_ref, w_ref, b_ref, u_ref, un_ref,
             dx_ref, dw_ref, db_ref, du_ref, xext_ref, dext_ref) = refs
        else:
            (d_ref, dn_ref, x_ref, xp_ref, xn_ref, w_ref, b_ref,
             dx_ref, dw_ref, db_ref, xext_ref, dext_ref) = refs
        i = pl.program_id(1)
        xext_ref[0:HALO] = jnp.where(i > 0, xp_ref[...], 0.0)
        xext_ref[HALO:HALO + tm] = x_ref[...]
        xext_ref[HALO + tm:HALO + te] = xn_ref[...]

        @pl.when(i == 0)
        def _():
            dw_ref[...] = jnp.zeros_like(dw_ref)
            db_ref[...] = jnp.zeros_like(db_ref)

        for l0 in range(0, tc, LANES):
            ls = slice(l0, l0 + LANES)
            w = w_ref[:, ls]
            b = b_ref[:, ls]
            acc_w = [jnp.zeros((SUBLANES, LANES), F32) for _ in range(K)]
            acc_b = jnp.zeros((SUBLANES, LANES), F32)
            for r0 in range(0, te, CONV_CHUNK):
                n = min(CONV_CHUNK, te - r0)
                inside = r0 < tm
                taps = [xext_ref[pl.ds(HALO + r0 - (K - 1) + k, n), ls] for k in range(K)]
                pre = sum(t * w[k:k + 1, :] for k, t in enumerate(taps)) + b
                d = (d_ref[pl.ds(r0, n), ls] if inside else dn_ref[:, ls]).astype(F32)
                if gated:
                    u = u_ref[pl.ds(r0, n), ls] if inside else un_ref[:, ls]
                    cdf = 0.5 * (1.0 + lax.erf(pre * INV_SQRT2))
                    dpre = d * u * (cdf + pre * jnp.exp(-0.5 * pre * pre) * INV_SQRT_2PI)
                    if inside:
                        du_ref[pl.ds(r0, n), ls] = (d * (pre * cdf)).astype(du_ref.dtype)
                else:
                    dpre = d * _silu_grad(pre)
                if inside:
                    for k in range(K):
                        acc_w[k] = acc_w[k] + _fold8(dpre * taps[k])
                    acc_b = acc_b + _fold8(dpre)
                else:
                    dpre = jnp.where(i < ni - 1, dpre, 0.0)
                dext_ref[pl.ds(r0, n), ls] = dpre
            for r0 in range(0, tm, CONV_CHUNK):
                dx = _anticausal_conv(dext_ref, w, K, CONV_CHUNK, r0, ls)
                dx_ref[pl.ds(r0, CONV_CHUNK), ls] = dx.astype(dx_ref.dtype)
            dw_rows = [jnp.sum(a, axis=0, keepdims=True) for a in acc_w]
            dw_ref[:, ls] += jnp.concatenate(dw_rows + [jnp.zeros((SUBLANES - K, LANES), F32)], axis=0)
            db_ref[:, ls] += jnp.sum(acc_b, axis=0, keepdims=True)

    last = ni * (tm // HALO) - 1
    cur = lambda c0: pl.BlockSpec((tm, tc), lambda j, i: (i, j + c0))
    prev = lambda c0: pl.BlockSpec((HALO, tc), lambda j, i: (jnp.maximum(i * (tm // HALO) - 1, 0), j + c0))
    nxt = lambda c0: pl.BlockSpec((HALO, tc), lambda j, i: (jnp.minimum((i + 1) * (tm // HALO), last), j + c0))
    vec = lambda rows: pl.BlockSpec((rows, tc), lambda j, i: (0, j))
    in_specs = [cur(0), nxt(0), cur(c_in), prev(c_in), nxt(c_in), vec(K), vec(1)]
    args = [d_out, d_out, src, src, src, conv_w, conv_b]
    out_specs = [cur(0), vec(SUBLANES), vec(1)]
    out_shape = [jax.ShapeDtypeStruct((S, C), BF16), jax.ShapeDtypeStruct((SUBLANES, C), F32), jax.ShapeDtypeStruct((1, C), F32)]
    if gated:
        in_specs += [cur(c_gate), nxt(c_gate)]
        args += [src, src]
        out_specs.append(cur(0))
        out_shape.append(jax.ShapeDtypeStruct((S, C), BF16))
    outs = pl.pallas_call(
        body,
        name=name,
        grid=(C // tc, ni),
        in_specs=in_specs,
        out_specs=out_specs,
        out_shape=out_shape,
        scratch_shapes=[pltpu.VMEM((tm + 2 * HALO, tc), F32), pltpu.VMEM((te, tc), F32)],
        compiler_params=_cparams("parallel", "arbitrary"),
    )(*args)
    return outs[0], (outs[3] if gated else None), outs[1][:K], outs[2]


def ple_fwd(x2, zg, pp, bg, name):
    S, D = x2.shape
    tm = _pick(S, (512, 256, 128))

    def body(x_ref, z_ref, p_ref, b_ref, o_ref):
        o_ref[...] = x_ref[...] + _sigmoid(z_ref[...] + b_ref[...]) * p_ref[...]

    row = pl.BlockSpec((tm, D), lambda i: (i, 0))
    return pl.pallas_call(
        body, name=name, grid=(S // tm,), in_specs=[row, row, row, pl.BlockSpec((1, D), lambda i: (0, 0))], out_specs=row,
        out_shape=jax.ShapeDtypeStruct((S, D), F32), compiler_params=_cparams("parallel"),
    )(x2, zg, pp, bg)


def ple_bwd(dx3, zg, pp, bg, name):
    S, D = dx3.shape
    tm = _pick(S, (512, 256, 128))

    def body(d_ref, z_ref, p_ref, b_ref, dz_ref, dp_ref, db_ref):
        i = pl.program_id(0)
        d = d_ref[...]
        gate = _sigmoid(z_ref[...] + b_ref[...])
        dz = d * p_ref[...] * gate * (1.0 - gate)
        dz_ref[...] = dz.astype(dz_ref.dtype)
        dp_ref[...] = (d * gate).astype(dp_ref.dtype)
        part = jnp.sum(dz, axis=0, keepdims=True)

        @pl.when(i == 0)
        def _():
            db_ref[...] = part

        @pl.when(i > 0)
        def _():
            db_ref[...] += part

    row = pl.BlockSpec((tm, D), lambda i: (i, 0))
    vec = pl.BlockSpec((1, D), lambda i: (0, 0))
    return pl.pallas_call(
        body, name=name, grid=(S // tm,), in_specs=[row, row, row, vec], out_specs=[row, row, vec],
        out_shape=[jax.ShapeDtypeStruct((S, D), BF16), jax.ShapeDtypeStruct((S, D), BF16), jax.ShapeDtypeStruct((1, D), F32)],
        compiler_params=_cparams("arbitrary"),
    )(dx3, zg, pp, bg)


def loss_head(y, target, name):
    S, D = y.shape
    tm = _pick(S, (512, 256, 128))

    def body(y_ref, t_ref, loss_ref, dy_ref, acc_ref):
        i = pl.program_id(0)
        err = y_ref[...] - t_ref[...]
        dy_ref[...] = err * (1.0 / D)
        part = jnp.sum(err * err, axis=0, keepdims=True)

        @pl.when(i == 0)
        def _():
            acc_ref[...] = part

        @pl.when(i > 0)
        def _():
            acc_ref[...] += part

        @pl.when(i == pl.num_programs(0) - 1)
        def _():
            loss_ref[...] = jnp.zeros((1, LANES), F32) + (0.5 / D) * jnp.sum(acc_ref[...])

    row = pl.BlockSpec((tm, D), lambda i: (i, 0))
    return pl.pallas_call(
        body, name=name, grid=(S // tm,), in_specs=[row, row],
        out_specs=[pl.BlockSpec((1, LANES), lambda i: (0, 0)), row],
        out_shape=[jax.ShapeDtypeStruct((1, LANES), F32), jax.ShapeDtypeStruct((S, D), F32)],
        scratch_shapes=[pltpu.VMEM((1, D), F32)],
        compiler_params=_cparams("arbitrary"),
    )(y, target)


ATTN_TILE = 512
ATTN_SCALE = 1.0 / math.sqrt(HEAD_DIM)


def _attn_tile(S):
    return _pick(S, (ATTN_TILE, 256, 128))


def fox_gate_fwd(zt, bf, name):
    H, S = zt.shape
    tl = _pick(S, (512, 256, 128))

    def body(z_ref, b_ref, c_ref, carry_ref):
        i = pl.program_id(0)

        @pl.when(i == 0)
        def _():
            carry_ref[...] = jnp.zeros_like(carry_ref)

        z = z_ref[...] + b_ref[...]
        logf = jnp.minimum(z, 0.0) - _log1p_small(jnp.exp(-jnp.abs(z)))
        r = lax.broadcasted_iota(jnp.int32, (tl, tl), 0)
        c = lax.broadcasted_iota(jnp.int32, (tl, tl), 1)
        upper = (r <= c).astype(F32)
        cum = jnp.dot(logf, upper, precision=HIGHEST, preferred_element_type=F32) + carry_ref[...]
        c_ref[...] = cum
        carry_ref[...] = cum[:, tl - 1:tl]

    return pl.pallas_call(
        body, name=name, grid=(S // tl,),
        in_specs=[pl.BlockSpec((H, tl), lambda i: (0, i)), pl.BlockSpec((H, 1), lambda i: (0, 0))],
        out_specs=pl.BlockSpec((H, tl), lambda i: (0, i)),
        out_shape=jax.ShapeDtypeStruct((H, S), F32),
        scratch_shapes=[pltpu.VMEM((H, 1), F32)],
        compiler_params=_cparams("arbitrary"),
    )(zt, bf)


def fox_gate_bwd(dc_q, dc_k, zt, bf, name):
    H, S = zt.shape
    tl = _pick(S, (512, 256, 128))
    nt = S // tl

    def body(dcq_ref, dck_ref, z_ref, b_ref, dz_ref, db_ref, carry_ref):
        i = pl.program_id(0)

        @pl.when(i == 0)
        def _():
            carry_ref[...] = jnp.zeros_like(carry_ref)
            db_ref[...] = jnp.zeros_like(db_ref)

        r = lax.broadcasted_iota(jnp.int32, (tl, tl), 0)
        c = lax.broadcasted_iota(jnp.int32, (tl, tl), 1)
        lower = (r >= c).astype(F32)
        dc = dcq_ref[...] + dck_ref[...]
        suffix = jnp.dot(dc, lower, precision=HIGHEST, preferred_element_type=F32) + carry_ref[...]
        carry_ref[...] = suffix[:, 0:1]
        dz = suffix * _sigmoid(-(z_ref[...] + b_ref[...]))
        dz_ref[...] = dz
        db_ref[...] += jnp.sum(dz, axis=1, keepdims=True)

    rev = pl.BlockSpec((H, tl), lambda i: (0, nt - 1 - i))
    return pl.pallas_call(
        body, name=name, grid=(nt,),
        in_specs=[rev, rev, rev, pl.BlockSpec((H, 1), lambda i: (0, 0))],
        out_specs=[rev, pl.BlockSpec((H, 1), lambda i: (0, 0))],
        out_shape=[jax.ShapeDtypeStruct((H, S), F32), jax.ShapeDtypeStruct((H, 1), F32)],
        scratch_shapes=[pltpu.VMEM((H, 1), F32)],
        compiler_params=_cparams("arbitrary"),
    )(dc_q, dc_k, zt, bf)


ATTN_SUB = 2
ATTN_CHUNK = 16


def _row_tiles(ref, j, n_tiles):
    if n_tiles == 1:
        return ref[0, j]
    return jnp.concatenate([ref[0, j + t] for t in range(n_tiles)], axis=1)


def _diag_mask(n_rows, width, row0):
    r = lax.broadcasted_iota(jnp.int32, (n_rows, width), 0) + row0
    c = lax.broadcasted_iota(jnp.int32, (n_rows, width), 1)
    return r >= c


def _causal_sweep(i, process):
    def pair_body(j2, carry):
        process(2 * j2, 2, False)
        return carry

    lax.fori_loop(0, i // 2, pair_body, 0)

    @pl.when(i % 2 == 1)
    def _():
        process(i - 1, 2, True)

    @pl.when(i % 2 == 0)
    def _():
        process(i, 1, True)


def flash_fwd(q, k, v_ones, c_col, c_row, name):
    H, S, Dh = q.shape
    T = _attn_tile(S)
    NT = S // T
    TS = T // ATTN_SUB

    def body(q_ref, k_ref, v_ref, cq_ref, ck_ref, o_ref, lse_ref, m_ref, acc_ref):
        i = pl.program_id(1)
        qs = q_ref[0] * ATTN_SCALE
        c_ref = cq_ref[0, 0:1, :]
        m_ref[...] = jnp.full_like(m_ref, NEG_INF)
        acc_ref[...] = jnp.zeros_like(acc_ref)

        def process(j, n_tiles, masked):
            width = n_tiles * T
            off = pl.multiple_of(j * T, T)
            kj = k_ref[0, pl.ds(off, width), :]
            vj = v_ref[0, pl.ds(off, width), :]
            ckj = _row_tiles(ck_ref, j, n_tiles) - c_ref
            ss = [_dot_nt(qs[u * TS:(u + 1) * TS], kj) for u in range(ATTN_SUB)]
            for u in range(ATTN_SUB):
                rows = slice(u * TS, (u + 1) * TS)
                m_prev = m_ref[rows]
                ps, m_news = [], []
                for r0 in range(0, TS, ATTN_CHUNK):
                    rc = slice(r0, r0 + ATTN_CHUNK)
                    s = ss[u][rc] - ckj
                    if masked:
                        s = jnp.where(_diag_mask(ATTN_CHUNK, width, u * TS + r0 + width - T), s, NEG_INF)
                    m_new = jnp.maximum(m_prev[rc], jnp.max(s, axis=1, keepdims=True))
                    ps.append(jnp.exp(s - jnp.tile(m_new, (1, width // LANES))).astype(BF16))
                    m_news.append(m_new)
                m_new = jnp.concatenate(m_news, axis=0)
                acc_ref[rows] = jnp.exp(m_prev - m_new) * acc_ref[rows] + _dot(jnp.concatenate(ps, axis=0), vj)
                m_ref[rows] = m_new

        _causal_sweep(i, process)
        acc = acc_ref[...]
        l = acc[:, Dh:Dh + 1]
        o_ref[0] = (acc[:, 0:Dh] / l).astype(o_ref.dtype)
        lse_ref[0] = m_ref[:, 0:1] + jnp.log(l) + (cq_ref[0] - c_ref)

    tile = pl.BlockSpec((1, T, Dh), lambda h, i: (h, i, 0))
    whole = pl.BlockSpec((1, S, Dh), lambda h, i: (h, 0, 0))
    whole_v = pl.BlockSpec((1, S, 2 * Dh), lambda h, i: (h, 0, 0))
    col = pl.BlockSpec((1, T, 1), lambda h, i: (h, i, 0))
    rows = pl.BlockSpec((1, NT, 1, T), lambda h, i: (h, 0, 0, 0))
    return pl.pallas_call(
        body, name=name, grid=(H, NT),
        in_specs=[tile, whole, whole_v, col, rows],
        out_specs=[tile, col],
        out_shape=[jax.ShapeDtypeStruct((H, S, Dh), BF16), jax.ShapeDtypeStruct((H, S, 1), F32)],
        scratch_shapes=[pltpu.VMEM((T, LANES), F32), pltpu.VMEM((T, 2 * Dh), F32)],
        compiler_params=_cparams("parallel", "parallel"),
    )(q, k, v_ones, c_col, c_row)


def flash_bwd_q(q, k, v, o, do, q_t, do_t, lse, c_col, c_row, name):
    H, S, Dh = q.shape
    T = _attn_tile(S)
    NT = S // T
    TS = T // ATTN_SUB

    def body(q_ref, k_ref, v_ref, o_ref, do_ref, qt_ref, dot_ref, lse_ref, cq_ref, ck_ref,
             dq_ref, dcq_ref, dkt_ref, dvt_ref, dck_ref, dq_acc, dcq_acc):
        i = pl.program_id(1)

        @pl.when(i == 0)
        def _():
            dkt_ref[...] = jnp.zeros_like(dkt_ref)
            dvt_ref[...] = jnp.zeros_like(dvt_ref)
            dck_ref[...] = jnp.zeros_like(dck_ref)

        qs = q_ref[0] * ATTN_SCALE
        do = do_ref[0]
        qs_t = qt_ref[0] * ATTN_SCALE
        do_t = dot_ref[0]
        delta = jnp.sum(do.astype(F32) * o_ref[0].astype(F32), axis=1, keepdims=True)
        bias = cq_ref[0] - lse_ref[0]
        dq_acc[...] = jnp.zeros_like(dq_acc)
        dcq_acc[...] = jnp.zeros_like(dcq_acc)

        def process(j, n_tiles, masked):
            width = n_tiles * T
            off = pl.multiple_of(j * T, T)
            kj = k_ref[0, pl.ds(off, width), :]
            vj = v_ref[0, pl.ds(off, width), :]
            ck = _row_tiles(ck_ref, j, n_tiles)
            halves = [slice(u * TS, (u + 1) * TS) for u in range(ATTN_SUB)]
            ss = [_dot_nt(qs[h], kj) for h in halves]
            dps = [_dot_nt(do[h], vj) for h in halves]
            dkt, dvt = [], []
            dck8 = jnp.zeros((SUBLANES, width), F32)
            for u, h in enumerate(halves):
                ps, dss, rowsums = [], [], []
                for r0 in range(0, TS, ATTN_CHUNK):
                    rc = slice(r0, r0 + ATTN_CHUNK)
                    p = jnp.exp(ss[u][rc] + (bias[h][rc] - ck))
                    if masked:
                        p = jnp.where(_diag_mask(ATTN_CHUNK, width, u * TS + r0 + width - T), p, 0.0)
                    ds = p * (dps[u][rc] - delta[h][rc])
                    ps.append(p.astype(BF16))
                    dss.append(ds.astype(BF16))
                    rowsums.append(jnp.sum(ds, axis=1, keepdims=True))
                    for g in range(0, ATTN_CHUNK, SUBLANES):
                        dck8 = dck8 + ds[g:g + SUBLANES]
                p, ds = jnp.concatenate(ps, axis=0), jnp.concatenate(dss, axis=0)
                dq_acc[h] += _dot(ds, kj)
                dcq_acc[h] += jnp.concatenate(rowsums, axis=0)
                dvt.append(_dot(do_t[:, h], p))
                dkt.append(_dot(qs_t[:, h], ds))
            dvt, dkt, dck = sum(dvt), sum(dkt), jnp.sum(dck8, axis=0, keepdims=True)
            for t in range(n_tiles):
                cols = slice(t * T, (t + 1) * T)
                dvt_ref[0, j + t] += dvt[:, cols]
                dkt_ref[0, j + t] += dkt[:, cols]
                dck_ref[0, j + t] -= dck[:, cols]

        _causal_sweep(i, process)
        dq_ref[0] = (dq_acc[...] * ATTN_SCALE).astype(dq_ref.dtype)
        dcq_ref[0] = dcq_acc[...]

    tile = pl.BlockSpec((1, T, Dh), lambda h, i: (h, i, 0))
    tile_t = pl.BlockSpec((1, Dh, T), lambda h, i: (h, 0, i))
    whole = pl.BlockSpec((1, S, Dh), lambda h, i: (h, 0, 0))
    col = pl.BlockSpec((1, T, 1), lambda h, i: (h, i, 0))
    rows = pl.BlockSpec((1, NT, 1, T), lambda h, i: (h, 0, 0, 0))
    acc_t = pl.BlockSpec((1, NT, Dh, T), lambda h, i: (h, 0, 0, 0))
    return pl.pallas_call(
        body, name=name, grid=(H, NT),
        in_specs=[tile, whole, whole, tile, tile, tile_t, tile_t, col, col, rows],
        out_specs=[tile, col, acc_t, acc_t, rows],
        out_shape=[jax.ShapeDtypeStruct((H, S, Dh), BF16), jax.ShapeDtypeStruct((H, S, 1), F32),
                   jax.ShapeDtypeStruct((H, NT, Dh, T), F32), jax.ShapeDtypeStruct((H, NT, Dh, T), F32),
                   jax.ShapeDtypeStruct((H, NT, 1, T), F32)],
        scratch_shapes=[pltpu.VMEM((T, Dh), F32), pltpu.VMEM((T, 1), F32)],
        compiler_params=_cparams("parallel", "arbitrary"),
    )(q, k, v, o, do, q_t, do_t, lse, c_col, c_row)


PAIR = 2 * HEAD_DIM


def _tri(n, lower):
    r = lax.broadcasted_iota(jnp.int32, (n, n), 0)
    c = lax.broadcasted_iota(jnp.int32, (n, n), 1)
    return (r >= c) if lower else (r <= c)


def _ssd_chunk_scalars(dt_raw, bias, a_log):
    Q = dt_raw.shape[0]
    dt = _softplus(dt_raw + bias)
    A = -jnp.exp(a_log)
    cum = jnp.dot(_tri(Q, True).astype(F32), dt * A, precision=HIGHEST, preferred_element_type=F32)
    tot = cum[Q - 1:Q, :]
    return dt, A, cum, tot


def _lane_pair(lo_mask, v, h0):
    return jnp.where(lo_mask, v[:, h0:h0 + 1], v[:, h0 + 1:h0 + 2])


def ssd_scan_fwd(xbc, proj, dt_col, dt_bias, a_log, d_inner, name):
    S, W = xbc.shape
    Q, N, G = SSM_CHUNK, SSM_STATE, SSM_GROUPS
    nc = S // Q
    n_pairs = d_inner // PAIR
    pairs_per_group = n_pairs // G
    GN = G * N

    def body(xbc_ref, dt_ref, bias_ref, alog_ref, y_ref, sin_ref, st_ref):
        c = pl.program_id(0)

        @pl.when(c == 0)
        def _():
            st_ref[...] = jnp.zeros_like(st_ref)

        sin_ref[0] = st_ref[...]
        dt, A, cum, tot = _ssd_chunk_scalars(dt_ref[...], bias_ref[...], alog_ref[...])
        cum_t = cum.T
        dt_t = dt.T
        ecum = jnp.exp(cum)
        wend = jnp.exp(tot - cum) * dt
        etot = jnp.exp(tot)
        lower = _tri(Q, True)
        lo = lax.broadcasted_iota(jnp.int32, (Q, PAIR), 1) < HEAD_DIM
        lo_row = lax.broadcasted_iota(jnp.int32, (1, PAIR), 1) < HEAD_DIM
        for g in range(G):
            Bg = xbc_ref[:, d_inner + g * N:d_inner + (g + 1) * N]
            Cg = xbc_ref[:, d_inner + GN + g * N:d_inner + GN + (g + 1) * N]
            CB = _dot_nt(Cg, Bg)
            for pp in range(pairs_per_group):
                pr = g * pairs_per_group + pp
                h0 = 2 * pr
                xw = xbc_ref[:, pr * PAIR:(pr + 1) * PAIR]
                ys = []
                for h in (h0, h0 + 1):
                    L = jnp.where(lower, jnp.exp(cum[:, h:h + 1] - cum_t[h:h + 1, :]), 0.0)
                    ys.append(_dot(CB * L * dt_t[h:h + 1, :], xw))
                st = st_ref[pr]
                y_inter = _dot(Cg, st) * _lane_pair(lo, ecum, h0)
                y_ref[:, pr * PAIR:(pr + 1) * PAIR] = jnp.where(lo, ys[0], ys[1]) + y_inter
                st_ref[pr] = _lane_pair(lo_row, etot, h0) * st + _dot_tn(Bg, xw * _lane_pair(lo, wend, h0))

    return pl.pallas_call(
        body, name=name, grid=(nc,),
        in_specs=[pl.BlockSpec((Q, W), lambda c: (c, 0)), pl.BlockSpec((Q, LANES), lambda c: (c, dt_col // LANES)),
                  pl.BlockSpec((1, LANES), lambda c: (0, 0)), pl.BlockSpec((1, LANES), lambda c: (0, 0))],
        out_specs=[pl.BlockSpec((Q, d_inner), lambda c: (c, 0)), pl.BlockSpec((1, n_pairs, N, PAIR), lambda c: (c, 0, 0, 0))],
        out_shape=[jax.ShapeDtypeStruct((S, d_inner), F32), jax.ShapeDtypeStruct((nc, n_pairs, N, PAIR), F32)],
        scratch_shapes=[pltpu.VMEM((n_pairs, N, PAIR), F32)],
        compiler_params=_cparams("arbitrary"),
    )(xbc, proj, dt_bias, a_log)


def ssd_scan_bwd(dy, dskip, xbc, proj, dt_col, dt_bias, a_log, states, d_inner, name):
    S, W = xbc.shape
    Q, N, G = SSM_CHUNK, SSM_STATE, SSM_GROUPS
    nc = S // Q
    n_pairs = d_inner // PAIR
    pairs_per_group = n_pairs // G
    GN = G * N

    def body(dy_ref, dskip_ref, xbc_ref, dt_ref, bias_ref, alog_ref, sin_ref,
             dxbc_ref, ddt_ref, dalog_ref, dbias_ref, dst_ref, rows_cum_ref, rows_dt_ref):
        step = pl.program_id(0)

        @pl.when(step == 0)
        def _():
            dst_ref[...] = jnp.zeros_like(dst_ref)
            dalog_ref[...] = jnp.zeros_like(dalog_ref)
            dbias_ref[...] = jnp.zeros_like(dbias_ref)

        rows_cum_ref[...] = jnp.zeros_like(rows_cum_ref)
        rows_dt_ref[...] = jnp.zeros_like(rows_dt_ref)
        dt_raw = dt_ref[...]
        bias = bias_ref[...]
        dt, A, cum, tot = _ssd_chunk_scalars(dt_raw, bias, alog_ref[...])
        cum_t = cum.T
        dt_t = dt.T
        ecum = jnp.exp(cum)
        eend = jnp.exp(tot - cum)
        wend = eend * dt
        etot = jnp.exp(tot)
        lower = _tri(Q, True)
        lane = lax.broadcasted_iota(jnp.int32, (Q, LANES), 1)
        lo = lane < HEAD_DIM
        lo_row = lax.broadcasted_iota(jnp.int32, (1, PAIR), 1) < HEAD_DIM
        lo_st = lax.broadcasted_iota(jnp.int32, (N, PAIR), 1) < HEAD_DIM
        last_row = lax.broadcasted_iota(jnp.int32, (Q, LANES), 0) == Q - 1
        dcum = jnp.zeros((Q, LANES), F32)
        ddt = jnp.zeros((Q, LANES), F32)
        for g in range(G):
            Bg = xbc_ref[:, d_inner + g * N:d_inner + (g + 1) * N]
            Cg = xbc_ref[:, d_inner + GN + g * N:d_inner + GN + (g + 1) * N]
            CB = _dot_nt(Cg, Bg)
            dCB = jnp.zeros((Q, Q), F32)
            dBg = jnp.zeros((Q, N), F32)
            dCg = jnp.zeros((Q, N), F32)
            for pp in range(pairs_per_group):
                pr = g * pairs_per_group + pp
                h0 = 2 * pr
                xw = xbc_ref[:, pr * PAIR:(pr + 1) * PAIR]
                dyp = dy_ref[:, pr * PAIR:(pr + 1) * PAIR]
                st = sin_ref[0, pr]
                dst = dst_ref[pr]
                ecum_p = _lane_pair(lo, ecum, h0)
                wend_p = _lane_pair(lo, wend, h0)
                etot_p = _lane_pair(lo_row, etot, h0)
                y2 = _dot(Cg, st)
                dye = dyp * ecum_p
                dCg = dCg + _dot_nt(dye, st)
                t1 = dye * y2
                bds = _dot(Bg, dst)
                qv = xw * bds
                dBg = dBg + _dot_nt(xw * wend_p, dst)
                sdot = dst * st
                dx = wend_p * bds
                for h, half, half_st in ((h0, lo, lo_st), (h0 + 1, ~lo, ~lo_st)):
                    sel = lane == h
                    t1_h = jnp.sum(jnp.where(half, t1, 0.0), axis=1, keepdims=True)
                    q_h = jnp.sum(jnp.where(half, qv, 0.0), axis=1, keepdims=True)
                    wq = wend[:, h:h + 1] * q_h
                    dtot_h = etot[:, h:h + 1] * jnp.sum(jnp.where(half_st, sdot, 0.0)) + jnp.sum(wq)
                    dcum = dcum + jnp.where(sel, t1_h - wq, 0.0) + jnp.where(sel & last_row, dtot_h, 0.0)
                    ddt = ddt + jnp.where(sel, eend[:, h:h + 1] * q_h, 0.0)
                    dt_row = dt_t[h:h + 1, :]
                    L = jnp.where(lower, jnp.exp(cum[:, h:h + 1] - cum_t[h:h + 1, :]), 0.0)
                    dyh = jnp.where(half, dyp, 0.0)
                    dM = _dot_nt(dyh, xw)
                    CBL = CB * L
                    Gp = dM * CBL
                    Gm = Gp * dt_row
                    rows_dt_ref[h:h + 1, :] = jnp.sum(Gp, axis=0, keepdims=True)
                    rows_cum_ref[h:h + 1, :] = -jnp.sum(Gm, axis=0, keepdims=True)
                    dcum = dcum + jnp.where(sel, jnp.sum(Gm, axis=1, keepdims=True), 0.0)
                    dCB = dCB + dM * L * dt_row
                    dx = dx + _dot_tn(CBL * dt_row, dyh)
                dxbc_ref[:, pr * PAIR:(pr + 1) * PAIR] = dx + dskip_ref[:, pr * PAIR:(pr + 1) * PAIR]
                dst_ref[pr] = _dot_tn(Cg, dye) + etot_p * dst
            dxbc_ref[:, d_inner + g * N:d_inner + (g + 1) * N] = dBg + _dot_tn(dCB, Cg)
            dxbc_ref[:, d_inner + GN + g * N:d_inner + GN + (g + 1) * N] = dCg + _dot(dCB, Bg)
        dcum = dcum + rows_cum_ref[...].T
        ddt = ddt + rows_dt_ref[...].T
        da = jnp.dot(_tri(Q, False).astype(F32), dcum, precision=HIGHEST, preferred_element_type=F32)
        ddt = ddt + da * A
        ddt_raw = ddt * _sigmoid(dt_raw + bias)
        ddt_ref[...] = ddt_raw
        dalog_ref[...] += jnp.sum(da * dt, axis=0, keepdims=True) * A
        dbias_ref[...] += jnp.sum(ddt_raw, axis=0, keepdims=True)

    rev = lambda width, col: pl.BlockSpec((Q, width), lambda s: (nc - 1 - s, col))
    vec = pl.BlockSpec((1, LANES), lambda s: (0, 0))
    return pl.pallas_call(
        body, name=name, grid=(nc,),
        in_specs=[rev(d_inner, 0), rev(d_inner, 0), rev(W, 0), rev(LANES, dt_col // LANES), vec, vec,
                  pl.BlockSpec((1, n_pairs, N, PAIR), lambda s: (nc - 1 - s, 0, 0, 0))],
        out_specs=[rev(W, 0), rev(LANES, 0), vec, vec],
        out_shape=[jax.ShapeDtypeStruct((S, W), F32), jax.ShapeDtypeStruct((S, LANES), F32),
                   jax.ShapeDtypeStruct((1, LANES), F32), jax.ShapeDtypeStruct((1, LANES), F32)],
        scratch_shapes=[pltpu.VMEM((n_pairs, N, PAIR), F32), pltpu.VMEM((LANES, Q), F32), pltpu.VMEM((LANES, Q), F32)],
        compiler_params=_cparams("arbitrary"),
    )(dy, dskip, xbc, proj, dt_bias, a_log, states)


def ssd_gate_fwd(y, xbc, proj, d_skip, norm_w, d_inner, name):
    S = y.shape[0]
    tm = _pick(S, (256, 128))
    gs = d_inner // SSM_GROUPS

    def body(y_ref, x_ref, z_ref, d_ref, w_ref, o_ref):
        for g in range(SSM_GROUPS):
            sl = slice(g * gs, (g + 1) * gs)
            y2 = (y_ref[:, sl] + d_ref[:, sl] * x_ref[:, sl]) * _silu(z_ref[:, sl])
            r = lax.rsqrt(jnp.mean(y2 * y2, axis=-1, keepdims=True) + RMS_EPS)
            o_ref[:, sl] = (y2 * r * w_ref[:, sl]).astype(o_ref.dtype)

    row = pl.BlockSpec((tm, d_inner), lambda i: (i, 0))
    vec = pl.BlockSpec((1, d_inner), lambda i: (0, 0))
    return pl.pallas_call(
        body, name=name, grid=(S // tm,), in_specs=[row, row, row, vec, vec], out_specs=row,
        out_shape=jax.ShapeDtypeStruct((S, d_inner), BF16), compiler_params=_cparams("parallel"),
    )(y, xbc, proj, d_skip, norm_w)


def ssd_gate_bwd(dyn, y, xbc, proj, d_skip, norm_w, d_inner, name):
    S = y.shape[0]
    tm = _pick(S, (256, 128))
    gs = d_inner // SSM_GROUPS

    def body(dyn_ref, y_ref, x_ref, z_ref, d_ref, w_ref, dy_ref, dskip_ref, dz_ref, dw_ref, dd_ref):
        i = pl.program_id(0)

        @pl.when(i == 0)
        def _():
            dw_ref[...] = jnp.zeros_like(dw_ref)
            dd_ref[...] = jnp.zeros_like(dd_ref)

        for g in range(SSM_GROUPS):
            sl = slice(g * gs, (g + 1) * gs)
            z = z_ref[:, sl]
            x = x_ref[:, sl]
            sz = _silu(z)
            ysum = y_ref[:, sl] + d_ref[:, sl] * x
            y2 = ysum * sz
            r = lax.rsqrt(jnp.mean(y2 * y2, axis=-1, keepdims=True) + RMS_EPS)
            dyn = dyn_ref[:, sl]
            a = dyn * w_ref[:, sl]
            dy2 = r * a - y2 * (r * r * r) * jnp.mean(a * y2, axis=-1, keepdims=True)
            dysum = dy2 * sz
            dy_ref[:, sl] = dysum
            dskip_ref[:, sl] = dysum * d_ref[:, sl]
            dz_ref[:, sl] = (dy2 * ysum * _silu_grad(z)).astype(dz_ref.dtype)
            dw_ref[:, sl] += jnp.sum(dyn * y2 * r, axis=0, keepdims=True)
            dd_ref[:, sl] += jnp.sum(dysum * x, axis=0, keepdims=True)

    row = pl.BlockSpec((tm, d_inner), lambda i: (i, 0))
    vec = pl.BlockSpec((1, d_inner), lambda i: (0, 0))
    return pl.pallas_call(
        body, name=name, grid=(S // tm,), in_specs=[row, row, row, row, vec, vec], out_specs=[row, row, row, vec, vec],
        out_shape=[jax.ShapeDtypeStruct((S, d_inner), F32), jax.ShapeDtypeStruct((S, d_inner), F32),
                   jax.ShapeDtypeStruct((S, d_inner), BF16), jax.ShapeDtypeStruct((1, d_inner), F32),
                   jax.ShapeDtypeStruct((1, d_inner), F32)],
        compiler_params=_cparams("arbitrary"),
    )(dyn, y, xbc, proj, d_skip, norm_w)


def _pad_to(a, axis, mult=LANES):
    n = a.shape[axis]
    extra = (-n) % mult
    if extra == 0:
        return a
    widths = [(0, 0)] * a.ndim
    widths[axis] = (0, extra)
    return jnp.pad(a, widths)


def _attn_fwd(x, w_in, b_f, w_out, tag):
    S, D = x.shape
    H = D // HEAD_DIM
    T = _attn_tile(S)
    proj = mm_nn(x, _pad_to(w_in, 1), F32, f"{tag}_proj")
    qkv = proj[:, :3 * D].astype(BF16).reshape(S, 3, H, HEAD_DIM).transpose(1, 2, 0, 3)
    zt = proj[:, 3 * D:3 * D + H].T
    bf = b_f.reshape(H, 1)
    c = fox_gate_fwd(zt, bf, f"{tag}_gate")
    c_col, c_row = c.reshape(H, S, 1), c.reshape(H, S // T, 1, T)
    v_ones = jnp.concatenate([qkv[2], jnp.ones_like(qkv[2])], axis=-1)
    o, lse = flash_fwd(qkv[0], qkv[1], v_ones, c_col, c_row, f"{tag}_flash")
    o_flat = o.transpose(1, 0, 2).reshape(S, D)
    mix = mm_nn(o_flat, w_out, F32, f"{tag}_out")
    return mix, (qkv, zt, bf, c_col, c_row, o, lse, o_flat)


def _attn_bwd(x, dmix, dx_add, alpha, w_in, w_out, saved, tag):
    S, D = x.shape
    H = D // HEAD_DIM
    T = _attn_tile(S)
    qkv, zt, bf, c_col, c_row, o, lse, o_flat = saved
    g_w_out = mm_tn(o_flat, dmix, f"{tag}_gwout")
    do = mm_nn(dmix, w_out.T, BF16, f"{tag}_do").reshape(S, H, HEAD_DIM)
    dq, dc_q, dk_t, dv_t, dc_k = flash_bwd_q(qkv[0], qkv[1], qkv[2], o, do.transpose(1, 0, 2), qkv[0].transpose(0, 2, 1),
                                             do.transpose(1, 2, 0), lse, c_col, c_row, f"{tag}_flash_bwd")
    dzt, dbf = fox_gate_bwd(dc_q.reshape(H, S), dc_k.reshape(H, S), zt, bf, f"{tag}_gate_bwd")
    keys_major = lambda t: t.transpose(1, 3, 0, 2).reshape(S, D).astype(BF16)
    dqkv = jnp.concatenate([dq.transpose(1, 0, 2).reshape(S, D), keys_major(dk_t), keys_major(dv_t)], axis=1)
    dzf = _pad_to(dzt.T, 1)
    g_w_in = jnp.concatenate([mm_tn(x, dqkv, f"{tag}_gwqkv"), mm_tn(x, dzf, f"{tag}_gwf")[:, :H]], axis=1)
    w_in_t = w_in.T
    dx = mm_nn(dqkv, w_in_t[:3 * D], F32, f"{tag}_dx_qkv", add=dx_add, add_scale=alpha)
    dx = mm_nn(dzf, _pad_to(w_in_t[3 * D:], 0), F32, f"{tag}_dx_f", add=dx)
    return dx, (g_w_in, dbf.reshape(H), g_w_out)


def _ssm_dims(D):
    d_inner = 2 * D
    gn = SSM_GROUPS * SSM_STATE
    return d_inner, d_inner + 2 * gn, d_inner // HEAD_DIM


def _ssm_fwd(x, w_in, conv_w, conv_b, dt_bias, a_log, d_skip, norm_w, w_out, tag):
    S, D = x.shape
    DI, XBC, HS = _ssm_dims(D)
    dt_col = DI + XBC
    proj = mm_nn(x, _pad_to(w_in, 1), F32, f"{tag}_proj")
    conv_b = conv_b.reshape(1, XBC)
    xbc = conv_act_fwd(proj, DI, XBC, conv_w, conv_b, F32, f"{tag}_conv")
    dt_bias_p = _pad_to(dt_bias.reshape(1, HS), 1)
    a_log_p = _pad_to(a_log.reshape(1, HS), 1)
    y, states = ssd_scan_fwd(xbc, proj, dt_col, dt_bias_p, a_log_p, DI, f"{tag}_scan")
    d_vec = jnp.repeat(d_skip, HEAD_DIM).reshape(1, DI)
    norm_w = norm_w.reshape(1, DI)
    yn = ssd_gate_fwd(y, xbc, proj, d_vec, norm_w, DI, f"{tag}_gate")
    mix = mm_nn(yn, w_out, F32, f"{tag}_out")
    return mix, (proj, xbc, conv_b, dt_bias_p, a_log_p, y, states, d_vec, norm_w, yn)


def _ssm_bwd(x, dmix, dx_add, alpha, w_in, conv_w, w_out, saved, tag):
    S, D = x.shape
    DI, XBC, HS = _ssm_dims(D)
    dt_col = DI + XBC
    proj, xbc, conv_b, dt_bias_p, a_log_p, y, states, d_vec, norm_w, yn = saved
    g_w_out = mm_tn(yn, dmix, f"{tag}_gwout")
    dyn = mm_nn(dmix, w_out.T, F32, f"{tag}_dyn")
    dy, dskip, dz, g_norm_w, g_dvec = ssd_gate_bwd(dyn, y, xbc, proj, d_vec, norm_w, DI, f"{tag}_gate_bwd")
    dxbc, ddt_raw, g_a_log, g_dt_bias = ssd_scan_bwd(dy, dskip, xbc, proj, dt_col, dt_bias_p, a_log_p, states, DI,
                                                     f"{tag}_scan_bwd")
    dxbc_raw, _, g_conv_w, g_conv_b = conv_act_bwd(dxbc, proj, DI, XBC, conv_w, conv_b, f"{tag}_conv_bwd")
    g_w_in = jnp.concatenate([mm_tn(x, dz, f"{tag}_gwz"), mm_tn(x, dxbc_raw, f"{tag}_gwxbc"),
                              mm_tn(x, ddt_raw, f"{tag}_gwdt")[:, :HS]], axis=1)
    w_in_t = w_in.T
    dx = mm_nn(dz, w_in_t[:DI], F32, f"{tag}_dx_z", add=dx_add, add_scale=alpha)
    dx = mm_nn(dxbc_raw, w_in_t[DI:dt_col], F32, f"{tag}_dx_xbc", add=dx)
    dx = mm_nn(ddt_raw, _pad_to(w_in_t[dt_col:], 0), F32, f"{tag}_dx_dt", add=dx)
    g_d = g_dvec.reshape(HS, HEAD_DIM).sum(axis=-1)
    grads = (g_w_in, g_conv_w, g_conv_b.reshape(XBC), g_dt_bias[0, :HS], g_a_log[0, :HS], g_d, g_norm_w.reshape(DI), g_w_out)
    return dx, grads


ATTN_KEYS = ("attn_w_in", "attn_b_f", "attn_w_out")
SSM_KEYS = ("ssm_w_in", "ssm_conv_w", "ssm_conv_b", "ssm_dt_bias", "ssm_A_log", "ssm_D", "ssm_norm_w", "ssm_w_out")
LAYER_KEYS = ("ln_mix_g", "ln_mix_b", "ffn_w_up", "ffn_conv_w", "ffn_conv_b", "ffn_w_down", "ln_ffn_g", "ln_ffn_b",
              "ple_w_proj", "ple_w_gate", "ple_b_gate")


def local_step(x, p, target, w):
    S, D = x.shape
    depth = p.shape[0]
    alpha = (2 * depth) ** 0.25
    F = w["ffn_w_down"].shape[1]
    saved = []
    for i in range(depth):
        j, tag = i // 2, f"l{i}"
        if i % 2 == 0:
            mix, msaved = _attn_fwd(x, w["attn_w_in"][j], w["attn_b_f"][j], w["attn_w_out"][j], tag + "_attn")
        else:
            mix, msaved = _ssm_fwd(x, w["ssm_w_in"][j], w["ssm_conv_w"][j], w["ssm_conv_b"][j], w["ssm_dt_bias"][j],
                                   w["ssm_A_log"][j], w["ssm_D"][j], w["ssm_norm_w"][j], w["ssm_w_out"][j], tag + "_ssm")
        row = lambda k: w[k][i].reshape(1, -1)
        x1, xhat1, rstd1 = ln_fwd(x, mix, row("ln_mix_g"), row("ln_mix_b"), alpha, tag + "_ln_mix")
        h = mm_nn(x1, w["ffn_w_up"][i], F32, tag + "_ffn_up")
        a = conv_act_fwd(h, F, F, w["ffn_conv_w"][i], row("ffn_conv_b"), BF16, tag + "_ffn_act", gate_col=0)
        ffn = mm_nn(a, w["ffn_w_down"][i], F32, tag + "_ffn_down")
        x2, xhat2, rstd2 = ln_fwd(x1, ffn, row("ln_ffn_g"), row("ln_ffn_b"), alpha, tag + "_ln_ffn")
        zg = mm_nn(x2, w["ple_w_gate"][i], F32, tag + "_ple_gate")
        pp = mm_nn(p[i], w["ple_w_proj"][i], F32, tag + "_ple_proj")
        x3 = ple_fwd(x2, zg, pp, row("ple_b_gate"), tag + "_ple")
        saved.append((x, msaved, x1, xhat1, rstd1, h, a, x2, xhat2, rstd2, zg, pp))
        x = x3

    loss_vec, d = loss_head(x, target, "loss_head")

    grads = {k: [None] * w[k].shape[0] for k in ATTN_KEYS + SSM_KEYS + LAYER_KEYS}
    for i in reversed(range(depth)):
        j, tag = i // 2, f"l{i}"
        x0, msaved, x1, xhat1, rstd1, h, a, x2, xhat2, rstd2, zg, pp = saved[i]
        row = lambda k: w[k][i].reshape(1, -1)
        dzg, dpp, g_bg = ple_bwd(d, zg, pp, row("ple_b_gate"), tag + "_ple_bwd")
        grads["ple_w_gate"][i] = mm_tn(x2, dzg, tag + "_gw_ple_gate")
        grads["ple_w_proj"][i] = mm_tn(p[i], dpp, tag + "_gw_ple_proj")
        grads["ple_b_gate"][i] = g_bg.reshape(D)
        dx2 = mm_nn(dzg, w["ple_w_gate"][i].T, F32, tag + "_dx2", add=d)
        dr2, g_g2, g_b2 = ln_bwd(dx2, xhat2, rstd2, row("ln_ffn_g"), tag + "_ln_ffn_bwd")
        grads["ln_ffn_g"][i], grads["ln_ffn_b"][i] = g_g2.reshape(D), g_b2.reshape(D)
        grads["ffn_w_down"][i] = mm_tn(a, dr2, tag + "_gw_down")
        da = mm_nn(dr2, w["ffn_w_down"][i].T, F32, tag + "_da")
        dgin, du, g_cw, g_cb = conv_act_bwd(da, h, F, F, w["ffn_conv_w"][i], row("ffn_conv_b"), tag + "_ffn_act_bwd", gate_col=0)
        grads["ffn_conv_w"][i], grads["ffn_conv_b"][i] = g_cw, g_cb.reshape(F)
        grads["ffn_w_up"][i] = jnp.concatenate([mm_tn(x1, du, tag + "_gw_up_u"), mm_tn(x1, dgin, tag + "_gw_up_g")], axis=1)
        w_up_t = w["ffn_w_up"][i].T
        dx1 = mm_nn(du, w_up_t[:F], F32, tag + "_dx1_u", add=dr2, add_scale=alpha)
        dx1 = mm_nn(dgin, w_up_t[F:], F32, tag + "_dx1_g", add=dx1)
        dr1, g_g1, g_b1 = ln_bwd(dx1, xhat1, rstd1, row("ln_mix_g"), tag + "_ln_mix_bwd")
        grads["ln_mix_g"][i], grads["ln_mix_b"][i] = g_g1.reshape(D), g_b1.reshape(D)
        if i % 2 == 0:
            d, mg = _attn_bwd(x0, dr1, dr1, alpha, w["attn_w_in"][j], w["attn_w_out"][j], msaved, tag + "_attn")
            for k, g in zip(ATTN_KEYS, mg):
                grads[k][j] = g
        else:
            d, mg = _ssm_bwd(x0, dr1, dr1, alpha, w["ssm_w_in"][j], w["ssm_conv_w"][j], w["ssm_w_out"][j], msaved, tag + "_ssm")
            for k, g in zip(SSM_KEYS, mg):
                grads[k][j] = g
    return loss_vec, d, {k: jnp.stack(v) for k, v in grads.items()}


MESH = pl.DeviceIdType.MESH
PACK_ELEMS = 2 * SUBLANES * LANES


def _exchange(src, gather, name):
    out_shape = (N_DEV,) + tuple(src.shape[-2:])

    def body(src_ref, out_ref, send_sems, recv_sems, local_sem):
        x, y, c = lax.axis_index("x"), lax.axis_index("y"), lax.axis_index("c")
        me = 4 * x + 2 * y + c

        def block_for(dev):
            return src_ref if gather else src_ref.at[dev]

        local = pltpu.make_async_copy(block_for(me), out_ref.at[me], local_sem)
        local.start()
        copies = []
        for k in range(1, N_DEV):
            px = 1 - x if k & 4 else x
            py = 1 - y if k & 2 else y
            pc = 1 - c if k & 1 else c
            cp = pltpu.make_async_remote_copy(
                src_ref=block_for(4 * px + 2 * py + pc), dst_ref=out_ref.at[me],
                send_sem=send_sems.at[k - 1], recv_sem=recv_sems.at[k - 1],
                device_id=(px, py, pc), device_id_type=MESH)
            cp.start()
            copies.append(cp)
        for cp in copies:
            cp.wait()
        local.wait()

    return pl.pallas_call(
        body, name=name,
        in_specs=[pl.BlockSpec(memory_space=pl.ANY)],
        out_specs=pl.BlockSpec(memory_space=pl.ANY),
        out_shape=jax.ShapeDtypeStruct(out_shape, src.dtype),
        scratch_shapes=[pltpu.SemaphoreType.DMA((N_DEV - 1,)), pltpu.SemaphoreType.DMA((N_DEV - 1,)), pltpu.SemaphoreType.DMA],
    )(src)


def reduce_adamw(parts, w, m, v, name):
    _, R, _ = parts.shape
    tr = _pick(R, (512, 256, 128, 64, 32, 16))

    def body(p_ref, w_ref, m_ref, v_ref, g_ref, d_ref, nm_ref, nv_ref):
        g = p_ref[0].astype(F32)
        for s in range(1, N_DEV):
            g = g + p_ref[s].astype(F32)
        nm = ADAM_B1 * m_ref[...] + (1.0 - ADAM_B1) * g
        nv = ADAM_B2 * v_ref[...] + (1.0 - ADAM_B2) * (g * g)
        m_hat = nm / (1.0 - ADAM_B1 ** ADAM_STEP)
        v_hat = nv / (1.0 - ADAM_B2 ** ADAM_STEP)
        g_ref[...] = g
        d_ref[...] = -ADAM_LR * (m_hat / (jnp.sqrt(v_hat) + ADAM_EPS) + ADAM_WD * w_ref[...])
        nm_ref[...] = nm
        nv_ref[...] = nv

    row = pl.BlockSpec((tr, LANES), lambda i: (i, 0))
    return pl.pallas_call(
        body, name=name, grid=(R // tr,),
        in_specs=[pl.BlockSpec((N_DEV, tr, LANES), lambda i: (0, i, 0)), row, row, row],
        out_specs=[row, row, row, row],
        out_shape=[jax.ShapeDtypeStruct((R, LANES), F32)] * 4,
        compiler_params=_cparams("parallel"),
    )(parts, w, m, v)


def _pack(arrays, dtype, lead=0):
    parts = []
    for a in arrays:
        head = a.shape[:lead]
        flat = a.astype(dtype).reshape(head + (-1,))
        flat = jnp.pad(flat, [(0, 0)] * lead + [(0, (-flat.shape[-1]) % PACK_ELEMS)])
        parts.append(flat.reshape(head + (-1, LANES)))
    return jnp.concatenate(parts, axis=lead)


def _unpack(packed, shapes):
    lead = packed.shape[:-2]
    out, r0 = [], 0
    for shape in shapes:
        n = math.prod(shape)
        rows = -(-n // PACK_ELEMS) * (PACK_ELEMS // LANES)
        seg = packed[..., r0:r0 + rows, :].reshape(lead + (rows * LANES,))[..., :n]
        out.append(seg.reshape(lead + tuple(shape)))
        r0 += rows
    return out


MATMUL_SHARDED = {"attn_w_in": 2, "attn_w_out": 1, "ssm_w_in": 2, "ssm_w_out": 1, "ffn_w_up": 2, "ffn_w_down": 1,
                  "ple_w_proj": 2, "ple_w_gate": 1}
SMALL_SHARDED = {"ssm_conv_w": 2, "ssm_conv_b": 1, "ssm_norm_w": 1, "ffn_conv_w": 2}
REPLICATED = ("attn_b_f", "ssm_dt_bias", "ssm_A_log", "ssm_D", "ln_mix_g", "ln_mix_b", "ffn_conv_b", "ln_ffn_g",
              "ln_ffn_b", "ple_b_gate")
WEIGHT_ORDER = ("attn_w_in", "attn_b_f", "attn_w_out", "ssm_w_in", "ssm_conv_w", "ssm_conv_b", "ssm_dt_bias", "ssm_A_log",
                "ssm_D", "ssm_norm_w", "ssm_w_out", "ln_mix_g", "ln_mix_b", "ffn_w_up", "ffn_conv_w", "ffn_conv_b",
                "ffn_w_down", "ln_ffn_g", "ln_ffn_b", "ple_w_proj", "ple_w_gate", "ple_b_gate")


def _join_shards(gathered, axis):
    moved = jnp.moveaxis(gathered, 0, axis)
    shape = list(moved.shape)
    shape[axis:axis + 2] = [shape[axis] * shape[axis + 1]]
    return moved.reshape(shape)


def _split_shards(full, axis):
    shape = list(full.shape)
    shape[axis:axis + 1] = [N_DEV, shape[axis] // N_DEV]
    return jnp.moveaxis(full.reshape(shape), axis, 0)


def kernel(x, p, attn_w_in, attn_b_f, attn_w_out, ssm_w_in, ssm_conv_w, ssm_conv_b, ssm_dt_bias, ssm_A_log, ssm_D, ssm_norm_w, ssm_w_out, ln_mix_g, ln_mix_b, ffn_w_up, ffn_conv_w, ffn_conv_b, ffn_w_down, ln_ffn_g, ln_ffn_b, ple_w_proj, ple_w_gate, ple_b_gate, loss_target, m_attn_w_in, m_attn_b_f, m_attn_w_out, m_ssm_w_in, m_ssm_conv_w, m_ssm_conv_b, m_ssm_dt_bias, m_ssm_A_log, m_ssm_D, m_ssm_norm_w, m_ssm_w_out, m_ln_mix_g, m_ln_mix_b, m_ffn_w_up, m_ffn_conv_w, m_ffn_conv_b, m_ffn_w_down, m_ln_ffn_g, m_ln_ffn_b, m_ple_w_proj, m_ple_w_gate, m_ple_b_gate, v_attn_w_in, v_attn_b_f, v_attn_w_out, v_ssm_w_in, v_ssm_conv_w, v_ssm_conv_b, v_ssm_dt_bias, v_ssm_A_log, v_ssm_D, v_ssm_norm_w, v_ssm_w_out, v_ln_mix_g, v_ln_mix_b, v_ffn_w_up, v_ffn_conv_w, v_ffn_conv_b, v_ffn_w_down, v_ln_ffn_g, v_ln_ffn_b, v_ple_w_proj, v_ple_w_gate, v_ple_b_gate):
    args = dict(locals())
    w_loc = {k: args[k] for k in WEIGHT_ORDER}
    m_loc = {k: args["m_" + k] for k in WEIGHT_ORDER}
    v_loc = {k: args["v_" + k] for k in WEIGHT_ORDER}
    mm_names, small_names = tuple(MATMUL_SHARDED), tuple(SMALL_SHARDED)
    sharded = mm_names + small_names
    axis_of = {**MATMUL_SHARDED, **SMALL_SHARDED}

    g_mm = _exchange(_pack([w_loc[k] for k in mm_names], BF16), True, "gather_matmul_weights")
    g_small = _exchange(_pack([w_loc[k] for k in small_names], F32), True, "gather_small_weights")
    w_full = {k: w_loc[k] for k in REPLICATED}
    for names, gathered in ((mm_names, g_mm), (small_names, g_small)):
        for k, blocks in zip(names, _unpack(gathered, [w_loc[k].shape for k in names])):
            w_full[k] = _join_shards(blocks, axis_of[k])

    loss_vec, grad_x, g_full = local_step(x[0], p[:, 0], loss_target[0], w_full)

    send = _pack([_split_shards(g_full[k], axis_of[k]) for k in sharded], BF16, lead=1)
    parts = _exchange(send, False, "exchange_weight_grads")
    shapes = [w_loc[k].shape for k in sharded]
    pk = lambda d: _pack([d[k] for k in sharded], F32)
    outs = reduce_adamw(parts, pk(w_loc), pk(m_loc), pk(v_loc), "reduce_adamw_sharded")
    res = {k: vals for k, vals in zip(sharded, zip(*[_unpack(o, shapes) for o in outs]))}

    rep_shapes = [w_loc[k].shape for k in REPLICATED] + [(1, LANES)]
    rparts = _exchange(_pack([g_full[k] for k in REPLICATED] + [loss_vec], F32), True, "gather_replicated_grads")
    zero = jnp.zeros((1, LANES), F32)
    rk = lambda d: _pack([d[k] for k in REPLICATED] + [zero], F32)
    routs = reduce_adamw(rparts, rk(w_loc), rk(m_loc), rk(v_loc), "reduce_adamw_replicated")
    runp = [_unpack(o, rep_shapes) for o in routs]
    for i, k in enumerate(REPLICATED):
        res[k] = tuple(u[i] for u in runp)
    loss = runp[0][-1][0, 0]

    return (loss, grad_x[None], *[res[k][0] for k in WEIGHT_ORDER], *[res[k][1] for k in WEIGHT_ORDER],
            *[res[k][2] for k in WEIGHT_ORDER], *[res[k][3] for k in WEIGHT_ORDER])
```

```python
import functools
import math

import jax
import jax.numpy as jnp
from jax import lax
from jax.experimental import pallas as pl
from jax.experimental.pallas import tpu as pltpu

F32 = jnp.float32
BF16 = jnp.bfloat16

LANES = 128
SUBLANES = 8
VMEM_LIMIT_BYTES = 56 * 1024 * 1024

N_DEV = 8
HEAD_DIM = 64
SSM_GROUPS = 8
SSM_STATE = 128
SSM_CHUNK = 128
SSM_CONV = 4
FFN_CONV = 3
LN_EPS = 1e-5
RMS_EPS = 1e-5
ADAM_LR, ADAM_B1, ADAM_B2, ADAM_EPS, ADAM_WD, ADAM_STEP = 0.001, 0.9, 0.999, 1e-08, 0.01, 10
NEG_INF = float("-inf")
HIGHEST = lax.Precision.HIGHEST
NT_DIMS = (((1,), (1,)), ((), ()))
TN_DIMS = (((0,), (0,)), ((), ()))


def _cparams(*sem):
    return pltpu.CompilerParams(dimension_semantics=sem, vmem_limit_bytes=VMEM_LIMIT_BYTES)


def _pick(n, candidates):
    for c in candidates:
        if n % c == 0:
            return c
    return n


def _dot(a, b):
    return jnp.dot(a.astype(BF16), b.astype(BF16), preferred_element_type=F32)


def _dot_nt(a, b):
    return lax.dot_general(a.astype(BF16), b.astype(BF16), NT_DIMS, preferred_element_type=F32)


def _dot_tn(a, b):
    return lax.dot_general(a.astype(BF16), b.astype(BF16), TN_DIMS, preferred_element_type=F32)


def _sigmoid(x):
    return 1.0 / (1.0 + jnp.exp(-x))


def _log1p_small(u):
    return jnp.where(u < 1e-3, u * (1.0 - u * (0.5 - u * (1.0 / 3.0))), jnp.log(1.0 + u))


def _softplus(x):
    return jnp.maximum(x, 0.0) + _log1p_small(jnp.exp(-jnp.abs(x)))


def mm_nn(a, b, out_dtype, name, add=None, add_scale=1.0):
    M, K = a.shape
    _, N = b.shape
    tm = _pick(M, (1024, 512, 256, 128))
    tn = N if N <= 1024 else _pick(N, (1408, 1024, 896, 768, 640, 512, 384, 256, 128))
    tk = K if K <= 2048 else _pick(K, (1408, 1024, 896, 768, 640, 512, 384, 256, 128))
    nk = K // tk

    def body(*refs):
        if add is None:
            a_ref, b_ref, o_ref, acc_ref = refs
        else:
            a_ref, b_ref, c_ref, o_ref, acc_ref = refs
        k = pl.program_id(2)
        part = _dot(a_ref[...], b_ref[...])

        @pl.when(k == 0)
        def _():
            acc_ref[...] = part

        @pl.when(k > 0)
        def _():
            acc_ref[...] += part

        @pl.when(k == nk - 1)
        def _():
            r = acc_ref[...]
            if add is not None:
                r = r + add_scale * c_ref[...].astype(F32)
            o_ref[...] = r.astype(o_ref.dtype)

    in_specs = [pl.BlockSpec((tm, tk), lambda i, j, k: (i, k)), pl.BlockSpec((tk, tn), lambda i, j, k: (k, j))]
    args = [a, b]
    if add is not None:
        in_specs.append(pl.BlockSpec((tm, tn), lambda i, j, k: (i, j)))
        args.append(add)
    return pl.pallas_call(
        body,
        name=name,
        grid=(M // tm, N // tn, nk),
        in_specs=in_specs,
        out_specs=pl.BlockSpec((tm, tn), lambda i, j, k: (i, j)),
        out_shape=jax.ShapeDtypeStruct((M, N), out_dtype),
        scratch_shapes=[pltpu.VMEM((tm, tn), F32)],
        compiler_params=_cparams("parallel", "parallel", "arbitrary"),
    )(*args)


def mm_tn(a, b, name):
    M, K = a.shape
    _, N = b.shape
    tm = _pick(M, (512, 256, 128))
    tk = K if K <= 1024 else _pick(K, (1408, 1024, 896, 768, 640, 512, 384, 256, 128))
    tn = N if N <= 1408 else _pick(N, (1408, 1024, 896, 768, 640, 512, 384, 256, 128))
    nm = M // tm

    def body(a_ref, b_ref, o_ref):
        m = pl.program_id(2)
        part = _dot_tn(a_ref[...], b_ref[...])

        @pl.when(m == 0)
        def _():
            o_ref[...] = part

        @pl.when(m > 0)
        def _():
            o_ref[...] += part

    return pl.pallas_call(
        body,
        name=name,
        grid=(K // tk, N // tn, nm),
        in_specs=[pl.BlockSpec((tm, tk), lambda i, j, m: (m, i)), pl.BlockSpec((tm, tn), lambda i, j, m: (m, j))],
        out_specs=pl.BlockSpec((tk, tn), lambda i, j, m: (i, j)),
        out_shape=jax.ShapeDtypeStruct((K, N), F32),
        compiler_params=_cparams("parallel", "parallel", "arbitrary"),
    )(a, b)


def ln_fwd(x, mix, g, b, alpha, name):
    S, D = x.shape
    tm = _pick(S, (512, 256, 128))

    def body(x_ref, mix_ref, g_ref, b_ref, y_ref, xhat_ref, rstd_ref):
        r = alpha * x_ref[...] + mix_ref[...]
        mu = jnp.mean(r, axis=-1, keepdims=True)
        xc = r - mu
        var = jnp.mean(xc * xc, axis=-1, keepdims=True)
        rstd = lax.rsqrt(var + LN_EPS)
        xhat = xc * rstd
        y_ref[...] = xhat * g_ref[...] + b_ref[...]
        xhat_ref[...] = xhat
        rstd_ref[...] = rstd

    row = pl.BlockSpec((tm, D), lambda i: (i, 0))
    vec = pl.BlockSpec((1, D), lambda i: (0, 0))
    return pl.pallas_call(
        body,
        name=name,
        grid=(S // tm,),
        in_specs=[row, row, vec, vec],
        out_specs=[row, row, pl.BlockSpec((tm, 1), lambda i: (i, 0))],
        out_shape=[jax.ShapeDtypeStruct((S, D), F32), jax.ShapeDtypeStruct((S, D), F32), jax.ShapeDtypeStruct((S, 1), F32)],
        compiler_params=_cparams("parallel"),
    )(x, mix, g, b)


def ln_bwd(dy, xhat, rstd, g, name):
    S, D = dy.shape
    tm = _pick(S, (512, 256, 128))

    def body(dy_ref, xhat_ref, rstd_ref, g_ref, dr_ref, dg_ref, db_ref):
        i = pl.program_id(0)
        dyv = dy_ref[...]
        xh = xhat_ref[...]
        dxh = dyv * g_ref[...]
        m1 = jnp.mean(dxh, axis=-1, keepdims=True)
        m2 = jnp.mean(dxh * xh, axis=-1, keepdims=True)
        dr_ref[...] = rstd_ref[...] * (dxh - m1 - xh * m2)
        dg_part = jnp.sum(dyv * xh, axis=0, keepdims=True)
        db_part = jnp.sum(dyv, axis=0, keepdims=True)

        @pl.when(i == 0)
        def _():
            dg_ref[...] = dg_part
            db_ref[...] = db_part

        @pl.when(i > 0)
        def _():
            dg_ref[...] += dg_part
            db_ref[...] += db_part

    row = pl.BlockSpec((tm, D), lambda i: (i, 0))
    vec = pl.BlockSpec((1, D), lambda i: (0, 0))
    return pl.pallas_call(
        body,
        name=name,
        grid=(S // tm,),
        in_specs=[row, row, pl.BlockSpec((tm, 1), lambda i: (i, 0)), vec],
        out_specs=[row, vec, vec],
        out_shape=[jax.ShapeDtypeStruct((S, D), F32), jax.ShapeDtypeStruct((1, D), F32), jax.ShapeDtypeStruct((1, D), F32)],
        compiler_params=_cparams("arbitrary"),
    )(dy, xhat, rstd, g)


HALO = SUBLANES


def _prev_halo_spec(tm, tc, col0):
    return pl.BlockSpec((HALO, tc), lambda i, j: (jnp.maximum(i * (tm // HALO) - 1, 0), j + col0))


def _next_halo_spec(tm, tc, col0, n_row_tiles):
    last = n_row_tiles * (tm // HALO) - 1
    return pl.BlockSpec((HALO, tc), lambda i, j: (jnp.minimum((i + 1) * (tm // HALO), last), j + col0))


CONV_CHUNK = 32


def _causal_conv(ext_ref, w, n_taps, n_rows, row0, lanes):
    acc = None
    for k in range(n_taps):
        term = ext_ref[pl.ds(row0 - (n_taps - 1) + k, n_rows), lanes] * w[k:k + 1, :]
        acc = term if acc is None else acc + term
    return acc


def _anticausal_conv(ext_ref, w, n_taps, n_rows, row0, lanes):
    acc = None
    for k in range(n_taps):
        term = ext_ref[pl.ds(row0 + n_taps - 1 - k, n_rows), lanes] * w[k:k + 1, :]
        acc = term if acc is None else acc + term
    return acc


def _fold8(a):
    acc = a[0:SUBLANES]
    for g in range(SUBLANES, a.shape[0], SUBLANES):
        acc = acc + a[g:g + SUBLANES]
    return acc


INV_SQRT2 = 1.0 / math.sqrt(2.0)
INV_SQRT_2PI = 1.0 / math.sqrt(2.0 * math.pi)


def _gelu(g):
    return 0.5 * g * (1.0 + lax.erf(g * INV_SQRT2))


def _silu(x):
    return x * _sigmoid(x)


def _silu_grad(x):
    s = _sigmoid(x)
    return s * (1.0 + x * (1.0 - s))


def _conv_tiles(S, C, cols):
    tm = _pick(S, (256, 128))
    for tc in (1408, 1024, 512, 256, 128):
        if C % tc == 0 and all(c % tc == 0 for c in cols):
            return tm, tc
    raise ValueError("no column tile for the conv kernels")


def conv_act_fwd(src, in_col, C, conv_w, conv_b, out_dtype, name, gate_col=None):
    S = src.shape[0]
    K = conv_w.shape[0]
    gated = gate_col is not None
    tm, tc = _conv_tiles(S, C, [in_col] + ([gate_col] if gated else []))
    c_in = in_col // tc
    c_gate = gate_col // tc if gated else 0

    def body(*refs):
        if gated:
            x_ref, xp_ref, w_ref, b_ref, u_ref, o_ref, ext_ref = refs
        else:
            x_ref, xp_ref, w_ref, b_ref, o_ref, ext_ref = refs
        i = pl.program_id(0)
        ext_ref[0:HALO] = jnp.where(i > 0, xp_ref[...], 0.0)
        ext_ref[HALO:HALO + tm] = x_ref[...]
        for l0 in range(0, tc, LANES):
            ls = slice(l0, l0 + LANES)
            w = w_ref[:, ls]
            b = b_ref[:, ls]
            for r0 in range(0, tm, CONV_CHUNK):
                pre = _causal_conv(ext_ref, w, K, CONV_CHUNK, HALO + r0, ls) + b
                out = _gelu(pre) * u_ref[pl.ds(r0, CONV_CHUNK), ls] if gated else _silu(pre)
                o_ref[pl.ds(r0, CONV_CHUNK), ls] = out.astype(o_ref.dtype)

    in_specs = [
        pl.BlockSpec((tm, tc), lambda i, j: (i, j + c_in)),
        _prev_halo_spec(tm, tc, c_in),
        pl.BlockSpec((K, tc), lambda i, j: (0, j)),
        pl.BlockSpec((1, tc), lambda i, j: (0, j)),
    ]
    args = [src, src, conv_w, conv_b]
    if gated:
        in_specs.append(pl.BlockSpec((tm, tc), lambda i, j: (i, j + c_gate)))
        args.append(src)
    return pl.pallas_call(
        body,
        name=name,
        grid=(S // tm, C // tc),
        in_specs=in_specs,
        out_specs=pl.BlockSpec((tm, tc), lambda i, j: (i, j)),
        out_shape=jax.ShapeDtypeStruct((S, C), out_dtype),
        scratch_shapes=[pltpu.VMEM((tm + HALO, tc), F32)],
        compiler_params=_cparams("parallel", "parallel"),
    )(*args)


def conv_act_bwd(d_out, src, in_col, C, conv_w, conv_b, name, gate_col=None):
    S = src.shape[0]
    K = conv_w.shape[0]
    gated = gate_col is not None
    tm, tc = _conv_tiles(S, C, [in_col] + ([gate_col] if gated else []))
    c_in = in_col // tc
    c_gate = gate_col // tc if gated else 0
    ni = S // tm
    te = tm + HALO

    def body(*refs):
        if gated:
            (d_ref, dn_ref, x_ref, xp_ref, xn_ref, w_ref, b_ref, u_ref, un_ref,
             dx_ref, dw_ref, db_ref, du_ref, xext_ref, dext_ref) = refs
        else:
            (d_ref, dn_ref, x_ref, xp_ref, xn_ref, w_ref, b_ref,
             dx_ref, dw_ref, db_ref, xext_ref, dext_ref) = refs
        i = pl.program_id(1)
        xext_ref[0:HALO] = jnp.where(i > 0, xp_ref[...], 0.0)
        xext_ref[HALO:HALO + tm] = x_ref[...]
        xext_ref[HALO + tm:HALO + te] = xn_ref[...]

        @pl.when(i == 0)
        def _():
            dw_ref[...] = jnp.zeros_like(dw_ref)
            db_ref[...] = jnp.zeros_like(db_ref)

        for l0 in range(0, tc, LANES):
            ls = slice(l0, l0 + LANES)
            w = w_ref[:, ls]
            b = b_ref[:, ls]
            acc_w = [jnp.zeros((SUBLANES, LANES), F32) for _ in range(K)]
            acc_b = jnp.zeros((SUBLANES, LANES), F32)
            for r0 in range(0, te, CONV_CHUNK):
                n = min(CONV_CHUNK, te - r0)
                inside = r0 < tm
                taps = [xext_ref[pl.ds(HALO + r0 - (K - 1) + k, n), ls] for k in range(K)]
                pre = sum(t * w[k:k + 1, :] for k, t in enumerate(taps)) + b
                d = (d_ref[pl.ds(r0, n), ls] if inside else dn_ref[:, ls]).astype(F32)
                if gated:
                    u = u_ref[pl.ds(r0, n), ls] if inside else un_ref[:, ls]
                    cdf = 0.5 * (1.0 + lax.erf(pre * INV_SQRT2))
                    dpre = d * u * (cdf + pre * jnp.exp(-0.5 * pre * pre) * INV_SQRT_2PI)
                    if inside:
                        du_ref[pl.ds(r0, n), ls] = (d * (pre * cdf)).astype(du_ref.dtype)
                else:
                    dpre = d * _silu_grad(pre)
                if inside:
                    for k in range(K):
                        acc_w[k] = acc_w[k] + _fold8(dpre * taps[k])
                    acc_b = acc_b + _fold8(dpre)
                else:
                    dpre = jnp.where(i < ni - 1, dpre, 0.0)
                dext_ref[pl.ds(r0, n), ls] = dpre
            for r0 in range(0, tm, CONV_CHUNK):
                dx = _anticausal_conv(dext_ref, w, K, CONV_CHUNK, r0, ls)
                dx_ref[pl.ds(r0, CONV_CHUNK), ls] = dx.astype(dx_ref.dtype)
            dw_rows = [jnp.sum(a, axis=0, keepdims=True) for a in acc_w]
            dw_ref[:, ls] += jnp.concatenate(dw_rows + [jnp.zeros((SUBLANES - K, LANES), F32)], axis=0)
            db_ref[:, ls] += jnp.sum(acc_b, axis=0, keepdims=True)

    last = ni * (tm // HALO) - 1
    cur = lambda c0: pl.BlockSpec((tm, tc), lambda j, i: (i, j + c0))
    prev = lambda c0: pl.BlockSpec((HALO, tc), lambda j, i: (jnp.maximum(i * (tm // HALO) - 1, 0), j + c0))
    nxt = lambda c0: pl.BlockSpec((HALO, tc), lambda j, i: (jnp.minimum((i + 1) * (tm // HALO), last), j + c0))
    vec = lambda rows: pl.BlockSpec((rows, tc), lambda j, i: (0, j))
    in_specs = [cur(0), nxt(0), cur(c_in), prev(c_in), nxt(c_in), vec(K), vec(1)]
    args = [d_out, d_out, src, src, src, conv_w, conv_b]
    out_specs = [cur(0), vec(SUBLANES), vec(1)]
    out_shape = [jax.ShapeDtypeStruct((S, C), BF16), jax.ShapeDtypeStruct((SUBLANES, C), F32), jax.ShapeDtypeStruct((1, C), F32)]
    if gated:
        in_specs += [cur(c_gate), nxt(c_gate)]
        args += [src, src]
        out_specs.append(cur(0))
        out_shape.append(jax.ShapeDtypeStruct((S, C), BF16))
    outs = pl.pallas_call(
        body,
        name=name,
        grid=(C // tc, ni),
        in_specs=in_specs,
        out_specs=out_specs,
        out_shape=out_shape,
        scratch_shapes=[pltpu.VMEM((tm + 2 * HALO, tc), F32), pltpu.VMEM((te, tc), F32)],
        compiler_params=_cparams("parallel", "arbitrary"),
    )(*args)
    return outs[0], (outs[3] if gated else None), outs[1][:K], outs[2]


def ple_fwd(x2, zg, pp, bg, name):
    S, D = x2.shape
    tm = _pick(S, (512, 256, 128))

    def body(x_ref, z_ref, p_ref, b_ref, o_ref):
        o_ref[...] = x_ref[...] + _sigmoid(z_ref[...] + b_ref[...]) * p_ref[...]

    row = pl.BlockSpec((tm, D), lambda i: (i, 0))
    return pl.pallas_call(
        body, name=name, grid=(S // tm,), in_specs=[row, row, row, pl.BlockSpec((1, D), lambda i: (0, 0))], out_specs=row,
        out_shape=jax.ShapeDtypeStruct((S, D), F32), compiler_params=_cparams("parallel"),
    )(x2, zg, pp, bg)


def ple_bwd(dx3, zg, pp, bg, name):
    S, D = dx3.shape
    tm = _pick(S, (512, 256, 128))

    def body(d_ref, z_ref, p_ref, b_ref, dz_ref, dp_ref, db_ref):
        i = pl.program_id(0)
        d = d_ref[...]
        gate = _sigmoid(z_ref[...] + b_ref[...])
        dz = d * p_ref[...] * gate * (1.0 - gate)
        dz_ref[...] = dz.astype(dz_ref.dtype)
        dp_ref[...] = (d * gate).astype(dp_ref.dtype)
        part = jnp.sum(dz, axis=0, keepdims=True)

        @pl.when(i == 0)
        def _():
            db_ref[...] = part

        @pl.when(i > 0)
        def _():
            db_ref[...] += part

    row = pl.BlockSpec((tm, D), lambda i: (i, 0))
    vec = pl.BlockSpec((1, D), lambda i: (0, 0))
    return pl.pallas_call(
        body, name=name, grid=(S // tm,), in_specs=[row, row, row, vec], out_specs=[row, row, vec],
        out_shape=[jax.ShapeDtypeStruct((S, D), BF16), jax.ShapeDtypeStruct((S, D), BF16), jax.ShapeDtypeStruct((1, D), F32)],
        compiler_params=_cparams("arbitrary"),
    )(dx3, zg, pp, bg)


def loss_head(y, target, name):
    S, D = y.shape
    tm = _pick(S, (512, 256, 128))

    def body(y_ref, t_ref, loss_ref, dy_ref, acc_ref):
        i = pl.program_id(0)
        err = y_ref[...] - t_ref[...]
        dy_ref[...] = err * (1.0 / D)
        part = jnp.sum(err * err, axis=0, keepdims=True)

        @pl.when(i == 0)
        def _():
            acc_ref[...] = part

        @pl.when(i > 0)
        def _():
            acc_ref[...] += part

        @pl.when(i == pl.num_programs(0) - 1)
        def _():
            loss_ref[...] = jnp.zeros((1, LANES), F32) + (0.5 / D) * jnp.sum(acc_ref[...])

    row = pl.BlockSpec((tm, D), lambda i: (i, 0))
    return pl.pallas_call(
        body, name=name, grid=(S // tm,), in_specs=[row, row],
        out_specs=[pl.BlockSpec((1, LANES), lambda i: (0, 0)), row],
        out_shape=[jax.ShapeDtypeStruct((1, LANES), F32), jax.ShapeDtypeStruct((S, D), F32)],
        scratch_shapes=[pltpu.VMEM((1, D), F32)],
        compiler_params=_cparams("arbitrary"),
    )(y, target)


ATTN_TILE = 512
ATTN_SCALE = 1.0 / math.sqrt(HEAD_DIM)


def _attn_tile(S):
    return _pick(S, (ATTN_TILE, 256, 128))


def fox_gate_fwd(zt, bf, name):
    H, S = zt.shape
    tl = _pick(S, (512, 256, 128))

    def body(z_ref, b_ref, c_ref, carry_ref):
        i = pl.program_id(0)

        @pl.when(i == 0)
        def _():
            carry_ref[...] = jnp.zeros_like(carry_ref)

        z = z_ref[...] + b_ref[...]
        logf = jnp.minimum(z, 0.0) - _log1p_small(jnp.exp(-jnp.abs(z)))
        r = lax.broadcasted_iota(jnp.int32, (tl, tl), 0)
        c = lax.broadcasted_iota(jnp.int32, (tl, tl), 1)
        upper = (r <= c).astype(F32)
        cum = jnp.dot(logf, upper, precision=HIGHEST, preferred_element_type=F32) + carry_ref[...]
        c_ref[...] = cum
        carry_ref[...] = cum[:, tl - 1:tl]

    return pl.pallas_call(
        body, name=name, grid=(S // tl,),
        in_specs=[pl.BlockSpec((H, tl), lambda i: (0, i)), pl.BlockSpec((H, 1), lambda i: (0, 0))],
        out_specs=pl.BlockSpec((H, tl), lambda i: (0, i)),
        out_shape=jax.ShapeDtypeStruct((H, S), F32),
        scratch_shapes=[pltpu.VMEM((H, 1), F32)],
        compiler_params=_cparams("arbitrary"),
    )(zt, bf)


def fox_gate_bwd(dc_q, dc_k, zt, bf, name):
    H, S = zt.shape
    tl = _pick(S, (512, 256, 128))
    nt = S // tl

    def body(dcq_ref, dck_ref, z_ref, b_ref, dz_ref, db_ref, carry_ref):
        i = pl.program_id(0)

        @pl.when(i == 0)
        def _():
            carry_ref[...] = jnp.zeros_like(carry_ref)
            db_ref[...] = jnp.zeros_like(db_ref)

        r = lax.broadcasted_iota(jnp.int32, (tl, tl), 0)
        c = lax.broadcasted_iota(jnp.int32, (tl, tl), 1)
        lower = (r >= c).astype(F32)
        dc = dcq_ref[...] + dck_ref[...]
        suffix = jnp.dot(dc, lower, precision=HIGHEST, preferred_element_type=F32) + carry_ref[...]
        carry_ref[...] = suffix[:, 0:1]
        dz = suffix * _sigmoid(-(z_ref[...] + b_ref[...]))
        dz_ref[...] = dz
        db_ref[...] += jnp.sum(dz, axis=1, keepdims=True)

    rev = pl.BlockSpec((H, tl), lambda i: (0, nt - 1 - i))
    return pl.pallas_call(
        body, name=name, grid=(nt,),
        in_specs=[rev, rev, rev, pl.BlockSpec((H, 1), lambda i: (0, 0))],
        out_specs=[rev, pl.BlockSpec((H, 1), lambda i: (0, 0))],
        out_shape=[jax.ShapeDtypeStruct((H, S), F32), jax.ShapeDtypeStruct((H, 1), F32)],
        scratch_shapes=[pltpu.VMEM((H, 1), F32)],
        compiler_params=_cparams("arbitrary"),
    )(dc_q, dc_k, zt, bf)


ATTN_SUB = 2
ATTN_CHUNK = 16


def _row_tiles(ref, j, n_tiles):
    if n_tiles == 1:
        return ref[0, j]
    return jnp.concatenate([ref[0, j + t] for t in range(n_tiles)], axis=1)


def _diag_mask(n_rows, width, row0):
    r = lax.broadcasted_iota(jnp.int32, (n_rows, width), 0) + row0
    c = lax.broadcasted_iota(jnp.int32, (n_rows, width), 1)
    return r >= c


def _causal_sweep(i, process):
    def pair_body(j2, carry):
        process(2 * j2, 2, False)
        return carry

    lax.fori_loop(0, i // 2, pair_body, 0)

    @pl.when(i % 2 == 1)
    def _():
        process(i - 1, 2, True)

    @pl.when(i % 2 == 0)
    def _():
        process(i, 1, True)


def _ride_along(ride, refs, n_in, n_out, n_scratch, first, last):
    n = len(ride)
    ins, srcs = refs[:n_in], refs[n_in:n_in + n]
    outs, dsts = refs[n_in + n:n_in + n + n_out], refs[n_in + n + n_out:n_in + 2 * n + n_out]
    scratch, sems = refs[n_in + 2 * n + n_out:n_in + 2 * n + n_out + n_scratch], refs[n_in + 2 * n + n_out + n_scratch:]
    copies = _ride_copies(ride, srcs, dsts, sems) if n else []

    @pl.when(first)
    def _():
        for cp in copies:
            cp.start()

    def finish():
        @pl.when(last)
        def _():
            for cp in copies:
                cp.wait()

    return ins, outs, scratch, finish


def flash_fwd(q, k, v_ones, c_col, c_row, name, ride=()):
    H, S, Dh = q.shape
    T = _attn_tile(S)
    NT = S // T
    TS = T // ATTN_SUB
    ride_in, ride_out, ride_shape, ride_sems = _ride_decl(ride)

    def body(*refs):
        h, i = pl.program_id(0), pl.program_id(1)
        (q_ref, k_ref, v_ref, cq_ref, ck_ref), (o_ref, lse_ref), (m_ref, acc_ref), finish = _ride_along(
            ride, refs, 5, 2, 2, (h == 0) & (i == 0), (h == H - 1) & (i == NT - 1))
        qs = q_ref[0] * ATTN_SCALE
        c_ref = cq_ref[0, 0:1, :]
        m_ref[...] = jnp.full_like(m_ref, NEG_INF)
        acc_ref[...] = jnp.zeros_like(acc_ref)

        def process(j, n_tiles, masked):
            width = n_tiles * T
            off = pl.multiple_of(j * T, T)
            kj = k_ref[0, pl.ds(off, width), :]
            vj = v_ref[0, pl.ds(off, width), :]
            ckj = _row_tiles(ck_ref, j, n_tiles) - c_ref
            ss = [_dot_nt(qs[u * TS:(u + 1) * TS], kj) for u in range(ATTN_SUB)]
            for u in range(ATTN_SUB):
                rows = slice(u * TS, (u + 1) * TS)
                m_prev = m_ref[rows]
                ps, m_news = [], []
                for r0 in range(0, TS, ATTN_CHUNK):
                    rc = slice(r0, r0 + ATTN_CHUNK)
                    s = ss[u][rc] - ckj
                    if masked:
                        s = jnp.where(_diag_mask(ATTN_CHUNK, width, u * TS + r0 + width - T), s, NEG_INF)
                    m_new = jnp.maximum(m_prev[rc], jnp.max(s, axis=1, keepdims=True))
                    ps.append(jnp.exp(s - jnp.tile(m_new, (1, width // LANES))).astype(BF16))
                    m_news.append(m_new)
                m_new = jnp.concatenate(m_news, axis=0)
                acc_ref[rows] = jnp.exp(m_prev - m_new) * acc_ref[rows] + _dot(jnp.concatenate(ps, axis=0), vj)
                m_ref[rows] = m_new

        _causal_sweep(i, process)
        acc = acc_ref[...]
        l = acc[:, Dh:Dh + 1]
        o_ref[0] = (acc[:, 0:Dh] / l).astype(o_ref.dtype)
        lse_ref[0] = m_ref[:, 0:1] + jnp.log(l) + (cq_ref[0] - c_ref)
        finish()

    tile = pl.BlockSpec((1, T, Dh), lambda h, i: (h, i, 0))
    whole = pl.BlockSpec((1, S, Dh), lambda h, i: (h, 0, 0))
    whole_v = pl.BlockSpec((1, S, 2 * Dh), lambda h, i: (h, 0, 0))
    col = pl.BlockSpec((1, T, 1), lambda h, i: (h, i, 0))
    rows = pl.BlockSpec((1, NT, 1, T), lambda h, i: (h, 0, 0, 0))
    return pl.pallas_call(
        body, name=name, grid=(H, NT),
        in_specs=[tile, whole, whole_v, col, rows] + ride_in,
        out_specs=[tile, col] + ride_out,
        out_shape=[jax.ShapeDtypeStruct((H, S, Dh), BF16), jax.ShapeDtypeStruct((H, S, 1), F32)] + ride_shape,
        scratch_shapes=[pltpu.VMEM((T, LANES), F32), pltpu.VMEM((T, 2 * Dh), F32)] + ride_sems,
        compiler_params=_cparams("arbitrary", "arbitrary"),
    )(q, k, v_ones, c_col, c_row, *[a for a, _ in ride])


def flash_bwd_q(q, k, v, o, do, q_t, do_t, lse, c_col, c_row, name, ride=()):
    H, S, Dh = q.shape
    T = _attn_tile(S)
    NT = S // T
    TS = T // ATTN_SUB

    ride_in, ride_out, ride_shape, ride_sems = _ride_decl(ride)

    def body(*refs):
        head, i = pl.program_id(0), pl.program_id(1)
        ((q_ref, k_ref, v_ref, o_ref, do_ref, qt_ref, dot_ref, lse_ref, cq_ref, ck_ref),
         (dq_ref, dcq_ref, dkt_ref, dvt_ref, dck_ref), (dq_acc, dcq_acc), finish) = _ride_along(
            ride, refs, 10, 5, 2, (head == 0) & (i == 0), (head == H - 1) & (i == NT - 1))

        @pl.when(i == 0)
        def _():
            dkt_ref[...] = jnp.zeros_like(dkt_ref)
            dvt_ref[...] = jnp.zeros_like(dvt_ref)
            dck_ref[...] = jnp.zeros_like(dck_ref)

        qs = q_ref[0] * ATTN_SCALE
        do = do_ref[0]
        qs_t = qt_ref[0] * ATTN_SCALE
        do_t = dot_ref[0]
        delta = jnp.sum(do.astype(F32) * o_ref[0].astype(F32), axis=1, keepdims=True)
        bias = cq_ref[0] - lse_ref[0]
        dq_acc[...] = jnp.zeros_like(dq_acc)
        dcq_acc[...] = jnp.zeros_like(dcq_acc)

        def process(j, n_tiles, masked):
            width = n_tiles * T
            off = pl.multiple_of(j * T, T)
            kj = k_ref[0, pl.ds(off, width), :]
            vj = v_ref[0, pl.ds(off, width), :]
            ck = _row_tiles(ck_ref, j, n_tiles)
            halves = [slice(u * TS, (u + 1) * TS) for u in range(ATTN_SUB)]
            ss = [_dot_nt(qs[h], kj) for h in halves]
            dps = [_dot_nt(do[h], vj) for h in halves]
            dkt, dvt = [], []
            dck8 = jnp.zeros((SUBLANES, width), F32)
            for u, h in enumerate(halves):
                ps, dss, rowsums = [], [], []
                for r0 in range(0, TS, ATTN_CHUNK):
                    rc = slice(r0, r0 + ATTN_CHUNK)
                    p = jnp.exp(ss[u][rc] + (bias[h][rc] - ck))
                    if masked:
                        p = jnp.where(_diag_mask(ATTN_CHUNK, width, u * TS + r0 + width - T), p, 0.0)
                    ds = p * (dps[u][rc] - delta[h][rc])
                    ps.append(p.astype(BF16))
                    dss.append(ds.astype(BF16))
                    rowsums.append(jnp.sum(ds, axis=1, keepdims=True))
                    for g in range(0, ATTN_CHUNK, SUBLANES):
                        dck8 = dck8 + ds[g:g + SUBLANES]
                p, ds = jnp.concatenate(ps, axis=0), jnp.concatenate(dss, axis=0)
                dq_acc[h] += _dot(ds, kj)
                dcq_acc[h] += jnp.concatenate(rowsums, axis=0)
                dvt.append(_dot(do_t[:, h], p))
                dkt.append(_dot(qs_t[:, h], ds))
            dvt, dkt, dck = sum(dvt), sum(dkt), jnp.sum(dck8, axis=0, keepdims=True)
            for t in range(n_tiles):
                cols = slice(t * T, (t + 1) * T)
                dvt_ref[0, j + t] += dvt[:, cols]
                dkt_ref[0, j + t] += dkt[:, cols]
                dck_ref[0, j + t] -= dck[:, cols]

        _causal_sweep(i, process)
        dq_ref[0] = (dq_acc[...] * ATTN_SCALE).astype(dq_ref.dtype)
        dcq_ref[0] = dcq_acc[...]
        finish()

    tile = pl.BlockSpec((1, T, Dh), lambda h, i: (h, i, 0))
    tile_t = pl.BlockSpec((1, Dh, T), lambda h, i: (h, 0, i))
    whole = pl.BlockSpec((1, S, Dh), lambda h, i: (h, 0, 0))
    col = pl.BlockSpec((1, T, 1), lambda h, i: (h, i, 0))
    rows = pl.BlockSpec((1, NT, 1, T), lambda h, i: (h, 0, 0, 0))
    acc_t = pl.BlockSpec((1, NT, Dh, T), lambda h, i: (h, 0, 0, 0))
    return pl.pallas_call(
        body, name=name, grid=(H, NT),
        in_specs=[tile, whole, whole, tile, tile, tile_t, tile_t, col, col, rows] + ride_in,
        out_specs=[tile, col, acc_t, acc_t, rows] + ride_out,
        out_shape=[jax.ShapeDtypeStruct((H, S, Dh), BF16), jax.ShapeDtypeStruct((H, S, 1), F32),
                   jax.ShapeDtypeStruct((H, NT, Dh, T), F32), jax.ShapeDtypeStruct((H, NT, Dh, T), F32),
                   jax.ShapeDtypeStruct((H, NT, 1, T), F32)] + ride_shape,
        scratch_shapes=[pltpu.VMEM((T, Dh), F32), pltpu.VMEM((T, 1), F32)] + ride_sems,
        compiler_params=_cparams("arbitrary", "arbitrary"),
    )(q, k, v, o, do, q_t, do_t, lse, c_col, c_row, *[a for a, _ in ride])


PAIR = 2 * HEAD_DIM


def _tri(n, lower):
    r = lax.broadcasted_iota(jnp.int32, (n, n), 0)
    c = lax.broadcasted_iota(jnp.int32, (n, n), 1)
    return (r >= c) if lower else (r <= c)


def _ssd_chunk_scalars(dt_raw, bias, a_log):
    Q = dt_raw.shape[0]
    dt = _softplus(dt_raw + bias)
    A = -jnp.exp(a_log)
    cum = jnp.dot(_tri(Q, True).astype(F32), dt * A, precision=HIGHEST, preferred_element_type=F32)
    tot = cum[Q - 1:Q, :]
    return dt, A, cum, tot


def _lane_pair(lo_mask, v, h0):
    return jnp.where(lo_mask, v[:, h0:h0 + 1], v[:, h0 + 1:h0 + 2])


def _head_expand(d_inner):
    return (jnp.arange(d_inner)[None, :] // HEAD_DIM == jnp.arange(LANES)[:, None]).astype(BF16)


def _split_dot(x, e, dims=None):
    hi = x.astype(BF16)
    lo = (x - hi.astype(F32)).astype(BF16)
    if dims is None:
        return jnp.dot(hi, e, preferred_element_type=F32) + jnp.dot(lo, e, preferred_element_type=F32)
    return (lax.dot_general(hi, e, dims, preferred_element_type=F32)
            + lax.dot_general(lo, e, dims, preferred_element_type=F32))


def ssd_scan_fwd(xbc, proj, dt_col, dt_bias, a_log, d_inner, name):
    S, W = xbc.shape
    Q, N, G = SSM_CHUNK, SSM_STATE, SSM_GROUPS
    nc = S // Q
    n_pairs = d_inner // PAIR
    pairs_per_group = n_pairs // G
    GN = G * N

    def body(xbc_ref, dt_ref, bias_ref, alog_ref, y_ref, sin_ref, st_ref):
        c = pl.program_id(0)

        @pl.when(c == 0)
        def _():
            st_ref[...] = jnp.zeros_like(st_ref)

        sin_ref[0] = st_ref[...]
        dt, A, cum, tot = _ssd_chunk_scalars(dt_ref[...], bias_ref[...], alog_ref[...])
        cum_t = cum.T
        dt_t = dt.T
        ecum = jnp.exp(cum)
        wend = jnp.exp(tot - cum) * dt
        etot = jnp.exp(tot)
        lower = _tri(Q, True)
        lo = lax.broadcasted_iota(jnp.int32, (Q, PAIR), 1) < HEAD_DIM
        lo_row = lax.broadcasted_iota(jnp.int32, (1, PAIR), 1) < HEAD_DIM
        for g in range(G):
            Bg = xbc_ref[:, d_inner + g * N:d_inner + (g + 1) * N]
            Cg = xbc_ref[:, d_inner + GN + g * N:d_inner + GN + (g + 1) * N]
            CB = _dot_nt(Cg, Bg)
            for pp in range(pairs_per_group):
                pr = g * pairs_per_group + pp
                h0 = 2 * pr
                xw = xbc_ref[:, pr * PAIR:(pr + 1) * PAIR]
                ys = []
                for h in (h0, h0 + 1):
                    L = jnp.where(lower, jnp.exp(cum[:, h:h + 1] - cum_t[h:h + 1, :]), 0.0)
                    ys.append(_dot(CB * L * dt_t[h:h + 1, :], xw))
                st = st_ref[pr]
                y_inter = _dot(Cg, st) * _lane_pair(lo, ecum, h0)
                y_ref[:, pr * PAIR:(pr + 1) * PAIR] = jnp.where(lo, ys[0], ys[1]) + y_inter
                st_ref[pr] = _lane_pair(lo_row, etot, h0) * st + _dot_tn(Bg, xw * _lane_pair(lo, wend, h0))

    return pl.pallas_call(
        body, name=name, grid=(nc,),
        in_specs=[pl.BlockSpec((Q, W), lambda c: (c, 0)), pl.BlockSpec((Q, LANES), lambda c: (c, dt_col // LANES)),
                  pl.BlockSpec((1, LANES), lambda c: (0, 0)), pl.BlockSpec((1, LANES), lambda c: (0, 0))],
        out_specs=[pl.BlockSpec((Q, d_inner), lambda c: (c, 0)), pl.BlockSpec((1, n_pairs, N, PAIR), lambda c: (c, 0, 0, 0))],
        out_shape=[jax.ShapeDtypeStruct((S, d_inner), F32), jax.ShapeDtypeStruct((nc, n_pairs, N, PAIR), F32)],
        scratch_shapes=[pltpu.VMEM((n_pairs, N, PAIR), F32)],
        compiler_params=_cparams("arbitrary"),
    )(xbc, proj, dt_bias, a_log)


def ssd_scan_bwd(dy, dskip, xbc, proj, dt_col, dt_bias, a_log, states, expand, d_inner, name):
    S, W = xbc.shape
    Q, N, G = SSM_CHUNK, SSM_STATE, SSM_GROUPS
    nc = S // Q
    n_pairs = d_inner // PAIR
    pairs_per_group = n_pairs // G
    GN = G * N

    def body(dy_ref, dskip_ref, xbc_ref, dt_ref, bias_ref, alog_ref, sin_ref, e_ref,
             dxbc_ref, ddt_ref, dalog_ref, dbias_ref, dst_ref, rows_ref):
        step = pl.program_id(0)

        @pl.when(step == 0)
        def _():
            dst_ref[...] = jnp.zeros_like(dst_ref)
            dalog_ref[...] = jnp.zeros_like(dalog_ref)
            dbias_ref[...] = jnp.zeros_like(dbias_ref)

        dt_raw = dt_ref[...]
        bias = bias_ref[...]
        dt, A, cum, tot = _ssd_chunk_scalars(dt_raw, bias, alog_ref[...])
        cum_t = cum.T
        ecum = jnp.exp(cum)
        eend = jnp.exp(tot - cum)
        wend = eend * dt
        etot = jnp.exp(tot)
        e = e_ref[...]
        wide = _split_dot(jnp.concatenate([wend, dt, jnp.broadcast_to(etot, (SUBLANES, LANES))], axis=0), e)
        wend_w, dt_w, etot_w = wide[0:Q], wide[Q:2 * Q], wide[2 * Q:2 * Q + 1]
        ecum_w = _dot(ecum, e)
        upper = _tri(Q, False)
        lane = lax.broadcasted_iota(jnp.int32, (Q, LANES), 1)
        lo = lane < HEAD_DIM
        ones_q = jnp.ones((Q, LANES), BF16)
        ones_8 = jnp.ones((SUBLANES, Q), BF16)
        rows_ref[...] = jnp.zeros_like(rows_ref)
        dcum_src = jnp.zeros((Q, LANES), F32)
        xz_parts, xbds_parts, dss_parts, dyy2_parts = [], [], [], []
        for g in range(G):
            Bg = xbc_ref[:, d_inner + g * N:d_inner + (g + 1) * N]
            Cg = xbc_ref[:, d_inner + GN + g * N:d_inner + GN + (g + 1) * N]
            Cg_t = Cg.T
            CBt = _dot_nt(Bg, Cg)
            dCBt = jnp.zeros((Q, Q), F32)
            dBg = jnp.zeros((Q, N), F32)
            dCg = jnp.zeros((Q, N), F32)
            for pp in range(pairs_per_group):
                pr = g * pairs_per_group + pp
                h0 = 2 * pr
                sl = slice(pr * PAIR, (pr + 1) * PAIR)
                xw = xbc_ref[:, sl]
                dyp = dy_ref[:, sl]
                st = sin_ref[0, pr]
                dst = dst_ref[pr]
                wend_p = wend_w[:, sl]
                dt_p = dt_w[:, sl]
                dye = dyp * ecum_w[:, sl]
                dyy2_parts.append(dye * _dot(Cg, st))
                dCg = dCg + _dot_nt(dye, st)
                bds = _dot(Bg, dst)
                dBg = dBg + _dot_nt(xw * wend_p, dst)
                xbds_parts.append(xw * bds)
                dss_parts.append(dst * st)
                xdt = xw * dt_p
                z = None
                for h, half in ((h0, lo), (h0 + 1, ~lo)):
                    Lt = jnp.where(upper, jnp.exp(cum_t[h:h + 1, :] - cum[:, h:h + 1]), 0.0)
                    dyh = jnp.where(half, dyp, 0.0)
                    CBLt = CBt * Lt
                    zh = _dot(CBLt, dyh)
                    z = zh if z is None else z + zh
                    dMt = _dot_nt(xdt, dyh)
                    dCBt = dCBt + dMt * Lt
                    gm = (dMt * CBLt).astype(BF16)
                    dcum_src = dcum_src + jnp.where(lane == h, jnp.dot(gm, ones_q, preferred_element_type=F32), 0.0)
                    rows_ref[h:h + 1, :] = jnp.dot(ones_8, gm, preferred_element_type=F32)[0:1]
                xz_parts.append(xw * z)
                dxbc_ref[:, sl] = dt_p * z + wend_p * bds + dskip_ref[:, sl]
                dst_ref[pr] = _dot(Cg_t, dye) + etot_w[:, sl] * dst
            dxbc_ref[:, d_inner + g * N:d_inner + (g + 1) * N] = dBg + _dot(dCBt, Cg)
            dxbc_ref[:, d_inner + GN + g * N:d_inner + GN + (g + 1) * N] = dCg + _dot_tn(dCBt, Bg)
        head_sums = lambda wide_arr: _split_dot(wide_arr, e, NT_DIMS)
        dy_y2 = head_sums(jnp.concatenate(dyy2_parts, axis=1))
        x_z = head_sums(jnp.concatenate(xz_parts, axis=1))
        q = head_sums(jnp.concatenate(xbds_parts, axis=1))
        dst_st = jnp.sum(head_sums(jnp.concatenate(dss_parts, axis=1)), axis=0, keepdims=True)
        wq = wend * q
        last_row = lax.broadcasted_iota(jnp.int32, (Q, LANES), 0) == Q - 1
        dcum = (dy_y2 - wq + (rows_ref[...].T - dcum_src)
                + jnp.where(last_row, etot * dst_st + jnp.sum(wq, axis=0, keepdims=True), 0.0))
        ddt = eend * q + x_z
        da = jnp.dot(upper.astype(F32), dcum, precision=HIGHEST, preferred_element_type=F32)
        ddt = ddt + da * A
        ddt_raw = ddt * _sigmoid(dt_raw + bias)
        ddt_ref[...] = ddt_raw
        dalog_ref[...] += jnp.sum(da * dt, axis=0, keepdims=True) * A
        dbias_ref[...] += jnp.sum(ddt_raw, axis=0, keepdims=True)

    rev = lambda width, col: pl.BlockSpec((Q, width), lambda s: (nc - 1 - s, col))
    vec = pl.BlockSpec((1, LANES), lambda s: (0, 0))
    return pl.pallas_call(
        body, name=name, grid=(nc,),
        in_specs=[rev(d_inner, 0), rev(d_inner, 0), rev(W, 0), rev(LANES, dt_col // LANES), vec, vec,
                  pl.BlockSpec((1, n_pairs, N, PAIR), lambda s: (nc - 1 - s, 0, 0, 0)),
                  pl.BlockSpec((LANES, d_inner), lambda s: (0, 0))],
        out_specs=[rev(W, 0), rev(LANES, 0), vec, vec],
        out_shape=[jax.ShapeDtypeStruct((S, W), F32), jax.ShapeDtypeStruct((S, LANES), F32),
                   jax.ShapeDtypeStruct((1, LANES), F32), jax.ShapeDtypeStruct((1, LANES), F32)],
        scratch_shapes=[pltpu.VMEM((n_pairs, N, PAIR), F32), pltpu.VMEM((LANES, Q), F32)],
        compiler_params=_cparams("arbitrary"),
    )(dy, dskip, xbc, proj, dt_bias, a_log, states, expand)


def ssd_gate_fwd(y, xbc, proj, d_skip, norm_w, d_inner, name):
    S = y.shape[0]
    tm = _pick(S, (256, 128))
    gs = d_inner // SSM_GROUPS

    def body(y_ref, x_ref, z_ref, d_ref, w_ref, o_ref):
        for g in range(SSM_GROUPS):
            sl = slice(g * gs, (g + 1) * gs)
            y2 = (y_ref[:, sl] + d_ref[:, sl] * x_ref[:, sl]) * _silu(z_ref[:, sl])
            r = lax.rsqrt(jnp.mean(y2 * y2, axis=-1, keepdims=True) + RMS_EPS)
            o_ref[:, sl] = (y2 * r * w_ref[:, sl]).astype(o_ref.dtype)

    row = pl.BlockSpec((tm, d_inner), lambda i: (i, 0))
    vec = pl.BlockSpec((1, d_inner), lambda i: (0, 0))
    return pl.pallas_call(
        body, name=name, grid=(S // tm,), in_specs=[row, row, row, vec, vec], out_specs=row,
        out_shape=jax.ShapeDtypeStruct((S, d_inner), BF16), compiler_params=_cparams("parallel"),
    )(y, xbc, proj, d_skip, norm_w)


def ssd_gate_bwd(dyn, y, xbc, proj, d_skip, norm_w, d_inner, name):
    S = y.shape[0]
    tm = _pick(S, (256, 128))
    gs = d_inner // SSM_GROUPS

    def body(dyn_ref, y_ref, x_ref, z_ref, d_ref, w_ref, dy_ref, dskip_ref, dz_ref, dw_ref, dd_ref):
        i = pl.program_id(0)

        @pl.when(i == 0)
        def _():
            dw_ref[...] = jnp.zeros_like(dw_ref)
            dd_ref[...] = jnp.zeros_like(dd_ref)

        for g in range(SSM_GROUPS):
            sl = slice(g * gs, (g + 1) * gs)
            z = z_ref[:, sl]
            x = x_ref[:, sl]
            sz = _silu(z)
            ysum = y_ref[:, sl] + d_ref[:, sl] * x
            y2 = ysum * sz
            r = lax.rsqrt(jnp.mean(y2 * y2, axis=-1, keepdims=True) + RMS_EPS)
            dyn = dyn_ref[:, sl]
            a = dyn * w_ref[:, sl]
            dy2 = r * a - y2 * (r * r * r) * jnp.mean(a * y2, axis=-1, keepdims=True)
            dysum = dy2 * sz
            dy_ref[:, sl] = dysum
            dskip_ref[:, sl] = dysum * d_ref[:, sl]
            dz_ref[:, sl] = (dy2 * ysum * _silu_grad(z)).astype(dz_ref.dtype)
            dw_ref[:, sl] += jnp.sum(dyn * y2 * r, axis=0, keepdims=True)
            dd_ref[:, sl] += jnp.sum(dysum * x, axis=0, keepdims=True)

    row = pl.BlockSpec((tm, d_inner), lambda i: (i, 0))
    vec = pl.BlockSpec((1, d_inner), lambda i: (0, 0))
    return pl.pallas_call(
        body, name=name, grid=(S // tm,), in_specs=[row, row, row, row, vec, vec], out_specs=[row, row, row, vec, vec],
        out_shape=[jax.ShapeDtypeStruct((S, d_inner), F32), jax.ShapeDtypeStruct((S, d_inner), F32),
                   jax.ShapeDtypeStruct((S, d_inner), BF16), jax.ShapeDtypeStruct((1, d_inner), F32),
                   jax.ShapeDtypeStruct((1, d_inner), F32)],
        compiler_params=_cparams("arbitrary"),
    )(dyn, y, xbc, proj, d_skip, norm_w)


def _pad_to(a, axis, mult=LANES):
    n = a.shape[axis]
    extra = (-n) % mult
    if extra == 0:
        return a
    widths = [(0, 0)] * a.ndim
    widths[axis] = (0, extra)
    return jnp.pad(a, widths)


def _attn_fwd(x, w_in, b_f, w_out, tag, ride=()):
    S, D = x.shape
    H = D // HEAD_DIM
    T = _attn_tile(S)
    proj = mm_nn(x, _pad_to(w_in, 1), F32, f"{tag}_proj")
    qkv = proj[:, :3 * D].astype(BF16).reshape(S, 3, H, HEAD_DIM).transpose(1, 2, 0, 3)
    zt = proj[:, 3 * D:3 * D + H].T
    bf = b_f.reshape(H, 1)
    c = fox_gate_fwd(zt, bf, f"{tag}_gate")
    c_col, c_row = c.reshape(H, S, 1), c.reshape(H, S // T, 1, T)
    v_ones = jnp.concatenate([qkv[2], jnp.ones_like(qkv[2])], axis=-1)
    o, lse, *ride_out = flash_fwd(qkv[0], qkv[1], v_ones, c_col, c_row, f"{tag}_flash", ride)
    o_flat = o.transpose(1, 0, 2).reshape(S, D)
    mix = mm_nn(o_flat, w_out, F32, f"{tag}_out")
    return mix, (qkv, zt, bf, c_col, c_row, o, lse, o_flat), ride_out


def _attn_bwd(x, dmix, dx_add, alpha, w_in, w_out, saved, tag, ride=()):
    S, D = x.shape
    H = D // HEAD_DIM
    T = _attn_tile(S)
    qkv, zt, bf, c_col, c_row, o, lse, o_flat = saved
    g_w_out = mm_tn(o_flat, dmix, f"{tag}_gwout")
    do = mm_nn(dmix, w_out.T, BF16, f"{tag}_do").reshape(S, H, HEAD_DIM)
    dq, dc_q, dk_t, dv_t, dc_k, *ride_out = flash_bwd_q(
        qkv[0], qkv[1], qkv[2], o, do.transpose(1, 0, 2), qkv[0].transpose(0, 2, 1), do.transpose(1, 2, 0), lse, c_col, c_row,
        f"{tag}_flash_bwd", ride)
    dzt, dbf = fox_gate_bwd(dc_q.reshape(H, S), dc_k.reshape(H, S), zt, bf, f"{tag}_gate_bwd")
    keys_major = lambda t: t.transpose(1, 3, 0, 2).reshape(S, D).astype(BF16)
    dqkv = jnp.concatenate([dq.transpose(1, 0, 2).reshape(S, D), keys_major(dk_t), keys_major(dv_t)], axis=1)
    dzf = _pad_to(dzt.T, 1)
    g_w_in = jnp.concatenate([mm_tn(x, dqkv, f"{tag}_gwqkv"), mm_tn(x, dzf, f"{tag}_gwf")[:, :H]], axis=1)
    w_in_t = w_in.T
    dx = mm_nn(dqkv, w_in_t[:3 * D], F32, f"{tag}_dx_qkv", add=dx_add, add_scale=alpha)
    dx = mm_nn(dzf, _pad_to(w_in_t[3 * D:], 0), F32, f"{tag}_dx_f", add=dx)
    return dx, (g_w_in, dbf.reshape(H), g_w_out), ride_out


def _ssm_dims(D):
    d_inner = 2 * D
    gn = SSM_GROUPS * SSM_STATE
    return d_inner, d_inner + 2 * gn, d_inner // HEAD_DIM


def _ssm_fwd(x, w_in, conv_w, conv_b, dt_bias, a_log, d_skip, norm_w, w_out, tag):
    S, D = x.shape
    DI, XBC, HS = _ssm_dims(D)
    dt_col = DI + XBC
    proj = mm_nn(x, _pad_to(w_in, 1), F32, f"{tag}_proj")
    conv_b = conv_b.reshape(1, XBC)
    xbc = conv_act_fwd(proj, DI, XBC, conv_w, conv_b, F32, f"{tag}_conv")
    dt_bias_p = _pad_to(dt_bias.reshape(1, HS), 1)
    a_log_p = _pad_to(a_log.reshape(1, HS), 1)
    y, states = ssd_scan_fwd(xbc, proj, dt_col, dt_bias_p, a_log_p, DI, f"{tag}_scan")
    d_vec = jnp.repeat(d_skip, HEAD_DIM).reshape(1, DI)
    norm_w = norm_w.reshape(1, DI)
    yn = ssd_gate_fwd(y, xbc, proj, d_vec, norm_w, DI, f"{tag}_gate")
    mix = mm_nn(yn, w_out, F32, f"{tag}_out")
    return mix, (proj, xbc, conv_b, dt_bias_p, a_log_p, y, states, d_vec, norm_w, yn)


def _ssm_bwd(x, dmix, dx_add, alpha, w_in, conv_w, w_out, saved, tag):
    S, D = x.shape
    DI, XBC, HS = _ssm_dims(D)
    dt_col = DI + XBC
    proj, xbc, conv_b, dt_bias_p, a_log_p, y, states, d_vec, norm_w, yn = saved
    g_w_out = mm_tn(yn, dmix, f"{tag}_gwout")
    dyn = mm_nn(dmix, w_out.T, F32, f"{tag}_dyn")
    dy, dskip, dz, g_norm_w, g_dvec = ssd_gate_bwd(dyn, y, xbc, proj, d_vec, norm_w, DI, f"{tag}_gate_bwd")
    dxbc, ddt_raw, g_a_log, g_dt_bias = ssd_scan_bwd(dy, dskip, xbc, proj, dt_col, dt_bias_p, a_log_p, states,
                                                     _head_expand(DI), DI, f"{tag}_scan_bwd")
    dxbc_raw, _, g_conv_w, g_conv_b = conv_act_bwd(dxbc, proj, DI, XBC, conv_w, conv_b, f"{tag}_conv_bwd")
    g_w_in = jnp.concatenate([mm_tn(x, dz, f"{tag}_gwz"), mm_tn(x, dxbc_raw, f"{tag}_gwxbc"),
                              mm_tn(x, ddt_raw, f"{tag}_gwdt")[:, :HS]], axis=1)
    w_in_t = w_in.T
    dx = mm_nn(dz, w_in_t[:DI], F32, f"{tag}_dx_z", add=dx_add, add_scale=alpha)
    dx = mm_nn(dxbc_raw, w_in_t[DI:dt_col], F32, f"{tag}_dx_xbc", add=dx)
    dx = mm_nn(ddt_raw, _pad_to(w_in_t[dt_col:], 0), F32, f"{tag}_dx_dt", add=dx)
    g_d = g_dvec.reshape(HS, HEAD_DIM).sum(axis=-1)
    grads = (g_w_in, g_conv_w, g_conv_b.reshape(XBC), g_dt_bias[0, :HS], g_a_log[0, :HS], g_d, g_norm_w.reshape(DI), g_w_out)
    return dx, grads


ATTN_KEYS = ("attn_w_in", "attn_b_f", "attn_w_out")
SSM_KEYS = ("ssm_w_in", "ssm_conv_w", "ssm_conv_b", "ssm_dt_bias", "ssm_A_log", "ssm_D", "ssm_norm_w", "ssm_w_out")
LAYER_KEYS = ("ln_mix_g", "ln_mix_b", "ffn_w_up", "ffn_conv_w", "ffn_conv_b", "ffn_w_down", "ln_ffn_g", "ln_ffn_b",
              "ple_w_proj", "ple_w_gate", "ple_b_gate")


def local_step(x, p, target, w, fwd_ride=(), after_fwd_ride=None, before_last_bwd=None):
    S, D = x.shape
    depth = p.shape[0]
    alpha = (2 * depth) ** 0.25
    saved = []
    bwd_ride_out = []
    for i in range(depth):
        j, tag = i // 2, f"l{i}"
        if i % 2 == 0:
            mix, msaved, ride_out = _attn_fwd(x, w["attn_w_in"][j], w["attn_b_f"][j], w["attn_w_out"][j], tag + "_attn",
                                              fwd_ride if i == 0 else ())
            if i == 0 and after_fwd_ride is not None:
                w = {**w, **after_fwd_ride(ride_out)}
        else:
            mix, msaved = _ssm_fwd(x, w["ssm_w_in"][j], w["ssm_conv_w"][j], w["ssm_conv_b"][j], w["ssm_dt_bias"][j],
                                   w["ssm_A_log"][j], w["ssm_D"][j], w["ssm_norm_w"][j], w["ssm_w_out"][j], tag + "_ssm")
        row = lambda k: w[k][i].reshape(1, -1)
        F = w["ffn_w_down"].shape[1]
        x1, xhat1, rstd1 = ln_fwd(x, mix, row("ln_mix_g"), row("ln_mix_b"), alpha, tag + "_ln_mix")
        h = mm_nn(x1, w["ffn_w_up"][i], F32, tag + "_ffn_up")
        a = conv_act_fwd(h, F, F, w["ffn_conv_w"][i], row("ffn_conv_b"), BF16, tag + "_ffn_act", gate_col=0)
        ffn = mm_nn(a, w["ffn_w_down"][i], F32, tag + "_ffn_down")
        x2, xhat2, rstd2 = ln_fwd(x1, ffn, row("ln_ffn_g"), row("ln_ffn_b"), alpha, tag + "_ln_ffn")
        zg = mm_nn(x2, w["ple_w_gate"][i], F32, tag + "_ple_gate")
        pp = mm_nn(p[i], w["ple_w_proj"][i], F32, tag + "_ple_proj")
        x3 = ple_fwd(x2, zg, pp, row("ple_b_gate"), tag + "_ple")
        saved.append((x, msaved, x1, xhat1, rstd1, h, a, x2, xhat2, rstd2, zg, pp))
        x = x3

    loss_vec, d = loss_head(x, target, "loss_head")

    grads = {k: [None] * w[k].shape[0] for k in ATTN_KEYS + SSM_KEYS + LAYER_KEYS}
    for i in reversed(range(depth)):
        j, tag = i // 2, f"l{i}"
        x0, msaved, x1, xhat1, rstd1, h, a, x2, xhat2, rstd2, zg, pp = saved[i]
        row = lambda k: w[k][i].reshape(1, -1)
        dzg, dpp, g_bg = ple_bwd(d, zg, pp, row("ple_b_gate"), tag + "_ple_bwd")
        grads["ple_w_gate"][i] = mm_tn(x2, dzg, tag + "_gw_ple_gate")
        grads["ple_w_proj"][i] = mm_tn(p[i], dpp, tag + "_gw_ple_proj")
        grads["ple_b_gate"][i] = g_bg.reshape(D)
        dx2 = mm_nn(dzg, w["ple_w_gate"][i].T, F32, tag + "_dx2", add=d)
        dr2, g_g2, g_b2 = ln_bwd(dx2, xhat2, rstd2, row("ln_ffn_g"), tag + "_ln_ffn_bwd")
        grads["ln_ffn_g"][i], grads["ln_ffn_b"][i] = g_g2.reshape(D), g_b2.reshape(D)
        grads["ffn_w_down"][i] = mm_tn(a, dr2, tag + "_gw_down")
        da = mm_nn(dr2, w["ffn_w_down"][i].T, F32, tag + "_da")
        dgin, du, g_cw, g_cb = conv_act_bwd(da, h, F, F, w["ffn_conv_w"][i], row("ffn_conv_b"), tag + "_ffn_act_bwd", gate_col=0)
        grads["ffn_conv_w"][i], grads["ffn_conv_b"][i] = g_cw, g_cb.reshape(F)
        grads["ffn_w_up"][i] = jnp.concatenate([mm_tn(x1, du, tag + "_gw_up_u"), mm_tn(x1, dgin, tag + "_gw_up_g")], axis=1)
        w_up_t = w["ffn_w_up"][i].T
        dx1 = mm_nn(du, w_up_t[:F], F32, tag + "_dx1_u", add=dr2, add_scale=alpha)
        dx1 = mm_nn(dgin, w_up_t[F:], F32, tag + "_dx1_g", add=dx1)
        dr1, g_g1, g_b1 = ln_bwd(dx1, xhat1, rstd1, row("ln_mix_g"), tag + "_ln_mix_bwd")
        grads["ln_mix_g"][i], grads["ln_mix_b"][i] = g_g1.reshape(D), g_b1.reshape(D)
        if i % 2 == 0:
            ride = before_last_bwd(grads) if (i == 0 and before_last_bwd is not None) else ()
            d, mg, ride_out = _attn_bwd(x0, dr1, dr1, alpha, w["attn_w_in"][j], w["attn_w_out"][j], msaved, tag + "_attn", ride)
            if i == 0:
                bwd_ride_out = ride_out
            for k, g in zip(ATTN_KEYS, mg):
                grads[k][j] = g
        else:
            d, mg = _ssm_bwd(x0, dr1, dr1, alpha, w["ssm_w_in"][j], w["ssm_conv_w"][j], w["ssm_w_out"][j], msaved, tag + "_ssm")
            for k, g in zip(SSM_KEYS, mg):
                grads[k][j] = g
    return loss_vec, d, {k: jnp.stack(v) for k, v in grads.items()}, bwd_ride_out


MESH = pl.DeviceIdType.MESH
PACK_ELEMS = 2 * SUBLANES * LANES


def _exchange_copies(src_ref, out_ref, gather, send_sems, recv_sems, local_sem):
    x, y, c = lax.axis_index("x"), lax.axis_index("y"), lax.axis_index("c")
    me = 4 * x + 2 * y + c

    def block_for(dev):
        return src_ref if gather else src_ref.at[dev]

    copies = [pltpu.make_async_copy(block_for(me), out_ref.at[me], local_sem)]
    for k in range(1, N_DEV):
        px = 1 - x if k & 4 else x
        py = 1 - y if k & 2 else y
        pc = 1 - c if k & 1 else c
        copies.append(pltpu.make_async_remote_copy(
            src_ref=block_for(4 * px + 2 * py + pc), dst_ref=out_ref.at[me],
            send_sem=send_sems.at[k - 1], recv_sem=recv_sems.at[k - 1],
            device_id=(px, py, pc), device_id_type=MESH))
    return copies


def _exchange(src, gather, name):
    def body(src_ref, out_ref, send_sems, recv_sems, local_sem):
        copies = _exchange_copies(src_ref, out_ref, gather, send_sems, recv_sems, local_sem)
        for cp in copies:
            cp.start()
        for cp in copies:
            cp.wait()

    return pl.pallas_call(
        body, name=name,
        in_specs=[pl.BlockSpec(memory_space=pl.ANY)],
        out_specs=pl.BlockSpec(memory_space=pl.ANY),
        out_shape=jax.ShapeDtypeStruct((N_DEV,) + tuple(src.shape[-2:]), src.dtype),
        scratch_shapes=[pltpu.SemaphoreType.DMA((N_DEV - 1,)), pltpu.SemaphoreType.DMA((N_DEV - 1,)), pltpu.SemaphoreType.DMA],
    )(src)


def _ride_decl(ride):
    if not ride:
        return [], [], [], []
    hbm = pl.BlockSpec(memory_space=pl.ANY)
    out_shape = [jax.ShapeDtypeStruct((N_DEV,) + tuple(a.shape[-2:]), a.dtype) for a, _ in ride]
    sems = [pltpu.SemaphoreType.DMA((len(ride), N_DEV - 1)), pltpu.SemaphoreType.DMA((len(ride), N_DEV - 1)),
            pltpu.SemaphoreType.DMA((len(ride),))]
    return [hbm] * len(ride), [hbm] * len(ride), out_shape, sems


def _ride_copies(ride, src_refs, out_refs, sems):
    copies = []
    for r, ((_, gather), src_ref, out_ref) in enumerate(zip(ride, src_refs, out_refs)):
        copies += _exchange_copies(src_ref, out_ref, gather, sems[0].at[r], sems[1].at[r], sems[2].at[r])
    return copies


def reduce_adamw(parts, w, m, v, name):
    _, R, _ = parts.shape
    tr = _pick(R, (512, 256, 128, 64, 32, 16))

    def body(p_ref, w_ref, m_ref, v_ref, g_ref, d_ref, nm_ref, nv_ref):
        g = p_ref[0].astype(F32)
        for s in range(1, N_DEV):
            g = g + p_ref[s].astype(F32)
        nm = ADAM_B1 * m_ref[...] + (1.0 - ADAM_B1) * g
        nv = ADAM_B2 * v_ref[...] + (1.0 - ADAM_B2) * (g * g)
        m_hat = nm / (1.0 - ADAM_B1 ** ADAM_STEP)
        v_hat = nv / (1.0 - ADAM_B2 ** ADAM_STEP)
        g_ref[...] = g
        d_ref[...] = -ADAM_LR * (m_hat / (jnp.sqrt(v_hat) + ADAM_EPS) + ADAM_WD * w_ref[...])
        nm_ref[...] = nm
        nv_ref[...] = nv

    row = pl.BlockSpec((tr, LANES), lambda i: (i, 0))
    return pl.pallas_call(
        body, name=name, grid=(R // tr,),
        in_specs=[pl.BlockSpec((N_DEV, tr, LANES), lambda i: (0, i, 0)), row, row, row],
        out_specs=[row, row, row, row],
        out_shape=[jax.ShapeDtypeStruct((R, LANES), F32)] * 4,
        compiler_params=_cparams("parallel"),
    )(parts, w, m, v)


def _pack(arrays, dtype, lead=0):
    parts = []
    for a in arrays:
        head = a.shape[:lead]
        flat = a.astype(dtype).reshape(head + (-1,))
        flat = jnp.pad(flat, [(0, 0)] * lead + [(0, (-flat.shape[-1]) % PACK_ELEMS)])
        parts.append(flat.reshape(head + (-1, LANES)))
    return jnp.concatenate(parts, axis=lead)


def _unpack(packed, shapes):
    lead = packed.shape[:-2]
    out, r0 = [], 0
    for shape in shapes:
        n = math.prod(shape)
        rows = -(-n // PACK_ELEMS) * (PACK_ELEMS // LANES)
        seg = packed[..., r0:r0 + rows, :].reshape(lead + (rows * LANES,))[..., :n]
        out.append(seg.reshape(lead + tuple(shape)))
        r0 += rows
    return out


MATMUL_SHARDED = {"attn_w_in": 2, "attn_w_out": 1, "ssm_w_in": 2, "ssm_w_out": 1, "ffn_w_up": 2, "ffn_w_down": 1,
                  "ple_w_proj": 2, "ple_w_gate": 1}
SMALL_SHARDED = {"ssm_conv_w": 2, "ssm_conv_b": 1, "ssm_norm_w": 1, "ffn_conv_w": 2}
REPLICATED = ("attn_b_f", "ssm_dt_bias", "ssm_A_log", "ssm_D", "ln_mix_g", "ln_mix_b", "ffn_conv_b", "ln_ffn_g",
              "ln_ffn_b", "ple_b_gate")
WEIGHT_ORDER = ("attn_w_in", "attn_b_f", "attn_w_out", "ssm_w_in", "ssm_conv_w", "ssm_conv_b", "ssm_dt_bias", "ssm_A_log",
                "ssm_D", "ssm_norm_w", "ssm_w_out", "ln_mix_g", "ln_mix_b", "ffn_w_up", "ffn_conv_w", "ffn_conv_b",
                "ffn_w_down", "ln_ffn_g", "ln_ffn_b", "ple_w_proj", "ple_w_gate", "ple_b_gate")


def _join_shards(gathered, axis):
    moved = jnp.moveaxis(gathered, 0, axis)
    shape = list(moved.shape)
    shape[axis:axis + 2] = [shape[axis] * shape[axis + 1]]
    return moved.reshape(shape)


def _split_shards(full, axis):
    shape = list(full.shape)
    shape[axis:axis + 1] = [N_DEV, shape[axis] // N_DEV]
    return jnp.moveaxis(full.reshape(shape), axis, 0)


def kernel(x, p, attn_w_in, attn_b_f, attn_w_out, ssm_w_in, ssm_conv_w, ssm_conv_b, ssm_dt_bias, ssm_A_log, ssm_D, ssm_norm_w, ssm_w_out, ln_mix_g, ln_mix_b, ffn_w_up, ffn_conv_w, ffn_conv_b, ffn_w_down, ln_ffn_g, ln_ffn_b, ple_w_proj, ple_w_gate, ple_b_gate, loss_target, m_attn_w_in, m_attn_b_f, m_attn_w_out, m_ssm_w_in, m_ssm_conv_w, m_ssm_conv_b, m_ssm_dt_bias, m_ssm_A_log, m_ssm_D, m_ssm_norm_w, m_ssm_w_out, m_ln_mix_g, m_ln_mix_b, m_ffn_w_up, m_ffn_conv_w, m_ffn_conv_b, m_ffn_w_down, m_ln_ffn_g, m_ln_ffn_b, m_ple_w_proj, m_ple_w_gate, m_ple_b_gate, v_attn_w_in, v_attn_b_f, v_attn_w_out, v_ssm_w_in, v_ssm_conv_w, v_ssm_conv_b, v_ssm_dt_bias, v_ssm_A_log, v_ssm_D, v_ssm_norm_w, v_ssm_w_out, v_ln_mix_g, v_ln_mix_b, v_ffn_w_up, v_ffn_conv_w, v_ffn_conv_b, v_ffn_w_down, v_ln_ffn_g, v_ln_ffn_b, v_ple_w_proj, v_ple_w_gate, v_ple_b_gate):
    args = dict(locals())
    w_loc = {k: args[k] for k in WEIGHT_ORDER}
    m_loc = {k: args["m_" + k] for k in WEIGHT_ORDER}
    v_loc = {k: args["v_" + k] for k in WEIGHT_ORDER}
    axis_of = {**MATMUL_SHARDED, **SMALL_SHARDED}
    first_names = ("attn_w_in", "attn_w_out")
    mm_names = tuple(k for k in MATMUL_SHARDED if k not in first_names)
    small_names = tuple(SMALL_SHARDED)
    later_names = mm_names + small_names

    def joined(names, gathered):
        return {k: _join_shards(blocks, axis_of[k])
                for k, blocks in zip(names, _unpack(gathered, [w_loc[k].shape for k in names]))}

    g_first = _exchange(_pack([w_loc[k] for k in first_names], BF16), True, "gather_attn_weights")
    w_first = {**{k: w_loc[k] for k in REPLICATED}, **joined(first_names, g_first)}
    fwd_ride = ((_pack([w_loc[k] for k in mm_names], BF16), True), (_pack([w_loc[k] for k in small_names], F32), True))

    def after_fwd_ride(gathered):
        return {**joined(mm_names, gathered[0]), **joined(small_names, gathered[1])}

    def shard_blocks(grads, names):
        return _pack([_split_shards(grads[k], axis_of[k]) for k in names], BF16, lead=1)

    def before_last_bwd(grads):
        return ((shard_blocks({k: jnp.stack(grads[k]) for k in later_names}, later_names), False),)

    loss_vec, grad_x, g_full, (parts_later,) = local_step(x[0], p[:, 0], loss_target[0], w_first, fwd_ride, after_fwd_ride,
                                                          before_last_bwd)
    parts_first = _exchange(shard_blocks(g_full, first_names), False, "exchange_attn_grads")
    res = {}
    for names, parts, tag in ((later_names, parts_later, "later"), (first_names, parts_first, "attn")):
        shapes = [w_loc[k].shape for k in names]
        pk = lambda d: _pack([d[k] for k in names], F32)
        outs = reduce_adamw(parts, pk(w_loc), pk(m_loc), pk(v_loc), "reduce_adamw_" + tag)
        res.update({k: vals for k, vals in zip(names, zip(*[_unpack(o, shapes) for o in outs]))})

    rep_shapes = [w_loc[k].shape for k in REPLICATED] + [(1, LANES)]
    rparts = _exchange(_pack([g_full[k] for k in REPLICATED] + [loss_vec], F32), True, "gather_replicated_grads")
    zero = jnp.zeros((1, LANES), F32)
    rk = lambda d: _pack([d[k] for k in REPLICATED] + [zero], F32)
    routs = reduce_adamw(rparts, rk(w_loc), rk(m_loc), rk(v_loc), "reduce_adamw_replicated")
    runp = [_unpack(o, rep_shapes) for o in routs]
    for i, k in enumerate(REPLICATED):
        res[k] = tuple(u[i] for u in runp)
    loss = runp[0][-1][0, 0]

    return (loss, grad_x[None], *[res[k][0] for k in WEIGHT_ORDER], *[res[k][1] for k in WEIGHT_ORDER],
            *[res[k][2] for k in WEIGHT_ORDER], *[res[k][3] for k in WEIGHT_ORDER])
```

```python
import functools
import math

import jax
import jax.numpy as jnp
from jax import lax
from jax.experimental import pallas as pl
from jax.experimental.pallas import tpu as pltpu

F32 = jnp.float32
BF16 = jnp.bfloat16

LANES = 128
SUBLANES = 8
VMEM_LIMIT_BYTES = 56 * 1024 * 1024

N_DEV = 8
HEAD_DIM = 64
SSM_GROUPS = 8
SSM_STATE = 128
SSM_CHUNK = 128
SSM_CONV = 4
FFN_CONV = 3
LN_EPS = 1e-5
RMS_EPS = 1e-5
ADAM_LR, ADAM_B1, ADAM_B2, ADAM_EPS, ADAM_WD, ADAM_STEP = 0.001, 0.9, 0.999, 1e-08, 0.01, 10
NEG_INF = float("-inf")
HIGHEST = lax.Precision.HIGHEST
NT_DIMS = (((1,), (1,)), ((), ()))
TN_DIMS = (((0,), (0,)), ((), ()))


def _cparams(*sem):
    return pltpu.CompilerParams(dimension_semantics=sem, vmem_limit_bytes=VMEM_LIMIT_BYTES)


def _pick(n, candidates):
    for c in candidates:
        if n % c == 0:
            return c
    return n


def _dot(a, b):
    return jnp.dot(a.astype(BF16), b.astype(BF16), preferred_element_type=F32)


def _dot_nt(a, b):
    return lax.dot_general(a.astype(BF16), b.astype(BF16), NT_DIMS, preferred_element_type=F32)


def _dot_tn(a, b):
    return lax.dot_general(a.astype(BF16), b.astype(BF16), TN_DIMS, preferred_element_type=F32)


def _sigmoid(x):
    return 1.0 / (1.0 + jnp.exp(-x))


def _log1p_small(u):
    return jnp.where(u < 1e-3, u * (1.0 - u * (0.5 - u * (1.0 / 3.0))), jnp.log(1.0 + u))


def _softplus(x):
    return jnp.maximum(x, 0.0) + _log1p_small(jnp.exp(-jnp.abs(x)))


def mm_nn(a, b, out_dtype, name, add=None, add_scale=1.0):
    M, K = a.shape
    _, N = b.shape
    tm = _pick(M, (1024, 512, 256, 128))
    tn = N if N <= 1024 else _pick(N, (1408, 1024, 896, 768, 640, 512, 384, 256, 128))
    tk = K if K <= 2048 else _pick(K, (1408, 1024, 896, 768, 640, 512, 384, 256, 128))
    nk = K // tk

    def body(*refs):
        if add is None:
            a_ref, b_ref, o_ref, acc_ref = refs
        else:
            a_ref, b_ref, c_ref, o_ref, acc_ref = refs
        k = pl.program_id(2)
        part = _dot(a_ref[...], b_ref[...])

        @pl.when(k == 0)
        def _():
            acc_ref[...] = part

        @pl.when(k > 0)
        def _():
            acc_ref[...] += part

        @pl.when(k == nk - 1)
        def _():
            r = acc_ref[...]
            if add is not None:
                r = r + add_scale * c_ref[...].astype(F32)
            o_ref[...] = r.astype(o_ref.dtype)

    in_specs = [pl.BlockSpec((tm, tk), lambda i, j, k: (i, k)), pl.BlockSpec((tk, tn), lambda i, j, k: (k, j))]
    args = [a, b]
    if add is not None:
        in_specs.append(pl.BlockSpec((tm, tn), lambda i, j, k: (i, j)))
        args.append(add)
    return pl.pallas_call(
        body,
        name=name,
        grid=(M // tm, N // tn, nk),
        in_specs=in_specs,
        out_specs=pl.BlockSpec((tm, tn), lambda i, j, k: (i, j)),
        out_shape=jax.ShapeDtypeStruct((M, N), out_dtype),
        scratch_shapes=[pltpu.VMEM((tm, tn), F32)],
        compiler_params=_cparams("parallel", "parallel", "arbitrary"),
    )(*args)


def mm_tn(a, b, name):
    M, K = a.shape
    _, N = b.shape
    tm = _pick(M, (512, 256, 128))
    tk = K if K <= 1024 else _pick(K, (1408, 1024, 896, 768, 640, 512, 384, 256, 128))
    tn = N if N <= 1408 else _pick(N, (1408, 1024, 896, 768, 640, 512, 384, 256, 128))
    nm = M // tm

    def body(a_ref, b_ref, o_ref):
        m = pl.program_id(2)
        part = _dot_tn(a_ref[...], b_ref[...])

        @pl.when(m == 0)
        def _():
            o_ref[...] = part

        @pl.when(m > 0)
        def _():
            o_ref[...] += part

    return pl.pallas_call(
        body,
        name=name,
        grid=(K // tk, N // tn, nm),
        in_specs=[pl.BlockSpec((tm, tk), lambda i, j, m: (m, i)), pl.BlockSpec((tm, tn), lambda i, j, m: (m, j))],
        out_specs=pl.BlockSpec((tk, tn), lambda i, j, m: (i, j)),
        out_shape=jax.ShapeDtypeStruct((K, N), F32),
        compiler_params=_cparams("parallel", "parallel", "arbitrary"),
    )(a, b)


def ln_fwd(x, mix, g, b, alpha, name):
    S, D = x.shape
    tm = _pick(S, (512, 256, 128))

    def body(x_ref, mix_ref, g_ref, b_ref, y_ref, xhat_ref, rstd_ref):
        r = alpha * x_ref[...] + mix_ref[...]
        mu = jnp.mean(r, axis=-1, keepdims=True)
        xc = r - mu
        var = jnp.mean(xc * xc, axis=-1, keepdims=True)
        rstd = lax.rsqrt(var + LN_EPS)
        xhat = xc * rstd
        y_ref[...] = xhat * g_ref[...] + b_ref[...]
        xhat_ref[...] = xhat
        rstd_ref[...] = rstd

    row = pl.BlockSpec((tm, D), lambda i: (i, 0))
    vec = pl.BlockSpec((1, D), lambda i: (0, 0))
    return pl.pallas_call(
        body,
        name=name,
        grid=(S // tm,),
        in_specs=[row, row, vec, vec],
        out_specs=[row, row, pl.BlockSpec((tm, 1), lambda i: (i, 0))],
        out_shape=[jax.ShapeDtypeStruct((S, D), F32), jax.ShapeDtypeStruct((S, D), F32), jax.ShapeDtypeStruct((S, 1), F32)],
        compiler_params=_cparams("parallel"),
    )(x, mix, g, b)


def ln_bwd(dy, xhat, rstd, g, name):
    S, D = dy.shape
    tm = _pick(S, (512, 256, 128))

    def body(dy_ref, xhat_ref, rstd_ref, g_ref, dr_ref, dg_ref, db_ref):
        i = pl.program_id(0)
        dyv = dy_ref[...]
        xh = xhat_ref[...]
        dxh = dyv * g_ref[...]
        m1 = jnp.mean(dxh, axis=-1, keepdims=True)
        m2 = jnp.mean(dxh * xh, axis=-1, keepdims=True)
        dr_ref[...] = rstd_ref[...] * (dxh - m1 - xh * m2)
        dg_part = jnp.sum(dyv * xh, axis=0, keepdims=True)
        db_part = jnp.sum(dyv, axis=0, keepdims=True)

        @pl.when(i == 0)
        def _():
            dg_ref[...] = dg_part
            db_ref[...] = db_part

        @pl.when(i > 0)
        def _():
            dg_ref[...] += dg_part
            db_ref[...] += db_part

    row = pl.BlockSpec((tm, D), lambda i: (i, 0))
    vec = pl.BlockSpec((1, D), lambda i: (0, 0))
    return pl.pallas_call(
        body,
        name=name,
        grid=(S // tm,),
        in_specs=[row, row, pl.BlockSpec((tm, 1), lambda i: (i, 0)), vec],
        out_specs=[row, vec, vec],
        out_shape=[jax.ShapeDtypeStruct((S, D), F32), jax.ShapeDtypeStruct((1, D), F32), jax.ShapeDtypeStruct((1, D), F32)],
        compiler_params=_cparams("arbitrary"),
    )(dy, xhat, rstd, g)


HALO = SUBLANES


def _prev_halo_spec(tm, tc, col0):
    return pl.BlockSpec((HALO, tc), lambda i, j: (jnp.maximum(i * (tm // HALO) - 1, 0), j + col0))


def _next_halo_spec(tm, tc, col0, n_row_tiles):
    last = n_row_tiles * (tm // HALO) - 1
    return pl.BlockSpec((HALO, tc), lambda i, j: (jnp.minimum((i + 1) * (tm // HALO), last), j + col0))


CONV_CHUNK = 32


def _causal_conv(ext_ref, w, n_taps, n_rows, row0, lanes):
    acc = None
    for k in range(n_taps):
        term = ext_ref[pl.ds(row0 - (n_taps - 1) + k, n_rows), lanes] * w[k:k + 1, :]
        acc = term if acc is None else acc + term
    return acc


def _anticausal_conv(ext_ref, w, n_taps, n_rows, row0, lanes):
    acc = None
    for k in range(n_taps):
        term = ext_ref[pl.ds(row0 + n_taps - 1 - k, n_rows), lanes] * w[k:k + 1, :]
        acc = term if acc is None else acc + term
    return acc


def _fold8(a):
    acc = a[0:SUBLANES]
    for g in range(SUBLANES, a.shape[0], SUBLANES):
        acc = acc + a[g:g + SUBLANES]
    return acc


INV_SQRT2 = 1.0 / math.sqrt(2.0)
INV_SQRT_2PI = 1.0 / math.sqrt(2.0 * math.pi)


def _gelu(g):
    return 0.5 * g * (1.0 + lax.erf(g * INV_SQRT2))


def _silu(x):
    return x * _sigmoid(x)


def _silu_grad(x):
    s = _sigmoid(x)
    return s * (1.0 + x * (1.0 - s))


def _conv_tiles(S, C, cols):
    tm = _pick(S, (256, 128))
    for tc in (1408, 1024, 512, 256, 128):
        if C % tc == 0 and all(c % tc == 0 for c in cols):
            return tm, tc
    raise ValueError("no column tile for the conv kernels")


def conv_act_fwd(src, in_col, C, conv_w, conv_b, out_dtype, name, gate_col=None):
    S = src.shape[0]
    K = conv_w.shape[0]
    gated = gate_col is not None
    tm, tc = _conv_tiles(S, C, [in_col] + ([gate_col] if gated else []))
    c_in = in_col // tc
    c_gate = gate_col // tc if gated else 0

    def body(*refs):
        if gated:
            x_ref, xp_ref, w_ref, b_ref, u_ref, o_ref, ext_ref = refs
        else:
            x_ref, xp_ref, w_ref, b_ref, o_ref, ext_ref = refs
        i = pl.program_id(0)
        ext_ref[0:HALO] = jnp.where(i > 0, xp_ref[...], 0.0)
        ext_ref[HALO:HALO + tm] = x_ref[...]
        for l0 in range(0, tc, LANES):
            ls = slice(l0, l0 + LANES)
            w = w_ref[:, ls]
            b = b_ref[:, ls]
            for r0 in range(0, tm, CONV_CHUNK):
                pre = _causal_conv(ext_ref, w, K, CONV_CHUNK, HALO + r0, ls) + b
                out = _gelu(pre) * u_ref[pl.ds(r0, CONV_CHUNK), ls] if gated else _silu(pre)
                o_ref[pl.ds(r0, CONV_CHUNK), ls] = out.astype(o_ref.dtype)

    in_specs = [
        pl.BlockSpec((tm, tc), lambda i, j: (i, j + c_in)),
        _prev_halo_spec(tm, tc, c_in),
        pl.BlockSpec((K, tc), lambda i, j: (0, j)),
        pl.BlockSpec((1, tc), lambda i, j: (0, j)),
    ]
    args = [src, src, conv_w, conv_b]
    if gated:
        in_specs.append(pl.BlockSpec((tm, tc), lambda i, j: (i, j + c_gate)))
        args.append(src)
    return pl.pallas_call(
        body,
        name=name,
        grid=(S // tm, C // tc),
        in_specs=in_specs,
        out_specs=pl.BlockSpec((tm, tc), lambda i, j: (i, j)),
        out_shape=jax.ShapeDtypeStruct((S, C), out_dtype),
        scratch_shapes=[pltpu.VMEM((tm + HALO, tc), F32)],
        compiler_params=_cparams("parallel", "parallel"),
    )(*args)


def conv_act_bwd(d_out, src, in_col, C, conv_w, conv_b, name, gate_col=None):
    S = src.shape[0]
    K = conv_w.shape[0]
    gated = gate_col is not None
    tm, tc = _conv_tiles(S, C, [in_col] + ([gate_col] if gated else []))
    c_in = in_col // tc
    c_gate = gate_col // tc if gated else 0
    ni = S // tm
    te = tm + HALO

    def body(*refs):
        if gated:
            (d_ref, dn_ref, x_ref, xp_ref, xn_ref, w_ref, b_ref, u_ref, un_ref,
             dx_ref, dw_ref, db_ref, du_ref, xext_ref, dext_ref) = refs
        else:
            (d_ref, dn_ref, x_ref, xp_ref, xn_ref, w_ref, b_ref,
             dx_ref, dw_ref, db_ref, xext_ref, dext_ref) = refs
        i = pl.program_id(1)
        xext_ref[0:HALO] = jnp.where(i > 0, xp_ref[...], 0.0)
        xext_ref[HALO:HALO + tm] = x_ref[...]
        xext_ref[HALO + tm:HALO + te] = xn_ref[...]

        @pl.when(i == 0)
        def _():
            dw_ref[...] = jnp.zeros_like(dw_ref)
            db_ref[...] = jnp.zeros_like(db_ref)

        for l0 in range(0, tc, LANES):
            ls = slice(l0, l0 + LANES)
            w = w_ref[:, ls]
            b = b_ref[:, ls]
            acc_w = [jnp.zeros((SUBLANES, LANES), F32) for _ in range(K)]
            acc_b = jnp.zeros((SUBLANES, LANES), F32)
            for r0 in range(0, te, CONV_CHUNK):
                n = min(CONV_CHUNK, te - r0)
                inside = r0 < tm
                taps = [xext_ref[pl.ds(HALO + r0 - (K - 1) + k, n), ls] for k in range(K)]
                pre = sum(t * w[k:k + 1, :] for k, t in enumerate(taps)) + b
                d = (d_ref[pl.ds(r0, n), ls] if inside else dn_ref[:, ls]).astype(F32)
                if gated:
                    u = u_ref[pl.ds(r0, n), ls] if inside else un_ref[:, ls]
                    cdf = 0.5 * (1.0 + lax.erf(pre * INV_SQRT2))
                    dpre = d * u * (cdf + pre * jnp.exp(-0.5 * pre * pre) * INV_SQRT_2PI)
                    if inside:
                        du_ref[pl.ds(r0, n), ls] = (d * (pre * cdf)).astype(du_ref.dtype)
                else:
                    dpre = d * _silu_grad(pre)
                if inside:
                    for k in range(K):
                        acc_w[k] = acc_w[k] + _fold8(dpre * taps[k])
                    acc_b = acc_b + _fold8(dpre)
                else:
                    dpre = jnp.where(i < ni - 1, dpre, 0.0)
                dext_ref[pl.ds(r0, n), ls] = dpre
            for r0 in range(0, tm, CONV_CHUNK):
                dx = _anticausal_conv(dext_ref, w, K, CONV_CHUNK, r0, ls)
                dx_ref[pl.ds(r0, CONV_CHUNK), ls] = dx.astype(dx_ref.dtype)
            dw_rows = [jnp.sum(a, axis=0, keepdims=True) for a in acc_w]
            dw_ref[:, ls] += jnp.concatenate(dw_rows + [jnp.zeros((SUBLANES - K, LANES), F32)], axis=0)
            db_ref[:, ls] += jnp.sum(acc_b, axis=0, keepdims=True)

    last = ni * (tm // HALO) - 1
    cur = lambda c0: pl.BlockSpec((tm, tc), lambda j, i: (i, j + c0))
    prev = lambda c0: pl.BlockSpec((HALO, tc), lambda j, i: (jnp.maximum(i * (tm // HALO) - 1, 0), j + c0))
    nxt = lambda c0: pl.BlockSpec((HALO, tc), lambda j, i: (jnp.minimum((i + 1) * (tm // HALO), last), j + c0))
    vec = lambda rows: pl.BlockSpec((rows, tc), lambda j, i: (0, j))
    in_specs = [cur(0), nxt(0), cur(c_in), prev(c_in), nxt(c_in), vec(K), vec(1)]
    args = [d_out, d_out, src, src, src, conv_w, conv_b]
    out_specs = [cur(0), vec(SUBLANES), vec(1)]
    out_shape = [jax.ShapeDtypeStruct((S, C), BF16), jax.ShapeDtypeStruct((SUBLANES, C), F32), jax.ShapeDtypeStruct((1, C), F32)]
    if gated:
        in_specs += [cur(c_gate), nxt(c_gate)]
        args += [src, src]
        out_specs.append(cur(0))
        out_shape.append(jax.ShapeDtypeStruct((S, C), BF16))
    outs = pl.pallas_call(
        body,
        name=name,
        grid=(C // tc, ni),
        in_specs=in_specs,
        out_specs=out_specs,
        out_shape=out_shape,
        scratch_shapes=[pltpu.VMEM((tm + 2 * HALO, tc), F32), pltpu.VMEM((te, tc), F32)],
        compiler_params=_cparams("parallel", "arbitrary"),
    )(*args)
    return outs[0], (outs[3] if gated else None), outs[1][:K], outs[2]


def ple_fwd(x2, zg, pp, bg, name):
    S, D = x2.shape
    tm = _pick(S, (512, 256, 128))

    def body(x_ref, z_ref, p_ref, b_ref, o_ref):
        o_ref[...] = x_ref[...] + _sigmoid(z_ref[...] + b_ref[...]) * p_ref[...]

    row = pl.BlockSpec((tm, D), lambda i: (i, 0))
    return pl.pallas_call(
        body, name=name, grid=(S // tm,), in_specs=[row, row, row, pl.BlockSpec((1, D), lambda i: (0, 0))], out_specs=row,
        out_shape=jax.ShapeDtypeStruct((S, D), F32), compiler_params=_cparams("parallel"),
    )(x2, zg, pp, bg)


def ple_bwd(dx3, zg, pp, bg, name):
    S, D = dx3.shape
    tm = _pick(S, (512, 256, 128))

    def body(d_ref, z_ref, p_ref, b_ref, dz_ref, dp_ref, db_ref):
        i = pl.program_id(0)
        d = d_ref[...]
        gate = _sigmoid(z_ref[...] + b_ref[...])
        dz = d * p_ref[...] * gate * (1.0 - gate)
        dz_ref[...] = dz.astype(dz_ref.dtype)
        dp_ref[...] = (d * gate).astype(dp_ref.dtype)
        part = jnp.sum(dz, axis=0, keepdims=True)

        @pl.when(i == 0)
        def _():
            db_ref[...] = part

        @pl.when(i > 0)
        def _():
            db_ref[...] += part

    row = pl.BlockSpec((tm, D), lambda i: (i, 0))
    vec = pl.BlockSpec((1, D), lambda i: (0, 0))
    return pl.pallas_call(
        body, name=name, grid=(S // tm,), in_specs=[row, row, row, vec], out_specs=[row, row, vec],
        out_shape=[jax.ShapeDtypeStruct((S, D), BF16), jax.ShapeDtypeStruct((S, D), BF16), jax.ShapeDtypeStruct((1, D), F32)],
        compiler_params=_cparams("arbitrary"),
    )(dx3, zg, pp, bg)


def loss_head(y, target, name):
    S, D = y.shape
    tm = _pick(S, (512, 256, 128))

    def body(y_ref, t_ref, loss_ref, dy_ref, acc_ref):
        i = pl.program_id(0)
        err = y_ref[...] - t_ref[...]
        dy_ref[...] = err * (1.0 / D)
        part = jnp.sum(err * err, axis=0, keepdims=True)

        @pl.when(i == 0)
        def _():
            acc_ref[...] = part

        @pl.when(i > 0)
        def _():
            acc_ref[...] += part

        @pl.when(i == pl.num_programs(0) - 1)
        def _():
            loss_ref[...] = jnp.zeros((1, LANES), F32) + (0.5 / D) * jnp.sum(acc_ref[...])

    row = pl.BlockSpec((tm, D), lambda i: (i, 0))
    return pl.pallas_call(
        body, name=name, grid=(S // tm,), in_specs=[row, row],
        out_specs=[pl.BlockSpec((1, LANES), lambda i: (0, 0)), row],
        out_shape=[jax.ShapeDtypeStruct((1, LANES), F32), jax.ShapeDtypeStruct((S, D), F32)],
        scratch_shapes=[pltpu.VMEM((1, D), F32)],
        compiler_params=_cparams("arbitrary"),
    )(y, target)


ATTN_TILE = 1024
ATTN_SCALE = 1.0 / math.sqrt(HEAD_DIM)


def _attn_tile(S):
    return _pick(S, (ATTN_TILE, 512, 256, 128))


def fox_gate_fwd(zt, bf, name):
    H, S = zt.shape
    tl = _pick(S, (512, 256, 128))

    def body(z_ref, b_ref, c_ref, carry_ref):
        i = pl.program_id(0)

        @pl.when(i == 0)
        def _():
            carry_ref[...] = jnp.zeros_like(carry_ref)

        z = z_ref[...] + b_ref[...]
        logf = jnp.minimum(z, 0.0) - _log1p_small(jnp.exp(-jnp.abs(z)))
        r = lax.broadcasted_iota(jnp.int32, (tl, tl), 0)
        c = lax.broadcasted_iota(jnp.int32, (tl, tl), 1)
        upper = (r <= c).astype(F32)
        cum = jnp.dot(logf, upper, precision=HIGHEST, preferred_element_type=F32) + carry_ref[...]
        c_ref[...] = cum
        carry_ref[...] = cum[:, tl - 1:tl]

    return pl.pallas_call(
        body, name=name, grid=(S // tl,),
        in_specs=[pl.BlockSpec((H, tl), lambda i: (0, i)), pl.BlockSpec((H, 1), lambda i: (0, 0))],
        out_specs=pl.BlockSpec((H, tl), lambda i: (0, i)),
        out_shape=jax.ShapeDtypeStruct((H, S), F32),
        scratch_shapes=[pltpu.VMEM((H, 1), F32)],
        compiler_params=_cparams("arbitrary"),
    )(zt, bf)


def fox_gate_bwd(dc_q, dc_k, zt, bf, name):
    H, S = zt.shape
    tl = _pick(S, (512, 256, 128))
    nt = S // tl

    def body(dcq_ref, dck_ref, z_ref, b_ref, dz_ref, db_ref, carry_ref):
        i = pl.program_id(0)

        @pl.when(i == 0)
        def _():
            carry_ref[...] = jnp.zeros_like(carry_ref)
            db_ref[...] = jnp.zeros_like(db_ref)

        r = lax.broadcasted_iota(jnp.int32, (tl, tl), 0)
        c = lax.broadcasted_iota(jnp.int32, (tl, tl), 1)
        lower = (r >= c).astype(F32)
        dc = dcq_ref[...] + dck_ref[...]
        suffix = jnp.dot(dc, lower, precision=HIGHEST, preferred_element_type=F32) + carry_ref[...]
        carry_ref[...] = suffix[:, 0:1]
        dz = suffix * _sigmoid(-(z_ref[...] + b_ref[...]))
        dz_ref[...] = dz
        db_ref[...] += jnp.sum(dz, axis=1, keepdims=True)

    rev = pl.BlockSpec((H, tl), lambda i: (0, nt - 1 - i))
    return pl.pallas_call(
        body, name=name, grid=(nt,),
        in_specs=[rev, rev, rev, pl.BlockSpec((H, 1), lambda i: (0, 0))],
        out_specs=[rev, pl.BlockSpec((H, 1), lambda i: (0, 0))],
        out_shape=[jax.ShapeDtypeStruct((H, S), F32), jax.ShapeDtypeStruct((H, 1), F32)],
        scratch_shapes=[pltpu.VMEM((H, 1), F32)],
        compiler_params=_cparams("arbitrary"),
    )(dc_q, dc_k, zt, bf)


ATTN_SUB = 4
ATTN_CHUNK = 16


def _row_tiles(ref, j, n_tiles):
    if n_tiles == 1:
        return ref[0, j]
    return jnp.concatenate([ref[0, j + t] for t in range(n_tiles)], axis=1)


def _diag_mask(n_rows, width, row0):
    r = lax.broadcasted_iota(jnp.int32, (n_rows, width), 0) + row0
    c = lax.broadcasted_iota(jnp.int32, (n_rows, width), 1)
    return r >= c


def _causal_sweep(i, process):
    def pair_body(j2, carry):
        process(2 * j2, 2, False)
        return carry

    lax.fori_loop(0, i // 2, pair_body, 0)

    @pl.when(i % 2 == 1)
    def _():
        process(i - 1, 2, True)

    @pl.when(i % 2 == 0)
    def _():
        process(i, 1, True)


def _ride_along(ride, refs, n_in, n_out, n_scratch, first, last):
    n = len(ride)
    ins, srcs = refs[:n_in], refs[n_in:n_in + n]
    outs, dsts = refs[n_in + n:n_in + n + n_out], refs[n_in + n + n_out:n_in + 2 * n + n_out]
    scratch, sems = refs[n_in + 2 * n + n_out:n_in + 2 * n + n_out + n_scratch], refs[n_in + 2 * n + n_out + n_scratch:]
    copies = _ride_copies(ride, srcs, dsts, sems) if n else []

    @pl.when(first)
    def _():
        for cp in copies:
            cp.start()

    def finish():
        @pl.when(last)
        def _():
            for cp in copies:
                cp.wait()

    return ins, outs, scratch, finish


def flash_fwd(q, k, v_ones, c_col, c_row, name, ride=()):
    H, S, Dh = q.shape
    T = _attn_tile(S)
    NT = S // T
    TS = T // ATTN_SUB
    ride_in, ride_out, ride_shape, ride_sems = _ride_decl(ride)

    def body(*refs):
        h, i = pl.program_id(0), pl.program_id(1)
        (q_ref, k_ref, v_ref, cq_ref, ck_ref), (o_ref, lse_ref), (m_ref, acc_ref), finish = _ride_along(
            ride, refs, 5, 2, 2, (h == 0) & (i == 0), (h == H - 1) & (i == NT - 1))
        qs = q_ref[0] * ATTN_SCALE
        c_ref = cq_ref[0, 0:1, :]
        m_ref[...] = jnp.full_like(m_ref, NEG_INF)
        acc_ref[...] = jnp.zeros_like(acc_ref)
        halves = [slice(u * TS, (u + 1) * TS) for u in range(ATTN_SUB)]

        def keys(j, n_tiles):
            return pl.ds(pl.multiple_of(j * T, T), n_tiles * T)

        def softmax_pv(j, n_tiles, masked, s_of):
            width = n_tiles * T
            vj = v_ref[0, keys(j, n_tiles), :]
            ckj = _row_tiles(ck_ref, j, n_tiles) - c_ref
            for u, rows in enumerate(halves):
                m_prev = m_ref[rows]
                ps, m_news = [], []
                for r0 in range(0, TS, ATTN_CHUNK):
                    rc = slice(r0, r0 + ATTN_CHUNK)
                    s = s_of(slice(u * TS + r0, u * TS + r0 + ATTN_CHUNK)) - ckj
                    if masked:
                        s = jnp.where(_diag_mask(ATTN_CHUNK, width, u * TS + r0 + width - T), s, NEG_INF)
                    m_new = jnp.maximum(m_prev[rc], jnp.max(s, axis=1, keepdims=True))
                    ps.append(jnp.exp(s - jnp.tile(m_new, (1, width // LANES))).astype(BF16))
                    m_news.append(m_new)
                m_new = jnp.concatenate(m_news, axis=0)
                acc_ref[rows] = jnp.exp(m_prev - m_new) * acc_ref[rows] + _dot(jnp.concatenate(ps, axis=0), vj)
                m_ref[rows] = m_new

        def process(j, n_tiles, masked):
            kj = k_ref[0, keys(j, n_tiles), :]
            s = jnp.concatenate([_dot_nt(qs[rows], kj) for rows in halves], axis=0)
            softmax_pv(j, n_tiles, masked, lambda rc: s[rc])

        _causal_sweep(i, process)
        acc = acc_ref[...]
        l = acc[:, Dh:Dh + 1]
        o_ref[0] = (acc[:, 0:Dh] / l).astype(o_ref.dtype)
        lse_ref[0] = m_ref[:, 0:1] + jnp.log(l) + (cq_ref[0] - c_ref)
        finish()

    tile = pl.BlockSpec((1, T, Dh), lambda h, i: (h, i, 0))
    whole = pl.BlockSpec((1, S, Dh), lambda h, i: (h, 0, 0))
    whole_v = pl.BlockSpec((1, S, 2 * Dh), lambda h, i: (h, 0, 0))
    col = pl.BlockSpec((1, T, 1), lambda h, i: (h, i, 0))
    rows = pl.BlockSpec((1, NT, 1, T), lambda h, i: (h, 0, 0, 0))
    return pl.pallas_call(
        body, name=name, grid=(H, NT),
        in_specs=[tile, whole, whole_v, col, rows] + ride_in,
        out_specs=[tile, col] + ride_out,
        out_shape=[jax.ShapeDtypeStruct((H, S, Dh), BF16), jax.ShapeDtypeStruct((H, S, 1), F32)] + ride_shape,
        scratch_shapes=[pltpu.VMEM((T, LANES), F32), pltpu.VMEM((T, 2 * Dh), F32)] + ride_sems,
        compiler_params=_cparams("arbitrary", "arbitrary"),
    )(q, k, v_ones, c_col, c_row, *[a for a, _ in ride])


def flash_bwd_q(q, k, v, o, do, q_t, do_t, lse, c_col, c_row, name, ride=()):
    H, S, Dh = q.shape
    T = _attn_tile(S)
    NT = S // T
    TS = T // ATTN_SUB

    ride_in, ride_out, ride_shape, ride_sems = _ride_decl(ride)

    def body(*refs):
        head, i = pl.program_id(0), pl.program_id(1)
        ((q_ref, k_ref, v_ref, o_ref, do_ref, qt_ref, dot_ref, lse_ref, cq_ref, ck_ref),
         (dq_ref, dcq_ref, dkt_ref, dvt_ref, dck_ref), (dq_acc, dcq_acc), finish) = _ride_along(
            ride, refs, 10, 5, 2, (head == 0) & (i == 0), (head == H - 1) & (i == NT - 1))

        @pl.when(i == 0)
        def _():
            dkt_ref[...] = jnp.zeros_like(dkt_ref)
            dvt_ref[...] = jnp.zeros_like(dvt_ref)
            dck_ref[...] = jnp.zeros_like(dck_ref)

        qs = q_ref[0] * ATTN_SCALE
        do = do_ref[0]
        qs_t = qt_ref[0] * ATTN_SCALE
        do_t = dot_ref[0]
        delta = jnp.sum(do.astype(F32) * o_ref[0].astype(F32), axis=1, keepdims=True)
        bias = cq_ref[0] - lse_ref[0]
        dq_acc[...] = jnp.zeros_like(dq_acc)
        dcq_acc[...] = jnp.zeros_like(dcq_acc)

        def process(j, n_tiles, masked):
            width = n_tiles * T
            off = pl.multiple_of(j * T, T)
            kj = k_ref[0, pl.ds(off, width), :]
            vj = v_ref[0, pl.ds(off, width), :]
            ck = _row_tiles(ck_ref, j, n_tiles)
            halves = [slice(u * TS, (u + 1) * TS) for u in range(ATTN_SUB)]
            ss = [_dot_nt(qs[h], kj) for h in halves]
            dps = [_dot_nt(do[h], vj) for h in halves]
            dkt, dvt = [], []
            dck8 = jnp.zeros((SUBLANES, width), F32)
            for u, h in enumerate(halves):
                ps, dss, rowsums = [], [], []
                for r0 in range(0, TS, ATTN_CHUNK):
                    rc = slice(r0, r0 + ATTN_CHUNK)
                    p = jnp.exp(ss[u][rc] + (bias[h][rc] - ck))
                    if masked:
                        p = jnp.where(_diag_mask(ATTN_CHUNK, width, u * TS + r0 + width - T), p, 0.0)
                    ds = p * (dps[u][rc] - delta[h][rc])
                    ps.append(p.astype(BF16))
                    dss.append(ds.astype(BF16))
                    rowsums.append(jnp.sum(ds, axis=1, keepdims=True))
                    for g in range(0, ATTN_CHUNK, SUBLANES):
                        dck8 = dck8 + ds[g:g + SUBLANES]
                p, ds = jnp.concatenate(ps, axis=0), jnp.concatenate(dss, axis=0)
                dq_acc[h] += _dot(ds, kj)
                dcq_acc[h] += jnp.concatenate(rowsums, axis=0)
                dvt.append(_dot(do_t[:, h], p))
                dkt.append(_dot(qs_t[:, h], ds))
            dvt, dkt, dck = sum(dvt), sum(dkt), jnp.sum(dck8, axis=0, keepdims=True)
            for t in range(n_tiles):
                cols = slice(t * T, (t + 1) * T)
                dvt_ref[0, j + t] += dvt[:, cols]
                dkt_ref[0, j + t] += dkt[:, cols]
                dck_ref[0, j + t] -= dck[:, cols]

        _causal_sweep(i, process)
        dq_ref[0] = (dq_acc[...] * ATTN_SCALE).astype(dq_ref.dtype)
        dcq_ref[0] = dcq_acc[...]
        finish()

    tile = pl.BlockSpec((1, T, Dh), lambda h, i: (h, i, 0))
    tile_t = pl.BlockSpec((1, Dh, T), lambda h, i: (h, 0, i))
    whole = pl.BlockSpec((1, S, Dh), lambda h, i: (h, 0, 0))
    col = pl.BlockSpec((1, T, 1), lambda h, i: (h, i, 0))
    rows = pl.BlockSpec((1, NT, 1, T), lambda h, i: (h, 0, 0, 0))
    acc_t = pl.BlockSpec((1, NT, Dh, T), lambda h, i: (h, 0, 0, 0))
    return pl.pallas_call(
        body, name=name, grid=(H, NT),
        in_specs=[tile, whole, whole, tile, tile, tile_t, tile_t, col, col, rows] + ride_in,
        out_specs=[tile, col, acc_t, acc_t, rows] + ride_out,
        out_shape=[jax.ShapeDtypeStruct((H, S, Dh), BF16), jax.ShapeDtypeStruct((H, S, 1), F32),
                   jax.ShapeDtypeStruct((H, NT, Dh, T), F32), jax.ShapeDtypeStruct((H, NT, Dh, T), F32),
                   jax.ShapeDtypeStruct((H, NT, 1, T), F32)] + ride_shape,
        scratch_shapes=[pltpu.VMEM((T, Dh), F32), pltpu.VMEM((T, 1), F32)] + ride_sems,
        compiler_params=_cparams("arbitrary", "arbitrary"),
    )(q, k, v, o, do, q_t, do_t, lse, c_col, c_row, *[a for a, _ in ride])


PAIR = 2 * HEAD_DIM


def _tri(n, lower):
    r = lax.broadcasted_iota(jnp.int32, (n, n), 0)
    c = lax.broadcasted_iota(jnp.int32, (n, n), 1)
    return (r >= c) if lower else (r <= c)


def _ssd_chunk_scalars(dt_raw, bias, a_log):
    Q = dt_raw.shape[0]
    dt = _softplus(dt_raw + bias)
    A = -jnp.exp(a_log)
    cum = jnp.dot(_tri(Q, True).astype(F32), dt * A, precision=HIGHEST, preferred_element_type=F32)
    tot = cum[Q - 1:Q, :]
    return dt, A, cum, tot


def _lane_pair(lo_mask, v, h0):
    return jnp.where(lo_mask, v[:, h0:h0 + 1], v[:, h0 + 1:h0 + 2])


def _head_expand(d_inner):
    return (jnp.arange(d_inner)[None, :] // HEAD_DIM == jnp.arange(LANES)[:, None]).astype(BF16)


def _split_dot(x, e, dims=None):
    hi = x.astype(BF16)
    lo = (x - hi.astype(F32)).astype(BF16)
    if dims is None:
        return jnp.dot(hi, e, preferred_element_type=F32) + jnp.dot(lo, e, preferred_element_type=F32)
    return (lax.dot_general(hi, e, dims, preferred_element_type=F32)
            + lax.dot_general(lo, e, dims, preferred_element_type=F32))


def ssd_scan_fwd(xbc, proj, dt_col, dt_bias, a_log, d_inner, name):
    S, W = xbc.shape
    Q, N, G = SSM_CHUNK, SSM_STATE, SSM_GROUPS
    nc = S // Q
    n_pairs = d_inner // PAIR
    pairs_per_group = n_pairs // G
    GN = G * N

    def body(xbc_ref, dt_ref, bias_ref, alog_ref, y_ref, sin_ref, st_ref):
        c = pl.program_id(0)

        @pl.when(c == 0)
        def _():
            st_ref[...] = jnp.zeros_like(st_ref)

        sin_ref[0] = st_ref[...]
        dt, A, cum, tot = _ssd_chunk_scalars(dt_ref[...], bias_ref[...], alog_ref[...])
        cum_t = cum.T
        dt_t = dt.T
        ecum = jnp.exp(cum)
        wend = jnp.exp(tot - cum) * dt
        etot = jnp.exp(tot)
        lower = _tri(Q, True)
        lo = lax.broadcasted_iota(jnp.int32, (Q, PAIR), 1) < HEAD_DIM
        lo_row = lax.broadcasted_iota(jnp.int32, (1, PAIR), 1) < HEAD_DIM
        for g in range(G):
            Bg = xbc_ref[:, d_inner + g * N:d_inner + (g + 1) * N]
            Cg = xbc_ref[:, d_inner + GN + g * N:d_inner + GN + (g + 1) * N]
            CB = _dot_nt(Cg, Bg)
            for pp in range(pairs_per_group):
                pr = g * pairs_per_group + pp
                h0 = 2 * pr
                xw = xbc_ref[:, pr * PAIR:(pr + 1) * PAIR]
                ys = []
                for h in (h0, h0 + 1):
                    L = jnp.where(lower, jnp.exp(cum[:, h:h + 1] - cum_t[h:h + 1, :]), 0.0)
                    ys.append(_dot(CB * L * dt_t[h:h + 1, :], xw))
                st = st_ref[pr]
                y_inter = _dot(Cg, st) * _lane_pair(lo, ecum, h0)
                y_ref[:, pr * PAIR:(pr + 1) * PAIR] = jnp.where(lo, ys[0], ys[1]) + y_inter
                st_ref[pr] = _lane_pair(lo_row, etot, h0) * st + _dot_tn(Bg, xw * _lane_pair(lo, wend, h0))

    return pl.pallas_call(
        body, name=name, grid=(nc,),
        in_specs=[pl.BlockSpec((Q, W), lambda c: (c, 0)), pl.BlockSpec((Q, LANES), lambda c: (c, dt_col // LANES)),
                  pl.BlockSpec((1, LANES), lambda c: (0, 0)), pl.BlockSpec((1, LANES), lambda c: (0, 0))],
        out_specs=[pl.BlockSpec((Q, d_inner), lambda c: (c, 0)), pl.BlockSpec((1, n_pairs, N, PAIR), lambda c: (c, 0, 0, 0))],
        out_shape=[jax.ShapeDtypeStruct((S, d_inner), F32), jax.ShapeDtypeStruct((nc, n_pairs, N, PAIR), F32)],
        scratch_shapes=[pltpu.VMEM((n_pairs, N, PAIR), F32)],
        compiler_params=_cparams("arbitrary"),
    )(xbc, proj, dt_bias, a_log)


def ssd_scan_bwd(dy, dskip, xbc, proj, dt_col, dt_bias, a_log, states, expand, d_inner, name):
    S, W = xbc.shape
    Q, N, G = SSM_CHUNK, SSM_STATE, SSM_GROUPS
    nc = S // Q
    n_pairs = d_inner // PAIR
    pairs_per_group = n_pairs // G
    GN = G * N

    def body(dy_ref, dskip_ref, xbc_ref, dt_ref, bias_ref, alog_ref, sin_ref, e_ref,
             dxbc_ref, ddt_ref, dalog_ref, dbias_ref, dst_ref, rows_ref):
        step = pl.program_id(0)

        @pl.when(step == 0)
        def _():
            dst_ref[...] = jnp.zeros_like(dst_ref)
            dalog_ref[...] = jnp.zeros_like(dalog_ref)
            dbias_ref[...] = jnp.zeros_like(dbias_ref)

        dt_raw = dt_ref[...]
        bias = bias_ref[...]
        dt, A, cum, tot = _ssd_chunk_scalars(dt_raw, bias, alog_ref[...])
        cum_t = cum.T
        ecum = jnp.exp(cum)
        eend = jnp.exp(tot - cum)
        wend = eend * dt
        etot = jnp.exp(tot)
        e = e_ref[...]
        wide = _split_dot(jnp.concatenate([wend, dt, jnp.broadcast_to(etot, (SUBLANES, LANES))], axis=0), e)
        wend_w, dt_w, etot_w = wide[0:Q], wide[Q:2 * Q], wide[2 * Q:2 * Q + 1]
        ecum_w = _dot(ecum, e)
        upper = _tri(Q, False)
        lane = lax.broadcasted_iota(jnp.int32, (Q, LANES), 1)
        lo = lane < HEAD_DIM
        ones_q = jnp.ones((Q, LANES), BF16)
        ones_8 = jnp.ones((SUBLANES, Q), BF16)
        rows_ref[...] = jnp.zeros_like(rows_ref)
        dcum_src = jnp.zeros((Q, LANES), F32)
        xz_parts, xbds_parts, dss_parts, dyy2_parts = [], [], [], []
        for g in range(G):
            Bg = xbc_ref[:, d_inner + g * N:d_inner + (g + 1) * N]
            Cg = xbc_ref[:, d_inner + GN + g * N:d_inner + GN + (g + 1) * N]
            Cg_t = Cg.T
            CBt = _dot_nt(Bg, Cg)
            dCBt = jnp.zeros((Q, Q), F32)
            dBg = jnp.zeros((Q, N), F32)
            dCg = jnp.zeros((Q, N), F32)
            for pp in range(pairs_per_group):
                pr = g * pairs_per_group + pp
                h0 = 2 * pr
                sl = slice(pr * PAIR, (pr + 1) * PAIR)
                xw = xbc_ref[:, sl]
                dyp = dy_ref[:, sl]
                st = sin_ref[0, pr]
                dst = dst_ref[pr]
                wend_p = wend_w[:, sl]
                dt_p = dt_w[:, sl]
                dye = dyp * ecum_w[:, sl]
                dyy2_parts.append(dye * _dot(Cg, st))
                dCg = dCg + _dot_nt(dye, st)
                bds = _dot(Bg, dst)
                dBg = dBg + _dot_nt(xw * wend_p, dst)
                xbds_parts.append(xw * bds)
                dss_parts.append(dst * st)
                xdt = xw * dt_p
                z = None
                for h, half in ((h0, lo), (h0 + 1, ~lo)):
                    Lt = jnp.where(upper, jnp.exp(cum_t[h:h + 1, :] - cum[:, h:h + 1]), 0.0)
                    dyh = jnp.where(half, dyp, 0.0)
                    CBLt = CBt * Lt
                    zh = _dot(CBLt, dyh)
                    z = zh if z is None else z + zh
                    dMt = _dot_nt(xdt, dyh)
                    dCBt = dCBt + dMt * Lt
                    gm = (dMt * CBLt).astype(BF16)
                    dcum_src = dcum_src + jnp.where(lane == h, jnp.dot(gm, ones_q, preferred_element_type=F32), 0.0)
                    rows_ref[h:h + 1, :] = jnp.dot(ones_8, gm, preferred_element_type=F32)[0:1]
                xz_parts.append(xw * z)
                dxbc_ref[:, sl] = dt_p * z + wend_p * bds + dskip_ref[:, sl]
                dst_ref[pr] = _dot(Cg_t, dye) + etot_w[:, sl] * dst
            dxbc_ref[:, d_inner + g * N:d_inner + (g + 1) * N] = dBg + _dot(dCBt, Cg)
            dxbc_ref[:, d_inner + GN + g * N:d_inner + GN + (g + 1) * N] = dCg + _dot_tn(dCBt, Bg)
        head_sums = lambda wide_arr: _split_dot(wide_arr, e, NT_DIMS)
        dy_y2 = head_sums(jnp.concatenate(dyy2_parts, axis=1))
        x_z = head_sums(jnp.concatenate(xz_parts, axis=1))
        q = head_sums(jnp.concatenate(xbds_parts, axis=1))
        dst_st = jnp.sum(head_sums(jnp.concatenate(dss_parts, axis=1)), axis=0, keepdims=True)
        wq = wend * q
        last_row = lax.broadcasted_iota(jnp.int32, (Q, LANES), 0) == Q - 1
        dcum = (dy_y2 - wq + (rows_ref[...].T - dcum_src)
                + jnp.where(last_row, etot * dst_st + jnp.sum(wq, axis=0, keepdims=True), 0.0))
        ddt = eend * q + x_z
        da = jnp.dot(upper.astype(F32), dcum, precision=HIGHEST, preferred_element_type=F32)
        ddt = ddt + da * A
        ddt_raw = ddt * _sigmoid(dt_raw + bias)
        ddt_ref[...] = ddt_raw
        dalog_ref[...] += jnp.sum(da * dt, axis=0, keepdims=True) * A
        dbias_ref[...] += jnp.sum(ddt_raw, axis=0, keepdims=True)

    rev = lambda width, col: pl.BlockSpec((Q, width), lambda s: (nc - 1 - s, col))
    vec = pl.BlockSpec((1, LANES), lambda s: (0, 0))
    return pl.pallas_call(
        body, name=name, grid=(nc,),
        in_specs=[rev(d_inner, 0), rev(d_inner, 0), rev(W, 0), rev(LANES, dt_col // LANES), vec, vec,
                  pl.BlockSpec((1, n_pairs, N, PAIR), lambda s: (nc - 1 - s, 0, 0, 0)),
                  pl.BlockSpec((LANES, d_inner), lambda s: (0, 0))],
        out_specs=[rev(W, 0), rev(LANES, 0), vec, vec],
        out_shape=[jax.ShapeDtypeStruct((S, W), F32), jax.ShapeDtypeStruct((S, LANES), F32),
                   jax.ShapeDtypeStruct((1, LANES), F32), jax.ShapeDtypeStruct((1, LANES), F32)],
        scratch_shapes=[pltpu.VMEM((n_pairs, N, PAIR), F32), pltpu.VMEM((LANES, Q), F32)],
        compiler_params=_cparams("arbitrary"),
    )(dy, dskip, xbc, proj, dt_bias, a_log, states, expand)


def ssd_gate_fwd(y, xbc, proj, d_skip, norm_w, d_inner, name):
    S = y.shape[0]
    tm = _pick(S, (256, 128))
    gs = d_inner // SSM_GROUPS

    def body(y_ref, x_ref, z_ref, d_ref, w_ref, o_ref):
        for g in range(SSM_GROUPS):
            sl = slice(g * gs, (g + 1) * gs)
            y2 = (y_ref[:, sl] + d_ref[:, sl] * x_ref[:, sl]) * _silu(z_ref[:, sl])
            r = lax.rsqrt(jnp.mean(y2 * y2, axis=-1, keepdims=True) + RMS_EPS)
            o_ref[:, sl] = (y2 * r * w_ref[:, sl]).astype(o_ref.dtype)

    row = pl.BlockSpec((tm, d_inner), lambda i: (i, 0))
    vec = pl.BlockSpec((1, d_inner), lambda i: (0, 0))
    return pl.pallas_call(
        body, name=name, grid=(S // tm,), in_specs=[row, row, row, vec, vec], out_specs=row,
        out_shape=jax.ShapeDtypeStruct((S, d_inner), BF16), compiler_params=_cparams("parallel"),
    )(y, xbc, proj, d_skip, norm_w)


def ssd_gate_bwd(dyn, y, xbc, proj, d_skip, norm_w, d_inner, name):
    S = y.shape[0]
    tm = _pick(S, (256, 128))
    gs = d_inner // SSM_GROUPS

    def body(dyn_ref, y_ref, x_ref, z_ref, d_ref, w_ref, dy_ref, dskip_ref, dz_ref, dw_ref, dd_ref):
        i = pl.program_id(0)

        @pl.when(i == 0)
        def _():
            dw_ref[...] = jnp.zeros_like(dw_ref)
            dd_ref[...] = jnp.zeros_like(dd_ref)

        for g in range(SSM_GROUPS):
            sl = slice(g * gs, (g + 1) * gs)
            z = z_ref[:, sl]
            x = x_ref[:, sl]
            sz = _silu(z)
            ysum = y_ref[:, sl] + d_ref[:, sl] * x
            y2 = ysum * sz
            r = lax.rsqrt(jnp.mean(y2 * y2, axis=-1, keepdims=True) + RMS_EPS)
            dyn = dyn_ref[:, sl]
            a = dyn * w_ref[:, sl]
            dy2 = r * a - y2 * (r * r * r) * jnp.mean(a * y2, axis=-1, keepdims=True)
            dysum = dy2 * sz
            dy_ref[:, sl] = dysum
            dskip_ref[:, sl] = dysum * d_ref[:, sl]
            dz_ref[:, sl] = (dy2 * ysum * _silu_grad(z)).astype(dz_ref.dtype)
            dw_ref[:, sl] += jnp.sum(dyn * y2 * r, axis=0, keepdims=True)
            dd_ref[:, sl] += jnp.sum(dysum * x, axis=0, keepdims=True)

    row = pl.BlockSpec((tm, d_inner), lambda i: (i, 0))
    vec = pl.BlockSpec((1, d_inner), lambda i: (0, 0))
    return pl.pallas_call(
        body, name=name, grid=(S // tm,), in_specs=[row, row, row, row, vec, vec], out_specs=[row, row, row, vec, vec],
        out_shape=[jax.ShapeDtypeStruct((S, d_inner), F32), jax.ShapeDtypeStruct((S, d_inner), F32),
                   jax.ShapeDtypeStruct((S, d_inner), BF16), jax.ShapeDtypeStruct((1, d_inner), F32),
                   jax.ShapeDtypeStruct((1, d_inner), F32)],
        compiler_params=_cparams("arbitrary"),
    )(dyn, y, xbc, proj, d_skip, norm_w)


def _pad_to(a, axis, mult=LANES):
    n = a.shape[axis]
    extra = (-n) % mult
    if extra == 0:
        return a
    widths = [(0, 0)] * a.ndim
    widths[axis] = (0, extra)
    return jnp.pad(a, widths)


def _attn_fwd(x, w_in, b_f, w_out, tag, ride=()):
    S, D = x.shape
    H = D // HEAD_DIM
    T = _attn_tile(S)
    proj = mm_nn(x, _pad_to(w_in, 1), F32, f"{tag}_proj")
    qkv = proj[:, :3 * D].astype(BF16).reshape(S, 3, H, HEAD_DIM).transpose(1, 2, 0, 3)
    zt = proj[:, 3 * D:3 * D + H].T
    bf = b_f.reshape(H, 1)
    c = fox_gate_fwd(zt, bf, f"{tag}_gate")
    c_col, c_row = c.reshape(H, S, 1), c.reshape(H, S // T, 1, T)
    v_ones = jnp.concatenate([qkv[2], jnp.ones_like(qkv[2])], axis=-1)
    o, lse, *ride_out = flash_fwd(qkv[0], qkv[1], v_ones, c_col, c_row, f"{tag}_flash", ride)
    o_flat = o.transpose(1, 0, 2).reshape(S, D)
    mix = mm_nn(o_flat, w_out, F32, f"{tag}_out")
    return mix, (qkv, zt, bf, c_col, c_row, o, lse, o_flat), ride_out


def _attn_bwd(x, dmix, dx_add, alpha, w_in, w_out, saved, tag, ride=()):
    S, D = x.shape
    H = D // HEAD_DIM
    T = _attn_tile(S)
    qkv, zt, bf, c_col, c_row, o, lse, o_flat = saved
    g_w_out = mm_tn(o_flat, dmix, f"{tag}_gwout")
    do = mm_nn(dmix, w_out.T, BF16, f"{tag}_do").reshape(S, H, HEAD_DIM)
    dq, dc_q, dk_t, dv_t, dc_k, *ride_out = flash_bwd_q(
        qkv[0], qkv[1], qkv[2], o, do.transpose(1, 0, 2), qkv[0].transpose(0, 2, 1), do.transpose(1, 2, 0), lse, c_col, c_row,
        f"{tag}_flash_bwd", ride)
    dzt, dbf = fox_gate_bwd(dc_q.reshape(H, S), dc_k.reshape(H, S), zt, bf, f"{tag}_gate_bwd")
    keys_major = lambda t: t.transpose(1, 3, 0, 2).reshape(S, D).astype(BF16)
    dqkv = jnp.concatenate([dq.transpose(1, 0, 2).reshape(S, D), keys_major(dk_t), keys_major(dv_t)], axis=1)
    dzf = _pad_to(dzt.T, 1)
    g_w_in = jnp.concatenate([mm_tn(x, dqkv, f"{tag}_gwqkv"), mm_tn(x, dzf, f"{tag}_gwf")[:, :H]], axis=1)
    w_in_t = w_in.T
    dx = mm_nn(dqkv, w_in_t[:3 * D], F32, f"{tag}_dx_qkv", add=dx_add, add_scale=alpha)
    dx = mm_nn(dzf, _pad_to(w_in_t[3 * D:], 0), F32, f"{tag}_dx_f", add=dx)
    return dx, (g_w_in, dbf.reshape(H), g_w_out), ride_out


def _ssm_dims(D):
    d_inner = 2 * D
    gn = SSM_GROUPS * SSM_STATE
    return d_inner, d_inner + 2 * gn, d_inner // HEAD_DIM


def _ssm_fwd(x, w_in, conv_w, conv_b, dt_bias, a_log, d_skip, norm_w, w_out, tag):
    S, D = x.shape
    DI, XBC, HS = _ssm_dims(D)
    dt_col = DI + XBC
    proj = mm_nn(x, _pad_to(w_in, 1), F32, f"{tag}_proj")
    conv_b = conv_b.reshape(1, XBC)
    xbc = conv_act_fwd(proj, DI, XBC, conv_w, conv_b, F32, f"{tag}_conv")
    dt_bias_p = _pad_to(dt_bias.reshape(1, HS), 1)
    a_log_p = _pad_to(a_log.reshape(1, HS), 1)
    y, states = ssd_scan_fwd(xbc, proj, dt_col, dt_bias_p, a_log_p, DI, f"{tag}_scan")
    d_vec = jnp.repeat(d_skip, HEAD_DIM).reshape(1, DI)
    norm_w = norm_w.reshape(1, DI)
    yn = ssd_gate_fwd(y, xbc, proj, d_vec, norm_w, DI, f"{tag}_gate")
    mix = mm_nn(yn, w_out, F32, f"{tag}_out")
    return mix, (proj, xbc, conv_b, dt_bias_p, a_log_p, y, states, d_vec, norm_w, yn)


def _ssm_bwd(x, dmix, dx_add, alpha, w_in, conv_w, w_out, saved, tag):
    S, D = x.shape
    DI, XBC, HS = _ssm_dims(D)
    dt_col = DI + XBC
    proj, xbc, conv_b, dt_bias_p, a_log_p, y, states, d_vec, norm_w, yn = saved
    g_w_out = mm_tn(yn, dmix, f"{tag}_gwout")
    dyn = mm_nn(dmix, w_out.T, F32, f"{tag}_dyn")
    dy, dskip, dz, g_norm_w, g_dvec = ssd_gate_bwd(dyn, y, xbc, proj, d_vec, norm_w, DI, f"{tag}_gate_bwd")
    dxbc, ddt_raw, g_a_log, g_dt_bias = ssd_scan_bwd(dy, dskip, xbc, proj, dt_col, dt_bias_p, a_log_p, states,
                                                     _head_expand(DI), DI, f"{tag}_scan_bwd")
    dxbc_raw, _, g_conv_w, g_conv_b = conv_act_bwd(dxbc, proj, DI, XBC, conv_w, conv_b, f"{tag}_conv_bwd")
    g_w_in = jnp.concatenate([mm_tn(x, dz, f"{tag}_gwz"), mm_tn(x, dxbc_raw, f"{tag}_gwxbc"),
                              mm_tn(x, ddt_raw, f"{tag}_gwdt")[:, :HS]], axis=1)
    w_in_t = w_in.T
    dx = mm_nn(dz, w_in_t[:DI], F32, f"{tag}_dx_z", add=dx_add, add_scale=alpha)
    dx = mm_nn(dxbc_raw, w_in_t[DI:dt_col], F32, f"{tag}_dx_xbc", add=dx)
    dx = mm_nn(ddt_raw, _pad_to(w_in_t[dt_col:], 0), F32, f"{tag}_dx_dt", add=dx)
    g_d = g_dvec.reshape(HS, HEAD_DIM).sum(axis=-1)
    grads = (g_w_in, g_conv_w, g_conv_b.reshape(XBC), g_dt_bias[0, :HS], g_a_log[0, :HS], g_d, g_norm_w.reshape(DI), g_w_out)
    return dx, grads


ATTN_KEYS = ("attn_w_in", "attn_b_f", "attn_w_out")
SSM_KEYS = ("ssm_w_in", "ssm_conv_w", "ssm_conv_b", "ssm_dt_bias", "ssm_A_log", "ssm_D", "ssm_norm_w", "ssm_w_out")
LAYER_KEYS = ("ln_mix_g", "ln_mix_b", "ffn_w_up", "ffn_conv_w", "ffn_conv_b", "ffn_w_down", "ln_ffn_g", "ln_ffn_b",
              "ple_w_proj", "ple_w_gate", "ple_b_gate")


def local_step(x, p, target, w, fwd_ride=(), after_fwd_ride=None, before_last_bwd=None):
    S, D = x.shape
    depth = p.shape[0]
    alpha = (2 * depth) ** 0.25
    saved = []
    bwd_ride_out = []
    for i in range(depth):
        j, tag = i // 2, f"l{i}"
        if i % 2 == 0:
            mix, msaved, ride_out = _attn_fwd(x, w["attn_w_in"][j], w["attn_b_f"][j], w["attn_w_out"][j], tag + "_attn",
                                              fwd_ride if i == 0 else ())
            if i == 0 and after_fwd_ride is not None:
                w = {**w, **after_fwd_ride(ride_out)}
        else:
            mix, msaved = _ssm_fwd(x, w["ssm_w_in"][j], w["ssm_conv_w"][j], w["ssm_conv_b"][j], w["ssm_dt_bias"][j],
                                   w["ssm_A_log"][j], w["ssm_D"][j], w["ssm_norm_w"][j], w["ssm_w_out"][j], tag + "_ssm")
        row = lambda k: w[k][i].reshape(1, -1)
        F = w["ffn_w_down"].shape[1]
        x1, xhat1, rstd1 = ln_fwd(x, mix, row("ln_mix_g"), row("ln_mix_b"), alpha, tag + "_ln_mix")
        h = mm_nn(x1, w["ffn_w_up"][i], F32, tag + "_ffn_up")
        a = conv_act_fwd(h, F, F, w["ffn_conv_w"][i], row("ffn_conv_b"), BF16, tag + "_ffn_act", gate_col=0)
        ffn = mm_nn(a, w["ffn_w_down"][i], F32, tag + "_ffn_down")
        x2, xhat2, rstd2 = ln_fwd(x1, ffn, row("ln_ffn_g"), row("ln_ffn_b"), alpha, tag + "_ln_ffn")
        zg = mm_nn(x2, w["ple_w_gate"][i], F32, tag + "_ple_gate")
        pp = mm_nn(p[i], w["ple_w_proj"][i], F32, tag + "_ple_proj")
        x3 = ple_fwd(x2, zg, pp, row("ple_b_gate"), tag + "_ple")
        saved.append((x, msaved, x1, xhat1, rstd1, h, a, x2, xhat2, rstd2, zg, pp))
        x = x3

    loss_vec, d = loss_head(x, target, "loss_head")

    grads = {k: [None] * w[k].shape[0] for k in ATTN_KEYS + SSM_KEYS + LAYER_KEYS}
    for i in reversed(range(depth)):
        j, tag = i // 2, f"l{i}"
        x0, msaved, x1, xhat1, rstd1, h, a, x2, xhat2, rstd2, zg, pp = saved[i]
        row = lambda k: w[k][i].reshape(1, -1)
        dzg, dpp, g_bg = ple_bwd(d, zg, pp, row("ple_b_gate"), tag + "_ple_bwd")
        grads["ple_w_gate"][i] = mm_tn(x2, dzg, tag + "_gw_ple_gate")
        grads["ple_w_proj"][i] = mm_tn(p[i], dpp, tag + "_gw_ple_proj")
        grads["ple_b_gate"][i] = g_bg.reshape(D)
        dx2 = mm_nn(dzg, w["ple_w_gate"][i].T, F32, tag + "_dx2", add=d)
        dr2, g_g2, g_b2 = ln_bwd(dx2, xhat2, rstd2, row("ln_ffn_g"), tag + "_ln_ffn_bwd")
        grads["ln_ffn_g"][i], grads["ln_ffn_b"][i] = g_g2.reshape(D), g_b2.reshape(D)
        grads["ffn_w_down"][i] = mm_tn(a, dr2, tag + "_gw_down")
        da = mm_nn(dr2, w["ffn_w_down"][i].T, F32, tag + "_da")
        dgin, du, g_cw, g_cb = conv_act_bwd(da, h, F, F, w["ffn_conv_w"][i], row("ffn_conv_b"), tag + "_ffn_act_bwd", gate_col=0)
        grads["ffn_conv_w"][i], grads["ffn_conv_b"][i] = g_cw, g_cb.reshape(F)
        grads["ffn_w_up"][i] = jnp.concatenate([mm_tn(x1, du, tag + "_gw_up_u"), mm_tn(x1, dgin, tag + "_gw_up_g")], axis=1)
        w_up_t = w["ffn_w_up"][i].T
        dx1 = mm_nn(du, w_up_t[:F], F32, tag + "_dx1_u", add=dr2, add_scale=alpha)
        dx1 = mm_nn(dgin, w_up_t[F:], F32, tag + "_dx1_g", add=dx1)
        dr1, g_g1, g_b1 = ln_bwd(dx1, xhat1, rstd1, row("ln_mix_g"), tag + "_ln_mix_bwd")
        grads["ln_mix_g"][i], grads["ln_mix_b"][i] = g_g1.reshape(D), g_b1.reshape(D)
        if i % 2 == 0:
            ride = before_last_bwd(grads) if (i == 0 and before_last_bwd is not None) else ()
            d, mg, ride_out = _attn_bwd(x0, dr1, dr1, alpha, w["attn_w_in"][j], w["attn_w_out"][j], msaved, tag + "_attn", ride)
            if i == 0:
                bwd_ride_out = ride_out
            for k, g in zip(ATTN_KEYS, mg):
                grads[k][j] = g
        else:
            d, mg = _ssm_bwd(x0, dr1, dr1, alpha, w["ssm_w_in"][j], w["ssm_conv_w"][j], w["ssm_w_out"][j], msaved, tag + "_ssm")
            for k, g in zip(SSM_KEYS, mg):
                grads[k][j] = g
    return loss_vec, d, {k: jnp.stack(v) for k, v in grads.items()}, bwd_ride_out


MESH = pl.DeviceIdType.MESH
PACK_ELEMS = 2 * SUBLANES * LANES
PACK_ROWS = 512


def _exchange_copies(src_ref, out_ref, gather, send_sems, recv_sems, local_sem):
    x, y, c = lax.axis_index("x"), lax.axis_index("y"), lax.axis_index("c")
    me = 4 * x + 2 * y + c

    def block_for(dev):
        return src_ref if gather else src_ref.at[dev]

    copies = [pltpu.make_async_copy(block_for(me), out_ref.at[me], local_sem)]
    for k in range(1, N_DEV):
        px = 1 - x if k & 4 else x
        py = 1 - y if k & 2 else y
        pc = 1 - c if k & 1 else c
        copies.append(pltpu.make_async_remote_copy(
            src_ref=block_for(4 * px + 2 * py + pc), dst_ref=out_ref.at[me],
            send_sem=send_sems.at[k - 1], recv_sem=recv_sems.at[k - 1],
            device_id=(px, py, pc), device_id_type=MESH))
    return copies


def _exchange(src, gather, name):
    def body(src_ref, out_ref, send_sems, recv_sems, local_sem):
        copies = _exchange_copies(src_ref, out_ref, gather, send_sems, recv_sems, local_sem)
        for cp in copies:
            cp.start()
        for cp in copies:
            cp.wait()

    return pl.pallas_call(
        body, name=name,
        in_specs=[pl.BlockSpec(memory_space=pl.ANY)],
        out_specs=pl.BlockSpec(memory_space=pl.ANY),
        out_shape=jax.ShapeDtypeStruct((N_DEV,) + tuple(src.shape[-2:]), src.dtype),
        scratch_shapes=[pltpu.SemaphoreType.DMA((N_DEV - 1,)), pltpu.SemaphoreType.DMA((N_DEV - 1,)), pltpu.SemaphoreType.DMA],
    )(src)


def _ride_decl(ride):
    if not ride:
        return [], [], [], []
    hbm = pl.BlockSpec(memory_space=pl.ANY)
    out_shape = [jax.ShapeDtypeStruct((N_DEV,) + tuple(a.shape[-2:]), a.dtype) for a, _ in ride]
    sems = [pltpu.SemaphoreType.DMA((len(ride), N_DEV - 1)), pltpu.SemaphoreType.DMA((len(ride), N_DEV - 1)),
            pltpu.SemaphoreType.DMA((len(ride),))]
    return [hbm] * len(ride), [hbm] * len(ride), out_shape, sems


def _ride_copies(ride, src_refs, out_refs, sems):
    copies = []
    for r, ((_, gather), src_ref, out_ref) in enumerate(zip(ride, src_refs, out_refs)):
        copies += _exchange_copies(src_ref, out_ref, gather, sems[0].at[r], sems[1].at[r], sems[2].at[r])
    return copies


def reduce_adamw(parts, w, m, v, name):
    _, R, _ = parts.shape
    tr = _pick(R, (512, 256, 128, 64, 32, 16))

    def body(p_ref, w_ref, m_ref, v_ref, g_ref, d_ref, nm_ref, nv_ref):
        g = p_ref[0].astype(F32)
        for s in range(1, N_DEV):
            g = g + p_ref[s].astype(F32)
        nm = ADAM_B1 * m_ref[...] + (1.0 - ADAM_B1) * g
        nv = ADAM_B2 * v_ref[...] + (1.0 - ADAM_B2) * (g * g)
        m_hat = nm / (1.0 - ADAM_B1 ** ADAM_STEP)
        v_hat = nv / (1.0 - ADAM_B2 ** ADAM_STEP)
        g_ref[...] = g
        d_ref[...] = -ADAM_LR * (m_hat / (jnp.sqrt(v_hat) + ADAM_EPS) + ADAM_WD * w_ref[...])
        nm_ref[...] = nm
        nv_ref[...] = nv

    row = pl.BlockSpec((tr, LANES), lambda i: (i, 0))
    return pl.pallas_call(
        body, name=name, grid=(R // tr,),
        in_specs=[pl.BlockSpec((N_DEV, tr, LANES), lambda i: (0, i, 0)), row, row, row],
        out_specs=[row, row, row, row],
        out_shape=[jax.ShapeDtypeStruct((R, LANES), F32)] * 4,
        compiler_params=_cparams("parallel"),
    )(parts, w, m, v)


def _pack(arrays, dtype, lead=0):
    parts = []
    for a in arrays:
        head = a.shape[:lead]
        flat = a.astype(dtype).reshape(head + (-1,))
        flat = jnp.pad(flat, [(0, 0)] * lead + [(0, (-flat.shape[-1]) % PACK_ELEMS)])
        parts.append(flat.reshape(head + (-1, LANES)))
    rows = sum(part.shape[lead] for part in parts)
    if (-rows) % PACK_ROWS:
        parts.append(jnp.zeros(parts[0].shape[:lead] + ((-rows) % PACK_ROWS, LANES), dtype))
    return jnp.concatenate(parts, axis=lead)


def _unpack(packed, shapes):
    lead = packed.shape[:-2]
    out, r0 = [], 0
    for shape in shapes:
        n = math.prod(shape)
        rows = -(-n // PACK_ELEMS) * (PACK_ELEMS // LANES)
        seg = packed[..., r0:r0 + rows, :].reshape(lead + (rows * LANES,))[..., :n]
        out.append(seg.reshape(lead + tuple(shape)))
        r0 += rows
    return out


MATMUL_SHARDED = {"attn_w_in": 2, "attn_w_out": 1, "ssm_w_in": 2, "ssm_w_out": 1, "ffn_w_up": 2, "ffn_w_down": 1,
                  "ple_w_proj": 2, "ple_w_gate": 1}
SMALL_SHARDED = {"ssm_conv_w": 2, "ssm_conv_b": 1, "ssm_norm_w": 1, "ffn_conv_w": 2}
REPLICATED = ("attn_b_f", "ssm_dt_bias", "ssm_A_log", "ssm_D", "ln_mix_g", "ln_mix_b", "ffn_conv_b", "ln_ffn_g",
              "ln_ffn_b", "ple_b_gate")
WEIGHT_ORDER = ("attn_w_in", "attn_b_f", "attn_w_out", "ssm_w_in", "ssm_conv_w", "ssm_conv_b", "ssm_dt_bias", "ssm_A_log",
                "ssm_D", "ssm_norm_w", "ssm_w_out", "ln_mix_g", "ln_mix_b", "ffn_w_up", "ffn_conv_w", "ffn_conv_b",
                "ffn_w_down", "ln_ffn_g", "ln_ffn_b", "ple_w_proj", "ple_w_gate", "ple_b_gate")


def _join_shards(gathered, axis):
    moved = jnp.moveaxis(gathered, 0, axis)
    shape = list(moved.shape)
    shape[axis:axis + 2] = [shape[axis] * shape[axis + 1]]
    return moved.reshape(shape)


def _split_shards(full, axis):
    shape = list(full.shape)
    shape[axis:axis + 1] = [N_DEV, shape[axis] // N_DEV]
    return jnp.moveaxis(full.reshape(shape), axis, 0)


def kernel(x, p, attn_w_in, attn_b_f, attn_w_out, ssm_w_in, ssm_conv_w, ssm_conv_b, ssm_dt_bias, ssm_A_log, ssm_D, ssm_norm_w, ssm_w_out, ln_mix_g, ln_mix_b, ffn_w_up, ffn_conv_w, ffn_conv_b, ffn_w_down, ln_ffn_g, ln_ffn_b, ple_w_proj, ple_w_gate, ple_b_gate, loss_target, m_attn_w_in, m_attn_b_f, m_attn_w_out, m_ssm_w_in, m_ssm_conv_w, m_ssm_conv_b, m_ssm_dt_bias, m_ssm_A_log, m_ssm_D, m_ssm_norm_w, m_ssm_w_out, m_ln_mix_g, m_ln_mix_b, m_ffn_w_up, m_ffn_conv_w, m_ffn_conv_b, m_ffn_w_down, m_ln_ffn_g, m_ln_ffn_b, m_ple_w_proj, m_ple_w_gate, m_ple_b_gate, v_attn_w_in, v_attn_b_f, v_attn_w_out, v_ssm_w_in, v_ssm_conv_w, v_ssm_conv_b, v_ssm_dt_bias, v_ssm_A_log, v_ssm_D, v_ssm_norm_w, v_ssm_w_out, v_ln_mix_g, v_ln_mix_b, v_ffn_w_up, v_ffn_conv_w, v_ffn_conv_b, v_ffn_w_down, v_ln_ffn_g, v_ln_ffn_b, v_ple_w_proj, v_ple_w_gate, v_ple_b_gate):
    args = dict(locals())
    w_loc = {k: args[k] for k in WEIGHT_ORDER}
    m_loc = {k: args["m_" + k] for k in WEIGHT_ORDER}
    v_loc = {k: args["v_" + k] for k in WEIGHT_ORDER}
    axis_of = {**MATMUL_SHARDED, **SMALL_SHARDED}
    first_names = ("attn_w_in", "attn_w_out")
    mm_names = tuple(k for k in MATMUL_SHARDED if k not in first_names)
    small_names = tuple(SMALL_SHARDED)
    later_names = mm_names + small_names

    def joined(names, gathered):
        return {k: _join_shards(blocks, axis_of[k])
                for k, blocks in zip(names, _unpack(gathered, [w_loc[k].shape for k in names]))}

    g_first = _exchange(_pack([w_loc[k] for k in first_names], BF16), True, "gather_attn_weights")
    w_first = {**{k: w_loc[k] for k in REPLICATED}, **joined(first_names, g_first)}
    fwd_ride = ((_pack([w_loc[k] for k in mm_names], BF16), True), (_pack([w_loc[k] for k in small_names], F32), True))

    def after_fwd_ride(gathered):
        return {**joined(mm_names, gathered[0]), **joined(small_names, gathered[1])}

    def shard_blocks(grads, names):
        return _pack([_split_shards(grads[k], axis_of[k]) for k in names], BF16, lead=1)

    def before_last_bwd(grads):
        return ((shard_blocks({k: jnp.stack(grads[k]) for k in later_names}, later_names), False),)

    loss_vec, grad_x, g_full, (parts_later,) = local_step(x[0], p[:, 0], loss_target[0], w_first, fwd_ride, after_fwd_ride,
                                                          before_last_bwd)
    parts_first = _exchange(shard_blocks(g_full, first_names), False, "exchange_attn_grads")
    res = {}
    for names, parts, tag in ((later_names, parts_later, "later"), (first_names, parts_first, "attn")):
        shapes = [w_loc[k].shape for k in names]
        pk = lambda d: _pack([d[k] for k in names], F32)
        outs = reduce_adamw(parts, pk(w_loc), pk(m_loc), pk(v_loc), "reduce_adamw_" + tag)
        res.update({k: vals for k, vals in zip(names, zip(*[_unpack(o, shapes) for o in outs]))})

    rep_shapes = [w_loc[k].shape for k in REPLICATED] + [(1, LANES)]
    rparts = _exchange(_pack([g_full[k] for k in REPLICATED] + [loss_vec], F32), True, "gather_replicated_grads")
    zero = jnp.zeros((1, LANES), F32)
    rk = lambda d: _pack([d[k] for k in REPLICATED] + [zero], F32)
    routs = reduce_adamw(rparts, rk(w_loc), rk(m_loc), rk(v_loc), "reduce_adamw_replicated")
    runp = [_unpack(o, rep_shapes) for o in routs]
    for i, k in enumerate(REPLICATED):
        res[k] = tuple(u[i] for u in runp)
    loss = runp[0][-1][0, 0]

    return (loss, grad_x[None], *[res[k][0] for k in WEIGHT_ORDER], *[res[k][1] for k in WEIGHT_ORDER],
            *[res[k][2] for k in WEIGHT_ORDER], *[res[k][3] for k in WEIGHT_ORDER])
```

```python
import functools
import math

import jax
import jax.numpy as jnp
from jax import lax
from jax.experimental import pallas as pl
from jax.experimental.pallas import tpu as pltpu

F32 = jnp.float32
BF16 = jnp.bfloat16

LANES = 128
SUBLANES = 8
VMEM_LIMIT_BYTES = 56 * 1024 * 1024

N_DEV = 8
HEAD_DIM = 64
SSM_GROUPS = 8
SSM_STATE = 128
SSM_CHUNK = 128
SSM_CONV = 4
FFN_CONV = 3
LN_EPS = 1e-5
RMS_EPS = 1e-5
ADAM_LR, ADAM_B1, ADAM_B2, ADAM_EPS, ADAM_WD, ADAM_STEP = 0.001, 0.9, 0.999, 1e-08, 0.01, 10
NEG_INF = float("-inf")
HIGHEST = lax.Precision.HIGHEST
NT_DIMS = (((1,), (1,)), ((), ()))
TN_DIMS = (((0,), (0,)), ((), ()))


def _cparams(*sem):
    return pltpu.CompilerParams(dimension_semantics=sem, vmem_limit_bytes=VMEM_LIMIT_BYTES)


def _pick(n, candidates):
    for c in candidates:
        if n % c == 0:
            return c
    return n


def _dot(a, b):
    return jnp.dot(a.astype(BF16), b.astype(BF16), preferred_element_type=F32)


def _dot_nt(a, b):
    return lax.dot_general(a.astype(BF16), b.astype(BF16), NT_DIMS, preferred_element_type=F32)


def _dot_tn(a, b):
    return lax.dot_general(a.astype(BF16), b.astype(BF16), TN_DIMS, preferred_element_type=F32)


def _sigmoid(x):
    return 1.0 / (1.0 + jnp.exp(-x))


def _log1p_small(u):
    return jnp.where(u < 1e-3, u * (1.0 - u * (0.5 - u * (1.0 / 3.0))), jnp.log(1.0 + u))


def _softplus(x):
    return jnp.maximum(x, 0.0) + _log1p_small(jnp.exp(-jnp.abs(x)))


def mm_nn(a, b, out_dtype, name, add=None, add_scale=1.0):
    M, K = a.shape
    _, N = b.shape
    tm = _pick(M, (1024, 512, 256, 128))
    tn = N if N <= 1024 else _pick(N, (1408, 1024, 896, 768, 640, 512, 384, 256, 128))
    tk = K if K <= 2048 else _pick(K, (1408, 1024, 896, 768, 640, 512, 384, 256, 128))
    nk = K // tk

    def body(*refs):
        if add is None:
            a_ref, b_ref, o_ref, acc_ref = refs
        else:
            a_ref, b_ref, c_ref, o_ref, acc_ref = refs
        k = pl.program_id(2)
        part = _dot(a_ref[...], b_ref[...])

        @pl.when(k == 0)
        def _():
            acc_ref[...] = part

        @pl.when(k > 0)
        def _():
            acc_ref[...] += part

        @pl.when(k == nk - 1)
        def _():
            r = acc_ref[...]
            if add is not None:
                r = r + add_scale * c_ref[...].astype(F32)
            o_ref[...] = r.astype(o_ref.dtype)

    in_specs = [pl.BlockSpec((tm, tk), lambda i, j, k: (i, k)), pl.BlockSpec((tk, tn), lambda i, j, k: (k, j))]
    args = [a, b]
    if add is not None:
        in_specs.append(pl.BlockSpec((tm, tn), lambda i, j, k: (i, j)))
        args.append(add)
    return pl.pallas_call(
        body,
        name=name,
        grid=(M // tm, N // tn, nk),
        in_specs=in_specs,
        out_specs=pl.BlockSpec((tm, tn), lambda i, j, k: (i, j)),
        out_shape=jax.ShapeDtypeStruct((M, N), out_dtype),
        scratch_shapes=[pltpu.VMEM((tm, tn), F32)],
        compiler_params=_cparams("parallel", "parallel", "arbitrary"),
    )(*args)


def mm_tn(a, b, name):
    M, K = a.shape
    _, N = b.shape
    tm = _pick(M, (512, 256, 128))
    tk = K if K <= 1024 else _pick(K, (1408, 1024, 896, 768, 640, 512, 384, 256, 128))
    tn = N if N <= 1408 else _pick(N, (1408, 1024, 896, 768, 640, 512, 384, 256, 128))
    nm = M // tm

    def body(a_ref, b_ref, o_ref):
        m = pl.program_id(2)
        part = _dot_tn(a_ref[...], b_ref[...])

        @pl.when(m == 0)
        def _():
            o_ref[...] = part

        @pl.when(m > 0)
        def _():
            o_ref[...] += part

    return pl.pallas_call(
        body,
        name=name,
        grid=(K // tk, N // tn, nm),
        in_specs=[pl.BlockSpec((tm, tk), lambda i, j, m: (m, i)), pl.BlockSpec((tm, tn), lambda i, j, m: (m, j))],
        out_specs=pl.BlockSpec((tk, tn), lambda i, j, m: (i, j)),
        out_shape=jax.ShapeDtypeStruct((K, N), F32),
        compiler_params=_cparams("parallel", "parallel", "arbitrary"),
    )(a, b)


def ln_fwd(x, mix, g, b, alpha, name):
    S, D = x.shape
    tm = _pick(S, (512, 256, 128))

    def body(x_ref, mix_ref, g_ref, b_ref, y_ref, xhat_ref, rstd_ref):
        r = alpha * x_ref[...] + mix_ref[...]
        mu = jnp.mean(r, axis=-1, keepdims=True)
        xc = r - mu
        var = jnp.mean(xc * xc, axis=-1, keepdims=True)
        rstd = lax.rsqrt(var + LN_EPS)
        xhat = xc * rstd
        y_ref[...] = xhat * g_ref[...] + b_ref[...]
        xhat_ref[...] = xhat
        rstd_ref[...] = rstd

    row = pl.BlockSpec((tm, D), lambda i: (i, 0))
    vec = pl.BlockSpec((1, D), lambda i: (0, 0))
    return pl.pallas_call(
        body,
        name=name,
        grid=(S // tm,),
        in_specs=[row, row, vec, vec],
        out_specs=[row, row, pl.BlockSpec((tm, 1), lambda i: (i, 0))],
        out_shape=[jax.ShapeDtypeStruct((S, D), F32), jax.ShapeDtypeStruct((S, D), F32), jax.ShapeDtypeStruct((S, 1), F32)],
        compiler_params=_cparams("parallel"),
    )(x, mix, g, b)


def ln_bwd(dy, xhat, rstd, g, name):
    S, D = dy.shape
    tm = _pick(S, (512, 256, 128))

    def body(dy_ref, xhat_ref, rstd_ref, g_ref, dr_ref, dg_ref, db_ref):
        i = pl.program_id(0)
        dyv = dy_ref[...]
        xh = xhat_ref[...]
        dxh = dyv * g_ref[...]
        m1 = jnp.mean(dxh, axis=-1, keepdims=True)
        m2 = jnp.mean(dxh * xh, axis=-1, keepdims=True)
        dr_ref[...] = rstd_ref[...] * (dxh - m1 - xh * m2)
        dg_part = jnp.sum(dyv * xh, axis=0, keepdims=True)
        db_part = jnp.sum(dyv, axis=0, keepdims=True)

        @pl.when(i == 0)
        def _():
            dg_ref[...] = dg_part
            db_ref[...] = db_part

        @pl.when(i > 0)
        def _():
            dg_ref[...] += dg_part
            db_ref[...] += db_part

    row = pl.BlockSpec((tm, D), lambda i: (i, 0))
    vec = pl.BlockSpec((1, D), lambda i: (0, 0))
    return pl.pallas_call(
        body,
        name=name,
        grid=(S // tm,),
        in_specs=[row, row, pl.BlockSpec((tm, 1), lambda i: (i, 0)), vec],
        out_specs=[row, vec, vec],
        out_shape=[jax.ShapeDtypeStruct((S, D), F32), jax.ShapeDtypeStruct((1, D), F32), jax.ShapeDtypeStruct((1, D), F32)],
        compiler_params=_cparams("arbitrary"),
    )(dy, xhat, rstd, g)


HALO = 2 * SUBLANES


def _prev_halo_spec(tm, tc, col0):
    return pl.BlockSpec((HALO, tc), lambda i, j: (jnp.maximum(i * (tm // HALO) - 1, 0), j + col0))


def _next_halo_spec(tm, tc, col0, n_row_tiles):
    last = n_row_tiles * (tm // HALO) - 1
    return pl.BlockSpec((HALO, tc), lambda i, j: (jnp.minimum((i + 1) * (tm // HALO), last), j + col0))


CONV_CHUNK = 32


def _causal_conv(ext_ref, w, n_taps, n_rows, row0, lanes):
    acc = None
    for k in range(n_taps):
        term = ext_ref[pl.ds(row0 - (n_taps - 1) + k, n_rows), lanes] * w[k:k + 1, :]
        acc = term if acc is None else acc + term
    return acc


def _anticausal_conv(ext_ref, w, n_taps, n_rows, row0, lanes):
    acc = None
    for k in range(n_taps):
        term = ext_ref[pl.ds(row0 + n_taps - 1 - k, n_rows), lanes] * w[k:k + 1, :]
        acc = term if acc is None else acc + term
    return acc


def _fold8(a):
    acc = a[0:SUBLANES]
    for g in range(SUBLANES, a.shape[0], SUBLANES):
        acc = acc + a[g:g + SUBLANES]
    return acc


INV_SQRT2 = 1.0 / math.sqrt(2.0)
INV_SQRT_2PI = 1.0 / math.sqrt(2.0 * math.pi)


def _gelu(g):
    return 0.5 * g * (1.0 + lax.erf(g * INV_SQRT2))


def _silu(x):
    return x * _sigmoid(x)


def _silu_grad(x):
    s = _sigmoid(x)
    return s * (1.0 + x * (1.0 - s))


def _conv_tiles(S, C, cols):
    tm = _pick(S, (256, 128))
    for tc in (1408, 1024, 512, 256, 128):
        if C % tc == 0 and all(c % tc == 0 for c in cols):
            return tm, tc
    raise ValueError("no column tile for the conv kernels")


def conv_act_fwd(src, in_col, C, conv_w, conv_b, out_dtype, name, gate_col=None):
    S = src.shape[0]
    K = conv_w.shape[0]
    gated = gate_col is not None
    tm, tc = _conv_tiles(S, C, [in_col] + ([gate_col] if gated else []))
    c_in = in_col // tc
    c_gate = gate_col // tc if gated else 0

    def body(*refs):
        if gated:
            x_ref, xp_ref, w_ref, b_ref, u_ref, o_ref, ext_ref = refs
        else:
            x_ref, xp_ref, w_ref, b_ref, o_ref, ext_ref = refs
        i = pl.program_id(0)
        ext_ref[0:HALO] = jnp.where(i > 0, xp_ref[...].astype(F32), 0.0)
        ext_ref[HALO:HALO + tm] = x_ref[...].astype(F32)
        for l0 in range(0, tc, LANES):
            ls = slice(l0, l0 + LANES)
            w = w_ref[:, ls]
            b = b_ref[:, ls]
            for r0 in range(0, tm, CONV_CHUNK):
                pre = _causal_conv(ext_ref, w, K, CONV_CHUNK, HALO + r0, ls) + b
                out = _gelu(pre) * u_ref[pl.ds(r0, CONV_CHUNK), ls].astype(F32) if gated else _silu(pre)
                o_ref[pl.ds(r0, CONV_CHUNK), ls] = out.astype(o_ref.dtype)

    in_specs = [
        pl.BlockSpec((tm, tc), lambda i, j: (i, j + c_in)),
        _prev_halo_spec(tm, tc, c_in),
        pl.BlockSpec((K, tc), lambda i, j: (0, j)),
        pl.BlockSpec((1, tc), lambda i, j: (0, j)),
    ]
    args = [src, src, conv_w, conv_b]
    if gated:
        in_specs.append(pl.BlockSpec((tm, tc), lambda i, j: (i, j + c_gate)))
        args.append(src)
    return pl.pallas_call(
        body,
        name=name,
        grid=(S // tm, C // tc),
        in_specs=in_specs,
        out_specs=pl.BlockSpec((tm, tc), lambda i, j: (i, j)),
        out_shape=jax.ShapeDtypeStruct((S, C), out_dtype),
        scratch_shapes=[pltpu.VMEM((tm + HALO, tc), F32)],
        compiler_params=_cparams("parallel", "parallel"),
    )(*args)


def conv_act_bwd(d_out, src, in_col, C, conv_w, conv_b, name, gate_col=None):
    S = src.shape[0]
    K = conv_w.shape[0]
    gated = gate_col is not None
    tm, tc = _conv_tiles(S, C, [in_col] + ([gate_col] if gated else []))
    c_in = in_col // tc
    c_gate = gate_col // tc if gated else 0
    ni = S // tm
    te = tm + HALO

    def body(*refs):
        if gated:
            (d_ref, dn_ref, x_ref, xp_ref, xn_ref, w_ref, b_ref, u_ref, un_ref,
             dx_ref, dw_ref, db_ref, du_ref, xext_ref, dext_ref) = refs
        else:
            (d_ref, dn_ref, x_ref, xp_ref, xn_ref, w_ref, b_ref,
             dx_ref, dw_ref, db_ref, xext_ref, dext_ref) = refs
        i = pl.program_id(1)
        xext_ref[0:HALO] = jnp.where(i > 0, xp_ref[...].astype(F32), 0.0)
        xext_ref[HALO:HALO + tm] = x_ref[...].astype(F32)
        xext_ref[HALO + tm:HALO + te] = xn_ref[...].astype(F32)

        @pl.when(i == 0)
        def _():
            dw_ref[...] = jnp.zeros_like(dw_ref)
            db_ref[...] = jnp.zeros_like(db_ref)

        for l0 in range(0, tc, LANES):
            ls = slice(l0, l0 + LANES)
            w = w_ref[:, ls]
            b = b_ref[:, ls]
            acc_w = [jnp.zeros((SUBLANES, LANES), F32) for _ in range(K)]
            acc_b = jnp.zeros((SUBLANES, LANES), F32)
            for r0 in range(0, te, CONV_CHUNK):
                n = min(CONV_CHUNK, te - r0)
                inside = r0 < tm
                taps = [xext_ref[pl.ds(HALO + r0 - (K - 1) + k, n), ls] for k in range(K)]
                pre = sum(t * w[k:k + 1, :] for k, t in enumerate(taps)) + b
                d = (d_ref[pl.ds(r0, n), ls] if inside else dn_ref[:, ls]).astype(F32)
                if gated:
                    u = (u_ref[pl.ds(r0, n), ls] if inside else un_ref[:, ls]).astype(F32)
                    cdf = 0.5 * (1.0 + lax.erf(pre * INV_SQRT2))
                    dpre = d * u * (cdf + pre * jnp.exp(-0.5 * pre * pre) * INV_SQRT_2PI)
                    if inside:
                        du_ref[pl.ds(r0, n), ls] = (d * (pre * cdf)).astype(du_ref.dtype)
                else:
                    dpre = d * _silu_grad(pre)
                if inside:
                    for k in range(K):
                        acc_w[k] = acc_w[k] + _fold8(dpre * taps[k])
                    acc_b = acc_b + _fold8(dpre)
                else:
                    dpre = jnp.where(i < ni - 1, dpre, 0.0)
                dext_ref[pl.ds(r0, n), ls] = dpre
            for r0 in range(0, tm, CONV_CHUNK):
                dx = _anticausal_conv(dext_ref, w, K, CONV_CHUNK, r0, ls)
                dx_ref[pl.ds(r0, CONV_CHUNK), ls] = dx.astype(dx_ref.dtype)
            dw_rows = [jnp.sum(a, axis=0, keepdims=True) for a in acc_w]
            dw_ref[:, ls] += jnp.concatenate(dw_rows + [jnp.zeros((SUBLANES - K, LANES), F32)], axis=0)
            db_ref[:, ls] += jnp.sum(acc_b, axis=0, keepdims=True)

    last = ni * (tm // HALO) - 1
    cur = lambda c0: pl.BlockSpec((tm, tc), lambda j, i: (i, j + c0))
    prev = lambda c0: pl.BlockSpec((HALO, tc), lambda j, i: (jnp.maximum(i * (tm // HALO) - 1, 0), j + c0))
    nxt = lambda c0: pl.BlockSpec((HALO, tc), lambda j, i: (jnp.minimum((i + 1) * (tm // HALO), last), j + c0))
    vec = lambda rows: pl.BlockSpec((rows, tc), lambda j, i: (0, j))
    in_specs = [cur(0), nxt(0), cur(c_in), prev(c_in), nxt(c_in), vec(K), vec(1)]
    args = [d_out, d_out, src, src, src, conv_w, conv_b]
    out_specs = [cur(0), vec(SUBLANES), vec(1)]
    out_shape = [jax.ShapeDtypeStruct((S, C), BF16), jax.ShapeDtypeStruct((SUBLANES, C), F32), jax.ShapeDtypeStruct((1, C), F32)]
    if gated:
        in_specs += [cur(c_gate), nxt(c_gate)]
        args += [src, src]
        out_specs.append(cur(0))
        out_shape.append(jax.ShapeDtypeStruct((S, C), BF16))
    outs = pl.pallas_call(
        body,
        name=name,
        grid=(C // tc, ni),
        in_specs=in_specs,
        out_specs=out_specs,
        out_shape=out_shape,
        scratch_shapes=[pltpu.VMEM((tm + 2 * HALO, tc), F32), pltpu.VMEM((te, tc), F32)],
        compiler_params=_cparams("parallel", "arbitrary"),
    )(*args)
    return outs[0], (outs[3] if gated else None), outs[1][:K], outs[2]


def ple_fwd(x2, zg, pp, bg, name):
    S, D = x2.shape
    tm = _pick(S, (512, 256, 128))

    def body(x_ref, z_ref, p_ref, b_ref, o_ref):
        o_ref[...] = x_ref[...] + _sigmoid(z_ref[...] + b_ref[...]) * p_ref[...]

    row = pl.BlockSpec((tm, D), lambda i: (i, 0))
    return pl.pallas_call(
        body, name=name, grid=(S // tm,), in_specs=[row, row, row, pl.BlockSpec((1, D), lambda i: (0, 0))], out_specs=row,
        out_shape=jax.ShapeDtypeStruct((S, D), F32), compiler_params=_cparams("parallel"),
    )(x2, zg, pp, bg)


def ple_bwd(dx3, zg, pp, bg, name):
    S, D = dx3.shape
    tm = _pick(S, (512, 256, 128))

    def body(d_ref, z_ref, p_ref, b_ref, dz_ref, dp_ref, db_ref):
        i = pl.program_id(0)
        d = d_ref[...]
        gate = _sigmoid(z_ref[...] + b_ref[...])
        dz = d * p_ref[...] * gate * (1.0 - gate)
        dz_ref[...] = dz.astype(dz_ref.dtype)
        dp_ref[...] = (d * gate).astype(dp_ref.dtype)
        part = jnp.sum(dz, axis=0, keepdims=True)

        @pl.when(i == 0)
        def _():
            db_ref[...] = part

        @pl.when(i > 0)
        def _():
            db_ref[...] += part

    row = pl.BlockSpec((tm, D), lambda i: (i, 0))
    vec = pl.BlockSpec((1, D), lambda i: (0, 0))
    return pl.pallas_call(
        body, name=name, grid=(S // tm,), in_specs=[row, row, row, vec], out_specs=[row, row, vec],
        out_shape=[jax.ShapeDtypeStruct((S, D), BF16), jax.ShapeDtypeStruct((S, D), BF16), jax.ShapeDtypeStruct((1, D), F32)],
        compiler_params=_cparams("arbitrary"),
    )(dx3, zg, pp, bg)


def loss_head(y, target, name):
    S, D = y.shape
    tm = _pick(S, (512, 256, 128))

    def body(y_ref, t_ref, loss_ref, dy_ref, acc_ref):
        i = pl.program_id(0)
        err = y_ref[...] - t_ref[...]
        dy_ref[...] = err * (1.0 / D)
        part = jnp.sum(err * err, axis=0, keepdims=True)

        @pl.when(i == 0)
        def _():
            acc_ref[...] = part

        @pl.when(i > 0)
        def _():
            acc_ref[...] += part

        @pl.when(i == pl.num_programs(0) - 1)
        def _():
            loss_ref[...] = jnp.zeros((1, LANES), F32) + (0.5 / D) * jnp.sum(acc_ref[...])

    row = pl.BlockSpec((tm, D), lambda i: (i, 0))
    return pl.pallas_call(
        body, name=name, grid=(S // tm,), in_specs=[row, row],
        out_specs=[pl.BlockSpec((1, LANES), lambda i: (0, 0)), row],
        out_shape=[jax.ShapeDtypeStruct((1, LANES), F32), jax.ShapeDtypeStruct((S, D), F32)],
        scratch_shapes=[pltpu.VMEM((1, D), F32)],
        compiler_params=_cparams("arbitrary"),
    )(y, target)


ATTN_TILE = 1024
ATTN_SCALE = 1.0 / math.sqrt(HEAD_DIM)


def _attn_tile(S):
    return _pick(S, (ATTN_TILE, 512, 256, 128))


def fox_gate_fwd(zt, bf, name):
    H, S = zt.shape
    tl = _pick(S, (512, 256, 128))

    def body(z_ref, b_ref, c_ref, carry_ref):
        i = pl.program_id(0)

        @pl.when(i == 0)
        def _():
            carry_ref[...] = jnp.zeros_like(carry_ref)

        z = z_ref[...] + b_ref[...]
        logf = jnp.minimum(z, 0.0) - _log1p_small(jnp.exp(-jnp.abs(z)))
        r = lax.broadcasted_iota(jnp.int32, (tl, tl), 0)
        c = lax.broadcasted_iota(jnp.int32, (tl, tl), 1)
        upper = (r <= c).astype(F32)
        cum = jnp.dot(logf, upper, precision=HIGHEST, preferred_element_type=F32) + carry_ref[...]
        c_ref[...] = cum
        carry_ref[...] = cum[:, tl - 1:tl]

    return pl.pallas_call(
        body, name=name, grid=(S // tl,),
        in_specs=[pl.BlockSpec((H, tl), lambda i: (0, i)), pl.BlockSpec((H, 1), lambda i: (0, 0))],
        out_specs=pl.BlockSpec((H, tl), lambda i: (0, i)),
        out_shape=jax.ShapeDtypeStruct((H, S), F32),
        scratch_shapes=[pltpu.VMEM((H, 1), F32)],
        compiler_params=_cparams("arbitrary"),
    )(zt, bf)


def fox_gate_bwd(dc_q, dc_k, zt, bf, name):
    H, S = zt.shape
    tl = _pick(S, (512, 256, 128))
    nt = S // tl

    def body(dcq_ref, dck_ref, z_ref, b_ref, dz_ref, db_ref, carry_ref):
        i = pl.program_id(0)

        @pl.when(i == 0)
        def _():
            carry_ref[...] = jnp.zeros_like(carry_ref)
            db_ref[...] = jnp.zeros_like(db_ref)

        r = lax.broadcasted_iota(jnp.int32, (tl, tl), 0)
        c = lax.broadcasted_iota(jnp.int32, (tl, tl), 1)
        lower = (r >= c).astype(F32)
        dc = dcq_ref[...] + dck_ref[...]
        suffix = jnp.dot(dc, lower, precision=HIGHEST, preferred_element_type=F32) + carry_ref[...]
        carry_ref[...] = suffix[:, 0:1]
        dz = suffix * _sigmoid(-(z_ref[...] + b_ref[...]))
        dz_ref[...] = dz
        db_ref[...] += jnp.sum(dz, axis=1, keepdims=True)

    rev = pl.BlockSpec((H, tl), lambda i: (0, nt - 1 - i))
    return pl.pallas_call(
        body, name=name, grid=(nt,),
        in_specs=[rev, rev, rev, pl.BlockSpec((H, 1), lambda i: (0, 0))],
        out_specs=[rev, pl.BlockSpec((H, 1), lambda i: (0, 0))],
        out_shape=[jax.ShapeDtypeStruct((H, S), F32), jax.ShapeDtypeStruct((H, 1), F32)],
        scratch_shapes=[pltpu.VMEM((H, 1), F32)],
        compiler_params=_cparams("arbitrary"),
    )(dc_q, dc_k, zt, bf)


ATTN_SUB = 4
ATTN_CHUNK = 16


def _row_tiles(ref, j, n_tiles):
    if n_tiles == 1:
        return ref[0, j]
    return jnp.concatenate([ref[0, j + t] for t in range(n_tiles)], axis=1)


def _diag_mask(n_rows, width, row0):
    r = lax.broadcasted_iota(jnp.int32, (n_rows, width), 0) + row0
    c = lax.broadcasted_iota(jnp.int32, (n_rows, width), 1)
    return r >= c


def _causal_sweep(i, process):
    def pair_body(j2, carry):
        process(2 * j2, 2, False)
        return carry

    lax.fori_loop(0, i // 2, pair_body, 0)

    @pl.when(i % 2 == 1)
    def _():
        process(i - 1, 2, True)

    @pl.when(i % 2 == 0)
    def _():
        process(i, 1, True)


def _ride_along(ride, refs, n_in, n_out, n_scratch, first, last):
    n = len(ride)
    ins, srcs = refs[:n_in], refs[n_in:n_in + n]
    outs, dsts = refs[n_in + n:n_in + n + n_out], refs[n_in + n + n_out:n_in + 2 * n + n_out]
    scratch, sems = refs[n_in + 2 * n + n_out:n_in + 2 * n + n_out + n_scratch], refs[n_in + 2 * n + n_out + n_scratch:]
    copies = _ride_copies(ride, srcs, dsts, sems) if n else []

    @pl.when(first)
    def _():
        for cp in copies:
            cp.start()

    def finish():
        @pl.when(last)
        def _():
            for cp in copies:
                cp.wait()

    return ins, outs, scratch, finish


def flash_fwd(q, k, v_ones, c_col, c_row, name, ride=()):
    H, S, Dh = q.shape
    T = _attn_tile(S)
    NT = S // T
    TS = T // ATTN_SUB
    ride_in, ride_out, ride_shape, ride_sems = _ride_decl(ride)

    def body(*refs):
        h, i = pl.program_id(0), pl.program_id(1)
        (q_ref, k_ref, v_ref, cq_ref, ck_ref), (o_ref, lse_ref), (m_ref, acc_ref), finish = _ride_along(
            ride, refs, 5, 2, 2, (h == 0) & (i == 0), (h == H - 1) & (i == NT - 1))
        qs = q_ref[0] * ATTN_SCALE
        c_ref = cq_ref[0, 0:1, :]
        m_ref[...] = jnp.full_like(m_ref, NEG_INF)
        acc_ref[...] = jnp.zeros_like(acc_ref)
        halves = [slice(u * TS, (u + 1) * TS) for u in range(ATTN_SUB)]

        def keys(j, n_tiles):
            return pl.ds(pl.multiple_of(j * T, T), n_tiles * T)

        def softmax_pv(j, n_tiles, masked, s_of):
            width = n_tiles * T
            vj = v_ref[0, keys(j, n_tiles), :]
            ckj = _row_tiles(ck_ref, j, n_tiles) - c_ref
            for u, rows in enumerate(halves):
                m_prev = m_ref[rows]
                ps, m_news = [], []
                for r0 in range(0, TS, ATTN_CHUNK):
                    rc = slice(r0, r0 + ATTN_CHUNK)
                    s = s_of(slice(u * TS + r0, u * TS + r0 + ATTN_CHUNK)) - ckj
                    if masked:
                        s = jnp.where(_diag_mask(ATTN_CHUNK, width, u * TS + r0 + width - T), s, NEG_INF)
                    m_new = jnp.maximum(m_prev[rc], jnp.max(s, axis=1, keepdims=True))
                    ps.append(jnp.exp(s - jnp.tile(m_new, (1, width // LANES))).astype(BF16))
                    m_news.append(m_new)
                m_new = jnp.concatenate(m_news, axis=0)
                acc_ref[rows] = jnp.exp(m_prev - m_new) * acc_ref[rows] + _dot(jnp.concatenate(ps, axis=0), vj)
                m_ref[rows] = m_new

        def process(j, n_tiles, masked):
            kj = k_ref[0, keys(j, n_tiles), :]
            s = jnp.concatenate([_dot_nt(qs[rows], kj) for rows in halves], axis=0)
            softmax_pv(j, n_tiles, masked, lambda rc: s[rc])

        _causal_sweep(i, process)
        acc = acc_ref[...]
        l = acc[:, Dh:Dh + 1]
        o_ref[0] = (acc[:, 0:Dh] / l).astype(o_ref.dtype)
        lse_ref[0] = m_ref[:, 0:1] + jnp.log(l) + (cq_ref[0] - c_ref)
        finish()

    tile = pl.BlockSpec((1, T, Dh), lambda h, i: (h, i, 0))
    whole = pl.BlockSpec((1, S, Dh), lambda h, i: (h, 0, 0))
    whole_v = pl.BlockSpec((1, S, 2 * Dh), lambda h, i: (h, 0, 0))
    col = pl.BlockSpec((1, T, 1), lambda h, i: (h, i, 0))
    rows = pl.BlockSpec((1, NT, 1, T), lambda h, i: (h, 0, 0, 0))
    return pl.pallas_call(
        body, name=name, grid=(H, NT),
        in_specs=[tile, whole, whole_v, col, rows] + ride_in,
        out_specs=[tile, col] + ride_out,
        out_shape=[jax.ShapeDtypeStruct((H, S, Dh), BF16), jax.ShapeDtypeStruct((H, S, 1), F32)] + ride_shape,
        scratch_shapes=[pltpu.VMEM((T, LANES), F32), pltpu.VMEM((T, 2 * Dh), F32)] + ride_sems,
        compiler_params=_cparams("arbitrary", "arbitrary"),
    )(q, k, v_ones, c_col, c_row, *[a for a, _ in ride])


def flash_bwd_q(q, k, v, o, do, q_t, do_t, lse, c_col, c_row, name, ride=()):
    H, S, Dh = q.shape
    T = _attn_tile(S)
    NT = S // T
    TS = T // ATTN_SUB

    ride_in, ride_out, ride_shape, ride_sems = _ride_decl(ride)

    def body(*refs):
        head, i = pl.program_id(0), pl.program_id(1)
        ((q_ref, k_ref, v_ref, o_ref, do_ref, qt_ref, dot_ref, lse_ref, cq_ref, ck_ref),
         (dq_ref, dcq_ref, dkt_ref, dvt_ref, dck_ref), (dq_acc, dcq_acc), finish) = _ride_along(
            ride, refs, 10, 5, 2, (head == 0) & (i == 0), (head == H - 1) & (i == NT - 1))

        @pl.when(i == 0)
        def _():
            dkt_ref[...] = jnp.zeros_like(dkt_ref)
            dvt_ref[...] = jnp.zeros_like(dvt_ref)
            dck_ref[...] = jnp.zeros_like(dck_ref)

        qs = q_ref[0] * ATTN_SCALE
        do = do_ref[0]
        qs_t = qt_ref[0] * ATTN_SCALE
        do_t = dot_ref[0]
        delta = jnp.sum(do.astype(F32) * o_ref[0].astype(F32), axis=1, keepdims=True)
        bias = cq_ref[0] - lse_ref[0]
        dq_acc[...] = jnp.zeros_like(dq_acc)
        dcq_acc[...] = jnp.zeros_like(dcq_acc)

        def process(j, n_tiles, masked):
            width = n_tiles * T
            off = pl.multiple_of(j * T, T)
            kj = k_ref[0, pl.ds(off, width), :]
            vj = v_ref[0, pl.ds(off, width), :]
            ck = _row_tiles(ck_ref, j, n_tiles)
            halves = [slice(u * TS, (u + 1) * TS) for u in range(ATTN_SUB)]
            ss = [_dot_nt(qs[h], kj) for h in halves]
            dps = [_dot_nt(do[h], vj) for h in halves]
            dkt, dvt = [], []
            dck8 = jnp.zeros((SUBLANES, width), F32)
            for u, h in enumerate(halves):
                ps, dss, rowsums = [], [], []
                for r0 in range(0, TS, ATTN_CHUNK):
                    rc = slice(r0, r0 + ATTN_CHUNK)
                    p = jnp.exp(ss[u][rc] + (bias[h][rc] - ck))
                    if masked:
                        p = jnp.where(_diag_mask(ATTN_CHUNK, width, u * TS + r0 + width - T), p, 0.0)
                    ds = p * (dps[u][rc] - delta[h][rc])
                    ps.append(p.astype(BF16))
                    dss.append(ds.astype(BF16))
                    rowsums.append(jnp.sum(ds, axis=1, keepdims=True))
                    for g in range(0, ATTN_CHUNK, SUBLANES):
                        dck8 = dck8 + ds[g:g + SUBLANES]
                p, ds = jnp.concatenate(ps, axis=0), jnp.concatenate(dss, axis=0)
                dq_acc[h] += _dot(ds, kj)
                dcq_acc[h] += jnp.concatenate(rowsums, axis=0)
                dvt.append(_dot(do_t[:, h], p))
                dkt.append(_dot(qs_t[:, h], ds))
            dvt, dkt, dck = sum(dvt), sum(dkt), jnp.sum(dck8, axis=0, keepdims=True)
            for t in range(n_tiles):
                cols = slice(t * T, (t + 1) * T)
                dvt_ref[0, j + t] += dvt[:, cols]
                dkt_ref[0, j + t] += dkt[:, cols]
                dck_ref[0, j + t] -= dck[:, cols]

        _causal_sweep(i, process)
        dq_ref[0] = (dq_acc[...] * ATTN_SCALE).astype(dq_ref.dtype)
        dcq_ref[0] = dcq_acc[...]
        finish()

    tile = pl.BlockSpec((1, T, Dh), lambda h, i: (h, i, 0))
    tile_t = pl.BlockSpec((1, Dh, T), lambda h, i: (h, 0, i))
    whole = pl.BlockSpec((1, S, Dh), lambda h, i: (h, 0, 0))
    col = pl.BlockSpec((1, T, 1), lambda h, i: (h, i, 0))
    rows = pl.BlockSpec((1, NT, 1, T), lambda h, i: (h, 0, 0, 0))
    acc_t = pl.BlockSpec((1, NT, Dh, T), lambda h, i: (h, 0, 0, 0))
    return pl.pallas_call(
        body, name=name, grid=(H, NT),
        in_specs=[tile, whole, whole, tile, tile, tile_t, tile_t, col, col, rows] + ride_in,
        out_specs=[tile, col, acc_t, acc_t, rows] + ride_out,
        out_shape=[jax.ShapeDtypeStruct((H, S, Dh), BF16), jax.ShapeDtypeStruct((H, S, 1), F32),
                   jax.ShapeDtypeStruct((H, NT, Dh, T), F32), jax.ShapeDtypeStruct((H, NT, Dh, T), F32),
                   jax.ShapeDtypeStruct((H, NT, 1, T), F32)] + ride_shape,
        scratch_shapes=[pltpu.VMEM((T, Dh), F32), pltpu.VMEM((T, 1), F32)] + ride_sems,
        compiler_params=_cparams("arbitrary", "arbitrary"),
    )(q, k, v, o, do, q_t, do_t, lse, c_col, c_row, *[a for a, _ in ride])


PAIR = 2 * HEAD_DIM


def _tri(n, lower):
    r = lax.broadcasted_iota(jnp.int32, (n, n), 0)
    c = lax.broadcasted_iota(jnp.int32, (n, n), 1)
    return (r >= c) if lower else (r <= c)


def _ssd_chunk_scalars(dt_raw, bias, a_log):
    Q = dt_raw.shape[0]
    dt = _softplus(dt_raw + bias)
    A = -jnp.exp(a_log)
    cum = jnp.dot(_tri(Q, True).astype(F32), dt * A, precision=HIGHEST, preferred_element_type=F32)
    tot = cum[Q - 1:Q, :]
    return dt, A, cum, tot


def _lane_pair(lo_mask, v, h0):
    return jnp.where(lo_mask, v[:, h0:h0 + 1], v[:, h0 + 1:h0 + 2])


def _head_expand(d_inner):
    return (jnp.arange(d_inner)[None, :] // HEAD_DIM == jnp.arange(LANES)[:, None]).astype(BF16)


def _split_dot(x, e, dims=None):
    hi = x.astype(BF16)
    lo = (x - hi.astype(F32)).astype(BF16)
    if dims is None:
        return jnp.dot(hi, e, preferred_element_type=F32) + jnp.dot(lo, e, preferred_element_type=F32)
    return (lax.dot_general(hi, e, dims, preferred_element_type=F32)
            + lax.dot_general(lo, e, dims, preferred_element_type=F32))


def ssd_scan_fwd(xbc, proj, dt_col, dt_bias, a_log, d_inner, name):
    S, W = xbc.shape
    Q, N, G = SSM_CHUNK, SSM_STATE, SSM_GROUPS
    nc = S // Q
    n_pairs = d_inner // PAIR
    pairs_per_group = n_pairs // G
    GN = G * N

    def body(xbc_ref, dt_ref, bias_ref, alog_ref, y_ref, sin_ref, st_ref):
        c = pl.program_id(0)

        @pl.when(c == 0)
        def _():
            st_ref[...] = jnp.zeros_like(st_ref)

        sin_ref[0] = st_ref[...]
        dt, A, cum, tot = _ssd_chunk_scalars(dt_ref[...], bias_ref[...], alog_ref[...])
        cum_t = cum.T
        dt_t = dt.T
        ecum = jnp.exp(cum)
        wend = jnp.exp(tot - cum) * dt
        etot = jnp.exp(tot)
        lower = _tri(Q, True)
        lo = lax.broadcasted_iota(jnp.int32, (Q, PAIR), 1) < HEAD_DIM
        lo_row = lax.broadcasted_iota(jnp.int32, (1, PAIR), 1) < HEAD_DIM
        for g in range(G):
            Bg = xbc_ref[:, d_inner + g * N:d_inner + (g + 1) * N]
            Cg = xbc_ref[:, d_inner + GN + g * N:d_inner + GN + (g + 1) * N]
            CB = _dot_nt(Cg, Bg)
            for pp in range(pairs_per_group):
                pr = g * pairs_per_group + pp
                h0 = 2 * pr
                xw = xbc_ref[:, pr * PAIR:(pr + 1) * PAIR]
                ys = []
                for h in (h0, h0 + 1):
                    L = jnp.where(lower, jnp.exp(cum[:, h:h + 1] - cum_t[h:h + 1, :]), 0.0)
                    ys.append(_dot(CB * L * dt_t[h:h + 1, :], xw))
                st = st_ref[pr]
                y_inter = _dot(Cg, st) * _lane_pair(lo, ecum, h0)
                y_ref[:, pr * PAIR:(pr + 1) * PAIR] = jnp.where(lo, ys[0], ys[1]) + y_inter
                st_ref[pr] = _lane_pair(lo_row, etot, h0) * st + _dot_tn(Bg, xw * _lane_pair(lo, wend, h0))

    return pl.pallas_call(
        body, name=name, grid=(nc,),
        in_specs=[pl.BlockSpec((Q, W), lambda c: (c, 0)), pl.BlockSpec((Q, LANES), lambda c: (c, dt_col // LANES)),
                  pl.BlockSpec((1, LANES), lambda c: (0, 0)), pl.BlockSpec((1, LANES), lambda c: (0, 0))],
        out_specs=[pl.BlockSpec((Q, d_inner), lambda c: (c, 0)), pl.BlockSpec((1, n_pairs, N, PAIR), lambda c: (c, 0, 0, 0))],
        out_shape=[jax.ShapeDtypeStruct((S, d_inner), F32), jax.ShapeDtypeStruct((nc, n_pairs, N, PAIR), F32)],
        scratch_shapes=[pltpu.VMEM((n_pairs, N, PAIR), F32)],
        compiler_params=_cparams("arbitrary"),
    )(xbc, proj, dt_bias, a_log)


def ssd_scan_bwd(dy, dskip, xbc, proj, dt_col, dt_bias, a_log, states, expand, d_inner, name):
    S, W = xbc.shape
    Q, N, G = SSM_CHUNK, SSM_STATE, SSM_GROUPS
    nc = S // Q
    n_pairs = d_inner // PAIR
    pairs_per_group = n_pairs // G
    GN = G * N

    def body(dy_ref, dskip_ref, xbc_ref, dt_ref, bias_ref, alog_ref, sin_ref, e_ref,
             dxbc_ref, ddt_ref, dalog_ref, dbias_ref, dst_ref, rows_ref):
        step = pl.program_id(0)

        @pl.when(step == 0)
        def _():
            dst_ref[...] = jnp.zeros_like(dst_ref)
            dalog_ref[...] = jnp.zeros_like(dalog_ref)
            dbias_ref[...] = jnp.zeros_like(dbias_ref)

        dt_raw = dt_ref[...]
        bias = bias_ref[...]
        dt, A, cum, tot = _ssd_chunk_scalars(dt_raw, bias, alog_ref[...])
        cum_t = cum.T
        ecum = jnp.exp(cum)
        eend = jnp.exp(tot - cum)
        wend = eend * dt
        etot = jnp.exp(tot)
        e = e_ref[...]
        wide = _split_dot(jnp.concatenate([wend, dt, jnp.broadcast_to(etot, (SUBLANES, LANES))], axis=0), e)
        wend_w, dt_w, etot_w = wide[0:Q], wide[Q:2 * Q], wide[2 * Q:2 * Q + 1]
        ecum_w = _dot(ecum, e)
        upper = _tri(Q, False)
        lane = lax.broadcasted_iota(jnp.int32, (Q, LANES), 1)
        lo = lane < HEAD_DIM
        ones_q = jnp.ones((Q, LANES), BF16)
        ones_8 = jnp.ones((SUBLANES, Q), BF16)
        rows_ref[...] = jnp.zeros_like(rows_ref)
        dcum_src = jnp.zeros((Q, LANES), F32)
        xz_parts, xbds_parts, dss_parts, dyy2_parts = [], [], [], []
        for g in range(G):
            Bg = xbc_ref[:, d_inner + g * N:d_inner + (g + 1) * N]
            Cg = xbc_ref[:, d_inner + GN + g * N:d_inner + GN + (g + 1) * N]
            Cg_t = Cg.T
            CBt = _dot_nt(Bg, Cg)
            dCBt = jnp.zeros((Q, Q), F32)
            dBg = jnp.zeros((Q, N), F32)
            dCg = jnp.zeros((Q, N), F32)
            for pp in range(pairs_per_group):
                pr = g * pairs_per_group + pp
                h0 = 2 * pr
                sl = slice(pr * PAIR, (pr + 1) * PAIR)
                xw = xbc_ref[:, sl]
                dyp = dy_ref[:, sl]
                st = sin_ref[0, pr]
                dst = dst_ref[pr]
                wend_p = wend_w[:, sl]
                dt_p = dt_w[:, sl]
                dye = dyp * ecum_w[:, sl]
                dyy2_parts.append(dye * _dot(Cg, st))
                dCg = dCg + _dot_nt(dye, st)
                bds = _dot(Bg, dst)
                dBg = dBg + _dot_nt(xw * wend_p, dst)
                xbds_parts.append(xw * bds)
                dss_parts.append(dst * st)
                xdt = xw * dt_p
                z = None
                for h, half in ((h0, lo), (h0 + 1, ~lo)):
                    Lt = jnp.where(upper, jnp.exp(cum_t[h:h + 1, :] - cum[:, h:h + 1]), 0.0)
                    dyh = jnp.where(half, dyp, 0.0)
                    CBLt = CBt * Lt
                    zh = _dot(CBLt, dyh)
                    z = zh if z is None else z + zh
                    dMt = _dot_nt(xdt, dyh)
                    dCBt = dCBt + dMt * Lt
                    gm = (dMt * CBLt).astype(BF16)
                    dcum_src = dcum_src + jnp.where(lane == h, jnp.dot(gm, ones_q, preferred_element_type=F32), 0.0)
                    rows_ref[h:h + 1, :] = jnp.dot(ones_8, gm, preferred_element_type=F32)[0:1]
                xz_parts.append(xw * z)
                dxbc_ref[:, sl] = dt_p * z + wend_p * bds + dskip_ref[:, sl]
                dst_ref[pr] = _dot(Cg_t, dye) + etot_w[:, sl] * dst
            dxbc_ref[:, d_inner + g * N:d_inner + (g + 1) * N] = dBg + _dot(dCBt, Cg)
            dxbc_ref[:, d_inner + GN + g * N:d_inner + GN + (g + 1) * N] = dCg + _dot_tn(dCBt, Bg)
        head_sums = lambda wide_arr: _split_dot(wide_arr, e, NT_DIMS)
        dy_y2 = head_sums(jnp.concatenate(dyy2_parts, axis=1))
        x_z = head_sums(jnp.concatenate(xz_parts, axis=1))
        q = head_sums(jnp.concatenate(xbds_parts, axis=1))
        dst_st = jnp.sum(head_sums(jnp.concatenate(dss_parts, axis=1)), axis=0, keepdims=True)
        wq = wend * q
        last_row = lax.broadcasted_iota(jnp.int32, (Q, LANES), 0) == Q - 1
        dcum = (dy_y2 - wq + (rows_ref[...].T - dcum_src)
                + jnp.where(last_row, etot * dst_st + jnp.sum(wq, axis=0, keepdims=True), 0.0))
        ddt = eend * q + x_z
        da = jnp.dot(upper.astype(F32), dcum, precision=HIGHEST, preferred_element_type=F32)
        ddt = ddt + da * A
        ddt_raw = ddt * _sigmoid(dt_raw + bias)
        ddt_ref[...] = ddt_raw
        dalog_ref[...] += jnp.sum(da * dt, axis=0, keepdims=True) * A
        dbias_ref[...] += jnp.sum(ddt_raw, axis=0, keepdims=True)

    rev = lambda width, col: pl.BlockSpec((Q, width), lambda s: (nc - 1 - s, col))
    vec = pl.BlockSpec((1, LANES), lambda s: (0, 0))
    return pl.pallas_call(
        body, name=name, grid=(nc,),
        in_specs=[rev(d_inner, 0), rev(d_inner, 0), rev(W, 0), rev(LANES, dt_col // LANES), vec, vec,
                  pl.BlockSpec((1, n_pairs, N, PAIR), lambda s: (nc - 1 - s, 0, 0, 0)),
                  pl.BlockSpec((LANES, d_inner), lambda s: (0, 0))],
        out_specs=[rev(W, 0), rev(LANES, 0), vec, vec],
        out_shape=[jax.ShapeDtypeStruct((S, W), F32), jax.ShapeDtypeStruct((S, LANES), F32),
                   jax.ShapeDtypeStruct((1, LANES), F32), jax.ShapeDtypeStruct((1, LANES), F32)],
        scratch_shapes=[pltpu.VMEM((n_pairs, N, PAIR), F32), pltpu.VMEM((LANES, Q), F32)],
        compiler_params=_cparams("arbitrary"),
    )(dy, dskip, xbc, proj, dt_bias, a_log, states, expand)


def ssd_gate_fwd(y, xbc, proj, d_skip, norm_w, d_inner, name):
    S = y.shape[0]
    tm = _pick(S, (256, 128))
    gs = d_inner // SSM_GROUPS

    def body(y_ref, x_ref, z_ref, d_ref, w_ref, o_ref):
        for g in range(SSM_GROUPS):
            sl = slice(g * gs, (g + 1) * gs)
            y2 = (y_ref[:, sl] + d_ref[:, sl] * x_ref[:, sl]) * _silu(z_ref[:, sl].astype(F32))
            r = lax.rsqrt(jnp.mean(y2 * y2, axis=-1, keepdims=True) + RMS_EPS)
            o_ref[:, sl] = (y2 * r * w_ref[:, sl]).astype(o_ref.dtype)

    row = pl.BlockSpec((tm, d_inner), lambda i: (i, 0))
    vec = pl.BlockSpec((1, d_inner), lambda i: (0, 0))
    return pl.pallas_call(
        body, name=name, grid=(S // tm,), in_specs=[row, row, row, vec, vec], out_specs=row,
        out_shape=jax.ShapeDtypeStruct((S, d_inner), BF16), compiler_params=_cparams("parallel"),
    )(y, xbc, proj, d_skip, norm_w)


def ssd_gate_bwd(dyn, y, xbc, proj, d_skip, norm_w, d_inner, name):
    S = y.shape[0]
    tm = _pick(S, (256, 128))
    gs = d_inner // SSM_GROUPS

    def body(dyn_ref, y_ref, x_ref, z_ref, d_ref, w_ref, dy_ref, dskip_ref, dz_ref, dw_ref, dd_ref):
        i = pl.program_id(0)

        @pl.when(i == 0)
        def _():
            dw_ref[...] = jnp.zeros_like(dw_ref)
            dd_ref[...] = jnp.zeros_like(dd_ref)

        for g in range(SSM_GROUPS):
            sl = slice(g * gs, (g + 1) * gs)
            z = z_ref[:, sl].astype(F32)
            x = x_ref[:, sl]
            sz = _silu(z)
            ysum = y_ref[:, sl] + d_ref[:, sl] * x
            y2 = ysum * sz
            r = lax.rsqrt(jnp.mean(y2 * y2, axis=-1, keepdims=True) + RMS_EPS)
            dyn = dyn_ref[:, sl].astype(F32)
            a = dyn * w_ref[:, sl]
            dy2 = r * a - y2 * (r * r * r) * jnp.mean(a * y2, axis=-1, keepdims=True)
            dysum = dy2 * sz
            dy_ref[:, sl] = dysum
            dskip_ref[:, sl] = dysum * d_ref[:, sl]
            dz_ref[:, sl] = (dy2 * ysum * _silu_grad(z)).astype(dz_ref.dtype)
            dw_ref[:, sl] += jnp.sum(dyn * y2 * r, axis=0, keepdims=True)
            dd_ref[:, sl] += jnp.sum(dysum * x, axis=0, keepdims=True)

    row = pl.BlockSpec((tm, d_inner), lambda i: (i, 0))
    vec = pl.BlockSpec((1, d_inner), lambda i: (0, 0))
    return pl.pallas_call(
        body, name=name, grid=(S // tm,), in_specs=[row, row, row, row, vec, vec], out_specs=[row, row, row, vec, vec],
        out_shape=[jax.ShapeDtypeStruct((S, d_inner), F32), jax.ShapeDtypeStruct((S, d_inner), F32),
                   jax.ShapeDtypeStruct((S, d_inner), BF16), jax.ShapeDtypeStruct((1, d_inner), F32),
                   jax.ShapeDtypeStruct((1, d_inner), F32)],
        compiler_params=_cparams("arbitrary"),
    )(dyn, y, xbc, proj, d_skip, norm_w)


def _pad_to(a, axis, mult=LANES):
    n = a.shape[axis]
    extra = (-n) % mult
    if extra == 0:
        return a
    widths = [(0, 0)] * a.ndim
    widths[axis] = (0, extra)
    return jnp.pad(a, widths)


def _attn_fwd(x, w_in, b_f, w_out, tag, ride=()):
    S, D = x.shape
    H = D // HEAD_DIM
    T = _attn_tile(S)
    qkv = mm_nn(x, w_in[:, :3 * D], BF16, f"{tag}_proj").reshape(S, 3, H, HEAD_DIM).transpose(1, 2, 0, 3)
    zt = mm_nn(x, _pad_to(w_in[:, 3 * D:], 1), F32, f"{tag}_proj_f")[:, :H].T
    bf = b_f.reshape(H, 1)
    c = fox_gate_fwd(zt, bf, f"{tag}_gate")
    c_col, c_row = c.reshape(H, S, 1), c.reshape(H, S // T, 1, T)
    v_ones = jnp.concatenate([qkv[2], jnp.ones_like(qkv[2])], axis=-1)
    o, lse, *ride_out = flash_fwd(qkv[0], qkv[1], v_ones, c_col, c_row, f"{tag}_flash", ride)
    o_flat = o.transpose(1, 0, 2).reshape(S, D)
    mix = mm_nn(o_flat, w_out, F32, f"{tag}_out")
    return mix, (qkv, zt, bf, c_col, c_row, o, lse, o_flat), ride_out


def _attn_bwd(x, dmix, dx_add, alpha, w_in, w_out, saved, tag, ride=()):
    S, D = x.shape
    H = D // HEAD_DIM
    T = _attn_tile(S)
    qkv, zt, bf, c_col, c_row, o, lse, o_flat = saved
    g_w_out = mm_tn(o_flat, dmix, f"{tag}_gwout")
    do = mm_nn(dmix, w_out.T, BF16, f"{tag}_do").reshape(S, H, HEAD_DIM)
    dq, dc_q, dk_t, dv_t, dc_k, *ride_out = flash_bwd_q(
        qkv[0], qkv[1], qkv[2], o, do.transpose(1, 0, 2), qkv[0].transpose(0, 2, 1), do.transpose(1, 2, 0), lse, c_col, c_row,
        f"{tag}_flash_bwd", ride)
    dzt, dbf = fox_gate_bwd(dc_q.reshape(H, S), dc_k.reshape(H, S), zt, bf, f"{tag}_gate_bwd")
    keys_major = lambda t: t.transpose(1, 3, 0, 2).reshape(S, D).astype(BF16)
    dqkv = jnp.concatenate([dq.transpose(1, 0, 2).reshape(S, D), keys_major(dk_t), keys_major(dv_t)], axis=1)
    dzf = _pad_to(dzt.T, 1)
    g_w_in = jnp.concatenate([mm_tn(x, dqkv, f"{tag}_gwqkv"), mm_tn(x, dzf, f"{tag}_gwf")[:, :H]], axis=1)
    w_in_t = w_in.T
    dx = mm_nn(dqkv, w_in_t[:3 * D], F32, f"{tag}_dx_qkv", add=dx_add, add_scale=alpha)
    dx = mm_nn(dzf, _pad_to(w_in_t[3 * D:], 0), F32, f"{tag}_dx_f", add=dx)
    return dx, (g_w_in, dbf.reshape(H), g_w_out), ride_out


def _ssm_dims(D):
    d_inner = 2 * D
    gn = SSM_GROUPS * SSM_STATE
    return d_inner, d_inner + 2 * gn, d_inner // HEAD_DIM


def _ssm_fwd(x, w_in, conv_w, conv_b, dt_bias, a_log, d_skip, norm_w, w_out, tag):
    S, D = x.shape
    DI, XBC, HS = _ssm_dims(D)
    dt_col = DI + XBC
    proj = mm_nn(x, w_in[:, :dt_col], BF16, f"{tag}_proj")
    dt_raw = mm_nn(x, _pad_to(w_in[:, dt_col:], 1), F32, f"{tag}_proj_dt")
    conv_b = conv_b.reshape(1, XBC)
    xbc = conv_act_fwd(proj, DI, XBC, conv_w, conv_b, F32, f"{tag}_conv")
    dt_bias_p = _pad_to(dt_bias.reshape(1, HS), 1)
    a_log_p = _pad_to(a_log.reshape(1, HS), 1)
    y, states = ssd_scan_fwd(xbc, dt_raw, 0, dt_bias_p, a_log_p, DI, f"{tag}_scan")
    d_vec = jnp.repeat(d_skip, HEAD_DIM).reshape(1, DI)
    norm_w = norm_w.reshape(1, DI)
    yn = ssd_gate_fwd(y, xbc, proj, d_vec, norm_w, DI, f"{tag}_gate")
    mix = mm_nn(yn, w_out, F32, f"{tag}_out")
    return mix, (proj, dt_raw, xbc, conv_b, dt_bias_p, a_log_p, y, states, d_vec, norm_w, yn)


def _ssm_bwd(x, dmix, dx_add, alpha, w_in, conv_w, w_out, saved, tag):
    S, D = x.shape
    DI, XBC, HS = _ssm_dims(D)
    dt_col = DI + XBC
    proj, dt_raw, xbc, conv_b, dt_bias_p, a_log_p, y, states, d_vec, norm_w, yn = saved
    g_w_out = mm_tn(yn, dmix, f"{tag}_gwout")
    dyn = mm_nn(dmix, w_out.T, BF16, f"{tag}_dyn")
    dy, dskip, dz, g_norm_w, g_dvec = ssd_gate_bwd(dyn, y, xbc, proj, d_vec, norm_w, DI, f"{tag}_gate_bwd")
    dxbc, ddt_raw, g_a_log, g_dt_bias = ssd_scan_bwd(dy, dskip, xbc, dt_raw, 0, dt_bias_p, a_log_p, states,
                                                     _head_expand(DI), DI, f"{tag}_scan_bwd")
    dxbc_raw, _, g_conv_w, g_conv_b = conv_act_bwd(dxbc, proj, DI, XBC, conv_w, conv_b, f"{tag}_conv_bwd")
    g_w_in = jnp.concatenate([mm_tn(x, dz, f"{tag}_gwz"), mm_tn(x, dxbc_raw, f"{tag}_gwxbc"),
                              mm_tn(x, ddt_raw, f"{tag}_gwdt")[:, :HS]], axis=1)
    w_in_t = w_in.T
    dx = mm_nn(dz, w_in_t[:DI], F32, f"{tag}_dx_z", add=dx_add, add_scale=alpha)
    dx = mm_nn(dxbc_raw, w_in_t[DI:dt_col], F32, f"{tag}_dx_xbc", add=dx)
    dx = mm_nn(ddt_raw, _pad_to(w_in_t[dt_col:], 0), F32, f"{tag}_dx_dt", add=dx)
    g_d = g_dvec.reshape(HS, HEAD_DIM).sum(axis=-1)
    grads = (g_w_in, g_conv_w, g_conv_b.reshape(XBC), g_dt_bias[0, :HS], g_a_log[0, :HS], g_d, g_norm_w.reshape(DI), g_w_out)
    return dx, grads


ATTN_KEYS = ("attn_w_in", "attn_b_f", "attn_w_out")
SSM_KEYS = ("ssm_w_in", "ssm_conv_w", "ssm_conv_b", "ssm_dt_bias", "ssm_A_log", "ssm_D", "ssm_norm_w", "ssm_w_out")
LAYER_KEYS = ("ln_mix_g", "ln_mix_b", "ffn_w_up", "ffn_conv_w", "ffn_conv_b", "ffn_w_down", "ln_ffn_g", "ln_ffn_b",
              "ple_w_proj", "ple_w_gate", "ple_b_gate")


def local_step(x, p, target, w, fwd_ride=(), after_fwd_ride=None, before_last_bwd=None):
    S, D = x.shape
    depth = p.shape[0]
    alpha = (2 * depth) ** 0.25
    saved = []
    bwd_ride_out = []
    for i in range(depth):
        j, tag = i // 2, f"l{i}"
        if i % 2 == 0:
            mix, msaved, ride_out = _attn_fwd(x, w["attn_w_in"][j], w["attn_b_f"][j], w["attn_w_out"][j], tag + "_attn",
                                              fwd_ride if i == 0 else ())
            if i == 0 and after_fwd_ride is not None:
                w = {**w, **after_fwd_ride(ride_out)}
        else:
            mix, msaved = _ssm_fwd(x, w["ssm_w_in"][j], w["ssm_conv_w"][j], w["ssm_conv_b"][j], w["ssm_dt_bias"][j],
                                   w["ssm_A_log"][j], w["ssm_D"][j], w["ssm_norm_w"][j], w["ssm_w_out"][j], tag + "_ssm")
        row = lambda k: w[k][i].reshape(1, -1)
        F = w["ffn_w_down"].shape[1]
        x1, xhat1, rstd1 = ln_fwd(x, mix, row("ln_mix_g"), row("ln_mix_b"), alpha, tag + "_ln_mix")
        h = mm_nn(x1, w["ffn_w_up"][i], BF16, tag + "_ffn_up")
        a = conv_act_fwd(h, F, F, w["ffn_conv_w"][i], row("ffn_conv_b"), BF16, tag + "_ffn_act", gate_col=0)
        ffn = mm_nn(a, w["ffn_w_down"][i], F32, tag + "_ffn_down")
        x2, xhat2, rstd2 = ln_fwd(x1, ffn, row("ln_ffn_g"), row("ln_ffn_b"), alpha, tag + "_ln_ffn")
        zg = mm_nn(x2, w["ple_w_gate"][i], F32, tag + "_ple_gate")
        pp = mm_nn(p[i], w["ple_w_proj"][i], F32, tag + "_ple_proj")
        x3 = ple_fwd(x2, zg, pp, row("ple_b_gate"), tag + "_ple")
        saved.append((x, msaved, x1, xhat1, rstd1, h, a, x2, xhat2, rstd2, zg, pp))
        x = x3

    loss_vec, d = loss_head(x, target, "loss_head")

    grads = {k: [None] * w[k].shape[0] for k in ATTN_KEYS + SSM_KEYS + LAYER_KEYS}
    for i in reversed(range(depth)):
        j, tag = i // 2, f"l{i}"
        x0, msaved, x1, xhat1, rstd1, h, a, x2, xhat2, rstd2, zg, pp = saved[i]
        row = lambda k: w[k][i].reshape(1, -1)
        dzg, dpp, g_bg = ple_bwd(d, zg, pp, row("ple_b_gate"), tag + "_ple_bwd")
        grads["ple_w_gate"][i] = mm_tn(x2, dzg, tag + "_gw_ple_gate")
        grads["ple_w_proj"][i] = mm_tn(p[i], dpp, tag + "_gw_ple_proj")
        grads["ple_b_gate"][i] = g_bg.reshape(D)
        dx2 = mm_nn(dzg, w["ple_w_gate"][i].T, F32, tag + "_dx2", add=d)
        dr2, g_g2, g_b2 = ln_bwd(dx2, xhat2, rstd2, row("ln_ffn_g"), tag + "_ln_ffn_bwd")
        grads["ln_ffn_g"][i], grads["ln_ffn_b"][i] = g_g2.reshape(D), g_b2.reshape(D)
        grads["ffn_w_down"][i] = mm_tn(a, dr2, tag + "_gw_down")
        da = mm_nn(dr2, w["ffn_w_down"][i].T, BF16, tag + "_da")
        dgin, du, g_cw, g_cb = conv_act_bwd(da, h, F, F, w["ffn_conv_w"][i], row("ffn_conv_b"), tag + "_ffn_act_bwd", gate_col=0)
        grads["ffn_conv_w"][i], grads["ffn_conv_b"][i] = g_cw, g_cb.reshape(F)
        grads["ffn_w_up"][i] = jnp.concatenate([mm_tn(x1, du, tag + "_gw_up_u"), mm_tn(x1, dgin, tag + "_gw_up_g")], axis=1)
        w_up_t = w["ffn_w_up"][i].T
        dx1 = mm_nn(du, w_up_t[:F], F32, tag + "_dx1_u", add=dr2, add_scale=alpha)
        dx1 = mm_nn(dgin, w_up_t[F:], F32, tag + "_dx1_g", add=dx1)
        dr1, g_g1, g_b1 = ln_bwd(dx1, xhat1, rstd1, row("ln_mix_g"), tag + "_ln_mix_bwd")
        grads["ln_mix_g"][i], grads["ln_mix_b"][i] = g_g1.reshape(D), g_b1.reshape(D)
        if i % 2 == 0:
            ride = before_last_bwd(grads) if (i == 0 and before_last_bwd is not None) else ()
            d, mg, ride_out = _attn_bwd(x0, dr1, dr1, alpha, w["attn_w_in"][j], w["attn_w_out"][j], msaved, tag + "_attn", ride)
            if i == 0:
                bwd_ride_out = ride_out
            for k, g in zip(ATTN_KEYS, mg):
                grads[k][j] = g
        else:
            d, mg = _ssm_bwd(x0, dr1, dr1, alpha, w["ssm_w_in"][j], w["ssm_conv_w"][j], w["ssm_w_out"][j], msaved, tag + "_ssm")
            for k, g in zip(SSM_KEYS, mg):
                grads[k][j] = g
    return loss_vec, d, {k: jnp.stack(v) for k, v in grads.items()}, bwd_ride_out


MESH = pl.DeviceIdType.MESH
PACK_ELEMS = 2 * SUBLANES * LANES
PACK_ROWS = 512


def _exchange_copies(src_ref, out_ref, gather, send_sems, recv_sems, local_sem):
    x, y, c = lax.axis_index("x"), lax.axis_index("y"), lax.axis_index("c")
    me = 4 * x + 2 * y + c

    def block_for(dev):
        return src_ref if gather else src_ref.at[dev]

    copies = [pltpu.make_async_copy(block_for(me), out_ref.at[me], local_sem)]
    for k in range(1, N_DEV):
        px = 1 - x if k & 4 else x
        py = 1 - y if k & 2 else y
        pc = 1 - c if k & 1 else c
        copies.append(pltpu.make_async_remote_copy(
            src_ref=block_for(4 * px + 2 * py + pc), dst_ref=out_ref.at[me],
            send_sem=send_sems.at[k - 1], recv_sem=recv_sems.at[k - 1],
            device_id=(px, py, pc), device_id_type=MESH))
    return copies


def _exchange(src, gather, name):
    def body(src_ref, out_ref, send_sems, recv_sems, local_sem):
        copies = _exchange_copies(src_ref, out_ref, gather, send_sems, recv_sems, local_sem)
        for cp in copies:
            cp.start()
        for cp in copies:
            cp.wait()

    return pl.pallas_call(
        body, name=name,
        in_specs=[pl.BlockSpec(memory_space=pl.ANY)],
        out_specs=pl.BlockSpec(memory_space=pl.ANY),
        out_shape=jax.ShapeDtypeStruct((N_DEV,) + tuple(src.shape[-2:]), src.dtype),
        scratch_shapes=[pltpu.SemaphoreType.DMA((N_DEV - 1,)), pltpu.SemaphoreType.DMA((N_DEV - 1,)), pltpu.SemaphoreType.DMA],
    )(src)


def _ride_decl(ride):
    if not ride:
        return [], [], [], []
    hbm = pl.BlockSpec(memory_space=pl.ANY)
    out_shape = [jax.ShapeDtypeStruct((N_DEV,) + tuple(a.shape[-2:]), a.dtype) for a, _ in ride]
    sems = [pltpu.SemaphoreType.DMA((len(ride), N_DEV - 1)), pltpu.SemaphoreType.DMA((len(ride), N_DEV - 1)),
            pltpu.SemaphoreType.DMA((len(ride),))]
    return [hbm] * len(ride), [hbm] * len(ride), out_shape, sems


def _ride_copies(ride, src_refs, out_refs, sems):
    copies = []
    for r, ((_, gather), src_ref, out_ref) in enumerate(zip(ride, src_refs, out_refs)):
        copies += _exchange_copies(src_ref, out_ref, gather, sems[0].at[r], sems[1].at[r], sems[2].at[r])
    return copies


def reduce_adamw(parts, w, m, v, name):
    _, R, _ = parts.shape
    tr = _pick(R, (512, 256, 128, 64, 32, 16))

    def body(p_ref, w_ref, m_ref, v_ref, g_ref, d_ref, nm_ref, nv_ref):
        g = p_ref[0].astype(F32)
        for s in range(1, N_DEV):
            g = g + p_ref[s].astype(F32)
        nm = ADAM_B1 * m_ref[...] + (1.0 - ADAM_B1) * g
        nv = ADAM_B2 * v_ref[...] + (1.0 - ADAM_B2) * (g * g)
        m_hat = nm / (1.0 - ADAM_B1 ** ADAM_STEP)
        v_hat = nv / (1.0 - ADAM_B2 ** ADAM_STEP)
        g_ref[...] = g
        d_ref[...] = -ADAM_LR * (m_hat / (jnp.sqrt(v_hat) + ADAM_EPS) + ADAM_WD * w_ref[...])
        nm_ref[...] = nm
        nv_ref[...] = nv

    row = pl.BlockSpec((tr, LANES), lambda i: (i, 0))
    return pl.pallas_call(
        body, name=name, grid=(R // tr,),
        in_specs=[pl.BlockSpec((N_DEV, tr, LANES), lambda i: (0, i, 0)), row, row, row],
        out_specs=[row, row, row, row],
        out_shape=[jax.ShapeDtypeStruct((R, LANES), F32)] * 4,
        compiler_params=_cparams("parallel"),
    )(parts, w, m, v)


def _pack(arrays, dtype, lead=0):
    parts = []
    for a in arrays:
        head = a.shape[:lead]
        flat = a.astype(dtype).reshape(head + (-1,))
        flat = jnp.pad(flat, [(0, 0)] * lead + [(0, (-flat.shape[-1]) % PACK_ELEMS)])
        parts.append(flat.reshape(head + (-1, LANES)))
    rows = sum(part.shape[lead] for part in parts)
    if (-rows) % PACK_ROWS:
        parts.append(jnp.zeros(parts[0].shape[:lead] + ((-rows) % PACK_ROWS, LANES), dtype))
    return jnp.concatenate(parts, axis=lead)


def _unpack(packed, shapes):
    lead = packed.shape[:-2]
    out, r0 = [], 0
    for shape in shapes:
        n = math.prod(shape)
        rows = -(-n // PACK_ELEMS) * (PACK_ELEMS // LANES)
        seg = packed[..., r0:r0 + rows, :].reshape(lead + (rows * LANES,))[..., :n]
        out.append(seg.reshape(lead + tuple(shape)))
        r0 += rows
    return out


MATMUL_SHARDED = {"attn_w_in": 2, "attn_w_out": 1, "ssm_w_in": 2, "ssm_w_out": 1, "ffn_w_up": 2, "ffn_w_down": 1,
                  "ple_w_proj": 2, "ple_w_gate": 1}
SMALL_SHARDED = {"ssm_conv_w": 2, "ssm_conv_b": 1, "ssm_norm_w": 1, "ffn_conv_w": 2}
REPLICATED = ("attn_b_f", "ssm_dt_bias", "ssm_A_log", "ssm_D", "ln_mix_g", "ln_mix_b", "ffn_conv_b", "ln_ffn_g",
              "ln_ffn_b", "ple_b_gate")
WEIGHT_ORDER = ("attn_w_in", "attn_b_f", "attn_w_out", "ssm_w_in", "ssm_conv_w", "ssm_conv_b", "ssm_dt_bias", "ssm_A_log",
                "ssm_D", "ssm_norm_w", "ssm_w_out", "ln_mix_g", "ln_mix_b", "ffn_w_up", "ffn_conv_w", "ffn_conv_b",
                "ffn_w_down", "ln_ffn_g", "ln_ffn_b", "ple_w_proj", "ple_w_gate", "ple_b_gate")


def _join_shards(gathered, axis):
    moved = jnp.moveaxis(gathered, 0, axis)
    shape = list(moved.shape)
    shape[axis:axis + 2] = [shape[axis] * shape[axis + 1]]
    return moved.reshape(shape)


def _split_shards(full, axis):
    shape = list(full.shape)
    shape[axis:axis + 1] = [N_DEV, shape[axis] // N_DEV]
    return jnp.moveaxis(full.reshape(shape), axis, 0)


def kernel(x, p, attn_w_in, attn_b_f, attn_w_out, ssm_w_in, ssm_conv_w, ssm_conv_b, ssm_dt_bias, ssm_A_log, ssm_D, ssm_norm_w, ssm_w_out, ln_mix_g, ln_mix_b, ffn_w_up, ffn_conv_w, ffn_conv_b, ffn_w_down, ln_ffn_g, ln_ffn_b, ple_w_proj, ple_w_gate, ple_b_gate, loss_target, m_attn_w_in, m_attn_b_f, m_attn_w_out, m_ssm_w_in, m_ssm_conv_w, m_ssm_conv_b, m_ssm_dt_bias, m_ssm_A_log, m_ssm_D, m_ssm_norm_w, m_ssm_w_out, m_ln_mix_g, m_ln_mix_b, m_ffn_w_up, m_ffn_conv_w, m_ffn_conv_b, m_ffn_w_down, m_ln_ffn_g, m_ln_ffn_b, m_ple_w_proj, m_ple_w_gate, m_ple_b_gate, v_attn_w_in, v_attn_b_f, v_attn_w_out, v_ssm_w_in, v_ssm_conv_w, v_ssm_conv_b, v_ssm_dt_bias, v_ssm_A_log, v_ssm_D, v_ssm_norm_w, v_ssm_w_out, v_ln_mix_g, v_ln_mix_b, v_ffn_w_up, v_ffn_conv_w, v_ffn_conv_b, v_ffn_w_down, v_ln_ffn_g, v_ln_ffn_b, v_ple_w_proj, v_ple_w_gate, v_ple_b_gate):
    args = dict(locals())
    w_loc = {k: args[k] for k in WEIGHT_ORDER}
    m_loc = {k: args["m_" + k] for k in WEIGHT_ORDER}
    v_loc = {k: args["v_" + k] for k in WEIGHT_ORDER}
    axis_of = {**MATMUL_SHARDED, **SMALL_SHARDED}
    first_names = ("attn_w_in", "attn_w_out")
    mm_names = tuple(k for k in MATMUL_SHARDED if k not in first_names)
    small_names = tuple(SMALL_SHARDED)
    later_names = mm_names + small_names

    def joined(names, gathered):
        return {k: _join_shards(blocks, axis_of[k])
                for k, blocks in zip(names, _unpack(gathered, [w_loc[k].shape for k in names]))}

    g_first = _exchange(_pack([w_loc[k] for k in first_names], BF16), True, "gather_attn_weights")
    w_first = {**{k: w_loc[k] for k in REPLICATED}, **joined(first_names, g_first)}
    fwd_ride = ((_pack([w_loc[k] for k in mm_names], BF16), True), (_pack([w_loc[k] for k in small_names], F32), True))

    def after_fwd_ride(gathered):
        return {**joined(mm_names, gathered[0]), **joined(small_names, gathered[1])}

    def shard_blocks(grads, names):
        return _pack([_split_shards(grads[k], axis_of[k]) for k in names], BF16, lead=1)

    def before_last_bwd(grads):
        return ((shard_blocks({k: jnp.stack(grads[k]) for k in later_names}, later_names), False),)

    loss_vec, grad_x, g_full, (parts_later,) = local_step(x[0], p[:, 0], loss_target[0], w_first, fwd_ride, after_fwd_ride,
                                                          before_last_bwd)
    parts_first = _exchange(shard_blocks(g_full, first_names), False, "exchange_attn_grads")
    res = {}
    for names, parts, tag in ((later_names, parts_later, "later"), (first_names, parts_first, "attn")):
        shapes = [w_loc[k].shape for k in names]
        pk = lambda d: _pack([d[k] for k in names], F32)
        outs = reduce_adamw(parts, pk(w_loc), pk(m_loc), pk(v_loc), "reduce_adamw_" + tag)
        res.update({k: vals for k, vals in zip(names, zip(*[_unpack(o, shapes) for o in outs]))})

    rep_shapes = [w_loc[k].shape for k in REPLICATED] + [(1, LANES)]
    rparts = _exchange(_pack([g_full[k] for k in REPLICATED] + [loss_vec], F32), True, "gather_replicated_grads")
    zero = jnp.zeros((1, LANES), F32)
    rk = lambda d: _pack([d[k] for k in REPLICATED] + [zero], F32)
    routs = reduce_adamw(rparts, rk(w_loc), rk(m_loc), rk(v_loc), "reduce_adamw_replicated")
    runp = [_unpack(o, rep_shapes) for o in routs]
    for i, k in enumerate(REPLICATED):
        res[k] = tuple(u[i] for u in runp)
    loss = runp[0][-1][0, 0]

    return (loss, grad_x[None], *[res[k][0] for k in WEIGHT_ORDER], *[res[k][1] for k in WEIGHT_ORDER],
            *[res[k][2] for k in WEIGHT_ORDER], *[res[k][3] for k in WEIGHT_ORDER])
```

```python
import functools
import math

import jax
import jax.numpy as jnp
from jax import lax
from jax.experimental import pallas as pl
from jax.experimental.pallas import tpu as pltpu

F32 = jnp.float32
BF16 = jnp.bfloat16

LANES = 128
SUBLANES = 8
VMEM_LIMIT_BYTES = 56 * 1024 * 1024

N_DEV = 8
HEAD_DIM = 64
SSM_GROUPS = 8
SSM_STATE = 128
SSM_CHUNK = 128
SSM_CONV = 4
FFN_CONV = 3
LN_EPS = 1e-5
RMS_EPS = 1e-5
ADAM_LR, ADAM_B1, ADAM_B2, ADAM_EPS, ADAM_WD, ADAM_STEP = 0.001, 0.9, 0.999, 1e-08, 0.01, 10
NEG_INF = float("-inf")
HIGHEST = lax.Precision.HIGHEST
NT_DIMS = (((1,), (1,)), ((), ()))
TN_DIMS = (((0,), (0,)), ((), ()))


def _cparams(*sem):
    return pltpu.CompilerParams(dimension_semantics=sem, vmem_limit_bytes=VMEM_LIMIT_BYTES)


def _pick(n, candidates):
    for c in candidates:
        if n % c == 0:
            return c
    return n


def _dot(a, b):
    return jnp.dot(a.astype(BF16), b.astype(BF16), preferred_element_type=F32)


def _dot_nt(a, b):
    return lax.dot_general(a.astype(BF16), b.astype(BF16), NT_DIMS, preferred_element_type=F32)


def _dot_tn(a, b):
    return lax.dot_general(a.astype(BF16), b.astype(BF16), TN_DIMS, preferred_element_type=F32)


def _sigmoid(x):
    return 1.0 / (1.0 + jnp.exp(-x))


def _log1p_small(u):
    return jnp.where(u < 1e-3, u * (1.0 - u * (0.5 - u * (1.0 / 3.0))), jnp.log(1.0 + u))


def _softplus(x):
    return jnp.maximum(x, 0.0) + _log1p_small(jnp.exp(-jnp.abs(x)))


def mm_nn(a, b, out_dtype, name, add=None, add_scale=1.0):
    M, K = a.shape
    _, N = b.shape
    tm = _pick(M, (1024, 512, 256, 128))
    tn = N if N <= 1024 else _pick(N, (1408, 1024, 896, 768, 640, 512, 384, 256, 128))
    tk = K if K <= 2048 else _pick(K, (1408, 1024, 896, 768, 640, 512, 384, 256, 128))
    nk = K // tk

    def body(*refs):
        if add is None:
            a_ref, b_ref, o_ref = refs[:3]
        else:
            a_ref, b_ref, c_ref, o_ref = refs[:4]
        part = _dot(a_ref[...], b_ref[...])

        def finish(r):
            if add is not None:
                r = r + add_scale * c_ref[...].astype(F32)
            o_ref[...] = r.astype(o_ref.dtype)

        if nk == 1:
            finish(part)
            return
        acc_ref = refs[-1]
        k = pl.program_id(2)

        @pl.when(k == 0)
        def _():
            acc_ref[...] = part

        @pl.when(k > 0)
        def _():
            acc_ref[...] += part

        @pl.when(k == nk - 1)
        def _():
            finish(acc_ref[...])

    in_specs = [pl.BlockSpec((tm, tk), lambda i, j, k: (i, k)), pl.BlockSpec((tk, tn), lambda i, j, k: (k, j))]
    args = [a, b]
    if add is not None:
        in_specs.append(pl.BlockSpec((tm, tn), lambda i, j, k: (i, j)))
        args.append(add)
    return pl.pallas_call(
        body,
        name=name,
        grid=(M // tm, N // tn, nk),
        in_specs=in_specs,
        out_specs=pl.BlockSpec((tm, tn), lambda i, j, k: (i, j)),
        out_shape=jax.ShapeDtypeStruct((M, N), out_dtype),
        scratch_shapes=[pltpu.VMEM((tm, tn), F32)] if nk > 1 else [],
        compiler_params=_cparams("parallel", "parallel", "arbitrary"),
    )(*args)


def mm_tn(a, b, name):
    M, K = a.shape
    _, N = b.shape
    tm = _pick(M, (1024, 512, 256, 128))
    tk = K if K <= 1024 else _pick(K, (1408, 1024, 896, 768, 640, 512, 384, 256, 128))
    tn = N if N <= 1408 else _pick(N, (1408, 1024, 896, 768, 640, 512, 384, 256, 128))
    nm = M // tm

    def body(a_ref, b_ref, o_ref):
        m = pl.program_id(2)
        part = _dot_tn(a_ref[...], b_ref[...])

        @pl.when(m == 0)
        def _():
            o_ref[...] = part

        @pl.when(m > 0)
        def _():
            o_ref[...] += part

    return pl.pallas_call(
        body,
        name=name,
        grid=(K // tk, N // tn, nm),
        in_specs=[pl.BlockSpec((tm, tk), lambda i, j, m: (m, i)), pl.BlockSpec((tm, tn), lambda i, j, m: (m, j))],
        out_specs=pl.BlockSpec((tk, tn), lambda i, j, m: (i, j)),
        out_shape=jax.ShapeDtypeStruct((K, N), F32),
        compiler_params=_cparams("parallel", "parallel", "arbitrary"),
    )(a, b)


def ln_fwd(x, mix, g, b, alpha, name):
    S, D = x.shape
    tm = _pick(S, (512, 256, 128))

    def body(x_ref, mix_ref, g_ref, b_ref, y_ref, xhat_ref, rstd_ref):
        r = alpha * x_ref[...] + mix_ref[...]
        mu = jnp.mean(r, axis=-1, keepdims=True)
        xc = r - mu
        var = jnp.mean(xc * xc, axis=-1, keepdims=True)
        rstd = lax.rsqrt(var + LN_EPS)
        xhat = xc * rstd
        y_ref[...] = xhat * g_ref[...] + b_ref[...]
        xhat_ref[...] = xhat
        rstd_ref[...] = rstd

    row = pl.BlockSpec((tm, D), lambda i: (i, 0))
    vec = pl.BlockSpec((1, D), lambda i: (0, 0))
    return pl.pallas_call(
        body,
        name=name,
        grid=(S // tm,),
        in_specs=[row, row, vec, vec],
        out_specs=[row, row, pl.BlockSpec((tm, 1), lambda i: (i, 0))],
        out_shape=[jax.ShapeDtypeStruct((S, D), F32), jax.ShapeDtypeStruct((S, D), F32), jax.ShapeDtypeStruct((S, 1), F32)],
        compiler_params=_cparams("parallel"),
    )(x, mix, g, b)


def ln_bwd(dy, xhat, rstd, g, name):
    S, D = dy.shape
    tm = _pick(S, (512, 256, 128))

    def body(dy_ref, xhat_ref, rstd_ref, g_ref, dr_ref, dg_ref, db_ref):
        i = pl.program_id(0)
        dyv = dy_ref[...]
        xh = xhat_ref[...]
        dxh = dyv * g_ref[...]
        m1 = jnp.mean(dxh, axis=-1, keepdims=True)
        m2 = jnp.mean(dxh * xh, axis=-1, keepdims=True)
        dr_ref[...] = rstd_ref[...] * (dxh - m1 - xh * m2)
        dg_part = jnp.sum(dyv * xh, axis=0, keepdims=True)
        db_part = jnp.sum(dyv, axis=0, keepdims=True)

        @pl.when(i == 0)
        def _():
            dg_ref[...] = dg_part
            db_ref[...] = db_part

        @pl.when(i > 0)
        def _():
            dg_ref[...] += dg_part
            db_ref[...] += db_part

    row = pl.BlockSpec((tm, D), lambda i: (i, 0))
    vec = pl.BlockSpec((1, D), lambda i: (0, 0))
    return pl.pallas_call(
        body,
        name=name,
        grid=(S // tm,),
        in_specs=[row, row, pl.BlockSpec((tm, 1), lambda i: (i, 0)), vec],
        out_specs=[row, vec, vec],
        out_shape=[jax.ShapeDtypeStruct((S, D), F32), jax.ShapeDtypeStruct((1, D), F32), jax.ShapeDtypeStruct((1, D), F32)],
        compiler_params=_cparams("arbitrary"),
    )(dy, xhat, rstd, g)


HALO = 2 * SUBLANES


def _prev_halo_spec(tm, tc, col0):
    return pl.BlockSpec((HALO, tc), lambda i, j: (jnp.maximum(i * (tm // HALO) - 1, 0), j + col0))


def _next_halo_spec(tm, tc, col0, n_row_tiles):
    last = n_row_tiles * (tm // HALO) - 1
    return pl.BlockSpec((HALO, tc), lambda i, j: (jnp.minimum((i + 1) * (tm // HALO), last), j + col0))


CONV_CHUNK = 32


def _causal_conv(ext_ref, w, n_taps, n_rows, row0, lanes):
    acc = None
    for k in range(n_taps):
        term = ext_ref[pl.ds(row0 - (n_taps - 1) + k, n_rows), lanes] * w[k:k + 1, :]
        acc = term if acc is None else acc + term
    return acc


def _anticausal_conv(ext_ref, w, n_taps, n_rows, row0, lanes):
    acc = None
    for k in range(n_taps):
        term = ext_ref[pl.ds(row0 + n_taps - 1 - k, n_rows), lanes] * w[k:k + 1, :]
        acc = term if acc is None else acc + term
    return acc


def _fold8(a):
    acc = a[0:SUBLANES]
    for g in range(SUBLANES, a.shape[0], SUBLANES):
        acc = acc + a[g:g + SUBLANES]
    return acc


INV_SQRT2 = 1.0 / math.sqrt(2.0)
INV_SQRT_2PI = 1.0 / math.sqrt(2.0 * math.pi)


def _gelu(g):
    return 0.5 * g * (1.0 + lax.erf(g * INV_SQRT2))


def _silu(x):
    return x * _sigmoid(x)


def _silu_grad(x):
    s = _sigmoid(x)
    return s * (1.0 + x * (1.0 - s))


def _conv_tiles(S, C, cols):
    tm = _pick(S, (256, 128))
    for tc in (1408, 1024, 512, 256, 128):
        if C % tc == 0 and all(c % tc == 0 for c in cols):
            return tm, tc
    raise ValueError("no column tile for the conv kernels")


def conv_act_fwd(src, in_col, C, conv_w, conv_b, out_dtype, name, gate_col=None):
    S = src.shape[0]
    K = conv_w.shape[0]
    gated = gate_col is not None
    tm, tc = _conv_tiles(S, C, [in_col] + ([gate_col] if gated else []))
    c_in = in_col // tc
    c_gate = gate_col // tc if gated else 0

    def body(*refs):
        if gated:
            x_ref, xp_ref, w_ref, b_ref, u_ref, o_ref, ext_ref = refs
        else:
            x_ref, xp_ref, w_ref, b_ref, o_ref, ext_ref = refs
        i = pl.program_id(0)
        ext_ref[0:HALO] = jnp.where(i > 0, xp_ref[...].astype(F32), 0.0)
        ext_ref[HALO:HALO + tm] = x_ref[...].astype(F32)
        for l0 in range(0, tc, LANES):
            ls = slice(l0, l0 + LANES)
            w = w_ref[:, ls]
            b = b_ref[:, ls]
            for r0 in range(0, tm, CONV_CHUNK):
                pre = _causal_conv(ext_ref, w, K, CONV_CHUNK, HALO + r0, ls) + b
                out = _gelu(pre) * u_ref[pl.ds(r0, CONV_CHUNK), ls].astype(F32) if gated else _silu(pre)
                o_ref[pl.ds(r0, CONV_CHUNK), ls] = out.astype(o_ref.dtype)

    in_specs = [
        pl.BlockSpec((tm, tc), lambda i, j: (i, j + c_in)),
        _prev_halo_spec(tm, tc, c_in),
        pl.BlockSpec((K, tc), lambda i, j: (0, j)),
        pl.BlockSpec((1, tc), lambda i, j: (0, j)),
    ]
    args = [src, src, conv_w, conv_b]
    if gated:
        in_specs.append(pl.BlockSpec((tm, tc), lambda i, j: (i, j + c_gate)))
        args.append(src)
    return pl.pallas_call(
        body,
        name=name,
        grid=(S // tm, C // tc),
        in_specs=in_specs,
        out_specs=pl.BlockSpec((tm, tc), lambda i, j: (i, j)),
        out_shape=jax.ShapeDtypeStruct((S, C), out_dtype),
        scratch_shapes=[pltpu.VMEM((tm + HALO, tc), F32)],
        compiler_params=_cparams("parallel", "parallel"),
    )(*args)


def conv_act_bwd(d_out, src, in_col, C, conv_w, conv_b, name, gate_col=None):
    S = src.shape[0]
    K = conv_w.shape[0]
    gated = gate_col is not None
    tm, tc = _conv_tiles(S, C, [in_col] + ([gate_col] if gated else []))
    c_in = in_col // tc
    c_gate = gate_col // tc if gated else 0
    ni = S // tm
    te = tm + HALO

    def body(*refs):
        if gated:
            (d_ref, dn_ref, x_ref, xp_ref, xn_ref, w_ref, b_ref, u_ref, un_ref,
             dx_ref, dw_ref, db_ref, du_ref, xext_ref, dext_ref) = refs
        else:
            (d_ref, dn_ref, x_ref, xp_ref, xn_ref, w_ref, b_ref,
             dx_ref, dw_ref, db_ref, xext_ref, dext_ref) = refs
        i = pl.program_id(1)
        xext_ref[0:HALO] = jnp.where(i > 0, xp_ref[...].astype(F32), 0.0)
        xext_ref[HALO:HALO + tm] = x_ref[...].astype(F32)
        xext_ref[HALO + tm:HALO + te] = xn_ref[...].astype(F32)

        @pl.when(i == 0)
        def _():
            dw_ref[...] = jnp.zeros_like(dw_ref)
            db_ref[...] = jnp.zeros_like(db_ref)

        for l0 in range(0, tc, LANES):
            ls = slice(l0, l0 + LANES)
            w = w_ref[:, ls]
            b = b_ref[:, ls]
            acc_w = [jnp.zeros((SUBLANES, LANES), F32) for _ in range(K)]
            acc_b = jnp.zeros((SUBLANES, LANES), F32)
            for r0 in range(0, te, CONV_CHUNK):
                n = min(CONV_CHUNK, te - r0)
                inside = r0 < tm
                taps = [xext_ref[pl.ds(HALO + r0 - (K - 1) + k, n), ls] for k in range(K)]
                pre = sum(t * w[k:k + 1, :] for k, t in enumerate(taps)) + b
                d = (d_ref[pl.ds(r0, n), ls] if inside else dn_ref[:, ls]).astype(F32)
                if gated:
                    u = (u_ref[pl.ds(r0, n), ls] if inside else un_ref[:, ls]).astype(F32)
                    cdf = 0.5 * (1.0 + lax.erf(pre * INV_SQRT2))
                    dpre = d * u * (cdf + pre * jnp.exp(-0.5 * pre * pre) * INV_SQRT_2PI)
                    if inside:
                        du_ref[pl.ds(r0, n), ls] = (d * (pre * cdf)).astype(du_ref.dtype)
                else:
                    dpre = d * _silu_grad(pre)
                if inside:
                    for k in range(K):
                        acc_w[k] = acc_w[k] + _fold8(dpre * taps[k])
                    acc_b = acc_b + _fold8(dpre)
                else:
                    dpre = jnp.where(i < ni - 1, dpre, 0.0)
                dext_ref[pl.ds(r0, n), ls] = dpre
            for r0 in range(0, tm, CONV_CHUNK):
                dx = _anticausal_conv(dext_ref, w, K, CONV_CHUNK, r0, ls)
                dx_ref[pl.ds(r0, CONV_CHUNK), ls] = dx.astype(dx_ref.dtype)
            dw_rows = [jnp.sum(a, axis=0, keepdims=True) for a in acc_w]
            dw_ref[:, ls] += jnp.concatenate(dw_rows + [jnp.zeros((SUBLANES - K, LANES), F32)], axis=0)
            db_ref[:, ls] += jnp.sum(acc_b, axis=0, keepdims=True)

    last = ni * (tm // HALO) - 1
    cur = lambda c0: pl.BlockSpec((tm, tc), lambda j, i: (i, j + c0))
    prev = lambda c0: pl.BlockSpec((HALO, tc), lambda j, i: (jnp.maximum(i * (tm // HALO) - 1, 0), j + c0))
    nxt = lambda c0: pl.BlockSpec((HALO, tc), lambda j, i: (jnp.minimum((i + 1) * (tm // HALO), last), j + c0))
    vec = lambda rows: pl.BlockSpec((rows, tc), lambda j, i: (0, j))
    in_specs = [cur(0), nxt(0), cur(c_in), prev(c_in), nxt(c_in), vec(K), vec(1)]
    args = [d_out, d_out, src, src, src, conv_w, conv_b]
    out_specs = [cur(0), vec(SUBLANES), vec(1)]
    out_shape = [jax.ShapeDtypeStruct((S, C), BF16), jax.ShapeDtypeStruct((SUBLANES, C), F32), jax.ShapeDtypeStruct((1, C), F32)]
    if gated:
        in_specs += [cur(c_gate), nxt(c_gate)]
        args += [src, src]
        out_specs.append(cur(0))
        out_shape.append(jax.ShapeDtypeStruct((S, C), BF16))
    outs = pl.pallas_call(
        body,
        name=name,
        grid=(C // tc, ni),
        in_specs=in_specs,
        out_specs=out_specs,
        out_shape=out_shape,
        scratch_shapes=[pltpu.VMEM((tm + 2 * HALO, tc), F32), pltpu.VMEM((te, tc), F32)],
        compiler_params=_cparams("parallel", "arbitrary"),
    )(*args)
    return outs[0], (outs[3] if gated else None), outs[1][:K], outs[2]


def ple_fwd(x2, zg, pp, bg, name):
    S, D = x2.shape
    tm = _pick(S, (512, 256, 128))

    def body(x_ref, z_ref, p_ref, b_ref, o_ref):
        o_ref[...] = x_ref[...] + _sigmoid(z_ref[...] + b_ref[...]) * p_ref[...]

    row = pl.BlockSpec((tm, D), lambda i: (i, 0))
    return pl.pallas_call(
        body, name=name, grid=(S // tm,), in_specs=[row, row, row, pl.BlockSpec((1, D), lambda i: (0, 0))], out_specs=row,
        out_shape=jax.ShapeDtypeStruct((S, D), F32), compiler_params=_cparams("parallel"),
    )(x2, zg, pp, bg)


def ple_bwd(dx3, zg, pp, bg, name):
    S, D = dx3.shape
    tm = _pick(S, (512, 256, 128))

    def body(d_ref, z_ref, p_ref, b_ref, dz_ref, dp_ref, db_ref):
        i = pl.program_id(0)
        d = d_ref[...]
        gate = _sigmoid(z_ref[...] + b_ref[...])
        dz = d * p_ref[...] * gate * (1.0 - gate)
        dz_ref[...] = dz.astype(dz_ref.dtype)
        dp_ref[...] = (d * gate).astype(dp_ref.dtype)
        part = jnp.sum(dz, axis=0, keepdims=True)

        @pl.when(i == 0)
        def _():
            db_ref[...] = part

        @pl.when(i > 0)
        def _():
            db_ref[...] += part

    row = pl.BlockSpec((tm, D), lambda i: (i, 0))
    vec = pl.BlockSpec((1, D), lambda i: (0, 0))
    return pl.pallas_call(
        body, name=name, grid=(S // tm,), in_specs=[row, row, row, vec], out_specs=[row, row, vec],
        out_shape=[jax.ShapeDtypeStruct((S, D), BF16), jax.ShapeDtypeStruct((S, D), BF16), jax.ShapeDtypeStruct((1, D), F32)],
        compiler_params=_cparams("arbitrary"),
    )(dx3, zg, pp, bg)


def loss_head(y, target, name):
    S, D = y.shape
    tm = _pick(S, (512, 256, 128))

    def body(y_ref, t_ref, loss_ref, dy_ref, acc_ref):
        i = pl.program_id(0)
        err = y_ref[...] - t_ref[...]
        dy_ref[...] = err * (1.0 / D)
        part = jnp.sum(err * err, axis=0, keepdims=True)

        @pl.when(i == 0)
        def _():
            acc_ref[...] = part

        @pl.when(i > 0)
        def _():
            acc_ref[...] += part

        @pl.when(i == pl.num_programs(0) - 1)
        def _():
            loss_ref[...] = jnp.zeros((1, LANES), F32) + (0.5 / D) * jnp.sum(acc_ref[...])

    row = pl.BlockSpec((tm, D), lambda i: (i, 0))
    return pl.pallas_call(
        body, name=name, grid=(S // tm,), in_specs=[row, row],
        out_specs=[pl.BlockSpec((1, LANES), lambda i: (0, 0)), row],
        out_shape=[jax.ShapeDtypeStruct((1, LANES), F32), jax.ShapeDtypeStruct((S, D), F32)],
        scratch_shapes=[pltpu.VMEM((1, D), F32)],
        compiler_params=_cparams("arbitrary"),
    )(y, target)


ATTN_TILE = 1024
ATTN_SCALE = 1.0 / math.sqrt(HEAD_DIM)


def _attn_tile(S):
    return _pick(S, (ATTN_TILE, 512, 256, 128))


def fox_gate_fwd(zt, bf, name):
    H, S = zt.shape
    tl = _pick(S, (512, 256, 128))

    def body(z_ref, b_ref, c_ref, carry_ref):
        i = pl.program_id(0)

        @pl.when(i == 0)
        def _():
            carry_ref[...] = jnp.zeros_like(carry_ref)

        z = z_ref[...] + b_ref[...]
        logf = jnp.minimum(z, 0.0) - _log1p_small(jnp.exp(-jnp.abs(z)))
        r = lax.broadcasted_iota(jnp.int32, (tl, tl), 0)
        c = lax.broadcasted_iota(jnp.int32, (tl, tl), 1)
        upper = (r <= c).astype(F32)
        cum = jnp.dot(logf, upper, precision=HIGHEST, preferred_element_type=F32) + carry_ref[...]
        c_ref[...] = cum
        carry_ref[...] = cum[:, tl - 1:tl]

    return pl.pallas_call(
        body, name=name, grid=(S // tl,),
        in_specs=[pl.BlockSpec((H, tl), lambda i: (0, i)), pl.BlockSpec((H, 1), lambda i: (0, 0))],
        out_specs=pl.BlockSpec((H, tl), lambda i: (0, i)),
        out_shape=jax.ShapeDtypeStruct((H, S), F32),
        scratch_shapes=[pltpu.VMEM((H, 1), F32)],
        compiler_params=_cparams("arbitrary"),
    )(zt, bf)


def fox_gate_bwd(dc_q, dc_k, zt, bf, name):
    H, S = zt.shape
    tl = _pick(S, (512, 256, 128))
    nt = S // tl

    def body(dcq_ref, dck_ref, z_ref, b_ref, dz_ref, db_ref, carry_ref):
        i = pl.program_id(0)

        @pl.when(i == 0)
        def _():
            carry_ref[...] = jnp.zeros_like(carry_ref)
            db_ref[...] = jnp.zeros_like(db_ref)

        r = lax.broadcasted_iota(jnp.int32, (tl, tl), 0)
        c = lax.broadcasted_iota(jnp.int32, (tl, tl), 1)
        lower = (r >= c).astype(F32)
        dc = dcq_ref[...] + dck_ref[...]
        suffix = jnp.dot(dc, lower, precision=HIGHEST, preferred_element_type=F32) + carry_ref[...]
        carry_ref[...] = suffix[:, 0:1]
        dz = suffix * _sigmoid(-(z_ref[...] + b_ref[...]))
        dz_ref[...] = dz
        db_ref[...] += jnp.sum(dz, axis=1, keepdims=True)

    rev = pl.BlockSpec((H, tl), lambda i: (0, nt - 1 - i))
    return pl.pallas_call(
        body, name=name, grid=(nt,),
        in_specs=[rev, rev, rev, pl.BlockSpec((H, 1), lambda i: (0, 0))],
        out_specs=[rev, pl.BlockSpec((H, 1), lambda i: (0, 0))],
        out_shape=[jax.ShapeDtypeStruct((H, S), F32), jax.ShapeDtypeStruct((H, 1), F32)],
        scratch_shapes=[pltpu.VMEM((H, 1), F32)],
        compiler_params=_cparams("arbitrary"),
    )(dc_q, dc_k, zt, bf)


ATTN_SUB = 4
ATTN_CHUNK = 16


def _row_tiles(ref, j, n_tiles):
    if n_tiles == 1:
        return ref[0, j]
    return jnp.concatenate([ref[0, j + t] for t in range(n_tiles)], axis=1)


def _diag_mask(n_rows, width, row0):
    r = lax.broadcasted_iota(jnp.int32, (n_rows, width), 0) + row0
    c = lax.broadcasted_iota(jnp.int32, (n_rows, width), 1)
    return r >= c


def _causal_sweep(i, process):
    def pair_body(j2, carry):
        process(2 * j2, 2, False)
        return carry

    lax.fori_loop(0, i // 2, pair_body, 0)

    @pl.when(i % 2 == 1)
    def _():
        process(i - 1, 2, True)

    @pl.when(i % 2 == 0)
    def _():
        process(i, 1, True)


def _ride_along(ride, refs, n_in, n_out, n_scratch, first, last):
    n = len(ride)
    ins, srcs = refs[:n_in], refs[n_in:n_in + n]
    outs, dsts = refs[n_in + n:n_in + n + n_out], refs[n_in + n + n_out:n_in + 2 * n + n_out]
    scratch, sems = refs[n_in + 2 * n + n_out:n_in + 2 * n + n_out + n_scratch], refs[n_in + 2 * n + n_out + n_scratch:]
    copies = _ride_copies(ride, srcs, dsts, sems) if n else []

    @pl.when(first)
    def _():
        for cp in copies:
            cp.start()

    def finish():
        @pl.when(last)
        def _():
            for cp in copies:
                cp.wait()

    return ins, outs, scratch, finish


def flash_fwd(q, k, v_ones, c_col, c_row, name, ride=()):
    H, S, Dh = q.shape
    T = _attn_tile(S)
    NT = S // T
    TS = T // ATTN_SUB
    ride_in, ride_out, ride_shape, ride_sems = _ride_decl(ride)

    def body(*refs):
        h, i = pl.program_id(0), pl.program_id(1)
        (q_ref, k_ref, v_ref, cq_ref, ck_ref), (o_ref, lse_ref), (m_ref, acc_ref), finish = _ride_along(
            ride, refs, 5, 2, 2, (h == 0) & (i == 0), (h == H - 1) & (i == NT - 1))
        qs = q_ref[0] * ATTN_SCALE
        c_ref = cq_ref[0, 0:1, :]
        m_ref[...] = jnp.full_like(m_ref, NEG_INF)
        acc_ref[...] = jnp.zeros_like(acc_ref)
        halves = [slice(u * TS, (u + 1) * TS) for u in range(ATTN_SUB)]

        def keys(j, n_tiles):
            return pl.ds(pl.multiple_of(j * T, T), n_tiles * T)

        def softmax_pv(j, n_tiles, masked, s_of):
            width = n_tiles * T
            vj = v_ref[0, keys(j, n_tiles), :]
            ckj = _row_tiles(ck_ref, j, n_tiles) - c_ref
            for u, rows in enumerate(halves):
                m_prev = m_ref[rows]
                ps, m_news = [], []
                for r0 in range(0, TS, ATTN_CHUNK):
                    rc = slice(r0, r0 + ATTN_CHUNK)
                    s = s_of(slice(u * TS + r0, u * TS + r0 + ATTN_CHUNK)) - ckj
                    if masked:
                        s = jnp.where(_diag_mask(ATTN_CHUNK, width, u * TS + r0 + width - T), s, NEG_INF)
                    m_new = jnp.maximum(m_prev[rc], jnp.max(s, axis=1, keepdims=True))
                    ps.append(jnp.exp(s - jnp.tile(m_new, (1, width // LANES))).astype(BF16))
                    m_news.append(m_new)
                m_new = jnp.concatenate(m_news, axis=0)
                acc_ref[rows] = jnp.exp(m_prev - m_new) * acc_ref[rows] + _dot(jnp.concatenate(ps, axis=0), vj)
                m_ref[rows] = m_new

        def process(j, n_tiles, masked):
            kj = k_ref[0, keys(j, n_tiles), :]
            s = jnp.concatenate([_dot_nt(qs[rows], kj) for rows in halves], axis=0)
            softmax_pv(j, n_tiles, masked, lambda rc: s[rc])

        _causal_sweep(i, process)
        acc = acc_ref[...]
        l = acc[:, Dh:Dh + 1]
        o_ref[0] = (acc[:, 0:Dh] / l).astype(o_ref.dtype)
        lse_ref[0] = m_ref[:, 0:1] + jnp.log(l) + (cq_ref[0] - c_ref)
        finish()

    tile = pl.BlockSpec((1, T, Dh), lambda h, i: (h, i, 0))
    whole = pl.BlockSpec((1, S, Dh), lambda h, i: (h, 0, 0))
    whole_v = pl.BlockSpec((1, S, 2 * Dh), lambda h, i: (h, 0, 0))
    col = pl.BlockSpec((1, T, 1), lambda h, i: (h, i, 0))
    rows = pl.BlockSpec((1, NT, 1, T), lambda h, i: (h, 0, 0, 0))
    return pl.pallas_call(
        body, name=name, grid=(H, NT),
        in_specs=[tile, whole, whole_v, col, rows] + ride_in,
        out_specs=[tile, col] + ride_out,
        out_shape=[jax.ShapeDtypeStruct((H, S, Dh), BF16), jax.ShapeDtypeStruct((H, S, 1), F32)] + ride_shape,
        scratch_shapes=[pltpu.VMEM((T, LANES), F32), pltpu.VMEM((T, 2 * Dh), F32)] + ride_sems,
        compiler_params=_cparams("arbitrary", "arbitrary"),
    )(q, k, v_ones, c_col, c_row, *[a for a, _ in ride])


def flash_bwd_q(q, k, v, o, do, q_t, do_t, lse, c_col, c_row, name, ride=()):
    H, S, Dh = q.shape
    T = _attn_tile(S)
    NT = S // T
    TS = T // ATTN_SUB

    ride_in, ride_out, ride_shape, ride_sems = _ride_decl(ride)

    def body(*refs):
        head, i = pl.program_id(0), pl.program_id(1)
        ((q_ref, k_ref, v_ref, o_ref, do_ref, qt_ref, dot_ref, lse_ref, cq_ref, ck_ref),
         (dq_ref, dcq_ref, dkt_ref, dvt_ref, dck_ref), (dq_acc, dcq_acc), finish) = _ride_along(
            ride, refs, 10, 5, 2, (head == 0) & (i == 0), (head == H - 1) & (i == NT - 1))

        @pl.when(i == 0)
        def _():
            dkt_ref[...] = jnp.zeros_like(dkt_ref)
            dvt_ref[...] = jnp.zeros_like(dvt_ref)
            dck_ref[...] = jnp.zeros_like(dck_ref)

        qs = q_ref[0] * ATTN_SCALE
        do = do_ref[0]
        qs_t = qt_ref[0] * ATTN_SCALE
        do_t = dot_ref[0]
        delta = jnp.sum(do.astype(F32) * o_ref[0].astype(F32), axis=1, keepdims=True)
        bias = cq_ref[0] - lse_ref[0]
        dq_acc[...] = jnp.zeros_like(dq_acc)
        dcq_acc[...] = jnp.zeros_like(dcq_acc)

        def process(j, n_tiles, masked):
            width = n_tiles * T
            off = pl.multiple_of(j * T, T)
            kj = k_ref[0, pl.ds(off, width), :]
            vj = v_ref[0, pl.ds(off, width), :]
            ck = _row_tiles(ck_ref, j, n_tiles)
            halves = [slice(u * TS, (u + 1) * TS) for u in range(ATTN_SUB)]
            ss = [_dot_nt(qs[h], kj) for h in halves]
            dps = [_dot_nt(do[h], vj) for h in halves]
            dkt, dvt = [], []
            dck8 = jnp.zeros((SUBLANES, width), F32)
            for u, h in enumerate(halves):
                ps, dss, rowsums = [], [], []
                for r0 in range(0, TS, ATTN_CHUNK):
                    rc = slice(r0, r0 + ATTN_CHUNK)
                    p = jnp.exp(ss[u][rc] + (bias[h][rc] - ck))
                    if masked:
                        p = jnp.where(_diag_mask(ATTN_CHUNK, width, u * TS + r0 + width - T), p, 0.0)
                    ds = p * (dps[u][rc] - delta[h][rc])
                    ps.append(p.astype(BF16))
                    dss.append(ds.astype(BF16))
                    rowsums.append(jnp.sum(ds, axis=1, keepdims=True))
                    for g in range(0, ATTN_CHUNK, SUBLANES):
                        dck8 = dck8 + ds[g:g + SUBLANES]
                p, ds = jnp.concatenate(ps, axis=0), jnp.concatenate(dss, axis=0)
                dq_acc[h] += _dot(ds, kj)
                dcq_acc[h] += jnp.concatenate(rowsums, axis=0)
                dvt.append(_dot(do_t[:, h], p))
                dkt.append(_dot(qs_t[:, h], ds))
            dvt, dkt, dck = sum(dvt), sum(dkt), jnp.sum(dck8, axis=0, keepdims=True)
            for t in range(n_tiles):
                cols = slice(t * T, (t + 1) * T)
                dvt_ref[0, j + t] += dvt[:, cols]
                dkt_ref[0, j + t] += dkt[:, cols]
                dck_ref[0, j + t] -= dck[:, cols]

        _causal_sweep(i, process)
        dq_ref[0] = (dq_acc[...] * ATTN_SCALE).astype(dq_ref.dtype)
        dcq_ref[0] = dcq_acc[...]
        finish()

    tile = pl.BlockSpec((1, T, Dh), lambda h, i: (h, i, 0))
    tile_t = pl.BlockSpec((1, Dh, T), lambda h, i: (h, 0, i))
    whole = pl.BlockSpec((1, S, Dh), lambda h, i: (h, 0, 0))
    col = pl.BlockSpec((1, T, 1), lambda h, i: (h, i, 0))
    rows = pl.BlockSpec((1, NT, 1, T), lambda h, i: (h, 0, 0, 0))
    acc_t = pl.BlockSpec((1, NT, Dh, T), lambda h, i: (h, 0, 0, 0))
    return pl.pallas_call(
        body, name=name, grid=(H, NT),
        in_specs=[tile, whole, whole, tile, tile, tile_t, tile_t, col, col, rows] + ride_in,
        out_specs=[tile, col, acc_t, acc_t, rows] + ride_out,
        out_shape=[jax.ShapeDtypeStruct((H, S, Dh), BF16), jax.ShapeDtypeStruct((H, S, 1), F32),
                   jax.ShapeDtypeStruct((H, NT, Dh, T), F32), jax.ShapeDtypeStruct((H, NT, Dh, T), F32),
                   jax.ShapeDtypeStruct((H, NT, 1, T), F32)] + ride_shape,
        scratch_shapes=[pltpu.VMEM((T, Dh), F32), pltpu.VMEM((T, 1), F32)] + ride_sems,
        compiler_params=_cparams("arbitrary", "arbitrary"),
    )(q, k, v, o, do, q_t, do_t, lse, c_col, c_row, *[a for a, _ in ride])


PAIR = 2 * HEAD_DIM


def _tri(n, lower):
    r = lax.broadcasted_iota(jnp.int32, (n, n), 0)
    c = lax.broadcasted_iota(jnp.int32, (n, n), 1)
    return (r >= c) if lower else (r <= c)


def _ssd_chunk_scalars(dt_raw, bias, a_log):
    Q = dt_raw.shape[0]
    dt = _softplus(dt_raw + bias)
    A = -jnp.exp(a_log)
    cum = jnp.dot(_tri(Q, True).astype(F32), dt * A, precision=HIGHEST, preferred_element_type=F32)
    tot = cum[Q - 1:Q, :]
    return dt, A, cum, tot


def _lane_pair(lo_mask, v, h0):
    return jnp.where(lo_mask, v[:, h0:h0 + 1], v[:, h0 + 1:h0 + 2])


def _head_expand(d_inner):
    return (jnp.arange(d_inner)[None, :] // HEAD_DIM == jnp.arange(LANES)[:, None]).astype(BF16)


def _split_dot(x, e, dims=None):
    hi = x.astype(BF16)
    lo = (x - hi.astype(F32)).astype(BF16)
    if dims is None:
        return jnp.dot(hi, e, preferred_element_type=F32) + jnp.dot(lo, e, preferred_element_type=F32)
    return (lax.dot_general(hi, e, dims, preferred_element_type=F32)
            + lax.dot_general(lo, e, dims, preferred_element_type=F32))


def ssd_scan_fwd(xbc, proj, dt_col, dt_bias, a_log, d_inner, name):
    S, W = xbc.shape
    Q, N, G = SSM_CHUNK, SSM_STATE, SSM_GROUPS
    nc = S // Q
    n_pairs = d_inner // PAIR
    pairs_per_group = n_pairs // G
    GN = G * N

    def body(xbc_ref, dt_ref, bias_ref, alog_ref, y_ref, sin_ref, st_ref):
        c = pl.program_id(0)

        @pl.when(c == 0)
        def _():
            st_ref[...] = jnp.zeros_like(st_ref)

        sin_ref[0] = st_ref[...]
        dt, A, cum, tot = _ssd_chunk_scalars(dt_ref[...], bias_ref[...], alog_ref[...])
        cum_t = cum.T
        dt_t = dt.T
        ecum = jnp.exp(cum)
        wend = jnp.exp(tot - cum) * dt
        etot = jnp.exp(tot)
        lower = _tri(Q, True)
        lo = lax.broadcasted_iota(jnp.int32, (Q, PAIR), 1) < HEAD_DIM
        lo_row = lax.broadcasted_iota(jnp.int32, (1, PAIR), 1) < HEAD_DIM
        for g in range(G):
            Bg = xbc_ref[:, d_inner + g * N:d_inner + (g + 1) * N]
            Cg = xbc_ref[:, d_inner + GN + g * N:d_inner + GN + (g + 1) * N]
            CB = _dot_nt(Cg, Bg)
            for pp in range(pairs_per_group):
                pr = g * pairs_per_group + pp
                h0 = 2 * pr
                xw = xbc_ref[:, pr * PAIR:(pr + 1) * PAIR]
                ys = []
                for h in (h0, h0 + 1):
                    L = jnp.where(lower, jnp.exp(cum[:, h:h + 1] - cum_t[h:h + 1, :]), 0.0)
                    ys.append(_dot(CB * L * dt_t[h:h + 1, :], xw))
                st = st_ref[pr]
                y_inter = _dot(Cg, st) * _lane_pair(lo, ecum, h0)
                y_ref[:, pr * PAIR:(pr + 1) * PAIR] = jnp.where(lo, ys[0], ys[1]) + y_inter
                st_ref[pr] = _lane_pair(lo_row, etot, h0) * st + _dot_tn(Bg, xw * _lane_pair(lo, wend, h0))

    return pl.pallas_call(
        body, name=name, grid=(nc,),
        in_specs=[pl.BlockSpec((Q, W), lambda c: (c, 0)), pl.BlockSpec((Q, LANES), lambda c: (c, dt_col // LANES)),
                  pl.BlockSpec((1, LANES), lambda c: (0, 0)), pl.BlockSpec((1, LANES), lambda c: (0, 0))],
        out_specs=[pl.BlockSpec((Q, d_inner), lambda c: (c, 0)), pl.BlockSpec((1, n_pairs, N, PAIR), lambda c: (c, 0, 0, 0))],
        out_shape=[jax.ShapeDtypeStruct((S, d_inner), F32), jax.ShapeDtypeStruct((nc, n_pairs, N, PAIR), F32)],
        scratch_shapes=[pltpu.VMEM((n_pairs, N, PAIR), F32)],
        compiler_params=_cparams("arbitrary"),
    )(xbc, proj, dt_bias, a_log)


def ssd_scan_bwd(dy, dskip, xbc, proj, dt_col, dt_bias, a_log, states, expand, d_inner, name):
    S, W = xbc.shape
    Q, N, G = SSM_CHUNK, SSM_STATE, SSM_GROUPS
    nc = S // Q
    n_pairs = d_inner // PAIR
    pairs_per_group = n_pairs // G
    GN = G * N

    def body(dy_ref, dskip_ref, xbc_ref, dt_ref, bias_ref, alog_ref, sin_ref, e_ref,
             dxbc_ref, ddt_ref, dalog_ref, dbias_ref, dst_ref, rows_ref):
        step = pl.program_id(0)

        @pl.when(step == 0)
        def _():
            dst_ref[...] = jnp.zeros_like(dst_ref)
            dalog_ref[...] = jnp.zeros_like(dalog_ref)
            dbias_ref[...] = jnp.zeros_like(dbias_ref)

        dt_raw = dt_ref[...]
        bias = bias_ref[...]
        dt, A, cum, tot = _ssd_chunk_scalars(dt_raw, bias, alog_ref[...])
        cum_t = cum.T
        ecum = jnp.exp(cum)
        eend = jnp.exp(tot - cum)
        wend = eend * dt
        etot = jnp.exp(tot)
        e = e_ref[...]
        wide = _split_dot(jnp.concatenate([wend, dt, jnp.broadcast_to(etot, (SUBLANES, LANES))], axis=0), e)
        wend_w, dt_w, etot_w = wide[0:Q], wide[Q:2 * Q], wide[2 * Q:2 * Q + 1]
        ecum_w = _dot(ecum, e)
        upper = _tri(Q, False)
        lane = lax.broadcasted_iota(jnp.int32, (Q, LANES), 1)
        lo = lane < HEAD_DIM
        ones_q = jnp.ones((Q, LANES), BF16)
        ones_8 = jnp.ones((SUBLANES, Q), BF16)
        rows_ref[...] = jnp.zeros_like(rows_ref)
        dcum_src = jnp.zeros((Q, LANES), F32)
        xz_parts, xbds_parts, dss_parts, dyy2_parts = [], [], [], []
        for g in range(G):
            Bg = xbc_ref[:, d_inner + g * N:d_inner + (g + 1) * N]
            Cg = xbc_ref[:, d_inner + GN + g * N:d_inner + GN + (g + 1) * N]
            Cg_t = Cg.T
            CBt = _dot_nt(Bg, Cg)
            dCBt = jnp.zeros((Q, Q), F32)
            dBg = jnp.zeros((Q, N), F32)
            dCg = jnp.zeros((Q, N), F32)
            for pp in range(pairs_per_group):
                pr = g * pairs_per_group + pp
                h0 = 2 * pr
                sl = slice(pr * PAIR, (pr + 1) * PAIR)
                xw = xbc_ref[:, sl]
                dyp = dy_ref[:, sl]
                st = sin_ref[0, pr]
                dst = dst_ref[pr]
                wend_p = wend_w[:, sl]
                dt_p = dt_w[:, sl]
                dye = dyp * ecum_w[:, sl]
                dyy2_parts.append(dye * _dot(Cg, st))
                dCg = dCg + _dot_nt(dye, st)
                bds = _dot(Bg, dst)
                dBg = dBg + _dot_nt(xw * wend_p, dst)
                xbds_parts.append(xw * bds)
                dss_parts.append(dst * st)
                xdt = xw * dt_p
                z = None
                for h, half in ((h0, lo), (h0 + 1, ~lo)):
                    Lt = jnp.where(upper, jnp.exp(cum_t[h:h + 1, :] - cum[:, h:h + 1]), 0.0)
                    dyh = jnp.where(half, dyp, 0.0)
                    CBLt = CBt * Lt
                    zh = _dot(CBLt, dyh)
                    z = zh if z is None else z + zh
                    dMt = _dot_nt(xdt, dyh)
                    dCBt = dCBt + dMt * Lt
                    gm = (dMt * CBLt).astype(BF16)
                    dcum_src = dcum_src + jnp.where(lane == h, jnp.dot(gm, ones_q, preferred_element_type=F32), 0.0)
                    rows_ref[h:h + 1, :] = jnp.dot(ones_8, gm, preferred_element_type=F32)[0:1]
                xz_parts.append(xw * z)
                dxbc_ref[:, sl] = dt_p * z + wend_p * bds + dskip_ref[:, sl]
                dst_ref[pr] = _dot(Cg_t, dye) + etot_w[:, sl] * dst
            dxbc_ref[:, d_inner + g * N:d_inner + (g + 1) * N] = dBg + _dot(dCBt, Cg)
            dxbc_ref[:, d_inner + GN + g * N:d_inner + GN + (g + 1) * N] = dCg + _dot_tn(dCBt, Bg)
        head_sums = lambda wide_arr: _split_dot(wide_arr, e, NT_DIMS)
        dy_y2 = head_sums(jnp.concatenate(dyy2_parts, axis=1))
        x_z = head_sums(jnp.concatenate(xz_parts, axis=1))
        q = head_sums(jnp.concatenate(xbds_parts, axis=1))
        dst_st = jnp.sum(head_sums(jnp.concatenate(dss_parts, axis=1)), axis=0, keepdims=True)
        wq = wend * q
        last_row = lax.broadcasted_iota(jnp.int32, (Q, LANES), 0) == Q - 1
        dcum = (dy_y2 - wq + (rows_ref[...].T - dcum_src)
                + jnp.where(last_row, etot * dst_st + jnp.sum(wq, axis=0, keepdims=True), 0.0))
        ddt = eend * q + x_z
        da = jnp.dot(upper.astype(F32), dcum, precision=HIGHEST, preferred_element_type=F32)
        ddt = ddt + da * A
        ddt_raw = ddt * _sigmoid(dt_raw + bias)
        ddt_ref[...] = ddt_raw
        dalog_ref[...] += jnp.sum(da * dt, axis=0, keepdims=True) * A
        dbias_ref[...] += jnp.sum(ddt_raw, axis=0, keepdims=True)

    rev = lambda width, col: pl.BlockSpec((Q, width), lambda s: (nc - 1 - s, col))
    vec = pl.BlockSpec((1, LANES), lambda s: (0, 0))
    return pl.pallas_call(
        body, name=name, grid=(nc,),
        in_specs=[rev(d_inner, 0), rev(d_inner, 0), rev(W, 0), rev(LANES, dt_col // LANES), vec, vec,
                  pl.BlockSpec((1, n_pairs, N, PAIR), lambda s: (nc - 1 - s, 0, 0, 0)),
                  pl.BlockSpec((LANES, d_inner), lambda s: (0, 0))],
        out_specs=[rev(W, 0), rev(LANES, 0), vec, vec],
        out_shape=[jax.ShapeDtypeStruct((S, W), F32), jax.ShapeDtypeStruct((S, LANES), F32),
                   jax.ShapeDtypeStruct((1, LANES), F32), jax.ShapeDtypeStruct((1, LANES), F32)],
        scratch_shapes=[pltpu.VMEM((n_pairs, N, PAIR), F32), pltpu.VMEM((LANES, Q), F32)],
        compiler_params=_cparams("arbitrary"),
    )(dy, dskip, xbc, proj, dt_bias, a_log, states, expand)


def ssd_gate_fwd(y, xbc, proj, d_skip, norm_w, d_inner, name):
    S = y.shape[0]
    tm = _pick(S, (256, 128))
    gs = d_inner // SSM_GROUPS

    def body(y_ref, x_ref, z_ref, d_ref, w_ref, o_ref):
        for g in range(SSM_GROUPS):
            sl = slice(g * gs, (g + 1) * gs)
            y2 = (y_ref[:, sl] + d_ref[:, sl] * x_ref[:, sl]) * _silu(z_ref[:, sl].astype(F32))
            r = lax.rsqrt(jnp.mean(y2 * y2, axis=-1, keepdims=True) + RMS_EPS)
            o_ref[:, sl] = (y2 * r * w_ref[:, sl]).astype(o_ref.dtype)

    row = pl.BlockSpec((tm, d_inner), lambda i: (i, 0))
    vec = pl.BlockSpec((1, d_inner), lambda i: (0, 0))
    return pl.pallas_call(
        body, name=name, grid=(S // tm,), in_specs=[row, row, row, vec, vec], out_specs=row,
        out_shape=jax.ShapeDtypeStruct((S, d_inner), BF16), compiler_params=_cparams("parallel"),
    )(y, xbc, proj, d_skip, norm_w)


def ssd_gate_bwd(dyn, y, xbc, proj, d_skip, norm_w, d_inner, name):
    S = y.shape[0]
    tm = _pick(S, (256, 128))
    gs = d_inner // SSM_GROUPS

    def body(dyn_ref, y_ref, x_ref, z_ref, d_ref, w_ref, dy_ref, dskip_ref, dz_ref, dw_ref, dd_ref):
        i = pl.program_id(0)

        @pl.when(i == 0)
        def _():
            dw_ref[...] = jnp.zeros_like(dw_ref)
            dd_ref[...] = jnp.zeros_like(dd_ref)

        for g in range(SSM_GROUPS):
            sl = slice(g * gs, (g + 1) * gs)
            z = z_ref[:, sl].astype(F32)
            x = x_ref[:, sl]
            sz = _silu(z)
            ysum = y_ref[:, sl] + d_ref[:, sl] * x
            y2 = ysum * sz
            r = lax.rsqrt(jnp.mean(y2 * y2, axis=-1, keepdims=True) + RMS_EPS)
            dyn = dyn_ref[:, sl].astype(F32)
            a = dyn * w_ref[:, sl]
            dy2 = r * a - y2 * (r * r * r) * jnp.mean(a * y2, axis=-1, keepdims=True)
            dysum = dy2 * sz
            dy_ref[:, sl] = dysum
            dskip_ref[:, sl] = dysum * d_ref[:, sl]
            dz_ref[:, sl] = (dy2 * ysum * _silu_grad(z)).astype(dz_ref.dtype)
            dw_ref[:, sl] += jnp.sum(dyn * y2 * r, axis=0, keepdims=True)
            dd_ref[:, sl] += jnp.sum(dysum * x, axis=0, keepdims=True)

    row = pl.BlockSpec((tm, d_inner), lambda i: (i, 0))
    vec = pl.BlockSpec((1, d_inner), lambda i: (0, 0))
    return pl.pallas_call(
        body, name=name, grid=(S // tm,), in_specs=[row, row, row, row, vec, vec], out_specs=[row, row, row, vec, vec],
        out_shape=[jax.ShapeDtypeStruct((S, d_inner), F32), jax.ShapeDtypeStruct((S, d_inner), F32),
                   jax.ShapeDtypeStruct((S, d_inner), BF16), jax.ShapeDtypeStruct((1, d_inner), F32),
                   jax.ShapeDtypeStruct((1, d_inner), F32)],
        compiler_params=_cparams("arbitrary"),
    )(dyn, y, xbc, proj, d_skip, norm_w)


def _pad_to(a, axis, mult=LANES):
    n = a.shape[axis]
    extra = (-n) % mult
    if extra == 0:
        return a
    widths = [(0, 0)] * a.ndim
    widths[axis] = (0, extra)
    return jnp.pad(a, widths)


def _attn_fwd(x, w_in, b_f, w_out, tag, ride=()):
    S, D = x.shape
    H = D // HEAD_DIM
    T = _attn_tile(S)
    qkv = mm_nn(x, w_in[:, :3 * D], BF16, f"{tag}_proj").reshape(S, 3, H, HEAD_DIM).transpose(1, 2, 0, 3)
    zt = mm_nn(x, _pad_to(w_in[:, 3 * D:], 1), F32, f"{tag}_proj_f")[:, :H].T
    bf = b_f.reshape(H, 1)
    c = fox_gate_fwd(zt, bf, f"{tag}_gate")
    c_col, c_row = c.reshape(H, S, 1), c.reshape(H, S // T, 1, T)
    v_ones = jnp.concatenate([qkv[2], jnp.ones_like(qkv[2])], axis=-1)
    o, lse, *ride_out = flash_fwd(qkv[0], qkv[1], v_ones, c_col, c_row, f"{tag}_flash", ride)
    o_flat = o.transpose(1, 0, 2).reshape(S, D)
    mix = mm_nn(o_flat, w_out, F32, f"{tag}_out")
    return mix, (qkv, zt, bf, c_col, c_row, o, lse, o_flat), ride_out


def _attn_bwd(x, dmix, dx_add, alpha, w_in, w_out, saved, tag, ride=()):
    S, D = x.shape
    H = D // HEAD_DIM
    T = _attn_tile(S)
    qkv, zt, bf, c_col, c_row, o, lse, o_flat = saved
    g_w_out = mm_tn(o_flat, dmix, f"{tag}_gwout")
    do = mm_nn(dmix, w_out.T, BF16, f"{tag}_do").reshape(S, H, HEAD_DIM)
    dq, dc_q, dk_t, dv_t, dc_k, *ride_out = flash_bwd_q(
        qkv[0], qkv[1], qkv[2], o, do.transpose(1, 0, 2), qkv[0].transpose(0, 2, 1), do.transpose(1, 2, 0), lse, c_col, c_row,
        f"{tag}_flash_bwd", ride)
    dzt, dbf = fox_gate_bwd(dc_q.reshape(H, S), dc_k.reshape(H, S), zt, bf, f"{tag}_gate_bwd")
    keys_major = lambda t: t.transpose(1, 3, 0, 2).reshape(S, D).astype(BF16)
    dqkv = jnp.concatenate([dq.transpose(1, 0, 2).reshape(S, D), keys_major(dk_t), keys_major(dv_t)], axis=1)
    dzf = _pad_to(dzt.T, 1)
    g_w_in = jnp.concatenate([mm_tn(x, dqkv, f"{tag}_gwqkv"), mm_tn(x, dzf, f"{tag}_gwf")[:, :H]], axis=1)
    w_in_t = w_in.T
    dx = mm_nn(dqkv, w_in_t[:3 * D], F32, f"{tag}_dx_qkv", add=dx_add, add_scale=alpha)
    dx = mm_nn(dzf, _pad_to(w_in_t[3 * D:], 0), F32, f"{tag}_dx_f", add=dx)
    return dx, (g_w_in, dbf.reshape(H), g_w_out), ride_out


def _ssm_dims(D):
    d_inner = 2 * D
    gn = SSM_GROUPS * SSM_STATE
    return d_inner, d_inner + 2 * gn, d_inner // HEAD_DIM


def _ssm_fwd(x, w_in, conv_w, conv_b, dt_bias, a_log, d_skip, norm_w, w_out, tag):
    S, D = x.shape
    DI, XBC, HS = _ssm_dims(D)
    dt_col = DI + XBC
    proj = mm_nn(x, w_in[:, :dt_col], BF16, f"{tag}_proj")
    dt_raw = mm_nn(x, _pad_to(w_in[:, dt_col:], 1), F32, f"{tag}_proj_dt")
    conv_b = conv_b.reshape(1, XBC)
    xbc = conv_act_fwd(proj, DI, XBC, conv_w, conv_b, F32, f"{tag}_conv")
    dt_bias_p = _pad_to(dt_bias.reshape(1, HS), 1)
    a_log_p = _pad_to(a_log.reshape(1, HS), 1)
    y, states = ssd_scan_fwd(xbc, dt_raw, 0, dt_bias_p, a_log_p, DI, f"{tag}_scan")
    d_vec = jnp.repeat(d_skip, HEAD_DIM).reshape(1, DI)
    norm_w = norm_w.reshape(1, DI)
    yn = ssd_gate_fwd(y, xbc, proj, d_vec, norm_w, DI, f"{tag}_gate")
    mix = mm_nn(yn, w_out, F32, f"{tag}_out")
    return mix, (proj, dt_raw, xbc, conv_b, dt_bias_p, a_log_p, y, states, d_vec, norm_w, yn)


def _ssm_bwd(x, dmix, dx_add, alpha, w_in, conv_w, w_out, saved, tag):
    S, D = x.shape
    DI, XBC, HS = _ssm_dims(D)
    dt_col = DI + XBC
    proj, dt_raw, xbc, conv_b, dt_bias_p, a_log_p, y, states, d_vec, norm_w, yn = saved
    g_w_out = mm_tn(yn, dmix, f"{tag}_gwout")
    dyn = mm_nn(dmix, w_out.T, BF16, f"{tag}_dyn")
    dy, dskip, dz, g_norm_w, g_dvec = ssd_gate_bwd(dyn, y, xbc, proj, d_vec, norm_w, DI, f"{tag}_gate_bwd")
    dxbc, ddt_raw, g_a_log, g_dt_bias = ssd_scan_bwd(dy, dskip, xbc, dt_raw, 0, dt_bias_p, a_log_p, states,
                                                     _head_expand(DI), DI, f"{tag}_scan_bwd")
    dxbc_raw, _, g_conv_w, g_conv_b = conv_act_bwd(dxbc, proj, DI, XBC, conv_w, conv_b, f"{tag}_conv_bwd")
    g_w_in = jnp.concatenate([mm_tn(x, dz, f"{tag}_gwz"), mm_tn(x, dxbc_raw, f"{tag}_gwxbc"),
                              mm_tn(x, ddt_raw, f"{tag}_gwdt")[:, :HS]], axis=1)
    w_in_t = w_in.T
    dx = mm_nn(dz, w_in_t[:DI], F32, f"{tag}_dx_z", add=dx_add, add_scale=alpha)
    dx = mm_nn(dxbc_raw, w_in_t[DI:dt_col], F32, f"{tag}_dx_xbc", add=dx)
    dx = mm_nn(ddt_raw, _pad_to(w_in_t[dt_col:], 0), F32, f"{tag}_dx_dt", add=dx)
    g_d = g_dvec.reshape(HS, HEAD_DIM).sum(axis=-1)
    grads = (g_w_in, g_conv_w, g_conv_b.reshape(XBC), g_dt_bias[0, :HS], g_a_log[0, :HS], g_d, g_norm_w.reshape(DI), g_w_out)
    return dx, grads


ATTN_KEYS = ("attn_w_in", "attn_b_f", "attn_w_out")
SSM_KEYS = ("ssm_w_in", "ssm_conv_w", "ssm_conv_b", "ssm_dt_bias", "ssm_A_log", "ssm_D", "ssm_norm_w", "ssm_w_out")
LAYER_KEYS = ("ln_mix_g", "ln_mix_b", "ffn_w_up", "ffn_conv_w", "ffn_conv_b", "ffn_w_down", "ln_ffn_g", "ln_ffn_b",
              "ple_w_proj", "ple_w_gate", "ple_b_gate")


def local_step(x, p, target, w, fwd_ride=(), after_fwd_ride=None, before_last_bwd=None):
    S, D = x.shape
    depth = p.shape[0]
    alpha = (2 * depth) ** 0.25
    saved = []
    bwd_ride_out = []
    for i in range(depth):
        j, tag = i // 2, f"l{i}"
        if i % 2 == 0:
            mix, msaved, ride_out = _attn_fwd(x, w["attn_w_in"][j], w["attn_b_f"][j], w["attn_w_out"][j], tag + "_attn",
                                              fwd_ride if i == 0 else ())
            if i == 0 and after_fwd_ride is not None:
                w = {**w, **after_fwd_ride(ride_out)}
        else:
            mix, msaved = _ssm_fwd(x, w["ssm_w_in"][j], w["ssm_conv_w"][j], w["ssm_conv_b"][j], w["ssm_dt_bias"][j],
                                   w["ssm_A_log"][j], w["ssm_D"][j], w["ssm_norm_w"][j], w["ssm_w_out"][j], tag + "_ssm")
        row = lambda k: w[k][i].reshape(1, -1)
        F = w["ffn_w_down"].shape[1]
        x1, xhat1, rstd1 = ln_fwd(x, mix, row("ln_mix_g"), row("ln_mix_b"), alpha, tag + "_ln_mix")
        h = mm_nn(x1, w["ffn_w_up"][i], BF16, tag + "_ffn_up")
        a = conv_act_fwd(h, F, F, w["ffn_conv_w"][i], row("ffn_conv_b"), BF16, tag + "_ffn_act", gate_col=0)
        ffn = mm_nn(a, w["ffn_w_down"][i], F32, tag + "_ffn_down")
        x2, xhat2, rstd2 = ln_fwd(x1, ffn, row("ln_ffn_g"), row("ln_ffn_b"), alpha, tag + "_ln_ffn")
        zg = mm_nn(x2, w["ple_w_gate"][i], F32, tag + "_ple_gate")
        pp = mm_nn(p[i], w["ple_w_proj"][i], F32, tag + "_ple_proj")
        x3 = ple_fwd(x2, zg, pp, row("ple_b_gate"), tag + "_ple")
        saved.append((x, msaved, x1, xhat1, rstd1, h, a, x2, xhat2, rstd2, zg, pp))
        x = x3

    loss_vec, d = loss_head(x, target, "loss_head")

    grads = {k: [None] * w[k].shape[0] for k in ATTN_KEYS + SSM_KEYS + LAYER_KEYS}
    for i in reversed(range(depth)):
        j, tag = i // 2, f"l{i}"
        x0, msaved, x1, xhat1, rstd1, h, a, x2, xhat2, rstd2, zg, pp = saved[i]
        row = lambda k: w[k][i].reshape(1, -1)
        dzg, dpp, g_bg = ple_bwd(d, zg, pp, row("ple_b_gate"), tag + "_ple_bwd")
        grads["ple_w_gate"][i] = mm_tn(x2, dzg, tag + "_gw_ple_gate")
        grads["ple_w_proj"][i] = mm_tn(p[i], dpp, tag + "_gw_ple_proj")
        grads["ple_b_gate"][i] = g_bg.reshape(D)
        dx2 = mm_nn(dzg, w["ple_w_gate"][i].T, F32, tag + "_dx2", add=d)
        dr2, g_g2, g_b2 = ln_bwd(dx2, xhat2, rstd2, row("ln_ffn_g"), tag + "_ln_ffn_bwd")
        grads["ln_ffn_g"][i], grads["ln_ffn_b"][i] = g_g2.reshape(D), g_b2.reshape(D)
        grads["ffn_w_down"][i] = mm_tn(a, dr2, tag + "_gw_down")
        da = mm_nn(dr2, w["ffn_w_down"][i].T, BF16, tag + "_da")
        dgin, du, g_cw, g_cb = conv_act_bwd(da, h, F, F, w["ffn_conv_w"][i], row("ffn_conv_b"), tag + "_ffn_act_bwd", gate_col=0)
        grads["ffn_conv_w"][i], grads["ffn_conv_b"][i] = g_cw, g_cb.reshape(F)
        grads["ffn_w_up"][i] = jnp.concatenate([mm_tn(x1, du, tag + "_gw_up_u"), mm_tn(x1, dgin, tag + "_gw_up_g")], axis=1)
        w_up_t = w["ffn_w_up"][i].T
        dx1 = mm_nn(du, w_up_t[:F], F32, tag + "_dx1_u", add=dr2, add_scale=alpha)
        dx1 = mm_nn(dgin, w_up_t[F:], F32, tag + "_dx1_g", add=dx1)
        dr1, g_g1, g_b1 = ln_bwd(dx1, xhat1, rstd1, row("ln_mix_g"), tag + "_ln_mix_bwd")
        grads["ln_mix_g"][i], grads["ln_mix_b"][i] = g_g1.reshape(D), g_b1.reshape(D)
        if i % 2 == 0:
            ride = before_last_bwd(grads) if (i == 0 and before_last_bwd is not None) else ()
            d, mg, ride_out = _attn_bwd(x0, dr1, dr1, alpha, w["attn_w_in"][j], w["attn_w_out"][j], msaved, tag + "_attn", ride)
            if i == 0:
                bwd_ride_out = ride_out
            for k, g in zip(ATTN_KEYS, mg):
                grads[k][j] = g
        else:
            d, mg = _ssm_bwd(x0, dr1, dr1, alpha, w["ssm_w_in"][j], w["ssm_conv_w"][j], w["ssm_w_out"][j], msaved, tag + "_ssm")
            for k, g in zip(SSM_KEYS, mg):
                grads[k][j] = g
    return loss_vec, d, {k: jnp.stack(v) for k, v in grads.items()}, bwd_ride_out


MESH = pl.DeviceIdType.MESH
PACK_ELEMS = 2 * SUBLANES * LANES
PACK_ROWS = 512


def _exchange_copies(src_ref, out_ref, gather, send_sems, recv_sems, local_sem):
    x, y, c = lax.axis_index("x"), lax.axis_index("y"), lax.axis_index("c")
    me = 4 * x + 2 * y + c

    def block_for(dev):
        return src_ref if gather else src_ref.at[dev]

    copies = [pltpu.make_async_copy(block_for(me), out_ref.at[me], local_sem)]
    for k in range(1, N_DEV):
        px = 1 - x if k & 4 else x
        py = 1 - y if k & 2 else y
        pc = 1 - c if k & 1 else c
        copies.append(pltpu.make_async_remote_copy(
            src_ref=block_for(4 * px + 2 * py + pc), dst_ref=out_ref.at[me],
            send_sem=send_sems.at[k - 1], recv_sem=recv_sems.at[k - 1],
            device_id=(px, py, pc), device_id_type=MESH))
    return copies


def _exchange(src, gather, name):
    def body(src_ref, out_ref, send_sems, recv_sems, local_sem):
        copies = _exchange_copies(src_ref, out_ref, gather, send_sems, recv_sems, local_sem)
        for cp in copies:
            cp.start()
        for cp in copies:
            cp.wait()

    return pl.pallas_call(
        body, name=name,
        in_specs=[pl.BlockSpec(memory_space=pl.ANY)],
        out_specs=pl.BlockSpec(memory_space=pl.ANY),
        out_shape=jax.ShapeDtypeStruct((N_DEV,) + tuple(src.shape[-2:]), src.dtype),
        scratch_shapes=[pltpu.SemaphoreType.DMA((N_DEV - 1,)), pltpu.SemaphoreType.DMA((N_DEV - 1,)), pltpu.SemaphoreType.DMA],
    )(src)


def _ride_decl(ride):
    if not ride:
        return [], [], [], []
    hbm = pl.BlockSpec(memory_space=pl.ANY)
    out_shape = [jax.ShapeDtypeStruct((N_DEV,) + tuple(a.shape[-2:]), a.dtype) for a, _ in ride]
    sems = [pltpu.SemaphoreType.DMA((len(ride), N_DEV - 1)), pltpu.SemaphoreType.DMA((len(ride), N_DEV - 1)),
            pltpu.SemaphoreType.DMA((len(ride),))]
    return [hbm] * len(ride), [hbm] * len(ride), out_shape, sems


def _ride_copies(ride, src_refs, out_refs, sems):
    copies = []
    for r, ((_, gather), src_ref, out_ref) in enumerate(zip(ride, src_refs, out_refs)):
        copies += _exchange_copies(src_ref, out_ref, gather, sems[0].at[r], sems[1].at[r], sems[2].at[r])
    return copies


def reduce_adamw(parts, w, m, v, name):
    _, R, _ = parts.shape
    tr = _pick(R, (512, 256, 128, 64, 32, 16))

    def body(p_ref, w_ref, m_ref, v_ref, g_ref, d_ref, nm_ref, nv_ref):
        g = p_ref[0].astype(F32)
        for s in range(1, N_DEV):
            g = g + p_ref[s].astype(F32)
        nm = ADAM_B1 * m_ref[...] + (1.0 - ADAM_B1) * g
        nv = ADAM_B2 * v_ref[...] + (1.0 - ADAM_B2) * (g * g)
        m_hat = nm / (1.0 - ADAM_B1 ** ADAM_STEP)
        v_hat = nv / (1.0 - ADAM_B2 ** ADAM_STEP)
        g_ref[...] = g
        d_ref[...] = -ADAM_LR * (m_hat / (jnp.sqrt(v_hat) + ADAM_EPS) + ADAM_WD * w_ref[...])
        nm_ref[...] = nm
        nv_ref[...] = nv

    row = pl.BlockSpec((tr, LANES), lambda i: (i, 0))
    return pl.pallas_call(
        body, name=name, grid=(R // tr,),
        in_specs=[pl.BlockSpec((N_DEV, tr, LANES), lambda i: (0, i, 0)), row, row, row],
        out_specs=[row, row, row, row],
        out_shape=[jax.ShapeDtypeStruct((R, LANES), F32)] * 4,
        compiler_params=_cparams("parallel"),
    )(parts, w, m, v)


def _pack(arrays, dtype, lead=0):
    parts = []
    for a in arrays:
        head = a.shape[:lead]
        flat = a.astype(dtype).reshape(head + (-1,))
        flat = jnp.pad(flat, [(0, 0)] * lead + [(0, (-flat.shape[-1]) % PACK_ELEMS)])
        parts.append(flat.reshape(head + (-1, LANES)))
    rows = sum(part.shape[lead] for part in parts)
    if (-rows) % PACK_ROWS:
        parts.append(jnp.zeros(parts[0].shape[:lead] + ((-rows) % PACK_ROWS, LANES), dtype))
    return jnp.concatenate(parts, axis=lead)


def _unpack(packed, shapes):
    lead = packed.shape[:-2]
    out, r0 = [], 0
    for shape in shapes:
        n = math.prod(shape)
        rows = -(-n // PACK_ELEMS) * (PACK_ELEMS // LANES)
        seg = packed[..., r0:r0 + rows, :].reshape(lead + (rows * LANES,))[..., :n]
        out.append(seg.reshape(lead + tuple(shape)))
        r0 += rows
    return out


MATMUL_SHARDED = {"attn_w_in": 2, "attn_w_out": 1, "ssm_w_in": 2, "ssm_w_out": 1, "ffn_w_up": 2, "ffn_w_down": 1,
                  "ple_w_proj": 2, "ple_w_gate": 1}
SMALL_SHARDED = {"ssm_conv_w": 2, "ssm_conv_b": 1, "ssm_norm_w": 1, "ffn_conv_w": 2}
REPLICATED = ("attn_b_f", "ssm_dt_bias", "ssm_A_log", "ssm_D", "ln_mix_g", "ln_mix_b", "ffn_conv_b", "ln_ffn_g",
              "ln_ffn_b", "ple_b_gate")
WEIGHT_ORDER = ("attn_w_in", "attn_b_f", "attn_w_out", "ssm_w_in", "ssm_conv_w", "ssm_conv_b", "ssm_dt_bias", "ssm_A_log",
                "ssm_D", "ssm_norm_w", "ssm_w_out", "ln_mix_g", "ln_mix_b", "ffn_w_up", "ffn_conv_w", "ffn_conv_b",
                "ffn_w_down", "ln_ffn_g", "ln_ffn_b", "ple_w_proj", "ple_w_gate", "ple_b_gate")


def _join_shards(gathered, axis):
    moved = jnp.moveaxis(gathered, 0, axis)
    shape = list(moved.shape)
    shape[axis:axis + 2] = [shape[axis] * shape[axis + 1]]
    return moved.reshape(shape)


def _split_shards(full, axis):
    shape = list(full.shape)
    shape[axis:axis + 1] = [N_DEV, shape[axis] // N_DEV]
    return jnp.moveaxis(full.reshape(shape), axis, 0)


def kernel(x, p, attn_w_in, attn_b_f, attn_w_out, ssm_w_in, ssm_conv_w, ssm_conv_b, ssm_dt_bias, ssm_A_log, ssm_D, ssm_norm_w, ssm_w_out, ln_mix_g, ln_mix_b, ffn_w_up, ffn_conv_w, ffn_conv_b, ffn_w_down, ln_ffn_g, ln_ffn_b, ple_w_proj, ple_w_gate, ple_b_gate, loss_target, m_attn_w_in, m_attn_b_f, m_attn_w_out, m_ssm_w_in, m_ssm_conv_w, m_ssm_conv_b, m_ssm_dt_bias, m_ssm_A_log, m_ssm_D, m_ssm_norm_w, m_ssm_w_out, m_ln_mix_g, m_ln_mix_b, m_ffn_w_up, m_ffn_conv_w, m_ffn_conv_b, m_ffn_w_down, m_ln_ffn_g, m_ln_ffn_b, m_ple_w_proj, m_ple_w_gate, m_ple_b_gate, v_attn_w_in, v_attn_b_f, v_attn_w_out, v_ssm_w_in, v_ssm_conv_w, v_ssm_conv_b, v_ssm_dt_bias, v_ssm_A_log, v_ssm_D, v_ssm_norm_w, v_ssm_w_out, v_ln_mix_g, v_ln_mix_b, v_ffn_w_up, v_ffn_conv_w, v_ffn_conv_b, v_ffn_w_down, v_ln_ffn_g, v_ln_ffn_b, v_ple_w_proj, v_ple_w_gate, v_ple_b_gate):
    args = dict(locals())
    w_loc = {k: args[k] for k in WEIGHT_ORDER}
    m_loc = {k: args["m_" + k] for k in WEIGHT_ORDER}
    v_loc = {k: args["v_" + k] for k in WEIGHT_ORDER}
    axis_of = {**MATMUL_SHARDED, **SMALL_SHARDED}
    first_names = ("attn_w_in", "attn_w_out")
    mm_names = tuple(k for k in MATMUL_SHARDED if k not in first_names)
    small_names = tuple(SMALL_SHARDED)
    later_names = mm_names + small_names

    def joined(names, gathered):
        return {k: _join_shards(blocks, axis_of[k])
                for k, blocks in zip(names, _unpack(gathered, [w_loc[k].shape for k in names]))}

    g_first = _exchange(_pack([w_loc[k] for k in first_names], BF16), True, "gather_attn_weights")
    w_first = {**{k: w_loc[k] for k in REPLICATED}, **joined(first_names, g_first)}
    fwd_ride = ((_pack([w_loc[k] for k in mm_names], BF16), True), (_pack([w_loc[k] for k in small_names], F32), True))

    def after_fwd_ride(gathered):
        return {**joined(mm_names, gathered[0]), **joined(small_names, gathered[1])}

    def shard_blocks(grads, names):
        return _pack([_split_shards(grads[k], axis_of[k]) for k in names], BF16, lead=1)

    def before_last_bwd(grads):
        return ((shard_blocks({k: jnp.stack(grads[k]) for k in later_names}, later_names), False),)

    loss_vec, grad_x, g_full, (parts_later,) = local_step(x[0], p[:, 0], loss_target[0], w_first, fwd_ride, after_fwd_ride,
                                                          before_last_bwd)
    parts_first = _exchange(shard_blocks(g_full, first_names), False, "exchange_attn_grads")
    res = {}
    for names, parts, tag in ((later_names, parts_later, "later"), (first_names, parts_first, "attn")):
        shapes = [w_loc[k].shape for k in names]
        pk = lambda d: _pack([d[k] for k in names], F32)
        outs = reduce_adamw(parts, pk(w_loc), pk(m_loc), pk(v_loc), "reduce_adamw_" + tag)
        res.update({k: vals for k, vals in zip(names, zip(*[_unpack(o, shapes) for o in outs]))})

    rep_shapes = [w_loc[k].shape for k in REPLICATED] + [(1, LANES)]
    rparts = _exchange(_pack([g_full[k] for k in REPLICATED] + [loss_vec], F32), True, "gather_replicated_grads")
    zero = jnp.zeros((1, LANES), F32)
    rk = lambda d: _pack([d[k] for k in REPLICATED] + [zero], F32)
    routs = reduce_adamw(rparts, rk(w_loc), rk(m_loc), rk(v_loc), "reduce_adamw_replicated")
    runp = [_unpack(o, rep_shapes) for o in routs]
    for i, k in enumerate(REPLICATED):
        res[k] = tuple(u[i] for u in runp)
    loss = runp[0][-1][0, 0]

    return (loss, grad_x[None], *[res[k][0] for k in WEIGHT_ORDER], *[res[k][1] for k in WEIGHT_ORDER],
            *[res[k][2] for k in WEIGHT_ORDER], *[res[k][3] for k in WEIGHT_ORDER])
```

```python
import functools
import math

import jax
import jax.numpy as jnp
from jax import lax
from jax.experimental import pallas as pl
from jax.experimental.pallas import tpu as pltpu

F32 = jnp.float32
BF16 = jnp.bfloat16

LANES = 128
SUBLANES = 8
VMEM_LIMIT_BYTES = 56 * 1024 * 1024

N_DEV = 8
HEAD_DIM = 64
SSM_GROUPS = 8
SSM_STATE = 128
SSM_CHUNK = 128
SSM_CONV = 4
FFN_CONV = 3
LN_EPS = 1e-5
RMS_EPS = 1e-5
ADAM_LR, ADAM_B1, ADAM_B2, ADAM_EPS, ADAM_WD, ADAM_STEP = 0.001, 0.9, 0.999, 1e-08, 0.01, 10
NEG_INF = float("-inf")
HIGHEST = lax.Precision.HIGHEST
NT_DIMS = (((1,), (1,)), ((), ()))
TN_DIMS = (((0,), (0,)), ((), ()))


def _cparams(*sem):
    return pltpu.CompilerParams(dimension_semantics=sem, vmem_limit_bytes=VMEM_LIMIT_BYTES)


def _pick(n, candidates):
    for c in candidates:
        if n % c == 0:
            return c
    return n


def _dot(a, b):
    return jnp.dot(a.astype(BF16), b.astype(BF16), preferred_element_type=F32)


def _dot_nt(a, b):
    return lax.dot_general(a.astype(BF16), b.astype(BF16), NT_DIMS, preferred_element_type=F32)


def _dot_tn(a, b):
    return lax.dot_general(a.astype(BF16), b.astype(BF16), TN_DIMS, preferred_element_type=F32)


def _sigmoid(x):
    return 1.0 / (1.0 + jnp.exp(-x))


def _log1p_small(u):
    return jnp.where(u < 1e-3, u * (1.0 - u * (0.5 - u * (1.0 / 3.0))), jnp.log(1.0 + u))


def _softplus(x):
    return jnp.maximum(x, 0.0) + _log1p_small(jnp.exp(-jnp.abs(x)))


def mm_nn(a, b, out_dtype, name, add=None, add_scale=1.0):
    M, K = a.shape
    _, N = b.shape
    tm = _pick(M, (1024, 512, 256, 128))
    tn = N if N <= 1024 else _pick(N, (1408, 1024, 896, 768, 640, 512, 384, 256, 128))
    tk = K if K <= 2048 else _pick(K, (1408, 1024, 896, 768, 640, 512, 384, 256, 128))
    nk = K // tk

    def body(*refs):
        if add is None:
            a_ref, b_ref, o_ref = refs[:3]
        else:
            a_ref, b_ref, c_ref, o_ref = refs[:4]
        part = _dot(a_ref[...], b_ref[...])

        def finish(r):
            if add is not None:
                r = r + add_scale * c_ref[...].astype(F32)
            o_ref[...] = r.astype(o_ref.dtype)

        if nk == 1:
            finish(part)
            return
        acc_ref = refs[-1]
        k = pl.program_id(2)

        @pl.when(k == 0)
        def _():
            acc_ref[...] = part

        @pl.when(k > 0)
        def _():
            acc_ref[...] += part

        @pl.when(k == nk - 1)
        def _():
            finish(acc_ref[...])

    in_specs = [pl.BlockSpec((tm, tk), lambda i, j, k: (i, k)), pl.BlockSpec((tk, tn), lambda i, j, k: (k, j))]
    args = [a, b]
    if add is not None:
        in_specs.append(pl.BlockSpec((tm, tn), lambda i, j, k: (i, j)))
        args.append(add)
    return pl.pallas_call(
        body,
        name=name,
        grid=(M // tm, N // tn, nk),
        in_specs=in_specs,
        out_specs=pl.BlockSpec((tm, tn), lambda i, j, k: (i, j)),
        out_shape=jax.ShapeDtypeStruct((M, N), out_dtype),
        scratch_shapes=[pltpu.VMEM((tm, tn), F32)] if nk > 1 else [],
        compiler_params=_cparams("parallel", "parallel", "arbitrary"),
    )(*args)


def mm_tn(a, b, name):
    M, K = a.shape
    _, N = b.shape
    tm = _pick(M, (1024, 512, 256, 128))
    tk = K if K <= 1024 else _pick(K, (1408, 1024, 896, 768, 640, 512, 384, 256, 128))
    tn = N if N <= 1408 else _pick(N, (1408, 1024, 896, 768, 640, 512, 384, 256, 128))
    nm = M // tm

    def body(a_ref, b_ref, o_ref):
        m = pl.program_id(2)
        part = _dot_tn(a_ref[...], b_ref[...])

        @pl.when(m == 0)
        def _():
            o_ref[...] = part

        @pl.when(m > 0)
        def _():
            o_ref[...] += part

    return pl.pallas_call(
        body,
        name=name,
        grid=(K // tk, N // tn, nm),
        in_specs=[pl.BlockSpec((tm, tk), lambda i, j, m: (m, i)), pl.BlockSpec((tm, tn), lambda i, j, m: (m, j))],
        out_specs=pl.BlockSpec((tk, tn), lambda i, j, m: (i, j)),
        out_shape=jax.ShapeDtypeStruct((K, N), F32),
        compiler_params=_cparams("parallel", "parallel", "arbitrary"),
    )(a, b)


def ln_fwd(x, mix, g, b, alpha, name):
    S, D = x.shape
    tm = _pick(S, (512, 256, 128))

    def body(x_ref, mix_ref, g_ref, b_ref, y_ref, xhat_ref, rstd_ref):
        r = alpha * x_ref[...] + mix_ref[...]
        mu = jnp.mean(r, axis=-1, keepdims=True)
        xc = r - mu
        var = jnp.mean(xc * xc, axis=-1, keepdims=True)
        rstd = lax.rsqrt(var + LN_EPS)
        xhat = xc * rstd
        y_ref[...] = xhat * g_ref[...] + b_ref[...]
        xhat_ref[...] = xhat
        rstd_ref[...] = rstd

    row = pl.BlockSpec((tm, D), lambda i: (i, 0))
    vec = pl.BlockSpec((1, D), lambda i: (0, 0))
    return pl.pallas_call(
        body,
        name=name,
        grid=(S // tm,),
        in_specs=[row, row, vec, vec],
        out_specs=[row, row, pl.BlockSpec((tm, 1), lambda i: (i, 0))],
        out_shape=[jax.ShapeDtypeStruct((S, D), F32), jax.ShapeDtypeStruct((S, D), F32), jax.ShapeDtypeStruct((S, 1), F32)],
        compiler_params=_cparams("parallel"),
    )(x, mix, g, b)


def ln_bwd(dy, xhat, rstd, g, name):
    S, D = dy.shape
    tm = _pick(S, (512, 256, 128))

    def body(dy_ref, xhat_ref, rstd_ref, g_ref, dr_ref, dg_ref, db_ref):
        i = pl.program_id(0)
        dyv = dy_ref[...]
        xh = xhat_ref[...]
        dxh = dyv * g_ref[...]
        m1 = jnp.mean(dxh, axis=-1, keepdims=True)
        m2 = jnp.mean(dxh * xh, axis=-1, keepdims=True)
        dr_ref[...] = rstd_ref[...] * (dxh - m1 - xh * m2)
        dg_part = jnp.sum(dyv * xh, axis=0, keepdims=True)
        db_part = jnp.sum(dyv, axis=0, keepdims=True)

        @pl.when(i == 0)
        def _():
            dg_ref[...] = dg_part
            db_ref[...] = db_part

        @pl.when(i > 0)
        def _():
            dg_ref[...] += dg_part
            db_ref[...] += db_part

    row = pl.BlockSpec((tm, D), lambda i: (i, 0))
    vec = pl.BlockSpec((1, D), lambda i: (0, 0))
    return pl.pallas_call(
        body,
        name=name,
        grid=(S // tm,),
        in_specs=[row, row, pl.BlockSpec((tm, 1), lambda i: (i, 0)), vec],
        out_specs=[row, vec, vec],
        out_shape=[jax.ShapeDtypeStruct((S, D), F32), jax.ShapeDtypeStruct((1, D), F32), jax.ShapeDtypeStruct((1, D), F32)],
        compiler_params=_cparams("arbitrary"),
    )(dy, xhat, rstd, g)


HALO = 2 * SUBLANES


def _prev_halo_spec(tm, tc, col0):
    return pl.BlockSpec((HALO, tc), lambda i, j: (jnp.maximum(i * (tm // HALO) - 1, 0), j + col0))


def _next_halo_spec(tm, tc, col0, n_row_tiles):
    last = n_row_tiles * (tm // HALO) - 1
    return pl.BlockSpec((HALO, tc), lambda i, j: (jnp.minimum((i + 1) * (tm // HALO), last), j + col0))


CONV_CHUNK = 32


def _causal_conv(ext_ref, w, n_taps, n_rows, row0, lanes):
    acc = None
    for k in range(n_taps):
        term = ext_ref[pl.ds(row0 - (n_taps - 1) + k, n_rows), lanes] * w[k:k + 1, :]
        acc = term if acc is None else acc + term
    return acc


def _anticausal_conv(ext_ref, w, n_taps, n_rows, row0, lanes):
    acc = None
    for k in range(n_taps):
        term = ext_ref[pl.ds(row0 + n_taps - 1 - k, n_rows), lanes] * w[k:k + 1, :]
        acc = term if acc is None else acc + term
    return acc


def _fold8(a):
    acc = a[0:SUBLANES]
    for g in range(SUBLANES, a.shape[0], SUBLANES):
        acc = acc + a[g:g + SUBLANES]
    return acc


INV_SQRT2 = 1.0 / math.sqrt(2.0)
INV_SQRT_2PI = 1.0 / math.sqrt(2.0 * math.pi)


def _gelu(g):
    return 0.5 * g * (1.0 + lax.erf(g * INV_SQRT2))


def _silu(x):
    return x * _sigmoid(x)


def _silu_grad(x):
    s = _sigmoid(x)
    return s * (1.0 + x * (1.0 - s))


def _conv_tiles(S, C, cols):
    tm = _pick(S, (256, 128))
    for tc in (1408, 1024, 512, 256, 128):
        if C % tc == 0 and all(c % tc == 0 for c in cols):
            return tm, tc
    raise ValueError("no column tile for the conv kernels")


def conv_act_fwd(src, in_col, C, conv_w, conv_b, out_dtype, name, gate_col=None):
    S = src.shape[0]
    K = conv_w.shape[0]
    gated = gate_col is not None
    tm, tc = _conv_tiles(S, C, [in_col] + ([gate_col] if gated else []))
    c_in = in_col // tc
    c_gate = gate_col // tc if gated else 0

    def body(*refs):
        if gated:
            x_ref, xp_ref, w_ref, b_ref, u_ref, o_ref, ext_ref = refs
        else:
            x_ref, xp_ref, w_ref, b_ref, o_ref, ext_ref = refs
        i = pl.program_id(0)
        ext_ref[0:HALO] = jnp.where(i > 0, xp_ref[...].astype(F32), 0.0)
        ext_ref[HALO:HALO + tm] = x_ref[...].astype(F32)
        for l0 in range(0, tc, LANES):
            ls = slice(l0, l0 + LANES)
            w = w_ref[:, ls]
            b = b_ref[:, ls]
            for r0 in range(0, tm, CONV_CHUNK):
                pre = _causal_conv(ext_ref, w, K, CONV_CHUNK, HALO + r0, ls) + b
                out = _gelu(pre) * u_ref[pl.ds(r0, CONV_CHUNK), ls].astype(F32) if gated else _silu(pre)
                o_ref[pl.ds(r0, CONV_CHUNK), ls] = out.astype(o_ref.dtype)

    in_specs = [
        pl.BlockSpec((tm, tc), lambda i, j: (i, j + c_in)),
        _prev_halo_spec(tm, tc, c_in),
        pl.BlockSpec((K, tc), lambda i, j: (0, j)),
        pl.BlockSpec((1, tc), lambda i, j: (0, j)),
    ]
    args = [src, src, conv_w, conv_b]
    if gated:
        in_specs.append(pl.BlockSpec((tm, tc), lambda i, j: (i, j + c_gate)))
        args.append(src)
    return pl.pallas_call(
        body,
        name=name,
        grid=(S // tm, C // tc),
        in_specs=in_specs,
        out_specs=pl.BlockSpec((tm, tc), lambda i, j: (i, j)),
        out_shape=jax.ShapeDtypeStruct((S, C), out_dtype),
        scratch_shapes=[pltpu.VMEM((tm + HALO, tc), F32)],
        compiler_params=_cparams("parallel", "parallel"),
    )(*args)


def conv_act_bwd(d_out, src, in_col, C, conv_w, conv_b, name, gate_col=None):
    S = src.shape[0]
    K = conv_w.shape[0]
    gated = gate_col is not None
    tm, tc = _conv_tiles(S, C, [in_col] + ([gate_col] if gated else []))
    c_in = in_col // tc
    c_gate = gate_col // tc if gated else 0
    ni = S // tm
    te = tm + HALO

    def body(*refs):
        if gated:
            (d_ref, dn_ref, x_ref, xp_ref, xn_ref, w_ref, b_ref, u_ref, un_ref,
             dx_ref, dw_ref, db_ref, du_ref, xext_ref, dext_ref) = refs
        else:
            (d_ref, dn_ref, x_ref, xp_ref, xn_ref, w_ref, b_ref,
             dx_ref, dw_ref, db_ref, xext_ref, dext_ref) = refs
        i = pl.program_id(1)
        xext_ref[0:HALO] = jnp.where(i > 0, xp_ref[...].astype(F32), 0.0)
        xext_ref[HALO:HALO + tm] = x_ref[...].astype(F32)
        xext_ref[HALO + tm:HALO + te] = xn_ref[...].astype(F32)

        @pl.when(i == 0)
        def _():
            dw_ref[...] = jnp.zeros_like(dw_ref)
            db_ref[...] = jnp.zeros_like(db_ref)

        for l0 in range(0, tc, LANES):
            ls = slice(l0, l0 + LANES)
            w = w_ref[:, ls]
            b = b_ref[:, ls]
            acc_w = [jnp.zeros((SUBLANES, LANES), F32) for _ in range(K)]
            acc_b = jnp.zeros((SUBLANES, LANES), F32)
            for r0 in range(0, te, CONV_CHUNK):
                n = min(CONV_CHUNK, te - r0)
                inside = r0 < tm
                taps = [xext_ref[pl.ds(HALO + r0 - (K - 1) + k, n), ls] for k in range(K)]
                pre = sum(t * w[k:k + 1, :] for k, t in enumerate(taps)) + b
                d = (d_ref[pl.ds(r0, n), ls] if inside else dn_ref[:, ls]).astype(F32)
                if gated:
                    u = (u_ref[pl.ds(r0, n), ls] if inside else un_ref[:, ls]).astype(F32)
                    cdf = 0.5 * (1.0 + lax.erf(pre * INV_SQRT2))
                    dpre = d * u * (cdf + pre * jnp.exp(-0.5 * pre * pre) * INV_SQRT_2PI)
                    if inside:
                        du_ref[pl.ds(r0, n), ls] = (d * (pre * cdf)).astype(du_ref.dtype)
                else:
                    dpre = d * _silu_grad(pre)
                if inside:
                    for k in range(K):
                        acc_w[k] = acc_w[k] + _fold8(dpre * taps[k])
                    acc_b = acc_b + _fold8(dpre)
                else:
                    dpre = jnp.where(i < ni - 1, dpre, 0.0)
                dext_ref[pl.ds(r0, n), ls] = dpre
            for r0 in range(0, tm, CONV_CHUNK):
                dx = _anticausal_conv(dext_ref, w, K, CONV_CHUNK, r0, ls)
                dx_ref[pl.ds(r0, CONV_CHUNK), ls] = dx.astype(dx_ref.dtype)
            dw_rows = [jnp.sum(a, axis=0, keepdims=True) for a in acc_w]
            dw_ref[:, ls] += jnp.concatenate(dw_rows + [jnp.zeros((SUBLANES - K, LANES), F32)], axis=0)
            db_ref[:, ls] += jnp.sum(acc_b, axis=0, keepdims=True)

    last = ni * (tm // HALO) - 1
    cur = lambda c0: pl.BlockSpec((tm, tc), lambda j, i: (i, j + c0))
    prev = lambda c0: pl.BlockSpec((HALO, tc), lambda j, i: (jnp.maximum(i * (tm // HALO) - 1, 0), j + c0))
    nxt = lambda c0: pl.BlockSpec((HALO, tc), lambda j, i: (jnp.minimum((i + 1) * (tm // HALO), last), j + c0))
    vec = lambda rows: pl.BlockSpec((rows, tc), lambda j, i: (0, j))
    in_specs = [cur(0), nxt(0), cur(c_in), prev(c_in), nxt(c_in), vec(K), vec(1)]
    args = [d_out, d_out, src, src, src, conv_w, conv_b]
    out_specs = [cur(0), vec(SUBLANES), vec(1)]
    out_shape = [jax.ShapeDtypeStruct((S, C), BF16), jax.ShapeDtypeStruct((SUBLANES, C), F32), jax.ShapeDtypeStruct((1, C), F32)]
    if gated:
        in_specs += [cur(c_gate), nxt(c_gate)]
        args += [src, src]
        out_specs.append(cur(0))
        out_shape.append(jax.ShapeDtypeStruct((S, C), BF16))
    outs = pl.pallas_call(
        body,
        name=name,
        grid=(C // tc, ni),
        in_specs=in_specs,
        out_specs=out_specs,
        out_shape=out_shape,
        scratch_shapes=[pltpu.VMEM((tm + 2 * HALO, tc), F32), pltpu.VMEM((te, tc), F32)],
        compiler_params=_cparams("parallel", "arbitrary"),
    )(*args)
    return outs[0], (outs[3] if gated else None), outs[1][:K], outs[2]


def ple_fwd(x2, zg, pp, bg, name):
    S, D = x2.shape
    tm = _pick(S, (512, 256, 128))

    def body(x_ref, z_ref, p_ref, b_ref, o_ref):
        o_ref[...] = x_ref[...] + _sigmoid(z_ref[...] + b_ref[...]) * p_ref[...]

    row = pl.BlockSpec((tm, D), lambda i: (i, 0))
    return pl.pallas_call(
        body, name=name, grid=(S // tm,), in_specs=[row, row, row, pl.BlockSpec((1, D), lambda i: (0, 0))], out_specs=row,
        out_shape=jax.ShapeDtypeStruct((S, D), F32), compiler_params=_cparams("parallel"),
    )(x2, zg, pp, bg)


def ple_bwd(dx3, zg, pp, bg, name):
    S, D = dx3.shape
    tm = _pick(S, (512, 256, 128))

    def body(d_ref, z_ref, p_ref, b_ref, dz_ref, dp_ref, db_ref):
        i = pl.program_id(0)
        d = d_ref[...]
        gate = _sigmoid(z_ref[...] + b_ref[...])
        dz = d * p_ref[...] * gate * (1.0 - gate)
        dz_ref[...] = dz.astype(dz_ref.dtype)
        dp_ref[...] = (d * gate).astype(dp_ref.dtype)
        part = jnp.sum(dz, axis=0, keepdims=True)

        @pl.when(i == 0)
        def _():
            db_ref[...] = part

        @pl.when(i > 0)
        def _():
            db_ref[...] += part

    row = pl.BlockSpec((tm, D), lambda i: (i, 0))
    vec = pl.BlockSpec((1, D), lambda i: (0, 0))
    return pl.pallas_call(
        body, name=name, grid=(S // tm,), in_specs=[row, row, row, vec], out_specs=[row, row, vec],
        out_shape=[jax.ShapeDtypeStruct((S, D), BF16), jax.ShapeDtypeStruct((S, D), BF16), jax.ShapeDtypeStruct((1, D), F32)],
        compiler_params=_cparams("arbitrary"),
    )(dx3, zg, pp, bg)


def loss_head(y, target, name):
    S, D = y.shape
    tm = _pick(S, (512, 256, 128))

    def body(y_ref, t_ref, loss_ref, dy_ref, acc_ref):
        i = pl.program_id(0)
        err = y_ref[...] - t_ref[...]
        dy_ref[...] = err * (1.0 / D)
        part = jnp.sum(err * err, axis=0, keepdims=True)

        @pl.when(i == 0)
        def _():
            acc_ref[...] = part

        @pl.when(i > 0)
        def _():
            acc_ref[...] += part

        @pl.when(i == pl.num_programs(0) - 1)
        def _():
            loss_ref[...] = jnp.zeros((1, LANES), F32) + (0.5 / D) * jnp.sum(acc_ref[...])

    row = pl.BlockSpec((tm, D), lambda i: (i, 0))
    return pl.pallas_call(
        body, name=name, grid=(S // tm,), in_specs=[row, row],
        out_specs=[pl.BlockSpec((1, LANES), lambda i: (0, 0)), row],
        out_shape=[jax.ShapeDtypeStruct((1, LANES), F32), jax.ShapeDtypeStruct((S, D), F32)],
        scratch_shapes=[pltpu.VMEM((1, D), F32)],
        compiler_params=_cparams("arbitrary"),
    )(y, target)


ATTN_TILE = 1024
ATTN_SCALE = 1.0 / math.sqrt(HEAD_DIM)


def _attn_tile(S):
    return _pick(S, (ATTN_TILE, 512, 256, 128))


def fox_gate_fwd(zt, bf, name):
    H, S = zt.shape
    tl = _pick(S, (512, 256, 128))

    def body(z_ref, b_ref, c_ref, carry_ref):
        i = pl.program_id(0)

        @pl.when(i == 0)
        def _():
            carry_ref[...] = jnp.zeros_like(carry_ref)

        z = z_ref[...] + b_ref[...]
        logf = jnp.minimum(z, 0.0) - _log1p_small(jnp.exp(-jnp.abs(z)))
        r = lax.broadcasted_iota(jnp.int32, (tl, tl), 0)
        c = lax.broadcasted_iota(jnp.int32, (tl, tl), 1)
        upper = (r <= c).astype(F32)
        cum = jnp.dot(logf, upper, precision=HIGHEST, preferred_element_type=F32) + carry_ref[...]
        c_ref[...] = cum
        carry_ref[...] = cum[:, tl - 1:tl]

    return pl.pallas_call(
        body, name=name, grid=(S // tl,),
        in_specs=[pl.BlockSpec((H, tl), lambda i: (0, i)), pl.BlockSpec((H, 1), lambda i: (0, 0))],
        out_specs=pl.BlockSpec((H, tl), lambda i: (0, i)),
        out_shape=jax.ShapeDtypeStruct((H, S), F32),
        scratch_shapes=[pltpu.VMEM((H, 1), F32)],
        compiler_params=_cparams("arbitrary"),
    )(zt, bf)


def fox_gate_bwd(dc_q, dc_k, zt, bf, name):
    H, S = zt.shape
    tl = _pick(S, (512, 256, 128))
    nt = S // tl

    def body(dcq_ref, dck_ref, z_ref, b_ref, dz_ref, db_ref, carry_ref):
        i = pl.program_id(0)

        @pl.when(i == 0)
        def _():
            carry_ref[...] = jnp.zeros_like(carry_ref)
            db_ref[...] = jnp.zeros_like(db_ref)

        r = lax.broadcasted_iota(jnp.int32, (tl, tl), 0)
        c = lax.broadcasted_iota(jnp.int32, (tl, tl), 1)
        lower = (r >= c).astype(F32)
        dc = dcq_ref[...] + dck_ref[...]
        suffix = jnp.dot(dc, lower, precision=HIGHEST, preferred_element_type=F32) + carry_ref[...]
        carry_ref[...] = suffix[:, 0:1]
        dz = suffix * _sigmoid(-(z_ref[...] + b_ref[...]))
        dz_ref[...] = dz
        db_ref[...] += jnp.sum(dz, axis=1, keepdims=True)

    rev = pl.BlockSpec((H, tl), lambda i: (0, nt - 1 - i))
    return pl.pallas_call(
        body, name=name, grid=(nt,),
        in_specs=[rev, rev, rev, pl.BlockSpec((H, 1), lambda i: (0, 0))],
        out_specs=[rev, pl.BlockSpec((H, 1), lambda i: (0, 0))],
        out_shape=[jax.ShapeDtypeStruct((H, S), F32), jax.ShapeDtypeStruct((H, 1), F32)],
        scratch_shapes=[pltpu.VMEM((H, 1), F32)],
        compiler_params=_cparams("arbitrary"),
    )(dc_q, dc_k, zt, bf)


ATTN_SUB = 4
ATTN_CHUNK = 32


def _row_tiles(ref, j, n_tiles):
    if n_tiles == 1:
        return ref[0, j]
    return jnp.concatenate([ref[0, j + t] for t in range(n_tiles)], axis=1)


def _diag_mask(n_rows, width, row0):
    r = lax.broadcasted_iota(jnp.int32, (n_rows, width), 0) + row0
    c = lax.broadcasted_iota(jnp.int32, (n_rows, width), 1)
    return r >= c


def _causal_sweep(i, process):
    def pair_body(j2, carry):
        process(2 * j2, 2, False)
        return carry

    lax.fori_loop(0, i // 2, pair_body, 0)

    @pl.when(i % 2 == 1)
    def _():
        process(i - 1, 2, True)

    @pl.when(i % 2 == 0)
    def _():
        process(i, 1, True)


def _ride_along(ride, refs, n_in, n_out, n_scratch, first, last):
    n = len(ride)
    ins, srcs = refs[:n_in], refs[n_in:n_in + n]
    outs, dsts = refs[n_in + n:n_in + n + n_out], refs[n_in + n + n_out:n_in + 2 * n + n_out]
    scratch, sems = refs[n_in + 2 * n + n_out:n_in + 2 * n + n_out + n_scratch], refs[n_in + 2 * n + n_out + n_scratch:]
    copies = _ride_copies(ride, srcs, dsts, sems) if n else []

    @pl.when(first)
    def _():
        for cp in copies:
            cp.start()

    def finish():
        @pl.when(last)
        def _():
            for cp in copies:
                cp.wait()

    return ins, outs, scratch, finish


def flash_fwd(q, k, v_ones, c_col, c_row, name, ride=()):
    H, S, Dh = q.shape
    T = _attn_tile(S)
    NT = S // T
    TS = T // ATTN_SUB
    ride_in, ride_out, ride_shape, ride_sems = _ride_decl(ride)

    def body(*refs):
        h, i = pl.program_id(0), pl.program_id(1)
        (q_ref, k_ref, v_ref, cq_ref, ck_ref), (o_ref, lse_ref), (m_ref, acc_ref), finish = _ride_along(
            ride, refs, 5, 2, 2, (h == 0) & (i == 0), (h == H - 1) & (i == NT - 1))
        qs = q_ref[0] * ATTN_SCALE
        c_ref = cq_ref[0, 0:1, :]
        m_ref[...] = jnp.full_like(m_ref, NEG_INF)
        acc_ref[...] = jnp.zeros_like(acc_ref)
        halves = [slice(u * TS, (u + 1) * TS) for u in range(ATTN_SUB)]

        def keys(j, n_tiles):
            return pl.ds(pl.multiple_of(j * T, T), n_tiles * T)

        def softmax_pv(j, n_tiles, masked, s_of):
            width = n_tiles * T
            vj = v_ref[0, keys(j, n_tiles), :]
            ckj = _row_tiles(ck_ref, j, n_tiles) - c_ref
            for u, rows in enumerate(halves):
                m_prev = m_ref[rows]
                ps, m_news = [], []
                for r0 in range(0, TS, ATTN_CHUNK):
                    rc = slice(r0, r0 + ATTN_CHUNK)
                    s = s_of(slice(u * TS + r0, u * TS + r0 + ATTN_CHUNK)) - ckj
                    if masked:
                        s = jnp.where(_diag_mask(ATTN_CHUNK, width, u * TS + r0 + width - T), s, NEG_INF)
                    m_new = jnp.maximum(m_prev[rc], jnp.max(s, axis=1, keepdims=True))
                    ps.append(jnp.exp(s - jnp.tile(m_new, (1, width // LANES))).astype(BF16))
                    m_news.append(m_new)
                m_new = jnp.concatenate(m_news, axis=0)
                acc_ref[rows] = jnp.exp(m_prev - m_new) * acc_ref[rows] + _dot(jnp.concatenate(ps, axis=0), vj)
                m_ref[rows] = m_new

        def process(j, n_tiles, masked):
            kj = k_ref[0, keys(j, n_tiles), :]
            s = jnp.concatenate([_dot_nt(qs[rows], kj) for rows in halves], axis=0)
            softmax_pv(j, n_tiles, masked, lambda rc: s[rc])

        _causal_sweep(i, process)
        acc = acc_ref[...]
        l = acc[:, Dh:Dh + 1]
        o_ref[0] = (acc[:, 0:Dh] / l).astype(o_ref.dtype)
        lse_ref[0] = m_ref[:, 0:1] + jnp.log(l) + (cq_ref[0] - c_ref)
        finish()

    tile = pl.BlockSpec((1, T, Dh), lambda h, i: (h, i, 0))
    whole = pl.BlockSpec((1, S, Dh), lambda h, i: (h, 0, 0))
    whole_v = pl.BlockSpec((1, S, 2 * Dh), lambda h, i: (h, 0, 0))
    col = pl.BlockSpec((1, T, 1), lambda h, i: (h, i, 0))
    rows = pl.BlockSpec((1, NT, 1, T), lambda h, i: (h, 0, 0, 0))
    return pl.pallas_call(
        body, name=name, grid=(H, NT),
        in_specs=[tile, whole, whole_v, col, rows] + ride_in,
        out_specs=[tile, col] + ride_out,
        out_shape=[jax.ShapeDtypeStruct((H, S, Dh), BF16), jax.ShapeDtypeStruct((H, S, 1), F32)] + ride_shape,
        scratch_shapes=[pltpu.VMEM((T, LANES), F32), pltpu.VMEM((T, 2 * Dh), F32)] + ride_sems,
        compiler_params=_cparams("arbitrary", "arbitrary"),
    )(q, k, v_ones, c_col, c_row, *[a for a, _ in ride])


def flash_bwd_q(q, k, v, o, do, q_t, do_t, lse, c_col, c_row, name, ride=()):
    H, S, Dh = q.shape
    T = _attn_tile(S)
    NT = S // T
    TS = T // ATTN_SUB

    ride_in, ride_out, ride_shape, ride_sems = _ride_decl(ride)

    def body(*refs):
        head, i = pl.program_id(0), pl.program_id(1)
        ((q_ref, k_ref, v_ref, o_ref, do_ref, qt_ref, dot_ref, lse_ref, cq_ref, ck_ref),
         (dq_ref, dcq_ref, dkt_ref, dvt_ref, dck_ref), (dq_acc, dcq_acc), finish) = _ride_along(
            ride, refs, 10, 5, 2, (head == 0) & (i == 0), (head == H - 1) & (i == NT - 1))

        @pl.when(i == 0)
        def _():
            dkt_ref[...] = jnp.zeros_like(dkt_ref)
            dvt_ref[...] = jnp.zeros_like(dvt_ref)
            dck_ref[...] = jnp.zeros_like(dck_ref)

        qs = q_ref[0] * ATTN_SCALE
        do = do_ref[0]
        qs_t = qt_ref[0] * ATTN_SCALE
        do_t = dot_ref[0]
        delta = jnp.sum(do.astype(F32) * o_ref[0].astype(F32), axis=1, keepdims=True)
        bias = cq_ref[0] - lse_ref[0]
        dq_acc[...] = jnp.zeros_like(dq_acc)
        dcq_acc[...] = jnp.zeros_like(dcq_acc)

        def process(j, n_tiles, masked):
            width = n_tiles * T
            off = pl.multiple_of(j * T, T)
            kj = k_ref[0, pl.ds(off, width), :]
            vj = v_ref[0, pl.ds(off, width), :]
            ck = _row_tiles(ck_ref, j, n_tiles)
            halves = [slice(u * TS, (u + 1) * TS) for u in range(ATTN_SUB)]
            ss = [_dot_nt(qs[h], kj) for h in halves]
            dps = [_dot_nt(do[h], vj) for h in halves]
            dkt, dvt = [], []
            dck8 = jnp.zeros((SUBLANES, width), F32)
            for u, h in enumerate(halves):
                ps, dss, rowsums = [], [], []
                for r0 in range(0, TS, ATTN_CHUNK):
                    rc = slice(r0, r0 + ATTN_CHUNK)
                    p = jnp.exp(ss[u][rc] + (bias[h][rc] - ck))
                    if masked:
                        p = jnp.where(_diag_mask(ATTN_CHUNK, width, u * TS + r0 + width - T), p, 0.0)
                    ds = p * (dps[u][rc] - delta[h][rc])
                    ps.append(p.astype(BF16))
                    dss.append(ds.astype(BF16))
                    rowsums.append(jnp.sum(ds, axis=1, keepdims=True))
                    for g in range(0, ATTN_CHUNK, SUBLANES):
                        dck8 = dck8 + ds[g:g + SUBLANES]
                p, ds = jnp.concatenate(ps, axis=0), jnp.concatenate(dss, axis=0)
                dq_acc[h] += _dot(ds, kj)
                dcq_acc[h] += jnp.concatenate(rowsums, axis=0)
                dvt.append(_dot(do_t[:, h], p))
                dkt.append(_dot(qs_t[:, h], ds))
            dvt, dkt, dck = sum(dvt), sum(dkt), jnp.sum(dck8, axis=0, keepdims=True)
            for t in range(n_tiles):
                cols = slice(t * T, (t + 1) * T)
                dvt_ref[0, j + t] += dvt[:, cols]
                dkt_ref[0, j + t] += dkt[:, cols]
                dck_ref[0, j + t] -= dck[:, cols]

        _causal_sweep(i, process)
        dq_ref[0] = (dq_acc[...] * ATTN_SCALE).astype(dq_ref.dtype)
        dcq_ref[0] = dcq_acc[...]
        finish()

    tile = pl.BlockSpec((1, T, Dh), lambda h, i: (h, i, 0))
    tile_t = pl.BlockSpec((1, Dh, T), lambda h, i: (h, 0, i))
    whole = pl.BlockSpec((1, S, Dh), lambda h, i: (h, 0, 0))
    col = pl.BlockSpec((1, T, 1), lambda h, i: (h, i, 0))
    rows = pl.BlockSpec((1, NT, 1, T), lambda h, i: (h, 0, 0, 0))
    acc_t = pl.BlockSpec((1, NT, Dh, T), lambda h, i: (h, 0, 0, 0))
    return pl.pallas_call(
        body, name=name, grid=(H, NT),
        in_specs=[tile, whole, whole, tile, tile, tile_t, tile_t, col, col, rows] + ride_in,
        out_specs=[tile, col, acc_t, acc_t, rows] + ride_out,
        out_shape=[jax.ShapeDtypeStruct((H, S, Dh), BF16), jax.ShapeDtypeStruct((H, S, 1), F32),
                   jax.ShapeDtypeStruct((H, NT, Dh, T), F32), jax.ShapeDtypeStruct((H, NT, Dh, T), F32),
                   jax.ShapeDtypeStruct((H, NT, 1, T), F32)] + ride_shape,
        scratch_shapes=[pltpu.VMEM((T, Dh), F32), pltpu.VMEM((T, 1), F32)] + ride_sems,
        compiler_params=_cparams("arbitrary", "arbitrary"),
    )(q, k, v, o, do, q_t, do_t, lse, c_col, c_row, *[a for a, _ in ride])


PAIR = 2 * HEAD_DIM


def _tri(n, lower):
    r = lax.broadcasted_iota(jnp.int32, (n, n), 0)
    c = lax.broadcasted_iota(jnp.int32, (n, n), 1)
    return (r >= c) if lower else (r <= c)


def _ssd_chunk_scalars(dt_raw, bias, a_log):
    Q = dt_raw.shape[0]
    dt = _softplus(dt_raw + bias)
    A = -jnp.exp(a_log)
    cum = jnp.dot(_tri(Q, True).astype(F32), dt * A, precision=HIGHEST, preferred_element_type=F32)
    tot = cum[Q - 1:Q, :]
    return dt, A, cum, tot


def _lane_pair(lo_mask, v, h0):
    return jnp.where(lo_mask, v[:, h0:h0 + 1], v[:, h0 + 1:h0 + 2])


def _head_expand(d_inner):
    return (jnp.arange(d_inner)[None, :] // HEAD_DIM == jnp.arange(LANES)[:, None]).astype(BF16)


def _split_dot(x, e, dims=None):
    hi = x.astype(BF16)
    lo = (x - hi.astype(F32)).astype(BF16)
    if dims is None:
        return jnp.dot(hi, e, preferred_element_type=F32) + jnp.dot(lo, e, preferred_element_type=F32)
    return (lax.dot_general(hi, e, dims, preferred_element_type=F32)
            + lax.dot_general(lo, e, dims, preferred_element_type=F32))


def ssd_scan_fwd(xbc, proj, dt_col, dt_bias, a_log, d_inner, name):
    S, W = xbc.shape
    Q, N, G = SSM_CHUNK, SSM_STATE, SSM_GROUPS
    nc = S // Q
    n_pairs = d_inner // PAIR
    pairs_per_group = n_pairs // G
    GN = G * N

    def body(xbc_ref, dt_ref, bias_ref, alog_ref, y_ref, sin_ref, st_ref):
        c = pl.program_id(0)

        @pl.when(c == 0)
        def _():
            st_ref[...] = jnp.zeros_like(st_ref)

        sin_ref[0] = st_ref[...]
        dt, A, cum, tot = _ssd_chunk_scalars(dt_ref[...], bias_ref[...], alog_ref[...])
        cum_t = cum.T
        dt_t = dt.T
        ecum = jnp.exp(cum)
        wend = jnp.exp(tot - cum) * dt
        etot = jnp.exp(tot)
        lower = _tri(Q, True)
        lo = lax.broadcasted_iota(jnp.int32, (Q, PAIR), 1) < HEAD_DIM
        lo_row = lax.broadcasted_iota(jnp.int32, (1, PAIR), 1) < HEAD_DIM
        for g in range(G):
            Bg = xbc_ref[:, d_inner + g * N:d_inner + (g + 1) * N]
            Cg = xbc_ref[:, d_inner + GN + g * N:d_inner + GN + (g + 1) * N]
            CB = _dot_nt(Cg, Bg)
            for pp in range(pairs_per_group):
                pr = g * pairs_per_group + pp
                h0 = 2 * pr
                xw = xbc_ref[:, pr * PAIR:(pr + 1) * PAIR]
                ys = []
                for h in (h0, h0 + 1):
                    L = jnp.where(lower, jnp.exp(cum[:, h:h + 1] - cum_t[h:h + 1, :]), 0.0)
                    ys.append(_dot(CB * L * dt_t[h:h + 1, :], xw))
                st = st_ref[pr]
                y_inter = _dot(Cg, st) * _lane_pair(lo, ecum, h0)
                y_ref[:, pr * PAIR:(pr + 1) * PAIR] = jnp.where(lo, ys[0], ys[1]) + y_inter
                st_ref[pr] = _lane_pair(lo_row, etot, h0) * st + _dot_tn(Bg, xw * _lane_pair(lo, wend, h0))

    return pl.pallas_call(
        body, name=name, grid=(nc,),
        in_specs=[pl.BlockSpec((Q, W), lambda c: (c, 0)), pl.BlockSpec((Q, LANES), lambda c: (c, dt_col // LANES)),
                  pl.BlockSpec((1, LANES), lambda c: (0, 0)), pl.BlockSpec((1, LANES), lambda c: (0, 0))],
        out_specs=[pl.BlockSpec((Q, d_inner), lambda c: (c, 0)), pl.BlockSpec((1, n_pairs, N, PAIR), lambda c: (c, 0, 0, 0))],
        out_shape=[jax.ShapeDtypeStruct((S, d_inner), F32), jax.ShapeDtypeStruct((nc, n_pairs, N, PAIR), F32)],
        scratch_shapes=[pltpu.VMEM((n_pairs, N, PAIR), F32)],
        compiler_params=_cparams("arbitrary"),
    )(xbc, proj, dt_bias, a_log)


def ssd_scan_bwd(dy, dskip, xbc, proj, dt_col, dt_bias, a_log, states, expand, d_inner, name):
    S, W = xbc.shape
    Q, N, G = SSM_CHUNK, SSM_STATE, SSM_GROUPS
    nc = S // Q
    n_pairs = d_inner // PAIR
    pairs_per_group = n_pairs // G
    GN = G * N

    def body(dy_ref, dskip_ref, xbc_ref, dt_ref, bias_ref, alog_ref, sin_ref, e_ref,
             dxbc_ref, ddt_ref, dalog_ref, dbias_ref, dst_ref, rows_ref):
        step = pl.program_id(0)

        @pl.when(step == 0)
        def _():
            dst_ref[...] = jnp.zeros_like(dst_ref)
            dalog_ref[...] = jnp.zeros_like(dalog_ref)
            dbias_ref[...] = jnp.zeros_like(dbias_ref)

        dt_raw = dt_ref[...]
        bias = bias_ref[...]
        dt, A, cum, tot = _ssd_chunk_scalars(dt_raw, bias, alog_ref[...])
        cum_t = cum.T
        ecum = jnp.exp(cum)
        eend = jnp.exp(tot - cum)
        wend = eend * dt
        etot = jnp.exp(tot)
        e = e_ref[...]
        wide = _split_dot(jnp.concatenate([wend, dt, jnp.broadcast_to(etot, (SUBLANES, LANES))], axis=0), e)
        wend_w, dt_w, etot_w = wide[0:Q], wide[Q:2 * Q], wide[2 * Q:2 * Q + 1]
        ecum_w = _dot(ecum, e)
        upper = _tri(Q, False)
        lane = lax.broadcasted_iota(jnp.int32, (Q, LANES), 1)
        lo = lane < HEAD_DIM
        ones_q = jnp.ones((Q, LANES), BF16)
        ones_8 = jnp.ones((SUBLANES, Q), BF16)
        rows_ref[...] = jnp.zeros_like(rows_ref)
        dcum_src = jnp.zeros((Q, LANES), F32)
        xz_parts, xbds_parts, dss_parts, dyy2_parts = [], [], [], []
        for g in range(G):
            Bg = xbc_ref[:, d_inner + g * N:d_inner + (g + 1) * N]
            Cg = xbc_ref[:, d_inner + GN + g * N:d_inner + GN + (g + 1) * N]
            Cg_t = Cg.T
            CBt = _dot_nt(Bg, Cg)
            dCBt = jnp.zeros((Q, Q), F32)
            dBg = jnp.zeros((Q, N), F32)
            dCg = jnp.zeros((Q, N), F32)
            for pp in range(pairs_per_group):
                pr = g * pairs_per_group + pp
                h0 = 2 * pr
                sl = slice(pr * PAIR, (pr + 1) * PAIR)
                xw = xbc_ref[:, sl]
                dyp = dy_ref[:, sl]
                st = sin_ref[0, pr]
                dst = dst_ref[pr]
                wend_p = wend_w[:, sl]
                dt_p = dt_w[:, sl]
                dye = dyp * ecum_w[:, sl]
                dyy2_parts.append(dye * _dot(Cg, st))
                dCg = dCg + _dot_nt(dye, st)
                bds = _dot(Bg, dst)
                dBg = dBg + _dot_nt(xw * wend_p, dst)
                xbds_parts.append(xw * bds)
                dss_parts.append(dst * st)
                xdt = xw * dt_p
                z = None
                for h, half in ((h0, lo), (h0 + 1, ~lo)):
                    Lt = jnp.where(upper, jnp.exp(cum_t[h:h + 1, :] - cum[:, h:h + 1]), 0.0)
                    dyh = jnp.where(half, dyp, 0.0)
                    CBLt = CBt * Lt
                    zh = _dot(CBLt, dyh)
                    z = zh if z is None else z + zh
                    dMt = _dot_nt(xdt, dyh)
                    dCBt = dCBt + dMt * Lt
                    gm = (dMt * CBLt).astype(BF16)
                    dcum_src = dcum_src + jnp.where(lane == h, jnp.dot(gm, ones_q, preferred_element_type=F32), 0.0)
                    rows_ref[h:h + 1, :] = jnp.dot(ones_8, gm, preferred_element_type=F32)[0:1]
                xz_parts.append(xw * z)
                dxbc_ref[:, sl] = dt_p * z + wend_p * bds + dskip_ref[:, sl]
                dst_ref[pr] = _dot(Cg_t, dye) + etot_w[:, sl] * dst
            dxbc_ref[:, d_inner + g * N:d_inner + (g + 1) * N] = dBg + _dot(dCBt, Cg)
            dxbc_ref[:, d_inner + GN + g * N:d_inner + GN + (g + 1) * N] = dCg + _dot_tn(dCBt, Bg)
        head_sums = lambda wide_arr: _split_dot(wide_arr, e, NT_DIMS)
        dy_y2 = head_sums(jnp.concatenate(dyy2_parts, axis=1))
        x_z = head_sums(jnp.concatenate(xz_parts, axis=1))
        q = head_sums(jnp.concatenate(xbds_parts, axis=1))
        dst_st = jnp.sum(head_sums(jnp.concatenate(dss_parts, axis=1)), axis=0, keepdims=True)
        wq = wend * q
        last_row = lax.broadcasted_iota(jnp.int32, (Q, LANES), 0) == Q - 1
        dcum = (dy_y2 - wq + (rows_ref[...].T - dcum_src)
                + jnp.where(last_row, etot * dst_st + jnp.sum(wq, axis=0, keepdims=True), 0.0))
        ddt = eend * q + x_z
        da = jnp.dot(upper.astype(F32), dcum, precision=HIGHEST, preferred_element_type=F32)
        ddt = ddt + da * A
        ddt_raw = ddt * _sigmoid(dt_raw + bias)
        ddt_ref[...] = ddt_raw
        dalog_ref[...] += jnp.sum(da * dt, axis=0, keepdims=True) * A
        dbias_ref[...] += jnp.sum(ddt_raw, axis=0, keepdims=True)

    rev = lambda width, col: pl.BlockSpec((Q, width), lambda s: (nc - 1 - s, col))
    vec = pl.BlockSpec((1, LANES), lambda s: (0, 0))
    return pl.pallas_call(
        body, name=name, grid=(nc,),
        in_specs=[rev(d_inner, 0), rev(d_inner, 0), rev(W, 0), rev(LANES, dt_col // LANES), vec, vec,
                  pl.BlockSpec((1, n_pairs, N, PAIR), lambda s: (nc - 1 - s, 0, 0, 0)),
                  pl.BlockSpec((LANES, d_inner), lambda s: (0, 0))],
        out_specs=[rev(W, 0), rev(LANES, 0), vec, vec],
        out_shape=[jax.ShapeDtypeStruct((S, W), F32), jax.ShapeDtypeStruct((S, LANES), F32),
                   jax.ShapeDtypeStruct((1, LANES), F32), jax.ShapeDtypeStruct((1, LANES), F32)],
        scratch_shapes=[pltpu.VMEM((n_pairs, N, PAIR), F32), pltpu.VMEM((LANES, Q), F32)],
        compiler_params=_cparams("arbitrary"),
    )(dy, dskip, xbc, proj, dt_bias, a_log, states, expand)


def ssd_gate_fwd(y, xbc, proj, d_skip, norm_w, d_inner, name):
    S = y.shape[0]
    tm = _pick(S, (256, 128))
    gs = d_inner // SSM_GROUPS

    def body(y_ref, x_ref, z_ref, d_ref, w_ref, o_ref):
        for g in range(SSM_GROUPS):
            sl = slice(g * gs, (g + 1) * gs)
            y2 = (y_ref[:, sl] + d_ref[:, sl] * x_ref[:, sl]) * _silu(z_ref[:, sl].astype(F32))
            r = lax.rsqrt(jnp.mean(y2 * y2, axis=-1, keepdims=True) + RMS_EPS)
            o_ref[:, sl] = (y2 * r * w_ref[:, sl]).astype(o_ref.dtype)

    row = pl.BlockSpec((tm, d_inner), lambda i: (i, 0))
    vec = pl.BlockSpec((1, d_inner), lambda i: (0, 0))
    return pl.pallas_call(
        body, name=name, grid=(S // tm,), in_specs=[row, row, row, vec, vec], out_specs=row,
        out_shape=jax.ShapeDtypeStruct((S, d_inner), BF16), compiler_params=_cparams("parallel"),
    )(y, xbc, proj, d_skip, norm_w)


def ssd_gate_bwd(dyn, y, xbc, proj, d_skip, norm_w, d_inner, name):
    S = y.shape[0]
    tm = _pick(S, (256, 128))
    gs = d_inner // SSM_GROUPS

    def body(dyn_ref, y_ref, x_ref, z_ref, d_ref, w_ref, dy_ref, dskip_ref, dz_ref, dw_ref, dd_ref):
        i = pl.program_id(0)

        @pl.when(i == 0)
        def _():
            dw_ref[...] = jnp.zeros_like(dw_ref)
            dd_ref[...] = jnp.zeros_like(dd_ref)

        for g in range(SSM_GROUPS):
            sl = slice(g * gs, (g + 1) * gs)
            z = z_ref[:, sl].astype(F32)
            x = x_ref[:, sl]
            sz = _silu(z)
            ysum = y_ref[:, sl] + d_ref[:, sl] * x
            y2 = ysum * sz
            r = lax.rsqrt(jnp.mean(y2 * y2, axis=-1, keepdims=True) + RMS_EPS)
            dyn = dyn_ref[:, sl].astype(F32)
            a = dyn * w_ref[:, sl]
            dy2 = r * a - y2 * (r * r * r) * jnp.mean(a * y2, axis=-1, keepdims=True)
            dysum = dy2 * sz
            dy_ref[:, sl] = dysum
            dskip_ref[:, sl] = dysum * d_ref[:, sl]
            dz_ref[:, sl] = (dy2 * ysum * _silu_grad(z)).astype(dz_ref.dtype)
            dw_ref[:, sl] += jnp.sum(dyn * y2 * r, axis=0, keepdims=True)
            dd_ref[:, sl] += jnp.sum(dysum * x, axis=0, keepdims=True)

    row = pl.BlockSpec((tm, d_inner), lambda i: (i, 0))
    vec = pl.BlockSpec((1, d_inner), lambda i: (0, 0))
    return pl.pallas_call(
        body, name=name, grid=(S // tm,), in_specs=[row, row, row, row, vec, vec], out_specs=[row, row, row, vec, vec],
        out_shape=[jax.ShapeDtypeStruct((S, d_inner), F32), jax.ShapeDtypeStruct((S, d_inner), F32),
                   jax.ShapeDtypeStruct((S, d_inner), BF16), jax.ShapeDtypeStruct((1, d_inner), F32),
                   jax.ShapeDtypeStruct((1, d_inner), F32)],
        compiler_params=_cparams("arbitrary"),
    )(dyn, y, xbc, proj, d_skip, norm_w)


def _pad_to(a, axis, mult=LANES):
    n = a.shape[axis]
    extra = (-n) % mult
    if extra == 0:
        return a
    widths = [(0, 0)] * a.ndim
    widths[axis] = (0, extra)
    return jnp.pad(a, widths)


def _attn_fwd(x, w_in, b_f, w_out, tag, ride=()):
    S, D = x.shape
    H = D // HEAD_DIM
    T = _attn_tile(S)
    qkv = mm_nn(x, w_in[:, :3 * D], BF16, f"{tag}_proj").reshape(S, 3, H, HEAD_DIM).transpose(1, 2, 0, 3)
    zt = mm_nn(x, _pad_to(w_in[:, 3 * D:], 1), F32, f"{tag}_proj_f")[:, :H].T
    bf = b_f.reshape(H, 1)
    c = fox_gate_fwd(zt, bf, f"{tag}_gate")
    c_col, c_row = c.reshape(H, S, 1), c.reshape(H, S // T, 1, T)
    v_ones = jnp.concatenate([qkv[2], jnp.ones_like(qkv[2])], axis=-1)
    o, lse, *ride_out = flash_fwd(qkv[0], qkv[1], v_ones, c_col, c_row, f"{tag}_flash", ride)
    o_flat = o.transpose(1, 0, 2).reshape(S, D)
    mix = mm_nn(o_flat, w_out, F32, f"{tag}_out")
    return mix, (qkv, zt, bf, c_col, c_row, o, lse, o_flat), ride_out


def _attn_bwd(x, dmix, dx_add, alpha, w_in, w_out, saved, tag, ride=()):
    S, D = x.shape
    H = D // HEAD_DIM
    T = _attn_tile(S)
    qkv, zt, bf, c_col, c_row, o, lse, o_flat = saved
    g_w_out = mm_tn(o_flat, dmix, f"{tag}_gwout")
    do = mm_nn(dmix, w_out.T, BF16, f"{tag}_do").reshape(S, H, HEAD_DIM)
    dq, dc_q, dk_t, dv_t, dc_k, *ride_out = flash_bwd_q(
        qkv[0], qkv[1], qkv[2], o, do.transpose(1, 0, 2), qkv[0].transpose(0, 2, 1), do.transpose(1, 2, 0), lse, c_col, c_row,
        f"{tag}_flash_bwd", ride)
    dzt, dbf = fox_gate_bwd(dc_q.reshape(H, S), dc_k.reshape(H, S), zt, bf, f"{tag}_gate_bwd")
    keys_major = lambda t: t.transpose(1, 3, 0, 2).reshape(S, D).astype(BF16)
    dqkv = jnp.concatenate([dq.transpose(1, 0, 2).reshape(S, D), keys_major(dk_t), keys_major(dv_t)], axis=1)
    dzf = _pad_to(dzt.T, 1)
    g_w_in = jnp.concatenate([mm_tn(x, dqkv, f"{tag}_gwqkv"), mm_tn(x, dzf, f"{tag}_gwf")[:, :H]], axis=1)
    w_in_t = w_in.T
    dx = mm_nn(dqkv, w_in_t[:3 * D], F32, f"{tag}_dx_qkv", add=dx_add, add_scale=alpha)
    dx = mm_nn(dzf, _pad_to(w_in_t[3 * D:], 0), F32, f"{tag}_dx_f", add=dx)
    return dx, (g_w_in, dbf.reshape(H), g_w_out), ride_out


def _ssm_dims(D):
    d_inner = 2 * D
    gn = SSM_GROUPS * SSM_STATE
    return d_inner, d_inner + 2 * gn, d_inner // HEAD_DIM


def _ssm_fwd(x, w_in, conv_w, conv_b, dt_bias, a_log, d_skip, norm_w, w_out, tag):
    S, D = x.shape
    DI, XBC, HS = _ssm_dims(D)
    dt_col = DI + XBC
    proj = mm_nn(x, w_in[:, :dt_col], BF16, f"{tag}_proj")
    dt_raw = mm_nn(x, _pad_to(w_in[:, dt_col:], 1), F32, f"{tag}_proj_dt")
    conv_b = conv_b.reshape(1, XBC)
    xbc = conv_act_fwd(proj, DI, XBC, conv_w, conv_b, F32, f"{tag}_conv")
    dt_bias_p = _pad_to(dt_bias.reshape(1, HS), 1)
    a_log_p = _pad_to(a_log.reshape(1, HS), 1)
    y, states = ssd_scan_fwd(xbc, dt_raw, 0, dt_bias_p, a_log_p, DI, f"{tag}_scan")
    d_vec = jnp.repeat(d_skip, HEAD_DIM).reshape(1, DI)
    norm_w = norm_w.reshape(1, DI)
    yn = ssd_gate_fwd(y, xbc, proj, d_vec, norm_w, DI, f"{tag}_gate")
    mix = mm_nn(yn, w_out, F32, f"{tag}_out")
    return mix, (proj, dt_raw, xbc, conv_b, dt_bias_p, a_log_p, y, states, d_vec, norm_w, yn)


def _ssm_bwd(x, dmix, dx_add, alpha, w_in, conv_w, w_out, saved, tag):
    S, D = x.shape
    DI, XBC, HS = _ssm_dims(D)
    dt_col = DI + XBC
    proj, dt_raw, xbc, conv_b, dt_bias_p, a_log_p, y, states, d_vec, norm_w, yn = saved
    g_w_out = mm_tn(yn, dmix, f"{tag}_gwout")
    dyn = mm_nn(dmix, w_out.T, BF16, f"{tag}_dyn")
    dy, dskip, dz, g_norm_w, g_dvec = ssd_gate_bwd(dyn, y, xbc, proj, d_vec, norm_w, DI, f"{tag}_gate_bwd")
    dxbc, ddt_raw, g_a_log, g_dt_bias = ssd_scan_bwd(dy, dskip, xbc, dt_raw, 0, dt_bias_p, a_log_p, states,
                                                     _head_expand(DI), DI, f"{tag}_scan_bwd")
    dxbc_raw, _, g_conv_w, g_conv_b = conv_act_bwd(dxbc, proj, DI, XBC, conv_w, conv_b, f"{tag}_conv_bwd")
    g_w_in = jnp.concatenate([mm_tn(x, dz, f"{tag}_gwz"), mm_tn(x, dxbc_raw, f"{tag}_gwxbc"),
                              mm_tn(x, ddt_raw, f"{tag}_gwdt")[:, :HS]], axis=1)
    w_in_t = w_in.T
    dx = mm_nn(dz, w_in_t[:DI], F32, f"{tag}_dx_z", add=dx_add, add_scale=alpha)
    dx = mm_nn(dxbc_raw, w_in_t[DI:dt_col], F32, f"{tag}_dx_xbc", add=dx)
    dx = mm_nn(ddt_raw, _pad_to(w_in_t[dt_col:], 0), F32, f"{tag}_dx_dt", add=dx)
    g_d = g_dvec.reshape(HS, HEAD_DIM).sum(axis=-1)
    grads = (g_w_in, g_conv_w, g_conv_b.reshape(XBC), g_dt_bias[0, :HS], g_a_log[0, :HS], g_d, g_norm_w.reshape(DI), g_w_out)
    return dx, grads


ATTN_KEYS = ("attn_w_in", "attn_b_f", "attn_w_out")
SSM_KEYS = ("ssm_w_in", "ssm_conv_w", "ssm_conv_b", "ssm_dt_bias", "ssm_A_log", "ssm_D", "ssm_norm_w", "ssm_w_out")
LAYER_KEYS = ("ln_mix_g", "ln_mix_b", "ffn_w_up", "ffn_conv_w", "ffn_conv_b", "ffn_w_down", "ln_ffn_g", "ln_ffn_b",
              "ple_w_proj", "ple_w_gate", "ple_b_gate")


def local_step(x, p, target, w, fwd_ride=(), after_fwd_ride=None, before_last_bwd=None):
    S, D = x.shape
    depth = p.shape[0]
    alpha = (2 * depth) ** 0.25
    saved = []
    bwd_ride_out = []
    for i in range(depth):
        j, tag = i // 2, f"l{i}"
        if i % 2 == 0:
            mix, msaved, ride_out = _attn_fwd(x, w["attn_w_in"][j], w["attn_b_f"][j], w["attn_w_out"][j], tag + "_attn",
                                              fwd_ride if i == 0 else ())
            if i == 0 and after_fwd_ride is not None:
                w = {**w, **after_fwd_ride(ride_out)}
        else:
            mix, msaved = _ssm_fwd(x, w["ssm_w_in"][j], w["ssm_conv_w"][j], w["ssm_conv_b"][j], w["ssm_dt_bias"][j],
                                   w["ssm_A_log"][j], w["ssm_D"][j], w["ssm_norm_w"][j], w["ssm_w_out"][j], tag + "_ssm")
        row = lambda k: w[k][i].reshape(1, -1)
        F = w["ffn_w_down"].shape[1]
        x1, xhat1, rstd1 = ln_fwd(x, mix, row("ln_mix_g"), row("ln_mix_b"), alpha, tag + "_ln_mix")
        h = mm_nn(x1, w["ffn_w_up"][i], BF16, tag + "_ffn_up")
        a = conv_act_fwd(h, F, F, w["ffn_conv_w"][i], row("ffn_conv_b"), BF16, tag + "_ffn_act", gate_col=0)
        ffn = mm_nn(a, w["ffn_w_down"][i], F32, tag + "_ffn_down")
        x2, xhat2, rstd2 = ln_fwd(x1, ffn, row("ln_ffn_g"), row("ln_ffn_b"), alpha, tag + "_ln_ffn")
        zg = mm_nn(x2, w["ple_w_gate"][i], F32, tag + "_ple_gate")
        pp = mm_nn(p[i], w["ple_w_proj"][i], F32, tag + "_ple_proj")
        x3 = ple_fwd(x2, zg, pp, row("ple_b_gate"), tag + "_ple")
        saved.append((x, msaved, x1, xhat1, rstd1, h, a, x2, xhat2, rstd2, zg, pp))
        x = x3

    loss_vec, d = loss_head(x, target, "loss_head")

    grads = {k: [None] * w[k].shape[0] for k in ATTN_KEYS + SSM_KEYS + LAYER_KEYS}
    for i in reversed(range(depth)):
        j, tag = i // 2, f"l{i}"
        x0, msaved, x1, xhat1, rstd1, h, a, x2, xhat2, rstd2, zg, pp = saved[i]
        row = lambda k: w[k][i].reshape(1, -1)
        dzg, dpp, g_bg = ple_bwd(d, zg, pp, row("ple_b_gate"), tag + "_ple_bwd")
        grads["ple_w_gate"][i] = mm_tn(x2, dzg, tag + "_gw_ple_gate")
        grads["ple_w_proj"][i] = mm_tn(p[i], dpp, tag + "_gw_ple_proj")
        grads["ple_b_gate"][i] = g_bg.reshape(D)
        dx2 = mm_nn(dzg, w["ple_w_gate"][i].T, F32, tag + "_dx2", add=d)
        dr2, g_g2, g_b2 = ln_bwd(dx2, xhat2, rstd2, row("ln_ffn_g"), tag + "_ln_ffn_bwd")
        grads["ln_ffn_g"][i], grads["ln_ffn_b"][i] = g_g2.reshape(D), g_b2.reshape(D)
        grads["ffn_w_down"][i] = mm_tn(a, dr2, tag + "_gw_down")
        da = mm_nn(dr2, w["ffn_w_down"][i].T, BF16, tag + "_da")
        dgin, du, g_cw, g_cb = conv_act_bwd(da, h, F, F, w["ffn_conv_w"][i], row("ffn_conv_b"), tag + "_ffn_act_bwd", gate_col=0)
        grads["ffn_conv_w"][i], grads["ffn_conv_b"][i] = g_cw, g_cb.reshape(F)
        grads["ffn_w_up"][i] = jnp.concatenate([mm_tn(x1, du, tag + "_gw_up_u"), mm_tn(x1, dgin, tag + "_gw_up_g")], axis=1)
        w_up_t = w["ffn_w_up"][i].T
        dx1 = mm_nn(du, w_up_t[:F], F32, tag + "_dx1_u", add=dr2, add_scale=alpha)
        dx1 = mm_nn(dgin, w_up_t[F:], F32, tag + "_dx1_g", add=dx1)
        dr1, g_g1, g_b1 = ln_bwd(dx1, xhat1, rstd1, row("ln_mix_g"), tag + "_ln_mix_bwd")
        grads["ln_mix_g"][i], grads["ln_mix_b"][i] = g_g1.reshape(D), g_b1.reshape(D)
        if i % 2 == 0:
            ride = before_last_bwd(grads) if (i == 0 and before_last_bwd is not None) else ()
            d, mg, ride_out = _attn_bwd(x0, dr1, dr1, alpha, w["attn_w_in"][j], w["attn_w_out"][j], msaved, tag + "_attn", ride)
            if i == 0:
                bwd_ride_out = ride_out
            for k, g in zip(ATTN_KEYS, mg):
                grads[k][j] = g
        else:
            d, mg = _ssm_bwd(x0, dr1, dr1, alpha, w["ssm_w_in"][j], w["ssm_conv_w"][j], w["ssm_w_out"][j], msaved, tag + "_ssm")
            for k, g in zip(SSM_KEYS, mg):
                grads[k][j] = g
    return loss_vec, d, {k: jnp.stack(v) for k, v in grads.items()}, bwd_ride_out


MESH = pl.DeviceIdType.MESH
PACK_ELEMS = 2 * SUBLANES * LANES
PACK_ROWS = 512


def _exchange_copies(src_ref, out_ref, gather, send_sems, recv_sems, local_sem):
    x, y, c = lax.axis_index("x"), lax.axis_index("y"), lax.axis_index("c")
    me = 4 * x + 2 * y + c

    def block_for(dev):
        return src_ref if gather else src_ref.at[dev]

    copies = [pltpu.make_async_copy(block_for(me), out_ref.at[me], local_sem)]
    for k in range(1, N_DEV):
        px = 1 - x if k & 4 else x
        py = 1 - y if k & 2 else y
        pc = 1 - c if k & 1 else c
        copies.append(pltpu.make_async_remote_copy(
            src_ref=block_for(4 * px + 2 * py + pc), dst_ref=out_ref.at[me],
            send_sem=send_sems.at[k - 1], recv_sem=recv_sems.at[k - 1],
            device_id=(px, py, pc), device_id_type=MESH))
    return copies


def _exchange(src, gather, name):
    def body(src_ref, out_ref, send_sems, recv_sems, local_sem):
        copies = _exchange_copies(src_ref, out_ref, gather, send_sems, recv_sems, local_sem)
        for cp in copies:
            cp.start()
        for cp in copies:
            cp.wait()

    return pl.pallas_call(
        body, name=name,
        in_specs=[pl.BlockSpec(memory_space=pl.ANY)],
        out_specs=pl.BlockSpec(memory_space=pl.ANY),
        out_shape=jax.ShapeDtypeStruct((N_DEV,) + tuple(src.shape[-2:]), src.dtype),
        scratch_shapes=[pltpu.SemaphoreType.DMA((N_DEV - 1,)), pltpu.SemaphoreType.DMA((N_DEV - 1,)), pltpu.SemaphoreType.DMA],
    )(src)


def _ride_decl(ride):
    if not ride:
        return [], [], [], []
    hbm = pl.BlockSpec(memory_space=pl.ANY)
    out_shape = [jax.ShapeDtypeStruct((N_DEV,) + tuple(a.shape[-2:]), a.dtype) for a, _ in ride]
    sems = [pltpu.SemaphoreType.DMA((len(ride), N_DEV - 1)), pltpu.SemaphoreType.DMA((len(ride), N_DEV - 1)),
            pltpu.SemaphoreType.DMA((len(ride),))]
    return [hbm] * len(ride), [hbm] * len(ride), out_shape, sems


def _ride_copies(ride, src_refs, out_refs, sems):
    copies = []
    for r, ((_, gather), src_ref, out_ref) in enumerate(zip(ride, src_refs, out_refs)):
        copies += _exchange_copies(src_ref, out_ref, gather, sems[0].at[r], sems[1].at[r], sems[2].at[r])
    return copies


def reduce_adamw(parts, w, m, v, name):
    _, R, _ = parts.shape
    tr = _pick(R, (512, 256, 128, 64, 32, 16))

    def body(p_ref, w_ref, m_ref, v_ref, g_ref, d_ref, nm_ref, nv_ref):
        g = p_ref[0].astype(F32)
        for s in range(1, N_DEV):
            g = g + p_ref[s].astype(F32)
        nm = ADAM_B1 * m_ref[...] + (1.0 - ADAM_B1) * g
        nv = ADAM_B2 * v_ref[...] + (1.0 - ADAM_B2) * (g * g)
        m_hat = nm / (1.0 - ADAM_B1 ** ADAM_STEP)
        v_hat = nv / (1.0 - ADAM_B2 ** ADAM_STEP)
        g_ref[...] = g
        d_ref[...] = -ADAM_LR * (m_hat / (jnp.sqrt(v_hat) + ADAM_EPS) + ADAM_WD * w_ref[...])
        nm_ref[...] = nm
        nv_ref[...] = nv

    row = pl.BlockSpec((tr, LANES), lambda i: (i, 0))
    return pl.pallas_call(
        body, name=name, grid=(R // tr,),
        in_specs=[pl.BlockSpec((N_DEV, tr, LANES), lambda i: (0, i, 0)), row, row, row],
        out_specs=[row, row, row, row],
        out_shape=[jax.ShapeDtypeStruct((R, LANES), F32)] * 4,
        compiler_params=_cparams("parallel"),
    )(parts, w, m, v)


def _pack(arrays, dtype, lead=0):
    parts = []
    for a in arrays:
        head = a.shape[:lead]
        flat = a.astype(dtype).reshape(head + (-1,))
        flat = jnp.pad(flat, [(0, 0)] * lead + [(0, (-flat.shape[-1]) % PACK_ELEMS)])
        parts.append(flat.reshape(head + (-1, LANES)))
    rows = sum(part.shape[lead] for part in parts)
    if (-rows) % PACK_ROWS:
        parts.append(jnp.zeros(parts[0].shape[:lead] + ((-rows) % PACK_ROWS, LANES), dtype))
    return jnp.concatenate(parts, axis=lead)


def _unpack(packed, shapes):
    lead = packed.shape[:-2]
    out, r0 = [], 0
    for shape in shapes:
        n = math.prod(shape)
        rows = -(-n // PACK_ELEMS) * (PACK_ELEMS // LANES)
        seg = packed[..., r0:r0 + rows, :].reshape(lead + (rows * LANES,))[..., :n]
        out.append(seg.reshape(lead + tuple(shape)))
        r0 += rows
    return out


MATMUL_SHARDED = {"attn_w_in": 2, "attn_w_out": 1, "ssm_w_in": 2, "ssm_w_out": 1, "ffn_w_up": 2, "ffn_w_down": 1,
                  "ple_w_proj": 2, "ple_w_gate": 1}
SMALL_SHARDED = {"ssm_conv_w": 2, "ssm_conv_b": 1, "ssm_norm_w": 1, "ffn_conv_w": 2}
REPLICATED = ("attn_b_f", "ssm_dt_bias", "ssm_A_log", "ssm_D", "ln_mix_g", "ln_mix_b", "ffn_conv_b", "ln_ffn_g",
              "ln_ffn_b", "ple_b_gate")
WEIGHT_ORDER = ("attn_w_in", "attn_b_f", "attn_w_out", "ssm_w_in", "ssm_conv_w", "ssm_conv_b", "ssm_dt_bias", "ssm_A_log",
                "ssm_D", "ssm_norm_w", "ssm_w_out", "ln_mix_g", "ln_mix_b", "ffn_w_up", "ffn_conv_w", "ffn_conv_b",
                "ffn_w_down", "ln_ffn_g", "ln_ffn_b", "ple_w_proj", "ple_w_gate", "ple_b_gate")


def _join_shards(gathered, axis):
    moved = jnp.moveaxis(gathered, 0, axis)
    shape = list(moved.shape)
    shape[axis:axis + 2] = [shape[axis] * shape[axis + 1]]
    return moved.reshape(shape)


def _split_shards(full, axis):
    shape = list(full.shape)
    shape[axis:axis + 1] = [N_DEV, shape[axis] // N_DEV]
    return jnp.moveaxis(full.reshape(shape), axis, 0)


def kernel(x, p, attn_w_in, attn_b_f, attn_w_out, ssm_w_in, ssm_conv_w, ssm_conv_b, ssm_dt_bias, ssm_A_log, ssm_D, ssm_norm_w, ssm_w_out, ln_mix_g, ln_mix_b, ffn_w_up, ffn_conv_w, ffn_conv_b, ffn_w_down, ln_ffn_g, ln_ffn_b, ple_w_proj, ple_w_gate, ple_b_gate, loss_target, m_attn_w_in, m_attn_b_f, m_attn_w_out, m_ssm_w_in, m_ssm_conv_w, m_ssm_conv_b, m_ssm_dt_bias, m_ssm_A_log, m_ssm_D, m_ssm_norm_w, m_ssm_w_out, m_ln_mix_g, m_ln_mix_b, m_ffn_w_up, m_ffn_conv_w, m_ffn_conv_b, m_ffn_w_down, m_ln_ffn_g, m_ln_ffn_b, m_ple_w_proj, m_ple_w_gate, m_ple_b_gate, v_attn_w_in, v_attn_b_f, v_attn_w_out, v_ssm_w_in, v_ssm_conv_w, v_ssm_conv_b, v_ssm_dt_bias, v_ssm_A_log, v_ssm_D, v_ssm_norm_w, v_ssm_w_out, v_ln_mix_g, v_ln_mix_b, v_ffn_w_up, v_ffn_conv_w, v_ffn_conv_b, v_ffn_w_down, v_ln_ffn_g, v_ln_ffn_b, v_ple_w_proj, v_ple_w_gate, v_ple_b_gate):
    args = dict(locals())
    w_loc = {k: args[k] for k in WEIGHT_ORDER}
    m_loc = {k: args["m_" + k] for k in WEIGHT_ORDER}
    v_loc = {k: args["v_" + k] for k in WEIGHT_ORDER}
    axis_of = {**MATMUL_SHARDED, **SMALL_SHARDED}
    first_names = ("attn_w_in", "attn_w_out")
    mm_names = tuple(k for k in MATMUL_SHARDED if k not in first_names)
    small_names = tuple(SMALL_SHARDED)
    later_names = mm_names + small_names

    def joined(names, gathered):
        return {k: _join_shards(blocks, axis_of[k])
                for k, blocks in zip(names, _unpack(gathered, [w_loc[k].shape for k in names]))}

    g_first = _exchange(_pack([w_loc[k] for k in first_names], BF16), True, "gather_attn_weights")
    w_first = {**{k: w_loc[k] for k in REPLICATED}, **joined(first_names, g_first)}
    fwd_ride = ((_pack([w_loc[k] for k in mm_names], BF16), True), (_pack([w_loc[k] for k in small_names], F32), True))

    def after_fwd_ride(gathered):
        return {**joined(mm_names, gathered[0]), **joined(small_names, gathered[1])}

    def shard_blocks(grads, names):
        return _pack([_split_shards(grads[k], axis_of[k]) for k in names], BF16, lead=1)

    def before_last_bwd(grads):
        return ((shard_blocks({k: jnp.stack(grads[k]) for k in later_names}, later_names), False),)

    loss_vec, grad_x, g_full, (parts_later,) = local_step(x[0], p[:, 0], loss_target[0], w_first, fwd_ride, after_fwd_ride,
                                                          before_last_bwd)
    parts_first = _exchange(shard_blocks(g_full, first_names), False, "exchange_attn_grads")
    res = {}
    for names, parts, tag in ((later_names, parts_later, "later"), (first_names, parts_first, "attn")):
        shapes = [w_loc[k].shape for k in names]
        pk = lambda d: _pack([d[k] for k in names], F32)
        outs = reduce_adamw(parts, pk(w_loc), pk(m_loc), pk(v_loc), "reduce_adamw_" + tag)
        res.update({k: vals for k, vals in zip(names, zip(*[_unpack(o, shapes) for o in outs]))})

    rep_shapes = [w_loc[k].shape for k in REPLICATED] + [(1, LANES)]
    rparts = _exchange(_pack([g_full[k] for k in REPLICATED] + [loss_vec], F32), True, "gather_replicated_grads")
    zero = jnp.zeros((1, LANES), F32)
    rk = lambda d: _pack([d[k] for k in REPLICATED] + [zero], F32)
    routs = reduce_adamw(rparts, rk(w_loc), rk(m_loc), rk(v_loc), "reduce_adamw_replicated")
    runp = [_unpack(o, rep_shapes) for o in routs]
    for i, k in enumerate(REPLICATED):
        res[k] = tuple(u[i] for u in runp)
    loss = runp[0][-1][0, 0]

    return (loss, grad_x[None], *[res[k][0] for k in WEIGHT_ORDER], *[res[k][1] for k in WEIGHT_ORDER],
            *[res[k][2] for k in WEIGHT_ORDER], *[res[k][3] for k in WEIGHT_ORDER])
```
